```python
import jax
import jax.numpy as jnp
from jax import lax
import numpy as np

D_MODEL = 1024
BATCH = 32
SEQ = 256
DEPTH = 4
DEC_BATCH = 2
DEC_SEQ = 1024
PAST_LEN = 512

F32 = jnp.float32
GRID_W = 64
BLOCK = 128
NEG_INF = -1e30
EPS = 1e-6
ROPE_THETA = 10000.0

A_HEADS = 4
A_KV_HEADS = 2
A_HEAD_DIM = 64
WINDOW = 128
B_HEADS = 4
B_KEY_DIM = 128
B_VAL_DIM = 128
CONV_W = 3
CHUNK = 64
C_HEADS = 4
C_KV_HEADS = 2
C_HEAD_DIM = 64
N_GROUPS = 4
EXPERTS_PER_GROUP = 4
N_EXPERTS = N_GROUPS * EXPERTS_PER_GROUP
D_EXPERT = 256
TOP_K = 2

A_WIDTH = A_HEADS * A_HEAD_DIM
B_WIDTH = B_HEADS * B_VAL_DIM
C_WIDTH = C_HEADS * C_HEAD_DIM
MIX_WIDTH = A_WIDTH + B_WIDTH + C_WIDTH
IN_SIZES = (A_WIDTH, A_KV_HEADS * A_HEAD_DIM, A_KV_HEADS * A_HEAD_DIM,
            B_HEADS * B_KEY_DIM, B_HEADS * B_KEY_DIM, B_WIDTH, B_WIDTH, 4 * B_HEADS,
            C_WIDTH, C_KV_HEADS * C_HEAD_DIM, C_KV_HEADS * C_HEAD_DIM)
D_IN = sum(IN_SIZES)

kernel_name = 'hybrid_diffusion_prefix_trunk_step'


def rmsnorm(x, g):
    xf = x.astype(F32)
    y = xf * lax.rsqrt(jnp.mean(xf * xf, -1, keepdims=True) + EPS)
    return (y * g.astype(F32)).astype(x.dtype)


def l2norm(x):
    xf = x.astype(F32)
    return (xf * lax.rsqrt(jnp.sum(xf * xf, -1, keepdims=True) + EPS)).astype(x.dtype)


def split_columns(z):
    out, start = [], 0
    for n in IN_SIZES:
        out.append(z[..., start:start + n])
        start += n
    return out


def axial_rope(n_tok, head_dim):
    t = jnp.arange(n_tok)
    row = (t // GRID_W).astype(F32)
    col = (t % GRID_W).astype(F32)
    n_freq = head_dim // 4
    inv_freq = ROPE_THETA ** (-jnp.arange(n_freq, dtype=F32) / n_freq)
    ang = jnp.concatenate([row[:, None] * inv_freq, col[:, None] * inv_freq], -1)
    return jnp.cos(ang), jnp.sin(ang)


def apply_rope(x, cos, sin):
    xf = x.astype(F32)
    half = x.shape[-1] // 2
    x1, x2 = xf[..., :half], xf[..., half:]
    c = cos[None, :, None, :]
    s = sin[None, :, None, :]
    return jnp.concatenate([x1 * c - x2 * s, x2 * c + x1 * s], -1).astype(x.dtype)


def gqa_block(q, k, v, mask, sink):
    s = jnp.einsum('bqhgd,bshd->bhgqs', q, k, preferred_element_type=F32) * (q.shape[-1] ** -0.5)
    if mask is not None:
        s = jnp.where(mask, s, NEG_INF)
    if sink is not None:
        col = jnp.broadcast_to(sink.astype(F32)[None, :, :, None, None], s.shape[:-1] + (1,))
        p = jax.nn.softmax(jnp.concatenate([s, col], -1), -1)[..., :-1]
    else:
        p = jax.nn.softmax(s, -1)
    return jnp.einsum('bhgqs,bshd->bqhgd', p.astype(v.dtype), v)


def query_blocks(q, n_kv):
    b, t, h, dh = q.shape
    return q.reshape(b, t // BLOCK, BLOCK, n_kv, h // n_kv, dh).transpose(1, 0, 2, 3, 4, 5)


def merge_blocks(o):
    nb, b, qn, hkv, g, dh = o.shape
    return o.transpose(1, 0, 2, 3, 4, 5).reshape(b, nb * qn, hkv * g * dh)


def dense_attention(q, k, v, sink):
    n_kv = k.shape[2]
    sk = None if sink is None else sink.reshape(n_kv, -1)
    o = lax.map(lambda qb: gqa_block(qb, k, v, None, sk), query_blocks(q, n_kv))
    return merge_blocks(o)


def window_attention(q, k, v, k_ctx, v_ctx, sink):
    t = q.shape[1]
    n_kv = k.shape[2]
    sk = sink.reshape(n_kv, -1)
    pad = ((0, 0), (BLOCK, BLOCK), (0, 0), (0, 0))
    kp, vp = jnp.pad(k, pad), jnp.pad(v, pad)
    rel = jnp.arange(3 * BLOCK) - BLOCK
    qi = jnp.arange(BLOCK)
    in_window = jnp.abs(rel[None, :] - qi[:, None]) <= WINDOW
    ctx_mask = jnp.ones((BLOCK, k_ctx.shape[1]), bool)

    def one(args):
        b, qb = args
        start = b * BLOCK
        kb = lax.dynamic_slice_in_dim(kp, start, 3 * BLOCK, axis=1)
        vb = lax.dynamic_slice_in_dim(vp, start, 3 * BLOCK, axis=1)
        kpos = start + rel
        band = in_window & ((kpos >= 0) & (kpos < t))[None, :]
        mask = jnp.concatenate([ctx_mask, band], 1)
        return gqa_block(qb, jnp.concatenate([k_ctx, kb], 1), jnp.concatenate([v_ctx, vb], 1), mask, sk)

    o = lax.map(one, (jnp.arange(t // BLOCK), query_blocks(q, n_kv)))
    return merge_blocks(o)


def centred_conv(x, w):
    p = CONV_W // 2
    y = lax.conv_general_dilated(x, w[:, None, :].astype(x.dtype), (1,), [(p, p)],
                                 dimension_numbers=('NWC', 'WIO', 'NWC'),
                                 feature_group_count=x.shape[-1])
    return jax.nn.silu(y)


def gated_delta_chunked(q, k, v, g, beta, s0):
    b, t, h, dk = q.shape
    dv = v.shape[-1]
    n = t // CHUNK

    def chunks(x):
        return x.astype(F32).reshape(b, n, CHUNK, h, -1).transpose(1, 0, 3, 2, 4)

    qc = chunks(q) * (dk ** -0.5)
    kc, vc = chunks(k), chunks(v)
    gc = jnp.cumsum(g.astype(F32).reshape(b, n, CHUNK, h).transpose(1, 0, 3, 2), -1)
    bc = beta.astype(F32).reshape(b, n, CHUNK, h).transpose(1, 0, 3, 2)
    tril = jnp.tril(jnp.ones((CHUNK, CHUNK), bool))
    strict = jnp.tril(jnp.ones((CHUNK, CHUNK), bool), -1)
    diff = gc[..., :, None] - gc[..., None, :]
    decay = jnp.where(tril, jnp.exp(jnp.where(tril, diff, 0.0)), 0.0)
    kb = kc * bc[..., None]
    vb = vc * bc[..., None]
    a_mat = jnp.where(strict, jnp.einsum('...id,...jd->...ij', kb, kc) * decay, 0.0)
    eye = jnp.eye(CHUNK, dtype=F32)
    t_inv = lax.linalg.triangular_solve(eye + a_mat, jnp.broadcast_to(eye, a_mat.shape),
                                        left_side=True, lower=True, unit_diagonal=True)
    u = t_inv @ vb
    w = t_inv @ (kb * jnp.exp(gc)[..., None])
    attn = jnp.where(tril, jnp.einsum('...id,...jd->...ij', qc, kc) * decay, 0.0)
    g_last = gc[..., -1]

    def step(s, xs):
        qi, ki, ui, wi, ai, gi, gl = xs
        v_new = ui - wi @ s
        o = (qi * jnp.exp(gi)[..., None]) @ s + ai @ v_new
        s = s * jnp.exp(gl)[..., None, None] + jnp.swapaxes(ki * jnp.exp(gl[..., None] - gi)[..., None], -1, -2) @ v_new
        return s, o

    s_fin, o = lax.scan(step, s0.astype(F32), (qc, kc, u, w, attn, gc, g_last))
    o = o.transpose(1, 0, 3, 2, 4).reshape(b, t, h, dv)
    return o.astype(v.dtype), s_fin.astype(s0.dtype)


def deltanet_mixer(q, k, v, gate, ab, conv_w, a_log, dt_bias, norm_g, s_fwd0, s_bwd0):
    b, t, _ = q.shape
    qkv = centred_conv(jnp.concatenate([q, k, v], -1), conv_w)
    nk = B_HEADS * B_KEY_DIM
    q = l2norm(qkv[..., :nk].reshape(b, t, B_HEADS, B_KEY_DIM))
    k = l2norm(qkv[..., nk:2 * nk].reshape(b, t, B_HEADS, B_KEY_DIM))
    v = qkv[..., 2 * nk:].reshape(b, t, B_HEADS, B_VAL_DIM)
    abf = ab.astype(F32).reshape(b, t, 4, B_HEADS)
    a_log = a_log.astype(F32)
    dt_bias = dt_bias.astype(F32)
    g_f = -jnp.exp(a_log[0]) * jax.nn.softplus(abf[:, :, 0] + dt_bias[0])
    g_b = -jnp.exp(a_log[1]) * jax.nn.softplus(abf[:, :, 1] + dt_bias[1])
    beta_f = jax.nn.sigmoid(abf[:, :, 2])
    beta_b = jax.nn.sigmoid(abf[:, :, 3])
    o_f, s_f = gated_delta_chunked(q, k, v, g_f, beta_f, s_fwd0)
    flip = lambda z: jnp.flip(z, 1)
    o_b, s_b = gated_delta_chunked(flip(q), flip(k), flip(v), flip(g_b), flip(beta_b), s_bwd0)
    o = o_f + flip(o_b)
    o = rmsnorm(o, norm_g) * jax.nn.silu(gate.reshape(b, t, B_HEADS, B_VAL_DIM))
    return o.reshape(b, t, B_WIDTH), s_f, s_b


def hier_moe(h, w_group, b_group, w_expert, b_expert, w1, w3, w2):
    n = h.shape[0]
    g_logits = jnp.dot(h, w_group, preferred_element_type=F32) + b_group.astype(F32)
    g_sel = jnp.argmax(g_logits, -1)
    g_w = jnp.max(jax.nn.softmax(g_logits, -1), -1, keepdims=True)
    e_logits = (jnp.dot(h, w_expert, preferred_element_type=F32) + b_expert.astype(F32)).reshape(n, N_GROUPS, EXPERTS_PER_GROUP)
    e_logits = jnp.einsum('nge,ng->ne', e_logits, jax.nn.one_hot(g_sel, N_GROUPS, dtype=F32))
    top_p, top_i = lax.top_k(jax.nn.softmax(e_logits, -1), TOP_K)
    weights = g_w * top_p / jnp.sum(top_p, -1, keepdims=True)
    expert = g_sel[:, None] * EXPERTS_PER_GROUP + top_i
    gate = jnp.sum(jax.nn.one_hot(expert, N_EXPERTS, dtype=F32) * weights[..., None], 1)
    hid = jax.nn.silu(h @ w1) * (h @ w3)
    hid = (hid.reshape(n, N_EXPERTS, D_EXPERT) * gate[:, :, None].astype(h.dtype)).reshape(n, N_EXPERTS * D_EXPERT)
    return hid @ w2


def trunk_layer(x, cond, lw, rope, ctx):
    b, t, _ = x.shape
    mods = jax.nn.silu(cond) @ lw['w_mod'] + lw['b_mod']
    sh1, sc1, g1, sh2, sc2, g2 = [m[:, None, :] for m in jnp.split(mods, 6, -1)]
    h = rmsnorm(x, lw['norm1_g']) * (1 + sc1) + sh1
    aq, ak, av, bq, bk, bv, bgate, bab, cq, ck, cv = split_columns(h @ lw['w_in'])
    aq = aq.reshape(b, t, A_HEADS, A_HEAD_DIM)
    ak = ak.reshape(b, t, A_KV_HEADS, A_HEAD_DIM)
    av = av.reshape(b, t, A_KV_HEADS, A_HEAD_DIM)
    cq = rmsnorm(cq.reshape(b, t, C_HEADS, C_HEAD_DIM), lw['c_q_norm'])
    ck = rmsnorm(ck.reshape(b, t, C_KV_HEADS, C_HEAD_DIM), lw['c_k_norm'])
    cv = cv.reshape(b, t, C_KV_HEADS, C_HEAD_DIM)
    if ctx is None:
        a_out = dense_attention(aq, ak, av, lw['a_sink'])
        c_out = dense_attention(cq, ck, cv, None)
        s_f0 = jnp.zeros((b, B_HEADS, B_KEY_DIM, B_VAL_DIM), x.dtype)
        s_b0 = s_f0
    else:
        ctx_ak, ctx_av, ctx_ck, ctx_cv, s_f0, s_b0 = ctx
        cos, sin = rope
        a_out = window_attention(apply_rope(aq, cos, sin), apply_rope(ak, cos, sin), av,
                                 ctx_ak, ctx_av, lw['a_sink'])
        c_out = dense_attention(apply_rope(cq, cos, sin),
                                jnp.concatenate([ctx_ck, apply_rope(ck, cos, sin)], 1),
                                jnp.concatenate([ctx_cv, cv], 1), None)
    b_out, s_f, s_b = deltanet_mixer(bq, bk, bv, bgate, bab, lw['b_conv'], lw['b_a_log'],
                                     lw['b_dt_bias'], lw['b_norm_g'], s_f0, s_b0)
    x = x + g1 * (jnp.concatenate([a_out, b_out, c_out], -1) @ lw['w_out'])
    h2 = rmsnorm(x, lw['norm2_g']) * (1 + sc2) + sh2
    ffn = hier_moe(h2.reshape(b * t, D_MODEL), lw['w_group'], lw['b_group'], lw['w_expert'],
                   lw['b_expert'], lw['w1'], lw['w3'], lw['w2'])
    x = x + g2 * ffn.reshape(b, t, D_MODEL)
    new_ctx = (ak, av, ck, cv, s_f, s_b) if ctx is None else None
    return x, new_ctx


def setup_inputs(seed: int = 0) -> dict:
    key = jax.random.key(seed)
    ks = iter(jax.random.split(key, 40))

    def nrm(shape, scale):
        return scale * jax.random.normal(next(ks), shape, F32)

    d = D_MODEL
    ef = N_EXPERTS * D_EXPERT
    a_log = jnp.log(jax.random.uniform(next(ks), (DEPTH, 2, B_HEADS), F32, minval=1.0, maxval=16.0))
    dt = jax.random.uniform(next(ks), (DEPTH, 2, B_HEADS), F32, minval=1e-3, maxval=1e-1)
    dt_bias = dt + jnp.log(-jnp.expm1(-dt))
    return {
        'x_prompt': nrm((BATCH, SEQ, d), 1.0),
        'x_sample': nrm((DEC_BATCH, DEC_SEQ, d), 1.0),
        'cache_a_k': nrm((DEC_BATCH, DEPTH, PAST_LEN, A_KV_HEADS, A_HEAD_DIM), 1.0),
        'cache_a_v': nrm((DEC_BATCH, DEPTH, PAST_LEN, A_KV_HEADS, A_HEAD_DIM), 1.0),
        'cache_c_k': nrm((DEC_BATCH, DEPTH, PAST_LEN, C_KV_HEADS, C_HEAD_DIM), 1.0),
        'cache_c_v': nrm((DEC_BATCH, DEPTH, PAST_LEN, C_KV_HEADS, C_HEAD_DIM), 1.0),
        'state_b_fwd': nrm((DEC_BATCH, DEPTH, B_HEADS, B_KEY_DIM, B_VAL_DIM), B_KEY_DIM ** -0.5),
        'state_b_bwd': nrm((DEC_BATCH, DEPTH, B_HEADS, B_KEY_DIM, B_VAL_DIM), B_KEY_DIM ** -0.5),
        'c': nrm((DEC_BATCH, d), 1.0),
        'c_ctx': nrm((d,), 1.0),
        'w_mod': nrm((DEPTH, d, 6 * d), 0.5 * d ** -0.5),
        'b_mod': nrm((DEPTH, 6 * d), 0.02),
        'norm1_g': 1.0 + nrm((DEPTH, d), 0.05),
        'norm2_g': 1.0 + nrm((DEPTH, d), 0.05),
        'w_in': nrm((DEPTH, d, D_IN), d ** -0.5),
        'a_sink': nrm((DEPTH, A_HEADS), 0.5),
        'b_conv': nrm((DEPTH, CONV_W, 2 * B_HEADS * B_KEY_DIM + B_WIDTH), CONV_W ** -0.5),
        'b_a_log': a_log,
        'b_dt_bias': dt_bias,
        'b_norm_g': 1.0 + nrm((DEPTH, B_VAL_DIM), 0.05),
        'c_q_norm': 1.0 + nrm((DEPTH, C_HEAD_DIM), 0.05),
        'c_k_norm': 1.0 + nrm((DEPTH, C_HEAD_DIM), 0.05),
        'w_out': nrm((DEPTH, MIX_WIDTH, d), MIX_WIDTH ** -0.5),
        'w_group': nrm((DEPTH, d, N_GROUPS), d ** -0.5),
        'b_group': nrm((DEPTH, N_GROUPS), 0.01),
        'w_expert': nrm((DEPTH, d, N_EXPERTS), d ** -0.5),
        'b_expert': nrm((DEPTH, N_EXPERTS), 0.01),
        'w1': nrm((DEPTH, d, ef), d ** -0.5),
        'w3': nrm((DEPTH, d, ef), d ** -0.5),
        'w2': nrm((DEPTH, ef, d), (TOP_K * D_EXPERT) ** -0.5),
        'final_norm_g': 1.0 + nrm((d,), 0.05),
    }


def reference(x_prompt, x_sample, cache_a_k, cache_a_v, cache_c_k, cache_c_v, state_b_fwd, state_b_bwd,
              c, c_ctx, w_mod, b_mod, norm1_g, norm2_g, w_in, a_sink, b_conv, b_a_log, b_dt_bias,
              b_norm_g, c_q_norm, c_k_norm, w_out, w_group, b_group, w_expert, b_expert, w1, w3, w2,
              final_norm_g):
    rope = axial_rope(x_sample.shape[1], A_HEAD_DIM)
    cond_ctx = jnp.broadcast_to(c_ctx[None, :], (x_prompt.shape[0], D_MODEL))
    xp, xs = x_prompt, x_sample
    new_ctx = []
    for l in range(DEPTH):
        lw = {
            'w_mod': w_mod[l], 'b_mod': b_mod[l], 'norm1_g': norm1_g[l], 'norm2_g': norm2_g[l],
            'w_in': w_in[l], 'a_sink': a_sink[l], 'b_conv': b_conv[l], 'b_a_log': b_a_log[l],
            'b_dt_bias': b_dt_bias[l], 'b_norm_g': b_norm_g[l], 'c_q_norm': c_q_norm[l],
            'c_k_norm': c_k_norm[l], 'w_out': w_out[l], 'w_group': w_group[l], 'b_group': b_group[l],
            'w_expert': w_expert[l], 'b_expert': b_expert[l], 'w1': w1[l], 'w3': w3[l], 'w2': w2[l],
        }
        xp, ctx_l = trunk_layer(xp, cond_ctx, lw, None, None)
        new_ctx.append(ctx_l)
        cache_l = (cache_a_k[:, l], cache_a_v[:, l], cache_c_k[:, l], cache_c_v[:, l],
                   state_b_fwd[:, l], state_b_bwd[:, l])
        xs, _ = trunk_layer(xs, c, lw, rope, cache_l)
    y_prompt = rmsnorm(xp, final_norm_g)
    y_sample = rmsnorm(xs, final_norm_g)
    new_cache_a_k = jnp.stack([e[0] for e in new_ctx], 1)
    new_cache_a_v = jnp.stack([e[1] for e in new_ctx], 1)
    new_cache_c_k = jnp.stack([e[2] for e in new_ctx], 1)
    new_cache_c_v = jnp.stack([e[3] for e in new_ctx], 1)
    new_state_b_fwd = jnp.stack([e[4] for e in new_ctx], 1)
    new_state_b_bwd = jnp.stack([e[5] for e in new_ctx], 1)
    return (y_prompt, y_sample, new_cache_a_k, new_cache_a_v, new_cache_c_k, new_cache_c_v, new_state_b_fwd, new_state_b_bwd)
```

```python
import functools

import jax
import jax.numpy as jnp
import numpy as np
from jax import lax
from jax.experimental import pallas as pl
from jax.experimental.pallas import tpu as pltpu

F32 = jnp.float32
BF16 = jnp.bfloat16

D_MODEL = 1024
DEPTH = 4
GRID_W = 64
EPS = 1e-6
NEG_INF = -1e30
ROPE_THETA = 10000.0
HEAD_DIM = 64
N_Q_HEADS = 4
WINDOW = 128
Q_BLOCK = 128
B_HEADS = 4
B_DIM = 128
CHUNK = 64
BD = B_HEADS * CHUNK
N_LEVELS = 6
N_GROUPS = 4
EXPERTS_PER_GROUP = 4
N_EXPERTS = 16
D_EXPERT = 256
ROUTER_LANE0 = N_GROUPS

LANES = 128
SUBLANES = 8
VMEM_LIMIT = 56 * 1024 * 1024

ZA_W, ZB_W, ZC_W, ZAB_W = 512, 2048, 512, 128
Z_W = ZA_W + ZB_W + ZC_W + ZAB_W


def _sigmoid(x):
    return 1.0 / (1.0 + jnp.exp(-x))


def _silu(x):
    return x * _sigmoid(x)


def _softplus(x):
    return jnp.maximum(x, 0.0) + jnp.log1p(jnp.exp(-jnp.abs(x)))


def _dot(a, b):
    return jnp.dot(a, b, preferred_element_type=F32)


def _dot_nt(a, b):
    return lax.dot_general(a, b, (((1,), (1,)), ((), ())), preferred_element_type=F32)


def _dot_tn(a, b):
    return lax.dot_general(a, b, (((0,), (0,)), ((), ())), preferred_element_type=F32)


def _split2(x):
    hi = x.astype(BF16)
    lo = (x - hi.astype(F32)).astype(BF16)
    return hi, lo


def _params(sem=None):
    return pltpu.CompilerParams(dimension_semantics=sem, vmem_limit_bytes=VMEM_LIMIT)


def _mods_kernel(cond_ref, w_ref, b_ref, o_ref):
    n_cond = cond_ref.shape[0]
    tn = w_ref.shape[2]
    reps = tn // LANES

    def body(kb, accs):
        r = pl.multiple_of(kb * SUBLANES, SUBLANES)
        w = w_ref[0, pl.ds(r, SUBLANES), :]
        out = []
        for m in range(n_cond):
            cm = cond_ref[m, pl.ds(r, SUBLANES), :]
            a = jnp.tile(_silu(cm), (1, reps))
            out.append(accs[m] + a * w)
        return tuple(out)

    zero = jnp.zeros((SUBLANES, tn), F32)
    accs = lax.fori_loop(0, w_ref.shape[1] // SUBLANES, body, (zero,) * n_cond)
    rows = [jnp.sum(a, axis=0, keepdims=True) + b_ref[0] for a in accs]
    rows.append(jnp.zeros((SUBLANES - n_cond, tn), F32))
    o_ref[0] = jnp.concatenate(rows, axis=0)


def _mods_call(cond_b, w_mod, b_mod):
    depth, d, n = w_mod.shape
    tn = 1536
    n_cond = cond_b.shape[0]
    return pl.pallas_call(
        _mods_kernel,
        grid=(depth, n // tn),
        in_specs=[
            pl.BlockSpec((n_cond, d, LANES), lambda l, j: (0, 0, 0)),
            pl.BlockSpec((1, d, tn), lambda l, j: (l, 0, j)),
            pl.BlockSpec((1, 1, tn), lambda l, j: (l, 0, j)),
        ],
        out_specs=pl.BlockSpec((1, SUBLANES, tn), lambda l, j: (l, 0, j)),
        out_shape=jax.ShapeDtypeStruct((depth, SUBLANES, n), F32),
        compiler_params=_params(("arbitrary", "arbitrary")),
        name="mods",
    )(cond_b, w_mod, b_mod.reshape(depth, 1, n))


def _modulated_norm(x, g, shift, scale):
    ms = jnp.mean(x * x, axis=-1, keepdims=True)
    y = x * lax.rsqrt(ms + EPS) * g
    return y * (1.0 + scale) + shift


def _inproj_kernel(x_ref, mod_ref, g_ref, w_ref, za_ref, zb_ref, zc_ref, zab_ref):
    m = mod_ref[0]
    h = _modulated_norm(x_ref[...], g_ref[...], m[0:1], m[1:2]).astype(BF16)
    za_ref[...] = _dot(h, w_ref[:, 0:ZA_W])
    step = 512
    for j in range(ZB_W // step):
        zb_ref[:, j * step:(j + 1) * step] = _dot(h, w_ref[:, ZA_W + j * step:ZA_W + (j + 1) * step])
    zc_ref[...] = _dot(h, w_ref[:, ZA_W + ZB_W:ZA_W + ZB_W + ZC_W])
    zab_ref[...] = _dot(h, w_ref[:, ZA_W + ZB_W + ZC_W:Z_W])


def _inproj_call(x, mods, g, w, slot_fn, tm):
    n_tok = x.shape[0]
    return pl.pallas_call(
        _inproj_kernel,
        grid=(n_tok // tm,),
        in_specs=[
            pl.BlockSpec((tm, D_MODEL), lambda i: (i, 0)),
            pl.BlockSpec((1, 6, D_MODEL), lambda i: (slot_fn(i), 0, 0)),
            pl.BlockSpec((1, D_MODEL), lambda i: (0, 0)),
            pl.BlockSpec((D_MODEL, Z_W), lambda i: (0, 0)),
        ],
        out_specs=[
            pl.BlockSpec((tm, ZA_W), lambda i: (i, 0)),
            pl.BlockSpec((tm, ZB_W), lambda i: (i, 0)),
            pl.BlockSpec((tm, ZC_W), lambda i: (i, 0)),
            pl.BlockSpec((tm, ZAB_W), lambda i: (i, 0)),
        ],
        out_shape=[
            jax.ShapeDtypeStruct((n_tok, ZA_W), F32),
            jax.ShapeDtypeStruct((n_tok, ZB_W), F32),
            jax.ShapeDtypeStruct((n_tok, ZC_W), F32),
            jax.ShapeDtypeStruct((n_tok, ZAB_W), F32),
        ],
        compiler_params=_params(("arbitrary",)),
        name="inproj",
    )(x, mods, g, w)


def _lane_lo(shape):
    return lax.broadcasted_iota(jnp.int32, shape, len(shape) - 1) % LANES < HEAD_DIM


def _store_kdup(dst_ref, off, k):
    n = k.shape[0]
    r = pltpu.roll(k, HEAD_DIM, 1)
    lo = _lane_lo(k.shape)
    dst_ref[0, off:off + n, :] = jnp.where(lo, k, r).astype(BF16)
    dst_ref[1, off:off + n, :] = jnp.where(lo, r, k).astype(BF16)


def _store_vsplit(dst_ref, off, v):
    n = v.shape[0]
    r = pltpu.roll(v, HEAD_DIM, 1)
    lo = _lane_lo(v.shape)
    z = jnp.zeros_like(v)
    dst_ref[0, off:off + n, :] = jnp.where(lo, v, z).astype(BF16)
    dst_ref[1, off:off + n, :] = jnp.where(lo, z, r).astype(BF16)
    dst_ref[2, off:off + n, :] = jnp.where(lo, r, z).astype(BF16)
    dst_ref[3, off:off + n, :] = jnp.where(lo, z, v).astype(BF16)


def _rope(x, cos, sin):
    first = (lax.broadcasted_iota(jnp.int32, x.shape, 1) // (HEAD_DIM // 2)) % 2 == 0
    partner = jnp.where(first, pltpu.roll(x, LANES - HEAD_DIM // 2, 1), pltpu.roll(x, HEAD_DIM // 2, 1))
    return x * cos + partner * sin


def _head_rmsnorm(x, g, seg_hi, seg_lo):
    hi, lo = _split2(x * x)
    ms = _dot(hi, seg_hi) + _dot(lo, seg_hi) + _dot(hi, seg_lo)
    return x * lax.rsqrt(ms + EPS) * g


def _attend(qt, segs, sink_pair):
    qb = qt.shape[0]
    lo = _lane_lo(qt.shape)
    z = jnp.zeros_like(qt)
    qs = jnp.concatenate([jnp.where(lo, qt, z), jnp.where(lo, z, qt)], axis=0).astype(BF16)
    scores = []
    for kdup, _, _, mask in segs:
        s = _dot_nt(qs, kdup)
        if mask is not None:
            s = jnp.where(mask, s, NEG_INF)
        scores.append(s)
    m = scores[0].max(axis=1, keepdims=True)
    for s in scores[1:]:
        m = jnp.maximum(m, s.max(axis=1, keepdims=True))
    if sink_pair is not None:
        row_a = lax.broadcasted_iota(jnp.int32, (2 * qb, 1), 0) < qb
        sink = jnp.where(row_a, sink_pair[0], sink_pair[1])
        m = jnp.maximum(m, sink)
        denom = jnp.exp(sink - m)
    else:
        denom = jnp.zeros((2 * qb, 1), F32)
    acc = jnp.zeros((qb, LANES), F32)
    for s, (_, vlo, vhi, _) in zip(scores, segs):
        p = jnp.exp(s - m)
        denom = denom + p.sum(axis=1, keepdims=True)
        pb = p.astype(BF16)
        acc = acc + _dot(pb[:qb], vlo) + _dot(pb[qb:], vhi)
    inv = 1.0 / denom
    return acc * jnp.where(lo, inv[:qb], inv[qb:])


def _attn_kernel(has_ctx, t, *refs):
    if has_ctx:
        (sink_ref, za_ref, zc_ref, cqn_ref, ckn_ref, seg_ref, cos_ref, sin_ref,
         cak_ref, cav_ref, cck_ref, ccv_ref,
         ao_ref, co_ref,
         ka_s, va_s, kc_s, vc_s, kctx_s, vctx_s, qa_s, qc_s) = refs
    else:
        (sink_ref, za_ref, zc_ref, cqn_ref, ckn_ref, seg_ref,
         ao_ref, co_ref, ckn_out_ref,
         ka_s, va_s, kc_s, vc_s, qa_s, qc_s) = refs
    scale = HEAD_DIM ** -0.5
    seg_hi = seg_ref[0]
    seg_lo = seg_ref[1]
    piece = 256
    n_ctx = cak_ref.shape[1] if has_ctx else 0

    for p0 in range(0, t, piece):
        rows = slice(p0, p0 + piece)
        ak = za_ref[rows, 256:384]
        av = za_ref[rows, 384:512]
        ck = _head_rmsnorm(zc_ref[rows, 256:384], ckn_ref[...], seg_hi, seg_lo)
        cv = zc_ref[rows, 384:512]
        if has_ctx:
            cos = cos_ref[rows, :]
            sin = sin_ref[rows, :]
            ak = _rope(ak, cos, sin)
            ck = _rope(ck, cos, sin)
            _store_kdup(ka_s, WINDOW + p0, ak)
            _store_vsplit(va_s, WINDOW + p0, av)
            _store_kdup(kc_s, n_ctx + p0, ck)
            _store_vsplit(vc_s, n_ctx + p0, cv)
        else:
            ckn_out_ref[rows, :] = ck
            _store_kdup(ka_s, p0, ak)
            _store_vsplit(va_s, p0, av)
            _store_kdup(kc_s, p0, ck)
            _store_vsplit(vc_s, p0, cv)
        for hk in range(2):
            cols = slice(hk * LANES, (hk + 1) * LANES)
            aq = za_ref[rows, cols]
            cq = _head_rmsnorm(zc_ref[rows, cols], cqn_ref[:, cols], seg_hi, seg_lo)
            if has_ctx:
                aq = _rope(aq, cos, sin)
                cq = _rope(cq, cos, sin)
            qa_s[rows, cols] = aq * scale
            qc_s[rows, cols] = cq * scale

    if has_ctx:
        zpad = jnp.zeros((WINDOW, LANES), BF16)
        for i in range(2):
            ka_s[i, 0:WINDOW, :] = zpad
            ka_s[i, WINDOW + t:2 * WINDOW + t, :] = zpad
        for i in range(4):
            va_s[i, 0:WINDOW, :] = zpad
            va_s[i, WINDOW + t:2 * WINDOW + t, :] = zpad
        for p0 in range(0, n_ctx, piece):
            rows = slice(p0, p0 + piece)
            _store_kdup(kctx_s, p0, cak_ref[0, rows, :])
            _store_vsplit(vctx_s, p0, cav_ref[0, rows, :])
            _store_kdup(kc_s, p0, cck_ref[0, rows, :])
            _store_vsplit(vc_s, p0, ccv_ref[0, rows, :])

        qb = Q_BLOCK
        span = 3 * qb
        qi = lax.broadcasted_iota(jnp.int32, (2 * qb, span), 0) % qb
        kj = lax.broadcasted_iota(jnp.int32, (2 * qb, span), 1)
        band = jnp.abs(kj - qb - qi) <= WINDOW

        def block(b, carry):
            r0 = pl.multiple_of(b * qb, qb)
            kpos = kj + (r0 - qb)
            mask = band & (kpos >= 0) & (kpos < t)
            for hk in range(2):
                cols = slice(hk * LANES, (hk + 1) * LANES)
                segs_a = [
                    (kctx_s[hk], vctx_s[2 * hk], vctx_s[2 * hk + 1], None),
                    (ka_s[hk, pl.ds(r0, span), :], va_s[2 * hk, pl.ds(r0, span), :],
                     va_s[2 * hk + 1, pl.ds(r0, span), :], mask),
                ]
                sinks = (sink_ref[2 * hk], sink_ref[2 * hk + 1])
                ao_ref[pl.ds(r0, qb), cols] = _attend(qa_s[pl.ds(r0, qb), cols], segs_a, sinks)
                segs_c = [(kc_s[hk], vc_s[2 * hk], vc_s[2 * hk + 1], None)]
                co_ref[pl.ds(r0, qb), cols] = _attend(qc_s[pl.ds(r0, qb), cols], segs_c, None)
            return carry

        lax.fori_loop(0, t // qb, block, 0)
    else:
        for hk in range(2):
            cols = slice(hk * LANES, (hk + 1) * LANES)
            sinks = (sink_ref[2 * hk], sink_ref[2 * hk + 1])
            segs_a = [(ka_s[hk], va_s[2 * hk], va_s[2 * hk + 1], None)]
            ao_ref[:, cols] = _attend(qa_s[:, cols], segs_a, sinks)
            segs_c = [(kc_s[hk], vc_s[2 * hk], vc_s[2 * hk + 1], None)]
            co_ref[:, cols] = _attend(qc_s[:, cols], segs_c, None)


def _attn_call(has_ctx, t, n_batch, row_block0, za, zc, sink, cqn, ckn, seg, ao_prev=None, co_prev=None,
               rope=None, ctx=None):
    n_tok = za.shape[0]
    tok_spec = lambda w: pl.BlockSpec((t, w), lambda b, *_: (row_block0 + b, 0))
    const2 = lambda shape: pl.BlockSpec(shape, lambda b, *_: (0,) * len(shape))
    in_specs = [tok_spec(ZA_W), tok_spec(ZC_W), const2((1, 256)), const2((1, 128)), const2((2, LANES, LANES))]
    args = [za, zc, cqn, ckn, seg]
    out_specs = [tok_spec(256), tok_spec(256)]
    out_shape = [jax.ShapeDtypeStruct((n_tok, 256), F32), jax.ShapeDtypeStruct((n_tok, 256), F32)]
    aliases = {}
    if has_ctx:
        n_ctx = ctx[0].shape[1]
        cos, sin = rope
        in_specs += [const2((t, LANES)), const2((t, LANES))]
        args += [cos, sin]
        ctx_spec = pl.BlockSpec((1, n_ctx, LANES), lambda b, *_: (b, 0, 0))
        in_specs += [ctx_spec] * 4
        args += list(ctx)
        in_specs += [pl.BlockSpec(memory_space=pl.ANY)] * 2
        args += [ao_prev, co_prev]
        aliases = {1 + len(args) - 2: 0, 1 + len(args) - 1: 1}
        scratch = [
            pltpu.VMEM((2, t + 2 * WINDOW, LANES), BF16), pltpu.VMEM((4, t + 2 * WINDOW, LANES), BF16),
            pltpu.VMEM((2, n_ctx + t, LANES), BF16), pltpu.VMEM((4, n_ctx + t, LANES), BF16),
            pltpu.VMEM((2, n_ctx, LANES), BF16), pltpu.VMEM((4, n_ctx, LANES), BF16),
            pltpu.VMEM((t, 256), F32), pltpu.VMEM((t, 256), F32),
        ]
    else:
        out_specs.append(tok_spec(128))
        out_shape.append(jax.ShapeDtypeStruct((n_tok, 128), F32))
        scratch = [
            pltpu.VMEM((2, t, LANES), BF16), pltpu.VMEM((4, t, LANES), BF16),
            pltpu.VMEM((2, t, LANES), BF16), pltpu.VMEM((4, t, LANES), BF16),
            pltpu.VMEM((t, 256), F32), pltpu.VMEM((t, 256), F32),
        ]

    def body(*refs):
        if has_ctx:
            n_in = len(args) + 1
            ins = refs[:n_in - 2]
            rest = refs[n_in:]
            _attn_kernel(True, t, *ins, *rest)
        else:
            _attn_kernel(False, t, *refs)

    return pl.pallas_call(
        body,
        grid_spec=pltpu.PrefetchScalarGridSpec(
            num_scalar_prefetch=1, grid=(n_batch,), in_specs=in_specs, out_specs=out_specs,
            scratch_shapes=scratch),
        out_shape=out_shape,
        input_output_aliases=aliases,
        compiler_params=_params(("arbitrary",)),
        name="attn_latent" if has_ctx else "attn_prompt",
    )(sink, *args)


def _stack_heads(x):
    return jnp.concatenate([x[:, h * B_DIM:(h + 1) * B_DIM] for h in range(B_HEADS)], axis=0)


def _delta_kernel(t, has_s0, *refs):
    if has_s0:
        (zb_ref, abc_ref, abr_ref, conv_ref, prmc_ref, prmr_ref, bng_ref, cum_ref, mask_ref,
         s0f_ref, s0b_ref, o_ref, qkv_s, of_s, ob_s, sf_s, sb_s) = refs
    else:
        (zb_ref, abc_ref, abr_ref, conv_ref, prmc_ref, prmr_ref, bng_ref, cum_ref, mask_ref,
         o_ref, sfo_ref, sbo_ref, qkv_s, of_s, ob_s, sf_s, sb_s) = refs
    n_chunks = t // CHUNK
    qk_w = B_HEADS * B_DIM

    row = lax.broadcasted_iota(jnp.int32, (t, LANES), 0)
    for j in range(3 * B_HEADS):
        cols = slice(j * LANES, (j + 1) * LANES)
        x = zb_ref[:, cols]
        prev = jnp.where(row == 0, 0.0, pltpu.roll(x, 1, 0))
        nxt = jnp.where(row == t - 1, 0.0, pltpu.roll(x, t - 1, 0))
        y = _silu(prev * conv_ref[0:1, cols] + x * conv_ref[1:2, cols] + nxt * conv_ref[2:3, cols])
        if j < 2 * B_HEADS:
            y = y * lax.rsqrt(jnp.sum(y * y, axis=-1, keepdims=True) + EPS)
        if j < B_HEADS:
            y = y * (B_DIM ** -0.5)
        qkv_s[:, cols] = y

    if has_s0:
        sf_s[...] = s0f_ref[0]
        sb_s[...] = s0b_ref[0]
    else:
        sf_s[...] = jnp.zeros_like(sf_s)
        sb_s[...] = jnp.zeros_like(sb_s)

    a_col = jnp.exp(prmc_ref[0:1, :])
    dtb_col = prmc_ref[1:2, :]
    a_row = jnp.exp(prmr_ref[0])
    dtb_row = prmr_ref[1]
    seg_lane = lax.broadcasted_iota(jnp.int32, (SUBLANES, BD), 1) % CHUNK

    def chunk_dir(c, d, s_ref, o_s):
        r0 = pl.multiple_of(c * CHUNK, CHUNK)
        kst = _stack_heads(qkv_s[pl.ds(r0, CHUNK), qk_w:2 * qk_w])
        qst = _stack_heads(qkv_s[pl.ds(r0, CHUNK), 0:qk_w])
        vst = _stack_heads(qkv_s[pl.ds(r0, CHUNK), 2 * qk_w:3 * qk_w])

        ab = abc_ref[pl.ds(r0, CHUNK), :]
        g_all = -a_col * _softplus(ab + dtb_col)
        b_all = _sigmoid(ab)
        g_st = jnp.concatenate([g_all[:, 4 * d + h:4 * d + h + 1] for h in range(B_HEADS)], axis=0)
        b_st = jnp.concatenate([b_all[:, 8 + 4 * d + h:9 + 4 * d + h] for h in range(B_HEADS)], axis=0)
        g1 = g_st.astype(BF16).astype(F32)
        g2 = (g_st - g1).astype(BF16).astype(F32)
        g3 = g_st - g1 - g2
        lane = lax.broadcasted_iota(jnp.int32, (BD, LANES), 1)
        packed = jnp.where(lane == 0, g1, jnp.where(lane == 1, g2, jnp.where(lane == 2, g3, 0.0)))
        cs = _dot(cum_ref[d], packed.astype(BF16))
        gcol = jnp.sum(cs[:BD], axis=1, keepdims=True)
        gtot = jnp.sum(cs[BD:], axis=1, keepdims=True)

        xr = abr_ref[0, c]
        y = (-a_row * _softplus(xr + dtb_row))
        for s in (1, 2, 4, 8, 16, 32):
            if d == 0:
                y = y + jnp.where(seg_lane >= s, pltpu.roll(y, s, 1), 0.0)
            else:
                y = y + jnp.where(seg_lane < CHUNK - s, pltpu.roll(y, BD - s, 1), 0.0)
        grow = y[d:d + 1]

        decay = jnp.exp(jnp.minimum(gcol - grow, 0.0))
        kq = _dot_nt(jnp.concatenate([kst, qst], axis=0).astype(BF16), kst.astype(BF16))
        a_mat = (b_st * kq[:BD]) * (decay * mask_ref[2 * d + 1])
        attn = kq[BD:] * (decay * mask_ref[2 * d])
        egc = jnp.exp(gcol)
        rk = jnp.concatenate([b_st * vst, (b_st * egc) * kst], axis=1)

        t_inv = mask_ref[4] - a_mat * mask_ref[5]
        for lvl in range(N_LEVELS - 1):
            t16 = t_inv.astype(BF16)
            et = _dot((a_mat * mask_ref[6 + lvl]).astype(BF16), t16)
            t_inv = t_inv - _dot(t16, et.astype(BF16))
        rk = _dot(t_inv.astype(BF16), rk.astype(BF16))
        u = rk[:, :B_DIM]
        w = rk[:, B_DIM:]

        qp = qst * egc
        tops, bots = [], []
        for h in range(B_HEADS):
            sh = s_ref[h * B_DIM:(h + 1) * B_DIM, :].astype(BF16)
            rows = slice(h * CHUNK, (h + 1) * CHUNK)
            x = _dot(jnp.concatenate([w[rows], qp[rows]], axis=0).astype(BF16), sh)
            tops.append(x[:CHUNK])
            bots.append(x[CHUNK:])
        v_new = u - jnp.concatenate(tops, axis=0)
        v_new16 = v_new.astype(BF16)
        o = jnp.concatenate(bots, axis=0) + _dot(attn.astype(BF16), v_new16)

        kd = (kst * jnp.exp(gtot - gcol)).astype(BF16)
        eg = jnp.exp(gtot)
        for h in range(B_HEADS):
            rows = slice(h * CHUNK, (h + 1) * CHUNK)
            srows = slice(h * B_DIM, (h + 1) * B_DIM)
            upd = _dot_tn(kd[rows], v_new16[rows])
            s_ref[srows, :] = s_ref[srows, :] * eg[h * CHUNK:h * CHUNK + 1, :] + upd
            o_s[pl.ds(r0, CHUNK), h * B_DIM:(h + 1) * B_DIM] = o[rows]

    def step(c, carry):
        chunk_dir(c, 0, sf_s, of_s)
        chunk_dir(n_chunks - 1 - c, 1, sb_s, ob_s)
        return carry

    lax.fori_loop(0, n_chunks, step, 0)

    if not has_s0:
        sfo_ref[0] = sf_s[...]
        sbo_ref[0] = sb_s[...]

    for h in range(B_HEADS):
        cols = slice(h * B_DIM, (h + 1) * B_DIM)
        x = of_s[:, cols] + ob_s[:, cols]
        yn = x * lax.rsqrt(jnp.mean(x * x, axis=-1, keepdims=True) + EPS) * bng_ref[...]
        o_ref[:, cols] = yn * _silu(zb_ref[:, 3 * qk_w + h * B_DIM:3 * qk_w + (h + 1) * B_DIM])


def _delta_call(has_s0, t, n_batch, row_block0, zb, zab, abr, conv, prmc, prmr, bng, cum, masks,
                o_prev=None, s0=None):
    n_tok = zb.shape[0]
    n_chunks = t // CHUNK
    tok_spec = lambda w: pl.BlockSpec((t, w), lambda b: (row_block0 + b, 0))
    const = lambda shape: pl.BlockSpec(shape, lambda b: (0,) * len(shape))
    s_shape = (B_HEADS * B_DIM, B_DIM)
    in_specs = [
        tok_spec(ZB_W), tok_spec(ZAB_W),
        pl.BlockSpec((1, n_chunks, SUBLANES, BD), lambda b: (b, 0, 0, 0)),
        const((3, 3 * B_HEADS * B_DIM)), const((2, LANES)), const((2, SUBLANES, BD)), const((1, B_DIM)),
        const((2, 2 * BD, BD)), const((5 + N_LEVELS, BD, BD)),
    ]
    args = [zb, zab, abr, conv, prmc, prmr, bng, cum, masks]
    out_specs = [tok_spec(B_HEADS * B_DIM)]
    out_shape = [jax.ShapeDtypeStruct((n_tok, B_HEADS * B_DIM), F32)]
    aliases = {}
    if has_s0:
        s_spec = pl.BlockSpec((1,) + s_shape, lambda b: (b, 0, 0))
        in_specs += [s_spec, s_spec, pl.BlockSpec(memory_space=pl.ANY)]
        args += [s0[0], s0[1], o_prev]
        aliases = {len(args) - 1: 0}
    else:
        s_spec = pl.BlockSpec((1,) + s_shape, lambda b: (b, 0, 0))
        out_specs += [s_spec, s_spec]
        out_shape += [jax.ShapeDtypeStruct((n_batch,) + s_shape, F32)] * 2
    scratch = [
        pltpu.VMEM((t, 3 * B_HEADS * B_DIM), F32),
        pltpu.VMEM((t, B_HEADS * B_DIM), F32), pltpu.VMEM((t, B_HEADS * B_DIM), F32),
        pltpu.VMEM(s_shape, F32), pltpu.VMEM(s_shape, F32),
    ]

    def body(*refs):
        if has_s0:
            n_in = len(args)
            _delta_kernel(t, True, *refs[:n_in - 1], *refs[n_in:])
        else:
            _delta_kernel(t, False, *refs)

    return pl.pallas_call(
        body,
        grid=(n_batch,),
        in_specs=in_specs,
        out_specs=out_specs,
        out_shape=out_shape,
        scratch_shapes=scratch,
        input_output_aliases=aliases,
        compiler_params=_params(("arbitrary",)),
        name="delta_latent" if has_s0 else "delta_prompt",
    )(*args)


def _outproj_kernel(x_ref, ma_ref, mb_ref, mc_ref, mod_ref, g_ref, wo_ref, wr_ref, br_ref,
                    x1_ref, h2_ref, gate_ref):
    m = mod_ref[0]
    y = (_dot(ma_ref[...].astype(BF16), wo_ref[0:256, :])
         + _dot(mb_ref[...].astype(BF16), wo_ref[256:768, :])
         + _dot(mc_ref[...].astype(BF16), wo_ref[768:1024, :]))
    x1 = x_ref[...] + m[2:3] * y
    x1_ref[...] = x1
    h2 = _modulated_norm(x1, g_ref[...], m[3:4], m[4:5])
    hi, lo = _split2(h2)
    h2_ref[...] = hi

    logits = _dot(hi, wr_ref[0]) + _dot(lo, wr_ref[0]) + _dot(hi, wr_ref[1]) + br_ref[...]
    lane = lax.broadcasted_iota(jnp.int32, logits.shape, 1)
    big = jnp.int32(LANES)
    is_group = lane < N_GROUPS
    gl = jnp.where(is_group, logits, -jnp.inf)
    gmax = gl.max(axis=1, keepdims=True)
    g_sel = jnp.where(gl == gmax, lane, big).min(axis=1, keepdims=True)
    g_w = 1.0 / jnp.where(is_group, jnp.exp(gl - gmax), 0.0).sum(axis=1, keepdims=True)
    e_idx = lane - ROUTER_LANE0
    elig = (e_idx >= 0) & (e_idx < N_EXPERTS) & ((e_idx // EXPERTS_PER_GROUP) == g_sel)
    el = jnp.where(elig, logits, -jnp.inf)
    m1 = el.max(axis=1, keepdims=True)
    i1 = jnp.where(el == m1, lane, big).min(axis=1, keepdims=True)
    el2 = jnp.where(lane == i1, -jnp.inf, el)
    m2 = el2.max(axis=1, keepdims=True)
    i2 = jnp.where(el2 == m2, lane, big).min(axis=1, keepdims=True)
    tt = jnp.exp(m2 - m1)
    w1 = g_w / (1.0 + tt)
    w2 = w1 * tt
    gate_ref[...] = jnp.where(lane == i1, w1, 0.0) + jnp.where(lane == i2, w2, 0.0)


def _outproj_call(x, ma, mb, mc, mods, g, wo, wr, br, slot_fn, tm):
    n_tok = x.shape[0]
    tok = lambda w: pl.BlockSpec((tm, w), lambda i: (i, 0))
    const = lambda shape: pl.BlockSpec(shape, lambda i: (0,) * len(shape))
    return pl.pallas_call(
        _outproj_kernel,
        grid=(n_tok // tm,),
        in_specs=[tok(D_MODEL), tok(256), tok(512), tok(256),
                  pl.BlockSpec((1, 6, D_MODEL), lambda i: (slot_fn(i), 0, 0)),
                  const((1, D_MODEL)), const((D_MODEL, D_MODEL)), const((2, D_MODEL, LANES)), const((1, LANES))],
        out_specs=[tok(D_MODEL), tok(D_MODEL), tok(LANES)],
        out_shape=[jax.ShapeDtypeStruct((n_tok, D_MODEL), F32),
                   jax.ShapeDtypeStruct((n_tok, D_MODEL), BF16),
                   jax.ShapeDtypeStruct((n_tok, LANES), F32)],
        compiler_params=_params(("arbitrary",)),
        name="outproj",
    )(x, ma, mb, mc, mods, g, wo, wr, br)


def _moe_kernel(h_ref, gate_ref, x1_ref, mod_ref, w1_ref, w3_ref, w2_ref, o_ref, acc_ref):
    j = pl.program_id(1)
    th = w1_ref.shape[1]
    h = h_ref[...]
    hid = _silu(_dot(h, w1_ref[...])) * _dot(h, w3_ref[...])
    gate = gate_ref[...]
    lane = lax.broadcasted_iota(jnp.int32, gate.shape, 1)
    n_e = th // D_EXPERT
    col = lax.broadcasted_iota(jnp.int32, hid.shape, 1) // D_EXPERT
    gmat = jnp.zeros(hid.shape, F32)
    for e in range(n_e):
        ge = jnp.where(lane == ROUTER_LANE0 + j * n_e + e, gate, 0.0).sum(axis=1, keepdims=True)
        gmat = jnp.where(col == e, ge, gmat)
    contrib = _dot((hid * gmat).astype(BF16), w2_ref[...])

    @pl.when(j == 0)
    def _():
        acc_ref[...] = contrib

    @pl.when(j > 0)
    def _():
        acc_ref[...] += contrib

    @pl.when(j == pl.num_programs(1) - 1)
    def _():
        o_ref[...] = x1_ref[...] + mod_ref[0][5:6] * acc_ref[...]


def _moe_call(h2, gate, x1, mods, w1, w3, w2, slot_fn, tm, th):
    n_tok = h2.shape[0]
    ef = w1.shape[1]
    tok = lambda w: pl.BlockSpec((tm, w), lambda i, j: (i, 0))
    return pl.pallas_call(
        _moe_kernel,
        grid=(n_tok // tm, ef // th),
        in_specs=[tok(D_MODEL), tok(LANES), tok(D_MODEL),
                  pl.BlockSpec((1, 6, D_MODEL), lambda i, j: (slot_fn(i), 0, 0)),
                  pl.BlockSpec((D_MODEL, th), lambda i, j: (0, j)),
                  pl.BlockSpec((D_MODEL, th), lambda i, j: (0, j)),
                  pl.BlockSpec((th, D_MODEL), lambda i, j: (j, 0))],
        out_specs=tok(D_MODEL),
        out_shape=jax.ShapeDtypeStruct((n_tok, D_MODEL), F32),
        scratch_shapes=[pltpu.VMEM((tm, D_MODEL), F32)],
        compiler_params=_params(("arbitrary", "arbitrary")),
        name="moe",
    )(h2, gate, x1, mods, w1, w3, w2)


def _final_norm_kernel(x_ref, g_ref, o_ref):
    x = x_ref[...]
    o_ref[...] = x * lax.rsqrt(jnp.mean(x * x, axis=-1, keepdims=True) + EPS) * g_ref[...]


def _final_norm_call(x, g, tm):
    n_tok = x.shape[0]
    return pl.pallas_call(
        _final_norm_kernel,
        grid=(n_tok // tm,),
        in_specs=[pl.BlockSpec((tm, D_MODEL), lambda i: (i, 0)), pl.BlockSpec((1, D_MODEL), lambda i: (0, 0))],
        out_specs=pl.BlockSpec((tm, D_MODEL), lambda i: (i, 0)),
        out_shape=jax.ShapeDtypeStruct((n_tok, D_MODEL), F32),
        compiler_params=_params(("arbitrary",)),
        name="final_norm",
    )(x, g)


def _rope_tables(t):
    pos = np.arange(t)
    n_freq = HEAD_DIM // 4
    inv_freq = ROPE_THETA ** (-jnp.arange(n_freq, dtype=F32) / n_freq)
    row = jnp.asarray(pos // GRID_W, F32)
    col = jnp.asarray(pos % GRID_W, F32)
    ang = jnp.concatenate([row[:, None] * inv_freq, col[:, None] * inv_freq], -1)
    cos, sin = jnp.cos(ang), jnp.sin(ang)
    cos_t = jnp.tile(jnp.concatenate([cos, cos], -1), (1, LANES // HEAD_DIM))
    sin_t = jnp.tile(jnp.concatenate([-sin, sin], -1), (1, LANES // HEAD_DIM))
    return cos_t, sin_t


def _delta_tables():
    r = np.arange(BD)
    same = (r[:, None] // CHUNK) == (r[None, :] // CHUNK)
    low = same & (r[:, None] >= r[None, :])
    low_s = same & (r[:, None] > r[None, :])
    up = same & (r[:, None] <= r[None, :])
    up_s = same & (r[:, None] < r[None, :])
    levels = []
    for k in range(N_LEVELS):
        s = 1 << k
        levels.append(((r[:, None] // (2 * s)) == (r[None, :] // (2 * s))) & ((r[:, None] // s) != (r[None, :] // s)))
    masks = jnp.asarray(np.stack([low, low_s, up, up_s, np.eye(BD, dtype=bool)] + levels).astype(np.float32))
    cum = np.stack([np.concatenate([low, same], 0), np.concatenate([up, same], 0)]).astype(np.float32)
    return masks, jnp.asarray(cum, BF16)


def _segment_mean_table():
    r = np.arange(LANES)
    seg = ((r[:, None] // HEAD_DIM) == (r[None, :] // HEAD_DIM)).astype(np.float32) / HEAD_DIM
    hi = jnp.asarray(seg, BF16)
    lo = (jnp.asarray(seg) - hi.astype(F32)).astype(BF16)
    return jnp.stack([hi, lo])


def _row_layout_gates(zab, n_batch, t):
    ab = zab[:, :4 * B_HEADS].reshape(n_batch, t // CHUNK, CHUNK, 4, B_HEADS)
    ab = ab.transpose(0, 1, 3, 4, 2).reshape(n_batch, t // CHUNK, 4, BD)
    return jnp.pad(ab, ((0, 0), (0, 0), (0, SUBLANES - 4), (0, 0)))


def kernel(x_prompt, x_sample, cache_a_k, cache_a_v, cache_c_k, cache_c_v, state_b_fwd, state_b_bwd, c, c_ctx, w_mod, b_mod, norm1_g, norm2_g, w_in, a_sink, b_conv, b_a_log, b_dt_bias, b_norm_g, c_q_norm, c_k_norm, w_out, w_group, b_group, w_expert, b_expert, w1, w3, w2, final_norm_g):
    n_p, t_p, d = x_prompt.shape
    n_s, t_s, _ = x_sample.shape
    depth = w_in.shape[0]
    past = cache_a_k.shape[2]
    tok_p = n_p * t_p
    n_tok = tok_p + n_s * t_s
    assert d == D_MODEL and tok_p % t_s == 0 and t_s % 512 == 0 and t_p % 256 == 0

    w_in_p = jnp.concatenate(
        [w_in[:, :, :ZA_W + ZB_W], w_in[:, :, ZA_W + ZB_W + 16:], w_in[:, :, ZA_W + ZB_W:ZA_W + ZB_W + 16],
         jnp.zeros((depth, d, ZAB_W - 16), F32)], axis=-1).astype(BF16)
    w_out16 = w_out.astype(BF16)
    w116, w316, w216 = w1.astype(BF16), w3.astype(BF16), w2.astype(BF16)
    w_r = jnp.concatenate([w_group, w_expert, jnp.zeros((depth, d, LANES - N_GROUPS - N_EXPERTS), F32)], -1)
    w_r_hi = w_r.astype(BF16)
    w_r_lo = (w_r - w_r_hi.astype(F32)).astype(BF16)
    w_r2 = jnp.stack([w_r_hi, w_r_lo], axis=1)
    b_r = jnp.concatenate([b_group, b_expert, jnp.zeros((depth, LANES - N_GROUPS - N_EXPERTS), F32)], -1)
    cqn = jnp.tile(c_q_norm, (1, 4)).reshape(depth, 1, 256)
    ckn = jnp.tile(c_k_norm, (1, 2)).reshape(depth, 1, 128)
    prmc = jnp.pad(jnp.stack([b_a_log.reshape(depth, 8), b_dt_bias.reshape(depth, 8)], 1),
                   ((0, 0), (0, 0), (0, LANES - 8)))
    rep = lambda p: jnp.pad(jnp.repeat(p, CHUNK, axis=-1), ((0, 0), (0, SUBLANES - 2), (0, 0)))
    prmr = jnp.stack([rep(b_a_log), rep(b_dt_bias)], 1)
    cos_t, sin_t = _rope_tables(t_s)
    masks, cum = _delta_tables()
    seg = _segment_mean_table()

    cond = jnp.concatenate([c_ctx[None, :], c], axis=0)
    cond_b = jnp.broadcast_to(cond[:, :, None], cond.shape + (LANES,))
    mods_all = _mods_call(cond_b, w_mod, b_mod)
    mods_all = mods_all.reshape(depth, SUBLANES, 6, d)

    def slot_fn(tm):
        per_s = t_s // tm
        first = tok_p // tm
        return lambda i: jnp.where(i < first, 0, 1 + (i - first) // per_s)

    x = jnp.concatenate([x_prompt.reshape(tok_p, d), x_sample.reshape(n_s * t_s, d)], axis=0)
    blk_s = tok_p // t_s
    new_ak, new_av, new_ck, new_cv, new_sf, new_sb = [], [], [], [], [], []
    tm = 512
    for l in range(depth):
        mods = mods_all[l]
        za, zb, zc, zab = _inproj_call(x, mods, norm1_g[l][None], w_in_p[l], slot_fn(tm), tm)

        ao, co, ckn_out = _attn_call(False, t_p, n_p, 0, za, zc, a_sink[l], cqn[l], ckn[l], seg)
        ctx = (cache_a_k[:, l].reshape(n_s, past, LANES), cache_a_v[:, l].reshape(n_s, past, LANES),
               cache_c_k[:, l].reshape(n_s, past, LANES), cache_c_v[:, l].reshape(n_s, past, LANES))
        ao, co = _attn_call(True, t_s, n_s, blk_s, za, zc, a_sink[l], cqn[l], ckn[l], seg,
                            ao_prev=ao, co_prev=co, rope=(cos_t, sin_t), ctx=ctx)

        abr_p = _row_layout_gates(zab[:tok_p], n_p, t_p)
        abr_s = _row_layout_gates(zab[tok_p:], n_s, t_s)
        bo, sf, sb = _delta_call(False, t_p, n_p, 0, zb, zab, abr_p, b_conv[l], prmc[l], prmr[l],
                                 b_norm_g[l][None], cum, masks)
        s0 = (state_b_fwd[:, l].reshape(n_s, B_HEADS * B_DIM, B_DIM),
              state_b_bwd[:, l].reshape(n_s, B_HEADS * B_DIM, B_DIM))
        (bo,) = _delta_call(True, t_s, n_s, blk_s, zb, zab, abr_s, b_conv[l], prmc[l], prmr[l],
                            b_norm_g[l][None], cum, masks, o_prev=bo, s0=s0)

        x1, h2, gate = _outproj_call(x, ao, bo, co, mods, norm2_g[l][None], w_out16[l], w_r2[l], b_r[l][None],
                                     slot_fn(tm), tm)
        x = _moe_call(h2, gate, x1, mods, w116[l], w316[l], w216[l], slot_fn(1024), 1024, 512)

        new_ak.append(za[:tok_p, 256:384].reshape(n_p, t_p, 2, HEAD_DIM))
        new_av.append(za[:tok_p, 384:512].reshape(n_p, t_p, 2, HEAD_DIM))
        new_ck.append(ckn_out[:tok_p].reshape(n_p, t_p, 2, HEAD_DIM))
        new_cv.append(zc[:tok_p, 384:512].reshape(n_p, t_p, 2, HEAD_DIM))
        new_sf.append(sf.reshape(n_p, B_HEADS, B_DIM, B_DIM))
        new_sb.append(sb.reshape(n_p, B_HEADS, B_DIM, B_DIM))

    y = _final_norm_call(x, final_norm_g[None], tm)
    y_prompt = y[:tok_p].reshape(n_p, t_p, d)
    y_sample = y[tok_p:].reshape(n_s, t_s, d)
    st = lambda xs: jnp.stack(xs, axis=1)
    return (y_prompt, y_sample, st(new_ak), st(new_av), st(new_ck), st(new_cv), st(new_sf), st(new_sb))
```

```python
import functools

import jax
import jax.numpy as jnp
import numpy as np
from jax import lax
from jax.experimental import pallas as pl
from jax.experimental.pallas import tpu as pltpu

F32 = jnp.float32
BF16 = jnp.bfloat16

D_MODEL = 1024
DEPTH = 4
GRID_W = 64
EPS = 1e-6
NEG_INF = -1e30
ROPE_THETA = 10000.0
HEAD_DIM = 64
N_Q_HEADS = 4
WINDOW = 128
Q_BLOCK = 128
B_HEADS = 4
B_DIM = 128
CHUNK = 64
BD = B_HEADS * CHUNK
PAIR = 2 * CHUNK
N_LEVELS = 6
PREP_UNROLL = 4
N_GROUPS = 4
EXPERTS_PER_GROUP = 4
N_EXPERTS = 16
D_EXPERT = 256
ROUTER_LANE0 = N_GROUPS

LANES = 128
SUBLANES = 8
VMEM_LIMIT = 56 * 1024 * 1024

ZA_W, ZB_W, ZC_W, ZAB_W = 512, 2048, 512, 128
Z_W = ZA_W + ZB_W + ZC_W + ZAB_W


def _sigmoid(x):
    return 1.0 / (1.0 + jnp.exp(-x))


def _silu(x):
    return x * _sigmoid(x)


def _softplus(x):
    return jnp.maximum(x, 0.0) + jnp.log1p(jnp.exp(-jnp.abs(x)))


def _dot(a, b):
    return jnp.dot(a, b, preferred_element_type=F32)


def _dot_nt(a, b):
    return lax.dot_general(a, b, (((1,), (1,)), ((), ())), preferred_element_type=F32)


def _dot_tn(a, b):
    return lax.dot_general(a, b, (((0,), (0,)), ((), ())), preferred_element_type=F32)


def _split2(x):
    hi = x.astype(BF16)
    lo = (x - hi.astype(F32)).astype(BF16)
    return hi, lo


def _params(sem=None):
    return pltpu.CompilerParams(dimension_semantics=sem, vmem_limit_bytes=VMEM_LIMIT)


def _mods_kernel(cond_ref, w_ref, b_ref, o_ref):
    n_cond = cond_ref.shape[0]
    tn = w_ref.shape[2]
    reps = tn // LANES

    def body(kb, accs):
        r = pl.multiple_of(kb * SUBLANES, SUBLANES)
        w = w_ref[0, pl.ds(r, SUBLANES), :]
        out = []
        for m in range(n_cond):
            cm = cond_ref[m, pl.ds(r, SUBLANES), :]
            a = jnp.tile(_silu(cm), (1, reps))
            out.append(accs[m] + a * w)
        return tuple(out)

    zero = jnp.zeros((SUBLANES, tn), F32)
    accs = lax.fori_loop(0, w_ref.shape[1] // SUBLANES, body, (zero,) * n_cond)
    rows = [jnp.sum(a, axis=0, keepdims=True) + b_ref[0] for a in accs]
    rows.append(jnp.zeros((SUBLANES - n_cond, tn), F32))
    o_ref[0] = jnp.concatenate(rows, axis=0)


def _mods_call(cond_b, w_mod, b_mod):
    depth, d, n = w_mod.shape
    tn = 1536
    n_cond = cond_b.shape[0]
    return pl.pallas_call(
        _mods_kernel,
        grid=(depth, n // tn),
        in_specs=[
            pl.BlockSpec((n_cond, d, LANES), lambda l, j: (0, 0, 0)),
            pl.BlockSpec((1, d, tn), lambda l, j: (l, 0, j)),
            pl.BlockSpec((1, 1, tn), lambda l, j: (l, 0, j)),
        ],
        out_specs=pl.BlockSpec((1, SUBLANES, tn), lambda l, j: (l, 0, j)),
        out_shape=jax.ShapeDtypeStruct((depth, SUBLANES, n), F32),
        compiler_params=_params(("arbitrary", "arbitrary")),
        name="mods",
    )(cond_b, w_mod, b_mod.reshape(depth, 1, n))


def _modulated_norm(x, g, shift, scale):
    ms = jnp.mean(x * x, axis=-1, keepdims=True)
    y = x * lax.rsqrt(ms + EPS) * g
    return y * (1.0 + scale) + shift


def _inproj_kernel(x_ref, mod_ref, g_ref, w_ref, za_ref, zb_ref, zc_ref, zab_ref):
    m = mod_ref[0]
    h = _modulated_norm(x_ref[...], g_ref[...], m[0:1], m[1:2]).astype(BF16)
    za_ref[...] = _dot(h, w_ref[:, 0:ZA_W])
    step = 512
    for j in range(ZB_W // step):
        zb_ref[:, j * step:(j + 1) * step] = _dot(h, w_ref[:, ZA_W + j * step:ZA_W + (j + 1) * step])
    zc_ref[...] = _dot(h, w_ref[:, ZA_W + ZB_W:ZA_W + ZB_W + ZC_W])
    zab_ref[...] = _dot(h, w_ref[:, ZA_W + ZB_W + ZC_W:Z_W])


def _inproj_call(x, mods, g, w, slot_fn, tm):
    n_tok = x.shape[0]
    return pl.pallas_call(
        _inproj_kernel,
        grid=(n_tok // tm,),
        in_specs=[
            pl.BlockSpec((tm, D_MODEL), lambda i: (i, 0)),
            pl.BlockSpec((1, 6, D_MODEL), lambda i: (slot_fn(i), 0, 0)),
            pl.BlockSpec((1, D_MODEL), lambda i: (0, 0)),
            pl.BlockSpec((D_MODEL, Z_W), lambda i: (0, 0)),
        ],
        out_specs=[
            pl.BlockSpec((tm, ZA_W), lambda i: (i, 0)),
            pl.BlockSpec((tm, ZB_W), lambda i: (i, 0)),
            pl.BlockSpec((tm, ZC_W), lambda i: (i, 0)),
            pl.BlockSpec((tm, ZAB_W), lambda i: (i, 0)),
        ],
        out_shape=[
            jax.ShapeDtypeStruct((n_tok, ZA_W), F32),
            jax.ShapeDtypeStruct((n_tok, ZB_W), F32),
            jax.ShapeDtypeStruct((n_tok, ZC_W), F32),
            jax.ShapeDtypeStruct((n_tok, ZAB_W), F32),
        ],
        compiler_params=_params(("arbitrary",)),
        name="inproj",
    )(x, mods, g, w)


def _lane_lo(shape):
    return lax.broadcasted_iota(jnp.int32, shape, len(shape) - 1) % LANES < HEAD_DIM


def _store_kdup(dst_ref, off, k):
    n = k.shape[0]
    r = pltpu.roll(k, HEAD_DIM, 1)
    lo = _lane_lo(k.shape)
    dst_ref[0, off:off + n, :] = jnp.where(lo, k, r).astype(BF16)
    dst_ref[1, off:off + n, :] = jnp.where(lo, r, k).astype(BF16)


def _store_vsplit(dst_ref, off, v):
    n = v.shape[0]
    r = pltpu.roll(v, HEAD_DIM, 1)
    lo = _lane_lo(v.shape)
    z = jnp.zeros_like(v)
    dst_ref[0, off:off + n, :] = jnp.where(lo, v, z).astype(BF16)
    dst_ref[1, off:off + n, :] = jnp.where(lo, z, r).astype(BF16)
    dst_ref[2, off:off + n, :] = jnp.where(lo, r, z).astype(BF16)
    dst_ref[3, off:off + n, :] = jnp.where(lo, z, v).astype(BF16)


def _rope(x, cos, sin):
    first = (lax.broadcasted_iota(jnp.int32, x.shape, 1) // (HEAD_DIM // 2)) % 2 == 0
    partner = jnp.where(first, pltpu.roll(x, LANES - HEAD_DIM // 2, 1), pltpu.roll(x, HEAD_DIM // 2, 1))
    return x * cos + partner * sin


def _head_rmsnorm(x, g, seg_hi, seg_lo):
    hi, lo = _split2(x * x)
    ms = _dot(hi, seg_hi) + _dot(lo, seg_hi) + _dot(hi, seg_lo)
    return x * lax.rsqrt(ms + EPS) * g


def _attend(qt, segs, sink_pair):
    qb = qt.shape[0]
    lo = _lane_lo(qt.shape)
    z = jnp.zeros_like(qt)
    qs = jnp.concatenate([jnp.where(lo, qt, z), jnp.where(lo, z, qt)], axis=0).astype(BF16)
    scores = []
    for kdup, _, _, mask in segs:
        s = _dot_nt(qs, kdup)
        if mask is not None:
            s = jnp.where(mask, s, NEG_INF)
        scores.append(s)
    m = scores[0].max(axis=1, keepdims=True)
    for s in scores[1:]:
        m = jnp.maximum(m, s.max(axis=1, keepdims=True))
    if sink_pair is not None:
        row_a = lax.broadcasted_iota(jnp.int32, (2 * qb, 1), 0) < qb
        sink = jnp.where(row_a, sink_pair[0], sink_pair[1])
        m = jnp.maximum(m, sink)
        denom = jnp.exp(sink - m)
    else:
        denom = jnp.zeros((2 * qb, 1), F32)
    acc = jnp.zeros((qb, LANES), F32)
    for s, (_, vlo, vhi, _) in zip(scores, segs):
        p = jnp.exp(s - m)
        denom = denom + p.sum(axis=1, keepdims=True)
        pb = p.astype(BF16)
        acc = acc + _dot(pb[:qb], vlo) + _dot(pb[qb:], vhi)
    inv = 1.0 / denom
    return acc * jnp.where(lo, inv[:qb], inv[qb:])


def _attn_kernel(has_ctx, t, *refs):
    if has_ctx:
        (sink_ref, za_ref, zc_ref, cqn_ref, ckn_ref, seg_ref, cos_ref, sin_ref,
         cak_ref, cav_ref, cck_ref, ccv_ref,
         ao_ref, co_ref,
         ka_s, va_s, kc_s, vc_s, kctx_s, vctx_s, qa_s, qc_s) = refs
    else:
        (sink_ref, za_ref, zc_ref, cqn_ref, ckn_ref, seg_ref,
         ao_ref, co_ref, ckn_out_ref,
         ka_s, va_s, kc_s, vc_s, qa_s, qc_s) = refs
    scale = HEAD_DIM ** -0.5
    seg_hi = seg_ref[0]
    seg_lo = seg_ref[1]
    piece = 256
    n_ctx = cak_ref.shape[1] if has_ctx else 0

    for p0 in range(0, t, piece):
        rows = slice(p0, p0 + piece)
        ak = za_ref[rows, 256:384]
        av = za_ref[rows, 384:512]
        ck = _head_rmsnorm(zc_ref[rows, 256:384], ckn_ref[...], seg_hi, seg_lo)
        cv = zc_ref[rows, 384:512]
        if has_ctx:
            cos = cos_ref[rows, :]
            sin = sin_ref[rows, :]
            ak = _rope(ak, cos, sin)
            ck = _rope(ck, cos, sin)
            _store_kdup(ka_s, WINDOW + p0, ak)
            _store_vsplit(va_s, WINDOW + p0, av)
            _store_kdup(kc_s, n_ctx + p0, ck)
            _store_vsplit(vc_s, n_ctx + p0, cv)
        else:
            ckn_out_ref[rows, :] = ck
            _store_kdup(ka_s, p0, ak)
            _store_vsplit(va_s, p0, av)
            _store_kdup(kc_s, p0, ck)
            _store_vsplit(vc_s, p0, cv)
        for hk in range(2):
            cols = slice(hk * LANES, (hk + 1) * LANES)
            aq = za_ref[rows, cols]
            cq = _head_rmsnorm(zc_ref[rows, cols], cqn_ref[:, cols], seg_hi, seg_lo)
            if has_ctx:
                aq = _rope(aq, cos, sin)
                cq = _rope(cq, cos, sin)
            qa_s[rows, cols] = aq * scale
            qc_s[rows, cols] = cq * scale

    if has_ctx:
        zpad = jnp.zeros((WINDOW, LANES), BF16)
        for i in range(2):
            ka_s[i, 0:WINDOW, :] = zpad
            ka_s[i, WINDOW + t:2 * WINDOW + t, :] = zpad
        for i in range(4):
            va_s[i, 0:WINDOW, :] = zpad
            va_s[i, WINDOW + t:2 * WINDOW + t, :] = zpad
        for p0 in range(0, n_ctx, piece):
            rows = slice(p0, p0 + piece)
            _store_kdup(kctx_s, p0, cak_ref[0, rows, :])
            _store_vsplit(vctx_s, p0, cav_ref[0, rows, :])
            _store_kdup(kc_s, p0, cck_ref[0, rows, :])
            _store_vsplit(vc_s, p0, ccv_ref[0, rows, :])

        qb = Q_BLOCK
        span = 3 * qb
        qi = lax.broadcasted_iota(jnp.int32, (2 * qb, span), 0) % qb
        kj = lax.broadcasted_iota(jnp.int32, (2 * qb, span), 1)
        band = jnp.abs(kj - qb - qi) <= WINDOW

        def block(b, carry):
            r0 = pl.multiple_of(b * qb, qb)
            kpos = kj + (r0 - qb)
            mask = band & (kpos >= 0) & (kpos < t)
            for hk in range(2):
                cols = slice(hk * LANES, (hk + 1) * LANES)
                segs_a = [
                    (kctx_s[hk], vctx_s[2 * hk], vctx_s[2 * hk + 1], None),
                    (ka_s[hk, pl.ds(r0, span), :], va_s[2 * hk, pl.ds(r0, span), :],
                     va_s[2 * hk + 1, pl.ds(r0, span), :], mask),
                ]
                sinks = (sink_ref[2 * hk], sink_ref[2 * hk + 1])
                ao_ref[pl.ds(r0, qb), cols] = _attend(qa_s[pl.ds(r0, qb), cols], segs_a, sinks)
                segs_c = [(kc_s[hk], vc_s[2 * hk], vc_s[2 * hk + 1], None)]
                co_ref[pl.ds(r0, qb), cols] = _attend(qc_s[pl.ds(r0, qb), cols], segs_c, None)
            return carry

        lax.fori_loop(0, t // qb, block, 0)
    else:
        for hk in range(2):
            cols = slice(hk * LANES, (hk + 1) * LANES)
            sinks = (sink_ref[2 * hk], sink_ref[2 * hk + 1])
            segs_a = [(ka_s[hk], va_s[2 * hk], va_s[2 * hk + 1], None)]
            ao_ref[:, cols] = _attend(qa_s[:, cols], segs_a, sinks)
            segs_c = [(kc_s[hk], vc_s[2 * hk], vc_s[2 * hk + 1], None)]
            co_ref[:, cols] = _attend(qc_s[:, cols], segs_c, None)


def _attn_call(has_ctx, t, n_batch, row_block0, za, zc, sink, cqn, ckn, seg, ao_prev=None, co_prev=None,
               rope=None, ctx=None):
    n_tok = za.shape[0]
    tok_spec = lambda w: pl.BlockSpec((t, w), lambda b, *_: (row_block0 + b, 0))
    const2 = lambda shape: pl.BlockSpec(shape, lambda b, *_: (0,) * len(shape))
    in_specs = [tok_spec(ZA_W), tok_spec(ZC_W), const2((1, 256)), const2((1, 128)), const2((2, LANES, LANES))]
    args = [za, zc, cqn, ckn, seg]
    out_specs = [tok_spec(256), tok_spec(256)]
    out_shape = [jax.ShapeDtypeStruct((n_tok, 256), F32), jax.ShapeDtypeStruct((n_tok, 256), F32)]
    aliases = {}
    if has_ctx:
        n_ctx = ctx[0].shape[1]
        cos, sin = rope
        in_specs += [const2((t, LANES)), const2((t, LANES))]
        args += [cos, sin]
        ctx_spec = pl.BlockSpec((1, n_ctx, LANES), lambda b, *_: (b, 0, 0))
        in_specs += [ctx_spec] * 4
        args += list(ctx)
        in_specs += [pl.BlockSpec(memory_space=pl.ANY)] * 2
        args += [ao_prev, co_prev]
        aliases = {1 + len(args) - 2: 0, 1 + len(args) - 1: 1}
        scratch = [
            pltpu.VMEM((2, t + 2 * WINDOW, LANES), BF16), pltpu.VMEM((4, t + 2 * WINDOW, LANES), BF16),
            pltpu.VMEM((2, n_ctx + t, LANES), BF16), pltpu.VMEM((4, n_ctx + t, LANES), BF16),
            pltpu.VMEM((2, n_ctx, LANES), BF16), pltpu.VMEM((4, n_ctx, LANES), BF16),
            pltpu.VMEM((t, 256), F32), pltpu.VMEM((t, 256), F32),
        ]
    else:
        out_specs.append(tok_spec(128))
        out_shape.append(jax.ShapeDtypeStruct((n_tok, 128), F32))
        scratch = [
            pltpu.VMEM((2, t, LANES), BF16), pltpu.VMEM((4, t, LANES), BF16),
            pltpu.VMEM((2, t, LANES), BF16), pltpu.VMEM((4, t, LANES), BF16),
            pltpu.VMEM((t, 256), F32), pltpu.VMEM((t, 256), F32),
        ]

    def body(*refs):
        if has_ctx:
            n_in = len(args) + 1
            ins = refs[:n_in - 2]
            rest = refs[n_in:]
            _attn_kernel(True, t, *ins, *rest)
        else:
            _attn_kernel(False, t, *refs)

    return pl.pallas_call(
        body,
        grid_spec=pltpu.PrefetchScalarGridSpec(
            num_scalar_prefetch=1, grid=(n_batch,), in_specs=in_specs, out_specs=out_specs,
            scratch_shapes=scratch),
        out_shape=out_shape,
        input_output_aliases=aliases,
        compiler_params=_params(("arbitrary",)),
        name="attn_latent" if has_ctx else "attn_prompt",
    )(sink, *args)


def _stack_pair(x, p):
    return jnp.concatenate([x[:, (2 * p + hl) * B_DIM:(2 * p + hl + 1) * B_DIM] for hl in range(2)], axis=0)


def _delta_kernel(t, has_s0, *refs):
    if has_s0:
        (zb_ref, abc_ref, abr_ref, conv_ref, prmc_ref, prmr_ref, bng_ref, cum_ref, mask_ref,
         s0f_ref, s0b_ref, o_ref, qkv_s, of_s, ob_s, sf_s, sb_s, u_s, wq_s, at_s, kd_s, eg_s) = refs
    else:
        (zb_ref, abc_ref, abr_ref, conv_ref, prmc_ref, prmr_ref, bng_ref, cum_ref, mask_ref,
         o_ref, sfo_ref, sbo_ref, qkv_s, of_s, ob_s, sf_s, sb_s, u_s, wq_s, at_s, kd_s, eg_s) = refs
    n_chunks = t // CHUNK
    qk_w = B_HEADS * B_DIM

    row = lax.broadcasted_iota(jnp.int32, (t, LANES), 0)
    for j in range(3 * B_HEADS):
        cols = slice(j * LANES, (j + 1) * LANES)
        x = zb_ref[:, cols]
        prev = jnp.where(row == 0, 0.0, pltpu.roll(x, 1, 0))
        nxt = jnp.where(row == t - 1, 0.0, pltpu.roll(x, t - 1, 0))
        y = _silu(prev * conv_ref[0:1, cols] + x * conv_ref[1:2, cols] + nxt * conv_ref[2:3, cols])
        if j < 2 * B_HEADS:
            y = y * lax.rsqrt(jnp.sum(y * y, axis=-1, keepdims=True) + EPS)
        if j < B_HEADS:
            y = y * (B_DIM ** -0.5)
        qkv_s[:, cols] = y

    if has_s0:
        sf_s[...] = s0f_ref[0]
        sb_s[...] = s0b_ref[0]
    else:
        sf_s[...] = jnp.zeros_like(sf_s)
        sb_s[...] = jnp.zeros_like(sb_s)

    a_col = jnp.exp(prmc_ref[0:1, :])
    dtb_col = prmc_ref[1:2, :]
    a_row = jnp.exp(prmr_ref[0])
    dtb_row = prmr_ref[1]
    seg_lane = lax.broadcasted_iota(jnp.int32, (SUBLANES, BD), 1) % CHUNK

    lane = lax.broadcasted_iota(jnp.int32, (PAIR, LANES), 1)

    def prepare(cc, carry):
        chains = []
        for k in range(PREP_UNROLL):
            c = cc * PREP_UNROLL + k
            r0 = pl.multiple_of(c * CHUNK, CHUNK)
            ab = abc_ref[pl.ds(r0, CHUNK), :]
            g_all = -a_col * _softplus(ab + dtb_col)
            b_all = _sigmoid(ab)
            gr = -a_row * _softplus(abr_ref[0, c] + dtb_row)
            pre, suf = gr, gr
            for s in (1, 2, 4, 8, 16, 32):
                pre = pre + jnp.where(seg_lane >= s, pltpu.roll(pre, s, 1), 0.0)
                suf = suf + jnp.where(seg_lane < CHUNK - s, pltpu.roll(suf, BD - s, 1), 0.0)
            for p in range(B_HEADS // 2):
                kst = _stack_pair(qkv_s[pl.ds(r0, CHUNK), qk_w:2 * qk_w], p)
                qst = _stack_pair(qkv_s[pl.ds(r0, CHUNK), 0:qk_w], p)
                vst = _stack_pair(qkv_s[pl.ds(r0, CHUNK), 2 * qk_w:3 * qk_w], p)
                kq = _dot_nt(jnp.concatenate([kst, qst], axis=0).astype(BF16), kst.astype(BF16))
                for d in range(2):
                    cg = 4 * d + 2 * p
                    g_st = jnp.concatenate([g_all[:, cg + hl:cg + hl + 1] for hl in range(2)], axis=0)
                    b_st = jnp.concatenate([b_all[:, 8 + cg + hl:9 + cg + hl] for hl in range(2)], axis=0)
                    g1 = g_st.astype(BF16).astype(F32)
                    g2 = (g_st - g1).astype(BF16).astype(F32)
                    g3 = g_st - g1 - g2
                    packed = jnp.where(lane == 0, g1, jnp.where(lane == 1, g2, jnp.where(lane == 2, g3, 0.0)))
                    grow = (pre if d == 0 else suf)[d:d + 1, p * PAIR:(p + 1) * PAIR]
                    chains.append(dict(c=c, p=p, d=d, kst=kst, qst=qst, vst=vst, kq=kq, b_st=b_st,
                                       packed=packed.astype(BF16), grow=grow))

        for ch in chains:
            ch["cs"] = _dot(cum_ref[ch["d"]], ch.pop("packed"))
        for ch in chains:
            d, cs, b_st, kq = ch["d"], ch.pop("cs"), ch["b_st"], ch.pop("kq")
            gcol = jnp.sum(cs[:PAIR], axis=1, keepdims=True)
            gtot = jnp.sum(cs[PAIR:], axis=1, keepdims=True)
            decay = jnp.exp(jnp.minimum(gcol - ch.pop("grow"), 0.0))
            ch["a_mat"] = (b_st * kq[:PAIR]) * (decay * mask_ref[2 * d + 1])
            ch["attn"] = (kq[PAIR:] * (decay * mask_ref[2 * d])).astype(BF16)
            ch["gcol"], ch["gtot"] = gcol, gtot
            ch["t_inv"] = mask_ref[4] - ch["a_mat"] * mask_ref[5]
        for lvl in range(N_LEVELS - 1):
            for ch in chains:
                ch["t16"] = ch["t_inv"].astype(BF16)
                ch["et"] = _dot((ch["a_mat"] * mask_ref[6 + lvl]).astype(BF16), ch["t16"])
            for ch in chains:
                ch["t_inv"] = ch["t_inv"] - _dot(ch.pop("t16"), ch.pop("et").astype(BF16))
        for ch in chains:
            egc = jnp.exp(ch["gcol"])
            rk = jnp.concatenate([ch["b_st"] * ch["vst"], (ch["b_st"] * egc) * ch["kst"]], axis=1)
            ch["rk"] = _dot(ch.pop("t_inv").astype(BF16), rk.astype(BF16))
            ch["qp16"] = (ch["qst"] * egc).astype(BF16)
        for ch in chains:
            c, p, d, rk, qp16 = ch["c"], ch["p"], ch["d"], ch["rk"], ch["qp16"]
            pair_rows = slice(p * PAIR, (p + 1) * PAIR)
            w16 = rk[:, B_DIM:].astype(BF16)
            u_s[d, c, pair_rows, :] = rk[:, :B_DIM]
            at_s[d, c, p] = ch["attn"]
            kd_s[d, c, pair_rows, :] = (ch["kst"] * jnp.exp(ch["gtot"] - ch["gcol"])).astype(BF16)
            eg = jnp.exp(ch["gtot"])
            for hl in range(2):
                h = 2 * p + hl
                rows = slice(hl * CHUNK, (hl + 1) * CHUNK)
                wq_s[d, c, h * 2 * CHUNK:h * 2 * CHUNK + CHUNK, :] = w16[rows]
                wq_s[d, c, h * 2 * CHUNK + CHUNK:(h + 1) * 2 * CHUNK, :] = qp16[rows]
                eg_s[d, c, h * SUBLANES:(h + 1) * SUBLANES, :] = jnp.broadcast_to(
                    eg[hl * CHUNK:hl * CHUNK + SUBLANES, :], (SUBLANES, LANES))
        return carry

    lax.fori_loop(0, n_chunks // PREP_UNROLL, prepare, 0)

    def scan_step(i, carry):
        for d, s_ref, o_s in ((0, sf_s, of_s), (1, sb_s, ob_s)):
            c = i if d == 0 else n_chunks - 1 - i
            r0 = pl.multiple_of(c * CHUNK, CHUNK)
            v_new, bots = [], []
            for h in range(B_HEADS):
                sh = s_ref[h * B_DIM:(h + 1) * B_DIM, :].astype(BF16)
                x = _dot(wq_s[d, c, h * 2 * CHUNK:(h + 1) * 2 * CHUNK, :], sh)
                v_new.append(u_s[d, c, h * CHUNK:(h + 1) * CHUNK, :] - x[:CHUNK])
                bots.append(x[CHUNK:])
            for p in range(B_HEADS // 2):
                vp16 = jnp.concatenate(v_new[2 * p:2 * p + 2], axis=0).astype(BF16)
                o = jnp.concatenate(bots[2 * p:2 * p + 2], axis=0) + _dot(at_s[d, c, p], vp16)
                for hl in range(2):
                    h = 2 * p + hl
                    rows = slice(hl * CHUNK, (hl + 1) * CHUNK)
                    srows = slice(h * B_DIM, (h + 1) * B_DIM)
                    upd = _dot_tn(kd_s[d, c, h * CHUNK:(h + 1) * CHUNK, :], vp16[rows])
                    eg = jnp.tile(eg_s[d, c, h * SUBLANES:(h + 1) * SUBLANES, :], (B_DIM // SUBLANES, 1))
                    s_ref[srows, :] = s_ref[srows, :] * eg + upd
                    o_s[pl.ds(r0, CHUNK), h * B_DIM:(h + 1) * B_DIM] = o[rows]
        return carry

    lax.fori_loop(0, n_chunks, scan_step, 0)

    if not has_s0:
        sfo_ref[0] = sf_s[...]
        sbo_ref[0] = sb_s[...]

    for h in range(B_HEADS):
        cols = slice(h * B_DIM, (h + 1) * B_DIM)
        x = of_s[:, cols] + ob_s[:, cols]
        yn = x * lax.rsqrt(jnp.mean(x * x, axis=-1, keepdims=True) + EPS) * bng_ref[...]
        o_ref[:, cols] = yn * _silu(zb_ref[:, 3 * qk_w + h * B_DIM:3 * qk_w + (h + 1) * B_DIM])


def _delta_call(has_s0, t, n_batch, row_block0, zb, zab, abr, conv, prmc, prmr, bng, cum, masks,
                o_prev=None, s0=None):
    n_tok = zb.shape[0]
    n_chunks = t // CHUNK
    tok_spec = lambda w: pl.BlockSpec((t, w), lambda b: (row_block0 + b, 0))
    const = lambda shape: pl.BlockSpec(shape, lambda b: (0,) * len(shape))
    s_shape = (B_HEADS * B_DIM, B_DIM)
    in_specs = [
        tok_spec(ZB_W), tok_spec(ZAB_W),
        pl.BlockSpec((1, n_chunks, SUBLANES, BD), lambda b: (b, 0, 0, 0)),
        const((3, 3 * B_HEADS * B_DIM)), const((2, LANES)), const((2, SUBLANES, BD)), const((1, B_DIM)),
        const((2, 2 * PAIR, PAIR)), const((5 + N_LEVELS, PAIR, PAIR)),
    ]
    args = [zb, zab, abr, conv, prmc, prmr, bng, cum, masks]
    out_specs = [tok_spec(B_HEADS * B_DIM)]
    out_shape = [jax.ShapeDtypeStruct((n_tok, B_HEADS * B_DIM), F32)]
    aliases = {}
    if has_s0:
        s_spec = pl.BlockSpec((1,) + s_shape, lambda b: (b, 0, 0))
        in_specs += [s_spec, s_spec, pl.BlockSpec(memory_space=pl.ANY)]
        args += [s0[0], s0[1], o_prev]
        aliases = {len(args) - 1: 0}
    else:
        s_spec = pl.BlockSpec((1,) + s_shape, lambda b: (b, 0, 0))
        out_specs += [s_spec, s_spec]
        out_shape += [jax.ShapeDtypeStruct((n_batch,) + s_shape, F32)] * 2
    scratch = [
        pltpu.VMEM((t, 3 * B_HEADS * B_DIM), F32),
        pltpu.VMEM((t, B_HEADS * B_DIM), F32), pltpu.VMEM((t, B_HEADS * B_DIM), F32),
        pltpu.VMEM(s_shape, F32), pltpu.VMEM(s_shape, F32),
        pltpu.VMEM((2, n_chunks, BD, B_DIM), F32),
        pltpu.VMEM((2, n_chunks, 2 * BD, B_DIM), BF16),
        pltpu.VMEM((2, n_chunks, B_HEADS // 2, PAIR, PAIR), BF16),
        pltpu.VMEM((2, n_chunks, BD, B_DIM), BF16),
        pltpu.VMEM((2, n_chunks, B_HEADS * SUBLANES, LANES), F32),
    ]

    def body(*refs):
        if has_s0:
            n_in = len(args)
            _delta_kernel(t, True, *refs[:n_in - 1], *refs[n_in:])
        else:
            _delta_kernel(t, False, *refs)

    return pl.pallas_call(
        body,
        grid=(n_batch,),
        in_specs=in_specs,
        out_specs=out_specs,
        out_shape=out_shape,
        scratch_shapes=scratch,
        input_output_aliases=aliases,
        compiler_params=_params(("arbitrary",)),
        name="delta_latent" if has_s0 else "delta_prompt",
    )(*args)


def _outproj_kernel(x_ref, ma_ref, mb_ref, mc_ref, mod_ref, g_ref, wo_ref, wr_ref, br_ref,
                    x1_ref, h2_ref, gate_ref):
    m = mod_ref[0]
    y = (_dot(ma_ref[...].astype(BF16), wo_ref[0:256, :])
         + _dot(mb_ref[...].astype(BF16), wo_ref[256:768, :])
         + _dot(mc_ref[...].astype(BF16), wo_ref[768:1024, :]))
    x1 = x_ref[...] + m[2:3] * y
    x1_ref[...] = x1
    h2 = _modulated_norm(x1, g_ref[...], m[3:4], m[4:5])
    hi, lo = _split2(h2)
    h2_ref[...] = hi

    logits = _dot(hi, wr_ref[0]) + _dot(lo, wr_ref[0]) + _dot(hi, wr_ref[1]) + br_ref[...]
    lane = lax.broadcasted_iota(jnp.int32, logits.shape, 1)
    big = jnp.int32(LANES)
    is_group = lane < N_GROUPS
    gl = jnp.where(is_group, logits, -jnp.inf)
    gmax = gl.max(axis=1, keepdims=True)
    g_sel = jnp.where(gl == gmax, lane, big).min(axis=1, keepdims=True)
    g_w = 1.0 / jnp.where(is_group, jnp.exp(gl - gmax), 0.0).sum(axis=1, keepdims=True)
    e_idx = lane - ROUTER_LANE0
    elig = (e_idx >= 0) & (e_idx < N_EXPERTS) & ((e_idx // EXPERTS_PER_GROUP) == g_sel)
    el = jnp.where(elig, logits, -jnp.inf)
    m1 = el.max(axis=1, keepdims=True)
    i1 = jnp.where(el == m1, lane, big).min(axis=1, keepdims=True)
    el2 = jnp.where(lane == i1, -jnp.inf, el)
    m2 = el2.max(axis=1, keepdims=True)
    i2 = jnp.where(el2 == m2, lane, big).min(axis=1, keepdims=True)
    tt = jnp.exp(m2 - m1)
    w1 = g_w / (1.0 + tt)
    w2 = w1 * tt
    gate_ref[...] = jnp.where(lane == i1, w1, 0.0) + jnp.where(lane == i2, w2, 0.0)


def _outproj_call(x, ma, mb, mc, mods, g, wo, wr, br, slot_fn, tm):
    n_tok = x.shape[0]
    tok = lambda w: pl.BlockSpec((tm, w), lambda i: (i, 0))
    const = lambda shape: pl.BlockSpec(shape, lambda i: (0,) * len(shape))
    return pl.pallas_call(
        _outproj_kernel,
        grid=(n_tok // tm,),
        in_specs=[tok(D_MODEL), tok(256), tok(512), tok(256),
                  pl.BlockSpec((1, 6, D_MODEL), lambda i: (slot_fn(i), 0, 0)),
                  const((1, D_MODEL)), const((D_MODEL, D_MODEL)), const((2, D_MODEL, LANES)), const((1, LANES))],
        out_specs=[tok(D_MODEL), tok(D_MODEL), tok(LANES)],
        out_shape=[jax.ShapeDtypeStruct((n_tok, D_MODEL), F32),
                   jax.ShapeDtypeStruct((n_tok, D_MODEL), BF16),
                   jax.ShapeDtypeStruct((n_tok, LANES), F32)],
        compiler_params=_params(("arbitrary",)),
        name="outproj",
    )(x, ma, mb, mc, mods, g, wo, wr, br)


def _moe_kernel(h_ref, gate_ref, x1_ref, mod_ref, w1_ref, w3_ref, w2_ref, o_ref, acc_ref):
    j = pl.program_id(1)
    th = w1_ref.shape[1]
    h = h_ref[...]
    hid = _silu(_dot(h, w1_ref[...])) * _dot(h, w3_ref[...])
    gate = gate_ref[...]
    lane = lax.broadcasted_iota(jnp.int32, gate.shape, 1)
    n_e = th // D_EXPERT
    col = lax.broadcasted_iota(jnp.int32, hid.shape, 1) // D_EXPERT
    gmat = jnp.zeros(hid.shape, F32)
    for e in range(n_e):
        ge = jnp.where(lane == ROUTER_LANE0 + j * n_e + e, gate, 0.0).sum(axis=1, keepdims=True)
        gmat = jnp.where(col == e, ge, gmat)
    contrib = _dot((hid * gmat).astype(BF16), w2_ref[...])

    @pl.when(j == 0)
    def _():
        acc_ref[...] = contrib

    @pl.when(j > 0)
    def _():
        acc_ref[...] += contrib

    @pl.when(j == pl.num_programs(1) - 1)
    def _():
        o_ref[...] = x1_ref[...] + mod_ref[0][5:6] * acc_ref[...]


def _moe_call(h2, gate, x1, mods, w1, w3, w2, slot_fn, tm, th):
    n_tok = h2.shape[0]
    ef = w1.shape[1]
    tok = lambda w: pl.BlockSpec((tm, w), lambda i, j: (i, 0))
    return pl.pallas_call(
        _moe_kernel,
        grid=(n_tok // tm, ef // th),
        in_specs=[tok(D_MODEL), tok(LANES), tok(D_MODEL),
                  pl.BlockSpec((1, 6, D_MODEL), lambda i, j: (slot_fn(i), 0, 0)),
                  pl.BlockSpec((D_MODEL, th), lambda i, j: (0, j)),
                  pl.BlockSpec((D_MODEL, th), lambda i, j: (0, j)),
                  pl.BlockSpec((th, D_MODEL), lambda i, j: (j, 0))],
        out_specs=tok(D_MODEL),
        out_shape=jax.ShapeDtypeStruct((n_tok, D_MODEL), F32),
        scratch_shapes=[pltpu.VMEM((tm, D_MODEL), F32)],
        compiler_params=_params(("arbitrary", "arbitrary")),
        name="moe",
    )(h2, gate, x1, mods, w1, w3, w2)


def _final_norm_kernel(x_ref, g_ref, o_ref):
    x = x_ref[...]
    o_ref[...] = x * lax.rsqrt(jnp.mean(x * x, axis=-1, keepdims=True) + EPS) * g_ref[...]


def _final_norm_call(x, g, tm):
    n_tok = x.shape[0]
    return pl.pallas_call(
        _final_norm_kernel,
        grid=(n_tok // tm,),
        in_specs=[pl.BlockSpec((tm, D_MODEL), lambda i: (i, 0)), pl.BlockSpec((1, D_MODEL), lambda i: (0, 0))],
        out_specs=pl.BlockSpec((tm, D_MODEL), lambda i: (i, 0)),
        out_shape=jax.ShapeDtypeStruct((n_tok, D_MODEL), F32),
        compiler_params=_params(("arbitrary",)),
        name="final_norm",
    )(x, g)


def _rope_tables(t):
    pos = np.arange(t)
    n_freq = HEAD_DIM // 4
    inv_freq = ROPE_THETA ** (-jnp.arange(n_freq, dtype=F32) / n_freq)
    row = jnp.asarray(pos // GRID_W, F32)
    col = jnp.asarray(pos % GRID_W, F32)
    ang = jnp.concatenate([row[:, None] * inv_freq, col[:, None] * inv_freq], -1)
    cos, sin = jnp.cos(ang), jnp.sin(ang)
    cos_t = jnp.tile(jnp.concatenate([cos, cos], -1), (1, LANES // HEAD_DIM))
    sin_t = jnp.tile(jnp.concatenate([-sin, sin], -1), (1, LANES // HEAD_DIM))
    return cos_t, sin_t


def _delta_tables():
    r = np.arange(PAIR)
    same = (r[:, None] // CHUNK) == (r[None, :] // CHUNK)
    low = same & (r[:, None] >= r[None, :])
    low_s = same & (r[:, None] > r[None, :])
    up = same & (r[:, None] <= r[None, :])
    up_s = same & (r[:, None] < r[None, :])
    levels = []
    for k in range(N_LEVELS):
        s = 1 << k
        levels.append(((r[:, None] // (2 * s)) == (r[None, :] // (2 * s))) & ((r[:, None] // s) != (r[None, :] // s)))
    masks = jnp.asarray(np.stack([low, low_s, up, up_s, np.eye(PAIR, dtype=bool)] + levels).astype(np.float32))
    cum = np.stack([np.concatenate([low, same], 0), np.concatenate([up, same], 0)]).astype(np.float32)
    return masks, jnp.asarray(cum, BF16)


def _segment_mean_table():
    r = np.arange(LANES)
    seg = ((r[:, None] // HEAD_DIM) == (r[None, :] // HEAD_DIM)).astype(np.float32) / HEAD_DIM
    hi = jnp.asarray(seg, BF16)
    lo = (jnp.asarray(seg) - hi.astype(F32)).astype(BF16)
    return jnp.stack([hi, lo])


def _row_layout_gates(zab, n_batch, t):
    ab = zab[:, :4 * B_HEADS].reshape(n_batch, t // CHUNK, CHUNK, 4, B_HEADS)
    ab = ab.transpose(0, 1, 3, 4, 2).reshape(n_batch, t // CHUNK, 4, BD)
    return jnp.pad(ab, ((0, 0), (0, 0), (0, SUBLANES - 4), (0, 0)))


def kernel(x_prompt, x_sample, cache_a_k, cache_a_v, cache_c_k, cache_c_v, state_b_fwd, state_b_bwd, c, c_ctx, w_mod, b_mod, norm1_g, norm2_g, w_in, a_sink, b_conv, b_a_log, b_dt_bias, b_norm_g, c_q_norm, c_k_norm, w_out, w_group, b_group, w_expert, b_expert, w1, w3, w2, final_norm_g):
    n_p, t_p, d = x_prompt.shape
    n_s, t_s, _ = x_sample.shape
    depth = w_in.shape[0]
    past = cache_a_k.shape[2]
    tok_p = n_p * t_p
    n_tok = tok_p + n_s * t_s
    assert d == D_MODEL and tok_p % t_s == 0 and t_s % 512 == 0 and t_p % 256 == 0

    w_in_p = jnp.concatenate(
        [w_in[:, :, :ZA_W + ZB_W], w_in[:, :, ZA_W + ZB_W + 16:], w_in[:, :, ZA_W + ZB_W:ZA_W + ZB_W + 16],
         jnp.zeros((depth, d, ZAB_W - 16), F32)], axis=-1).astype(BF16)
    w_out16 = w_out.astype(BF16)
    w116, w316, w216 = w1.astype(BF16), w3.astype(BF16), w2.astype(BF16)
    w_r = jnp.concatenate([w_group, w_expert, jnp.zeros((depth, d, LANES - N_GROUPS - N_EXPERTS), F32)], -1)
    w_r_hi = w_r.astype(BF16)
    w_r_lo = (w_r - w_r_hi.astype(F32)).astype(BF16)
    w_r2 = jnp.stack([w_r_hi, w_r_lo], axis=1)
    b_r = jnp.concatenate([b_group, b_expert, jnp.zeros((depth, LANES - N_GROUPS - N_EXPERTS), F32)], -1)
    cqn = jnp.tile(c_q_norm, (1, 4)).reshape(depth, 1, 256)
    ckn = jnp.tile(c_k_norm, (1, 2)).reshape(depth, 1, 128)
    prmc = jnp.pad(jnp.stack([b_a_log.reshape(depth, 8), b_dt_bias.reshape(depth, 8)], 1),
                   ((0, 0), (0, 0), (0, LANES - 8)))
    rep = lambda p: jnp.pad(jnp.repeat(p, CHUNK, axis=-1), ((0, 0), (0, SUBLANES - 2), (0, 0)))
    prmr = jnp.stack([rep(b_a_log), rep(b_dt_bias)], 1)
    cos_t, sin_t = _rope_tables(t_s)
    masks, cum = _delta_tables()
    seg = _segment_mean_table()

    cond = jnp.concatenate([c_ctx[None, :], c], axis=0)
    cond_b = jnp.broadcast_to(cond[:, :, None], cond.shape + (LANES,))
    mods_all = _mods_call(cond_b, w_mod, b_mod)
    mods_all = mods_all.reshape(depth, SUBLANES, 6, d)

    def slot_fn(tm):
        per_s = t_s // tm
        first = tok_p // tm
        return lambda i: jnp.where(i < first, 0, 1 + (i - first) // per_s)

    x = jnp.concatenate([x_prompt.reshape(tok_p, d), x_sample.reshape(n_s * t_s, d)], axis=0)
    blk_s = tok_p // t_s
    new_ak, new_av, new_ck, new_cv, new_sf, new_sb = [], [], [], [], [], []
    tm = 512
    for l in range(depth):
        mods = mods_all[l]
        za, zb, zc, zab = _inproj_call(x, mods, norm1_g[l][None], w_in_p[l], slot_fn(tm), tm)

        ao, co, ckn_out = _attn_call(False, t_p, n_p, 0, za, zc, a_sink[l], cqn[l], ckn[l], seg)
        ctx = (cache_a_k[:, l].reshape(n_s, past, LANES), cache_a_v[:, l].reshape(n_s, past, LANES),
               cache_c_k[:, l].reshape(n_s, past, LANES), cache_c_v[:, l].reshape(n_s, past, LANES))
        ao, co = _attn_call(True, t_s, n_s, blk_s, za, zc, a_sink[l], cqn[l], ckn[l], seg,
                            ao_prev=ao, co_prev=co, rope=(cos_t, sin_t), ctx=ctx)

        abr_p = _row_layout_gates(zab[:tok_p], n_p, t_p)
        abr_s = _row_layout_gates(zab[tok_p:], n_s, t_s)
        bo, sf, sb = _delta_call(False, t_p, n_p, 0, zb, zab, abr_p, b_conv[l], prmc[l], prmr[l],
                                 b_norm_g[l][None], cum, masks)
        s0 = (state_b_fwd[:, l].reshape(n_s, B_HEADS * B_DIM, B_DIM),
              state_b_bwd[:, l].reshape(n_s, B_HEADS * B_DIM, B_DIM))
        (bo,) = _delta_call(True, t_s, n_s, blk_s, zb, zab, abr_s, b_conv[l], prmc[l], prmr[l],
                            b_norm_g[l][None], cum, masks, o_prev=bo, s0=s0)

        x1, h2, gate = _outproj_call(x, ao, bo, co, mods, norm2_g[l][None], w_out16[l], w_r2[l], b_r[l][None],
                                     slot_fn(tm), tm)
        x = _moe_call(h2, gate, x1, mods, w116[l], w316[l], w216[l], slot_fn(1024), 1024, 512)

        new_ak.append(za[:tok_p, 256:384].reshape(n_p, t_p, 2, HEAD_DIM))
        new_av.append(za[:tok_p, 384:512].reshape(n_p, t_p, 2, HEAD_DIM))
        new_ck.append(ckn_out[:tok_p].reshape(n_p, t_p, 2, HEAD_DIM))
        new_cv.append(zc[:tok_p, 384:512].reshape(n_p, t_p, 2, HEAD_DIM))
        new_sf.append(sf.reshape(n_p, B_HEADS, B_DIM, B_DIM))
        new_sb.append(sb.reshape(n_p, B_HEADS, B_DIM, B_DIM))

    y = _final_norm_call(x, final_norm_g[None], tm)
    y_prompt = y[:tok_p].reshape(n_p, t_p, d)
    y_sample = y[tok_p:].reshape(n_s, t_s, d)
    st = lambda xs: jnp.stack(xs, axis=1)
    return (y_prompt, y_sample, st(new_ak), st(new_av), st(new_ck), st(new_cv), st(new_sf), st(new_sb))
```

```python
import functools

import jax
import jax.numpy as jnp
import numpy as np
from jax import lax
from jax.experimental import pallas as pl
from jax.experimental.pallas import tpu as pltpu

F32 = jnp.float32
BF16 = jnp.bfloat16

D_MODEL = 1024
DEPTH = 4
GRID_W = 64
EPS = 1e-6
NEG_INF = -1e30
ROPE_THETA = 10000.0
HEAD_DIM = 64
N_Q_HEADS = 4
WINDOW = 128
Q_BLOCK = 128
B_HEADS = 4
B_DIM = 128
CHUNK = 64
BD = B_HEADS * CHUNK
PAIR = 2 * CHUNK
N_LEVELS = 6
PREP_UNROLL = 4
N_GROUPS = 4
EXPERTS_PER_GROUP = 4
N_EXPERTS = 16
D_EXPERT = 256
ROUTER_LANE0 = N_GROUPS

LANES = 128
SUBLANES = 8
VMEM_LIMIT = 56 * 1024 * 1024

ZA_W, ZB_W, ZC_W, ZAB_W = 512, 2048, 512, 128
N_AB = 16
Z_W = ZA_W + ZB_W + ZC_W + ZAB_W


def _sigmoid(x):
    return 1.0 / (1.0 + jnp.exp(-x))


def _silu(x):
    return x * _sigmoid(x)


def _softplus(x):
    return jnp.maximum(x, 0.0) + jnp.log1p(jnp.exp(-jnp.abs(x)))


def _dot(a, b):
    return jnp.dot(a, b, preferred_element_type=F32)


def _dot_nt(a, b):
    return lax.dot_general(a, b, (((1,), (1,)), ((), ())), preferred_element_type=F32)


def _dot_tn(a, b):
    return lax.dot_general(a, b, (((0,), (0,)), ((), ())), preferred_element_type=F32)


def _split2(x):
    hi = x.astype(BF16)
    lo = (x - hi.astype(F32)).astype(BF16)
    return hi, lo


def _params(sem=None):
    return pltpu.CompilerParams(dimension_semantics=sem, vmem_limit_bytes=VMEM_LIMIT)


def _mods_kernel(cond_ref, w_ref, b_ref, o_ref, act_s):
    n_cond = cond_ref.shape[0]
    tn = w_ref.shape[2]
    reps = tn // LANES

    @pl.when((pl.program_id(0) == 0) & (pl.program_id(1) == 0))
    def _():
        act_s[...] = _silu(cond_ref[...])

    def body(kb, accs):
        r = pl.multiple_of(kb * SUBLANES, SUBLANES)
        w = w_ref[0, pl.ds(r, SUBLANES), :]
        return tuple(acc + jnp.tile(act_s[m, pl.ds(r, SUBLANES), :], (1, reps)) * w for m, acc in enumerate(accs))

    zero = jnp.zeros((SUBLANES, tn), F32)
    accs = lax.fori_loop(0, w_ref.shape[1] // SUBLANES, body, (zero,) * n_cond, unroll=4)
    rows = [jnp.sum(a, axis=0, keepdims=True) + b_ref[0] for a in accs]
    rows.append(jnp.zeros((SUBLANES - n_cond, tn), F32))
    o_ref[0] = jnp.concatenate(rows, axis=0)


def _mods_call(cond_b, w_mod, b_mod):
    depth, d, n = w_mod.shape
    tn = 1536
    n_cond = cond_b.shape[0]
    return pl.pallas_call(
        _mods_kernel,
        grid=(depth, n // tn),
        in_specs=[
            pl.BlockSpec((n_cond, d, LANES), lambda l, j: (0, 0, 0)),
            pl.BlockSpec((1, d, tn), lambda l, j: (l, 0, j)),
            pl.BlockSpec((1, 1, tn), lambda l, j: (l, 0, j)),
        ],
        out_specs=pl.BlockSpec((1, SUBLANES, tn), lambda l, j: (l, 0, j)),
        out_shape=jax.ShapeDtypeStruct((depth, SUBLANES, n), F32),
        scratch_shapes=[pltpu.VMEM((n_cond, d, LANES), F32)],
        compiler_params=_params(("arbitrary", "arbitrary")),
        name="mods",
    )(cond_b, w_mod, b_mod.reshape(depth, 1, n))


def _modulated_norm(x, g, shift, scale):
    ms = jnp.mean(x * x, axis=-1, keepdims=True)
    y = x * lax.rsqrt(ms + EPS) * g
    return y * (1.0 + scale) + shift


def _inproj_kernel(x_ref, mod_ref, g_ref, w_ref, wabt_ref, za_ref, zb_ref, zc_ref, zab_ref, zabt_ref):
    m = mod_ref[0, 0]
    h = _modulated_norm(x_ref[...], g_ref[0], m[0:1], m[1:2]).astype(BF16)
    za_ref[...] = _dot(h, w_ref[0, :, 0:ZA_W])
    step = 512
    for j in range(ZB_W // step):
        zb_ref[:, j * step:(j + 1) * step] = _dot(h, w_ref[0, :, ZA_W + j * step:ZA_W + (j + 1) * step])
    zc_ref[...] = _dot(h, w_ref[0, :, ZA_W + ZB_W:ZA_W + ZB_W + ZC_W])
    zab_ref[...] = _dot(h, w_ref[0, :, ZA_W + ZB_W + ZC_W:Z_W])
    zabt_ref[...] = _dot_nt(wabt_ref[0], h)


def _inproj_call(layer, x, mods, g, w, wabt, slot_fn, tm):
    n_tok = x.shape[0]
    n_ab = wabt.shape[1]
    return pl.pallas_call(
        _inproj_kernel,
        grid=(n_tok // tm,),
        in_specs=[
            pl.BlockSpec((tm, D_MODEL), lambda i: (i, 0)),
            pl.BlockSpec((1, 1, 6, D_MODEL), lambda i: (layer, slot_fn(i), 0, 0)),
            pl.BlockSpec((1, 1, D_MODEL), lambda i: (layer, 0, 0)),
            pl.BlockSpec((1, D_MODEL, Z_W), lambda i: (layer, 0, 0)),
            pl.BlockSpec((1, n_ab, D_MODEL), lambda i: (layer, 0, 0)),
        ],
        out_specs=[
            pl.BlockSpec((tm, ZA_W), lambda i: (i, 0)),
            pl.BlockSpec((tm, ZB_W), lambda i: (i, 0)),
            pl.BlockSpec((tm, ZC_W), lambda i: (i, 0)),
            pl.BlockSpec((tm, ZAB_W), lambda i: (i, 0)),
            pl.BlockSpec((n_ab, tm), lambda i: (0, i)),
        ],
        out_shape=[
            jax.ShapeDtypeStruct((n_tok, ZA_W), F32),
            jax.ShapeDtypeStruct((n_tok, ZB_W), F32),
            jax.ShapeDtypeStruct((n_tok, ZC_W), F32),
            jax.ShapeDtypeStruct((n_tok, ZAB_W), F32),
            jax.ShapeDtypeStruct((n_ab, n_tok), F32),
        ],
        compiler_params=_params(("arbitrary",)),
        name="inproj",
    )(x, mods, g, w, wabt)


def _lane_lo(shape):
    return lax.broadcasted_iota(jnp.int32, shape, len(shape) - 1) % LANES < HEAD_DIM


def _store_kdup(dst_ref, off, k):
    n = k.shape[0]
    r = pltpu.roll(k, HEAD_DIM, 1)
    lo = _lane_lo(k.shape)
    dst_ref[0, off:off + n, :] = jnp.where(lo, k, r).astype(BF16)
    dst_ref[1, off:off + n, :] = jnp.where(lo, r, k).astype(BF16)


def _store_vsplit(dst_ref, off, v):
    n = v.shape[0]
    r = pltpu.roll(v, HEAD_DIM, 1)
    lo = _lane_lo(v.shape)
    z = jnp.zeros_like(v)
    dst_ref[0, off:off + n, :] = jnp.where(lo, v, z).astype(BF16)
    dst_ref[1, off:off + n, :] = jnp.where(lo, z, r).astype(BF16)
    dst_ref[2, off:off + n, :] = jnp.where(lo, r, z).astype(BF16)
    dst_ref[3, off:off + n, :] = jnp.where(lo, z, v).astype(BF16)


def _rope(x, cos, sin):
    first = (lax.broadcasted_iota(jnp.int32, x.shape, 1) // (HEAD_DIM // 2)) % 2 == 0
    partner = jnp.where(first, pltpu.roll(x, LANES - HEAD_DIM // 2, 1), pltpu.roll(x, HEAD_DIM // 2, 1))
    return x * cos + partner * sin


def _head_rmsnorm(x, g, seg_hi, seg_lo):
    hi, lo = _split2(x * x)
    ms = _dot(hi, seg_hi) + _dot(lo, seg_hi) + _dot(hi, seg_lo)
    return x * lax.rsqrt(ms + EPS) * g


def _attend(qt, segs, sink_pair):
    qb = qt.shape[0]
    lo = _lane_lo(qt.shape)
    z = jnp.zeros_like(qt)
    qs = jnp.concatenate([jnp.where(lo, qt, z), jnp.where(lo, z, qt)], axis=0).astype(BF16)
    scores = []
    for kdup, _, _, mask in segs:
        s = _dot_nt(qs, kdup)
        if mask is not None:
            s = jnp.where(mask, s, NEG_INF)
        scores.append(s)
    m = scores[0].max(axis=1, keepdims=True)
    for s in scores[1:]:
        m = jnp.maximum(m, s.max(axis=1, keepdims=True))
    if sink_pair is not None:
        row_a = lax.broadcasted_iota(jnp.int32, (2 * qb, 1), 0) < qb
        sink = jnp.where(row_a, sink_pair[0], sink_pair[1])
        m = jnp.maximum(m, sink)
        denom = jnp.exp(sink - m)
    else:
        denom = jnp.zeros((2 * qb, 1), F32)
    acc = jnp.zeros((qb, LANES), F32)
    for s, (_, vlo, vhi, _) in zip(scores, segs):
        p = jnp.exp(s - m)
        denom = denom + p.sum(axis=1, keepdims=True)
        pb = p.astype(BF16)
        acc = acc + _dot(pb[:qb], vlo) + _dot(pb[qb:], vhi)
    inv = 1.0 / denom
    return acc * jnp.where(lo, inv[:qb], inv[qb:])


def _attn_kernel(has_ctx, t, layer, *refs):
    if has_ctx:
        (sink_ref, za_ref, zc_ref, cqn_ref, ckn_ref, seg_ref, cos_ref, sin_ref,
         cak_ref, cav_ref, cck_ref, ccv_ref,
         ao_ref, co_ref,
         ka_s, va_s, kc_s, vc_s, kctx_s, vctx_s, qa_s, qc_s) = refs
    else:
        (sink_ref, za_ref, zc_ref, cqn_ref, ckn_ref, seg_ref,
         ao_ref, co_ref, nak_ref, nav_ref, nck_ref, ncv_ref,
         ka_s, va_s, kc_s, vc_s, qa_s, qc_s) = refs
    scale = HEAD_DIM ** -0.5
    seg_hi = seg_ref[0]
    seg_lo = seg_ref[1]
    piece = 256
    n_ctx = cak_ref.shape[2] if has_ctx else 0

    for p0 in range(0, t, piece):
        rows = slice(p0, p0 + piece)
        ak = za_ref[rows, 256:384]
        av = za_ref[rows, 384:512]
        ck = _head_rmsnorm(zc_ref[rows, 256:384], ckn_ref[0], seg_hi, seg_lo)
        cv = zc_ref[rows, 384:512]
        if has_ctx:
            cos = cos_ref[rows, :]
            sin = sin_ref[rows, :]
            ak = _rope(ak, cos, sin)
            ck = _rope(ck, cos, sin)
            _store_kdup(ka_s, WINDOW + p0, ak)
            _store_vsplit(va_s, WINDOW + p0, av)
            _store_kdup(kc_s, n_ctx + p0, ck)
            _store_vsplit(vc_s, n_ctx + p0, cv)
        else:
            nak_ref[0, 0, rows, :] = ak
            nav_ref[0, 0, rows, :] = av
            nck_ref[0, 0, rows, :] = ck
            ncv_ref[0, 0, rows, :] = cv
            _store_kdup(ka_s, p0, ak)
            _store_vsplit(va_s, p0, av)
            _store_kdup(kc_s, p0, ck)
            _store_vsplit(vc_s, p0, cv)
        for hk in range(2):
            cols = slice(hk * LANES, (hk + 1) * LANES)
            aq = za_ref[rows, cols]
            cq = _head_rmsnorm(zc_ref[rows, cols], cqn_ref[0, :, cols], seg_hi, seg_lo)
            if has_ctx:
                aq = _rope(aq, cos, sin)
                cq = _rope(cq, cos, sin)
            qa_s[rows, cols] = aq * scale
            qc_s[rows, cols] = cq * scale

    if has_ctx:
        zpad = jnp.zeros((WINDOW, LANES), BF16)
        for i in range(2):
            ka_s[i, 0:WINDOW, :] = zpad
            ka_s[i, WINDOW + t:2 * WINDOW + t, :] = zpad
        for i in range(4):
            va_s[i, 0:WINDOW, :] = zpad
            va_s[i, WINDOW + t:2 * WINDOW + t, :] = zpad
        for p0 in range(0, n_ctx, piece):
            rows = slice(p0, p0 + piece)
            _store_kdup(kctx_s, p0, cak_ref[0, 0, rows, :])
            _store_vsplit(vctx_s, p0, cav_ref[0, 0, rows, :])
            _store_kdup(kc_s, p0, cck_ref[0, 0, rows, :])
            _store_vsplit(vc_s, p0, ccv_ref[0, 0, rows, :])

        qb = Q_BLOCK
        span = 3 * qb
        qi = lax.broadcasted_iota(jnp.int32, (2 * qb, span), 0) % qb
        kj = lax.broadcasted_iota(jnp.int32, (2 * qb, span), 1)
        band = jnp.abs(kj - qb - qi) <= WINDOW

        def block(b, carry):
            r0 = pl.multiple_of(b * qb, qb)
            kpos = kj + (r0 - qb)
            mask = band & (kpos >= 0) & (kpos < t)
            for hk in range(2):
                cols = slice(hk * LANES, (hk + 1) * LANES)
                segs_a = [
                    (kctx_s[hk], vctx_s[2 * hk], vctx_s[2 * hk + 1], None),
                    (ka_s[hk, pl.ds(r0, span), :], va_s[2 * hk, pl.ds(r0, span), :],
                     va_s[2 * hk + 1, pl.ds(r0, span), :], mask),
                ]
                sinks = (sink_ref[layer, 2 * hk], sink_ref[layer, 2 * hk + 1])
                ao_ref[pl.ds(r0, qb), cols] = _attend(qa_s[pl.ds(r0, qb), cols], segs_a, sinks)
                segs_c = [(kc_s[hk], vc_s[2 * hk], vc_s[2 * hk + 1], None)]
                co_ref[pl.ds(r0, qb), cols] = _attend(qc_s[pl.ds(r0, qb), cols], segs_c, None)
            return carry

        lax.fori_loop(0, t // qb, block, 0)
    else:
        for hk in range(2):
            cols = slice(hk * LANES, (hk + 1) * LANES)
            sinks = (sink_ref[layer, 2 * hk], sink_ref[layer, 2 * hk + 1])
            segs_a = [(ka_s[hk], va_s[2 * hk], va_s[2 * hk + 1], None)]
            ao_ref[:, cols] = _attend(qa_s[:, cols], segs_a, sinks)
            segs_c = [(kc_s[hk], vc_s[2 * hk], vc_s[2 * hk + 1], None)]
            co_ref[:, cols] = _attend(qc_s[:, cols], segs_c, None)


def _attn_call(has_ctx, t, n_batch, row_block0, layer, za, zc, sink, cqn, ckn, seg, prev=None, rope=None, ctx=None):
    n_tok = za.shape[0]
    depth = sink.shape[0]
    tok_spec = lambda w: pl.BlockSpec((t, w), lambda b, *_: (row_block0 + b, 0))
    const = lambda shape: pl.BlockSpec(shape, lambda b, *_: (0,) * len(shape))
    layer_spec = lambda shape: pl.BlockSpec((1,) + shape, lambda b, *_: (layer,) + (0,) * len(shape))
    in_specs = [tok_spec(ZA_W), tok_spec(ZC_W), layer_spec((1, 256)), layer_spec((1, 128)), const((2, LANES, LANES))]
    args = [za, zc, cqn, ckn, seg]
    out_specs = [tok_spec(256), tok_spec(256)]
    out_shape = [jax.ShapeDtypeStruct((n_tok, 256), F32), jax.ShapeDtypeStruct((n_tok, 256), F32)]
    if has_ctx:
        n_ctx = ctx[0].shape[2]
        in_specs += [const((t, LANES)), const((t, LANES))]
        args += list(rope)
        in_specs += [pl.BlockSpec((1, 1, n_ctx, LANES), lambda b, *_: (b, layer, 0, 0))] * 4
        args += list(ctx)
        scratch = [
            pltpu.VMEM((2, t + 2 * WINDOW, LANES), BF16), pltpu.VMEM((4, t + 2 * WINDOW, LANES), BF16),
            pltpu.VMEM((2, n_ctx + t, LANES), BF16), pltpu.VMEM((4, n_ctx + t, LANES), BF16),
            pltpu.VMEM((2, n_ctx, LANES), BF16), pltpu.VMEM((4, n_ctx, LANES), BF16),
            pltpu.VMEM((t, 256), F32), pltpu.VMEM((t, 256), F32),
        ]
    else:
        cache_spec = pl.BlockSpec((1, 1, t, LANES), lambda b, *_: (b, layer, 0, 0))
        out_specs += [cache_spec] * 4
        out_shape += [jax.ShapeDtypeStruct((n_batch, depth, t, LANES), F32)] * 4
        scratch = [
            pltpu.VMEM((2, t, LANES), BF16), pltpu.VMEM((4, t, LANES), BF16),
            pltpu.VMEM((2, t, LANES), BF16), pltpu.VMEM((4, t, LANES), BF16),
            pltpu.VMEM((t, 256), F32), pltpu.VMEM((t, 256), F32),
        ]
    n_real = len(args)
    aliases = {}
    if prev is not None:
        first_out = 0 if has_ctx else 2
        for k, arr in enumerate(prev):
            in_specs.append(pl.BlockSpec(memory_space=pl.ANY))
            args.append(arr)
            aliases[1 + n_real + k] = first_out + k

    def body(*refs):
        ins = refs[:1 + n_real]
        rest = refs[1 + len(args):]
        _attn_kernel(has_ctx, t, layer, *ins, *rest)

    return pl.pallas_call(
        body,
        grid_spec=pltpu.PrefetchScalarGridSpec(
            num_scalar_prefetch=1, grid=(n_batch,), in_specs=in_specs, out_specs=out_specs,
            scratch_shapes=scratch),
        out_shape=out_shape,
        input_output_aliases=aliases,
        compiler_params=_params(("arbitrary",)),
        name="attn_latent" if has_ctx else "attn_prompt",
    )(sink, *args)


def _stack_pair(x, p):
    return jnp.concatenate([x[:, (2 * p + hl) * B_DIM:(2 * p + hl + 1) * B_DIM] for hl in range(2)], axis=0)


def _delta_kernel(t, has_s0, *refs):
    if has_s0:
        (zb_ref, abc_ref, abt_ref, conv_ref, prmc_ref, prmr_ref, bng_ref, cum_ref, mask_ref,
         s0f_ref, s0b_ref, o_ref, qkv_s, of_s, ob_s, sf_s, sb_s, u_s, wq_s, at_s, kd_s, eg_s, pre_s, suf_s) = refs
    else:
        (zb_ref, abc_ref, abt_ref, conv_ref, prmc_ref, prmr_ref, bng_ref, cum_ref, mask_ref,
         o_ref, sfo_ref, sbo_ref, qkv_s, of_s, ob_s, sf_s, sb_s, u_s, wq_s, at_s, kd_s, eg_s, pre_s, suf_s) = refs
    n_chunks = t // CHUNK
    qk_w = B_HEADS * B_DIM

    row = lax.broadcasted_iota(jnp.int32, (t, LANES), 0)
    for j in range(3 * B_HEADS):
        cols = slice(j * LANES, (j + 1) * LANES)
        x = zb_ref[:, cols]
        prev = jnp.where(row == 0, 0.0, pltpu.roll(x, 1, 0))
        nxt = jnp.where(row == t - 1, 0.0, pltpu.roll(x, t - 1, 0))
        y = _silu(prev * conv_ref[0, 0:1, cols] + x * conv_ref[0, 1:2, cols] + nxt * conv_ref[0, 2:3, cols])
        if j < 2 * B_HEADS:
            y = y * lax.rsqrt(jnp.sum(y * y, axis=-1, keepdims=True) + EPS)
        if j < B_HEADS:
            y = y * (B_DIM ** -0.5)
        qkv_s[:, cols] = y

    if has_s0:
        sf_s[...] = s0f_ref[0, 0]
        sb_s[...] = s0b_ref[0, 0]
    else:
        sf_s[...] = jnp.zeros_like(sf_s)
        sb_s[...] = jnp.zeros_like(sb_s)

    a_col = jnp.exp(prmc_ref[0, 0:1, :])
    dtb_col = prmc_ref[0, 1:2, :]

    reps = t // LANES
    gr = -jnp.tile(jnp.exp(prmr_ref[0, 0]), (1, reps)) * _softplus(abt_ref[...] + jnp.tile(prmr_ref[0, 1], (1, reps)))
    seg_lane = lax.broadcasted_iota(jnp.int32, gr.shape, 1) % CHUNK
    pre, suf = gr, gr
    for s in (1, 2, 4, 8, 16, 32):
        pre = pre + jnp.where(seg_lane >= s, pltpu.roll(pre, s, 1), 0.0)
        suf = suf + jnp.where(seg_lane < CHUNK - s, pltpu.roll(suf, t - s, 1), 0.0)
    pre_s[...] = pre
    suf_s[...] = suf
    lane_lo = lax.broadcasted_iota(jnp.int32, (1, LANES), 1) < CHUNK

    lane = lax.broadcasted_iota(jnp.int32, (PAIR, LANES), 1)

    def prepare(cc, carry):
        chains = []
        for k in range(PREP_UNROLL):
            c = cc * PREP_UNROLL + k
            r0 = pl.multiple_of(c * CHUNK, CHUNK)
            ab = abc_ref[pl.ds(r0, CHUNK), :]
            g_all = -a_col * _softplus(ab + dtb_col)
            b_all = _sigmoid(ab)
            tile0 = pl.multiple_of((cc * PREP_UNROLL + k - k % 2) * CHUNK, LANES)
            run = (pre_s[:, pl.ds(tile0, LANES)], suf_s[:, pl.ds(tile0, LANES)])
            run_r = tuple(pltpu.roll(x, CHUNK, 1) for x in run)
            for p in range(B_HEADS // 2):
                kst = _stack_pair(qkv_s[pl.ds(r0, CHUNK), qk_w:2 * qk_w], p)
                qst = _stack_pair(qkv_s[pl.ds(r0, CHUNK), 0:qk_w], p)
                vst = _stack_pair(qkv_s[pl.ds(r0, CHUNK), 2 * qk_w:3 * qk_w], p)
                kq = _dot_nt(jnp.concatenate([kst, qst], axis=0).astype(BF16), kst.astype(BF16))
                for d in range(2):
                    cg = 4 * d + 2 * p
                    g_st = jnp.concatenate([g_all[:, cg + hl:cg + hl + 1] for hl in range(2)], axis=0)
                    b_st = jnp.concatenate([b_all[:, 8 + cg + hl:9 + cg + hl] for hl in range(2)], axis=0)
                    g1 = g_st.astype(BF16).astype(F32)
                    g2 = (g_st - g1).astype(BF16).astype(F32)
                    g3 = g_st - g1 - g2
                    packed = jnp.where(lane == 0, g1, jnp.where(lane == 1, g2, jnp.where(lane == 2, g3, 0.0)))
                    ra = 4 * d + 2 * p
                    if k % 2 == 0:
                        grow = jnp.where(lane_lo, run[d][ra:ra + 1], run_r[d][ra + 1:ra + 2])
                    else:
                        grow = jnp.where(lane_lo, run_r[d][ra:ra + 1], run[d][ra + 1:ra + 2])
                    chains.append(dict(c=c, p=p, d=d, kst=kst, qst=qst, vst=vst, kq=kq, b_st=b_st,
                                       packed=packed.astype(BF16), grow=grow))

        for ch in chains:
            ch["cs"] = _dot(cum_ref[ch["d"]], ch.pop("packed"))
        for ch in chains:
            d, cs, b_st, kq = ch["d"], ch.pop("cs"), ch["b_st"], ch.pop("kq")
            gcol = jnp.sum(cs[:PAIR], axis=1, keepdims=True)
            gtot = jnp.sum(cs[PAIR:], axis=1, keepdims=True)
            decay = jnp.exp(jnp.minimum(gcol - ch.pop("grow"), 0.0))
            ch["a_mat"] = (b_st * kq[:PAIR]) * (decay * mask_ref[2 * d + 1])
            ch["attn"] = (kq[PAIR:] * (decay * mask_ref[2 * d])).astype(BF16)
            ch["gcol"], ch["gtot"] = gcol, gtot
            ch["t_inv"] = mask_ref[4] - ch["a_mat"] * mask_ref[5]
        for lvl in range(N_LEVELS - 1):
            for ch in chains:
                ch["t16"] = ch["t_inv"].astype(BF16)
                ch["et"] = _dot((ch["a_mat"] * mask_ref[6 + lvl]).astype(BF16), ch["t16"])
            for ch in chains:
                ch["t_inv"] = ch["t_inv"] - _dot(ch.pop("t16"), ch.pop("et").astype(BF16))
        for ch in chains:
            egc = jnp.exp(ch["gcol"])
            rk = jnp.concatenate([ch["b_st"] * ch["vst"], (ch["b_st"] * egc) * ch["kst"]], axis=1)
            ch["rk"] = _dot(ch.pop("t_inv").astype(BF16), rk.astype(BF16))
            ch["qp16"] = (ch["qst"] * egc).astype(BF16)
        for ch in chains:
            c, p, d, rk, qp16 = ch["c"], ch["p"], ch["d"], ch["rk"], ch["qp16"]
            pair_rows = slice(p * PAIR, (p + 1) * PAIR)
            w16 = rk[:, B_DIM:].astype(BF16)
            u_s[d, c, pair_rows, :] = rk[:, :B_DIM]
            at_s[d, c, p] = ch["attn"]
            kd_s[d, c, pair_rows, :] = (ch["kst"] * jnp.exp(ch["gtot"] - ch["gcol"])).astype(BF16)
            eg = jnp.exp(ch["gtot"])
            for hl in range(2):
                h = 2 * p + hl
                rows = slice(hl * CHUNK, (hl + 1) * CHUNK)
                wq_s[d, c, h * 2 * CHUNK:h * 2 * CHUNK + CHUNK, :] = w16[rows]
                wq_s[d, c, h * 2 * CHUNK + CHUNK:(h + 1) * 2 * CHUNK, :] = qp16[rows]
                eg_s[d, c, h * SUBLANES:(h + 1) * SUBLANES, :] = jnp.broadcast_to(
                    eg[hl * CHUNK:hl * CHUNK + SUBLANES, :], (SUBLANES, LANES))
        return carry

    lax.fori_loop(0, n_chunks // PREP_UNROLL, prepare, 0)

    def scan_step(i, carry):
        for d, s_ref, o_s in ((0, sf_s, of_s), (1, sb_s, ob_s)):
            c = i if d == 0 else n_chunks - 1 - i
            r0 = pl.multiple_of(c * CHUNK, CHUNK)
            v_new, bots = [], []
            for h in range(B_HEADS):
                sh = s_ref[h * B_DIM:(h + 1) * B_DIM, :].astype(BF16)
                x = _dot(wq_s[d, c, h * 2 * CHUNK:(h + 1) * 2 * CHUNK, :], sh)
                v_new.append(u_s[d, c, h * CHUNK:(h + 1) * CHUNK, :] - x[:CHUNK])
                bots.append(x[CHUNK:])
            for p in range(B_HEADS // 2):
                vp16 = jnp.concatenate(v_new[2 * p:2 * p + 2], axis=0).astype(BF16)
                o = jnp.concatenate(bots[2 * p:2 * p + 2], axis=0) + _dot(at_s[d, c, p], vp16)
                for hl in range(2):
                    h = 2 * p + hl
                    rows = slice(hl * CHUNK, (hl + 1) * CHUNK)
                    srows = slice(h * B_DIM, (h + 1) * B_DIM)
                    upd = _dot_tn(kd_s[d, c, h * CHUNK:(h + 1) * CHUNK, :], vp16[rows])
                    eg = jnp.tile(eg_s[d, c, h * SUBLANES:(h + 1) * SUBLANES, :], (B_DIM // SUBLANES, 1))
                    s_ref[srows, :] = s_ref[srows, :] * eg + upd
                    o_s[pl.ds(r0, CHUNK), h * B_DIM:(h + 1) * B_DIM] = o[rows]
        return carry

    lax.fori_loop(0, n_chunks, scan_step, 0)

    if not has_s0:
        sfo_ref[0, 0] = sf_s[...]
        sbo_ref[0, 0] = sb_s[...]

    for h in range(B_HEADS):
        cols = slice(h * B_DIM, (h + 1) * B_DIM)
        x = of_s[:, cols] + ob_s[:, cols]
        yn = x * lax.rsqrt(jnp.mean(x * x, axis=-1, keepdims=True) + EPS) * bng_ref[0]
        o_ref[:, cols] = yn * _silu(zb_ref[:, 3 * qk_w + h * B_DIM:3 * qk_w + (h + 1) * B_DIM])


def _delta_call(has_s0, t, n_batch, row_block0, layer, zb, zab, zabt, conv, prmc, prmr, bng, cum, masks,
                prev=None, s0=None):
    n_tok = zb.shape[0]
    depth = conv.shape[0]
    n_chunks = t // CHUNK
    assert n_chunks % PREP_UNROLL == 0 and PREP_UNROLL % 2 == 0
    tok_spec = lambda w: pl.BlockSpec((t, w), lambda b: (row_block0 + b, 0))
    const = lambda shape: pl.BlockSpec(shape, lambda b: (0,) * len(shape))
    layer_spec = lambda shape: pl.BlockSpec((1,) + shape, lambda b: (layer,) + (0,) * len(shape))
    s_shape = (B_HEADS * B_DIM, B_DIM)
    s_spec = pl.BlockSpec((1, 1) + s_shape, lambda b: (b, layer, 0, 0))
    n_ab = zabt.shape[0]
    in_specs = [
        tok_spec(ZB_W), tok_spec(ZAB_W),
        pl.BlockSpec((n_ab, t), lambda b: (0, row_block0 + b)),
        layer_spec((3, 3 * B_HEADS * B_DIM)), layer_spec((2, LANES)), layer_spec((2, n_ab, LANES)),
        layer_spec((1, B_DIM)),
        const((2, 2 * PAIR, PAIR)), const((5 + N_LEVELS, PAIR, PAIR)),
    ]
    args = [zb, zab, zabt, conv, prmc, prmr, bng, cum, masks]
    out_specs = [tok_spec(B_HEADS * B_DIM)]
    out_shape = [jax.ShapeDtypeStruct((n_tok, B_HEADS * B_DIM), F32)]
    if has_s0:
        in_specs += [s_spec, s_spec]
        args += [s0[0], s0[1]]
    else:
        out_specs += [s_spec, s_spec]
        out_shape += [jax.ShapeDtypeStruct((n_batch, depth) + s_shape, F32)] * 2
    n_real = len(args)
    aliases = {}
    if prev is not None:
        first_out = 0 if has_s0 else 1
        for k, arr in enumerate(prev):
            in_specs.append(pl.BlockSpec(memory_space=pl.ANY))
            args.append(arr)
            aliases[n_real + k] = first_out + k
    scratch = [
        pltpu.VMEM((t, 3 * B_HEADS * B_DIM), F32),
        pltpu.VMEM((t, B_HEADS * B_DIM), F32), pltpu.VMEM((t, B_HEADS * B_DIM), F32),
        pltpu.VMEM(s_shape, F32), pltpu.VMEM(s_shape, F32),
        pltpu.VMEM((2, n_chunks, BD, B_DIM), F32),
        pltpu.VMEM((2, n_chunks, 2 * BD, B_DIM), BF16),
        pltpu.VMEM((2, n_chunks, B_HEADS // 2, PAIR, PAIR), BF16),
        pltpu.VMEM((2, n_chunks, BD, B_DIM), BF16),
        pltpu.VMEM((2, n_chunks, B_HEADS * SUBLANES, LANES), F32),
        pltpu.VMEM((n_ab, t), F32), pltpu.VMEM((n_ab, t), F32),
    ]

    def body(*refs):
        _delta_kernel(t, has_s0, *refs[:n_real], *refs[len(args):])

    return pl.pallas_call(
        body,
        grid=(n_batch,),
        in_specs=in_specs,
        out_specs=out_specs,
        out_shape=out_shape,
        scratch_shapes=scratch,
        input_output_aliases=aliases,
        compiler_params=_params(("arbitrary",)),
        name="delta_latent" if has_s0 else "delta_prompt",
    )(*args)


def _outproj_router(x, ma, mb, mc, m, g, wo_ref, wr_ref, br):
    y = (_dot(ma.astype(BF16), wo_ref[0, 0:256, :])
         + _dot(mb.astype(BF16), wo_ref[0, 256:768, :])
         + _dot(mc.astype(BF16), wo_ref[0, 768:1024, :]))
    x1 = x + m[2:3] * y
    h2 = _modulated_norm(x1, g, m[3:4], m[4:5])
    hi, lo = _split2(h2)

    logits = _dot(hi, wr_ref[0, 0]) + _dot(lo, wr_ref[0, 0]) + _dot(hi, wr_ref[0, 1]) + br
    lane = lax.broadcasted_iota(jnp.int32, logits.shape, 1)
    big = jnp.int32(LANES)
    is_group = lane < N_GROUPS
    gl = jnp.where(is_group, logits, -jnp.inf)
    gmax = gl.max(axis=1, keepdims=True)
    g_sel = jnp.where(gl == gmax, lane, big).min(axis=1, keepdims=True)
    g_w = 1.0 / jnp.where(is_group, jnp.exp(gl - gmax), 0.0).sum(axis=1, keepdims=True)
    e_idx = lane - ROUTER_LANE0
    elig = (e_idx >= 0) & (e_idx < N_EXPERTS) & ((e_idx // EXPERTS_PER_GROUP) == g_sel)
    el = jnp.where(elig, logits, -jnp.inf)
    m1 = el.max(axis=1, keepdims=True)
    i1 = jnp.where(el == m1, lane, big).min(axis=1, keepdims=True)
    el2 = jnp.where(lane == i1, -jnp.inf, el)
    m2 = el2.max(axis=1, keepdims=True)
    i2 = jnp.where(el2 == m2, lane, big).min(axis=1, keepdims=True)
    tt = jnp.exp(m2 - m1)
    w1 = g_w / (1.0 + tt)
    w2 = w1 * tt
    gate = jnp.where(lane == i1, w1, 0.0) + jnp.where(lane == i2, w2, 0.0)
    return x1, hi, gate


def _ffn_kernel(x_ref, ma_ref, mb_ref, mc_ref, mod_ref, g_ref, wo_ref, wr_ref, br_ref, w1_ref, w3_ref, w2_ref,
                o_ref, x1_s, h_s, gate_s, acc_s):
    j = pl.program_id(1)
    tm = x_ref.shape[0]
    th = w1_ref.shape[2]
    m = mod_ref[0, 0]

    @pl.when(j == 0)
    def _():
        sub = 256
        for r0 in range(0, tm, sub):
            rows = slice(r0, r0 + sub)
            x1, hi, gate = _outproj_router(x_ref[rows, :], ma_ref[rows, :], mb_ref[rows, :], mc_ref[rows, :],
                                           m, g_ref[0], wo_ref, wr_ref, br_ref[0])
            x1_s[rows, :] = x1
            h_s[rows, :] = hi
            gate_s[rows, :] = gate

    h = h_s[...]
    hid = _silu(_dot(h, w1_ref[0])) * _dot(h, w3_ref[0])
    gate = gate_s[...]
    lane = lax.broadcasted_iota(jnp.int32, gate.shape, 1)
    n_e = th // D_EXPERT
    col = lax.broadcasted_iota(jnp.int32, hid.shape, 1) // D_EXPERT
    gmat = jnp.zeros(hid.shape, F32)
    for e in range(n_e):
        ge = jnp.where(lane == ROUTER_LANE0 + j * n_e + e, gate, 0.0).sum(axis=1, keepdims=True)
        gmat = jnp.where(col == e, ge, gmat)
    contrib = _dot((hid * gmat).astype(BF16), w2_ref[0])

    @pl.when(j == 0)
    def _():
        acc_s[...] = contrib

    @pl.when(j > 0)
    def _():
        acc_s[...] += contrib

    @pl.when(j == pl.num_programs(1) - 1)
    def _():
        o_ref[...] = x1_s[...] + m[5:6] * acc_s[...]


def _ffn_call(layer, x, ma, mb, mc, mods, g, wo, wr, br, w1, w3, w2, slot_fn, tm, th):
    n_tok = x.shape[0]
    ef = w1.shape[2]
    tok = lambda w: pl.BlockSpec((tm, w), lambda i, j: (i, 0))
    layer_spec = lambda shape: pl.BlockSpec((1,) + shape, lambda i, j: (layer,) + (0,) * len(shape))
    return pl.pallas_call(
        _ffn_kernel,
        grid=(n_tok // tm, ef // th),
        in_specs=[tok(D_MODEL), tok(256), tok(512), tok(256),
                  pl.BlockSpec((1, 1, 6, D_MODEL), lambda i, j: (layer, slot_fn(i), 0, 0)),
                  layer_spec((1, D_MODEL)), layer_spec((D_MODEL, D_MODEL)), layer_spec((2, D_MODEL, LANES)),
                  layer_spec((1, LANES)),
                  pl.BlockSpec((1, D_MODEL, th), lambda i, j: (layer, 0, j)),
                  pl.BlockSpec((1, D_MODEL, th), lambda i, j: (layer, 0, j)),
                  pl.BlockSpec((1, th, D_MODEL), lambda i, j: (layer, j, 0))],
        out_specs=tok(D_MODEL),
        out_shape=jax.ShapeDtypeStruct((n_tok, D_MODEL), F32),
        scratch_shapes=[pltpu.VMEM((tm, D_MODEL), F32), pltpu.VMEM((tm, D_MODEL), BF16),
                        pltpu.VMEM((tm, LANES), F32), pltpu.VMEM((tm, D_MODEL), F32)],
        compiler_params=_params(("arbitrary", "arbitrary")),
        name="ffn",
    )(x, ma, mb, mc, mods, g, wo, wr, br, w1, w3, w2)


def _final_norm_kernel(x_ref, g_ref, o_ref):
    x = x_ref[...]
    o_ref[...] = x * lax.rsqrt(jnp.mean(x * x, axis=-1, keepdims=True) + EPS) * g_ref[...]


def _final_norm_call(x, g, tm, row0, n_rows):
    blk0 = row0 // tm
    return pl.pallas_call(
        _final_norm_kernel,
        grid=(n_rows // tm,),
        in_specs=[pl.BlockSpec((tm, D_MODEL), lambda i: (blk0 + i, 0)), pl.BlockSpec((1, D_MODEL), lambda i: (0, 0))],
        out_specs=pl.BlockSpec((tm, D_MODEL), lambda i: (i, 0)),
        out_shape=jax.ShapeDtypeStruct((n_rows, D_MODEL), F32),
        compiler_params=_params(("arbitrary",)),
        name="final_norm",
    )(x, g)


def _rope_tables(t):
    pos = np.arange(t)
    n_freq = HEAD_DIM // 4
    inv_freq = ROPE_THETA ** (-jnp.arange(n_freq, dtype=F32) / n_freq)
    row = jnp.asarray(pos // GRID_W, F32)
    col = jnp.asarray(pos % GRID_W, F32)
    ang = jnp.concatenate([row[:, None] * inv_freq, col[:, None] * inv_freq], -1)
    cos, sin = jnp.cos(ang), jnp.sin(ang)
    cos_t = jnp.tile(jnp.concatenate([cos, cos], -1), (1, LANES // HEAD_DIM))
    sin_t = jnp.tile(jnp.concatenate([-sin, sin], -1), (1, LANES // HEAD_DIM))
    return cos_t, sin_t


def _delta_tables():
    r = np.arange(PAIR)
    same = (r[:, None] // CHUNK) == (r[None, :] // CHUNK)
    low = same & (r[:, None] >= r[None, :])
    low_s = same & (r[:, None] > r[None, :])
    up = same & (r[:, None] <= r[None, :])
    up_s = same & (r[:, None] < r[None, :])
    levels = []
    for k in range(N_LEVELS):
        s = 1 << k
        levels.append(((r[:, None] // (2 * s)) == (r[None, :] // (2 * s))) & ((r[:, None] // s) != (r[None, :] // s)))
    masks = jnp.asarray(np.stack([low, low_s, up, up_s, np.eye(PAIR, dtype=bool)] + levels).astype(np.float32))
    cum = np.stack([np.concatenate([low, same], 0), np.concatenate([up, same], 0)]).astype(np.float32)
    return masks, jnp.asarray(cum, BF16)


def _segment_mean_table():
    r = np.arange(LANES)
    seg = ((r[:, None] // HEAD_DIM) == (r[None, :] // HEAD_DIM)).astype(np.float32) / HEAD_DIM
    hi = jnp.asarray(seg, BF16)
    lo = (jnp.asarray(seg) - hi.astype(F32)).astype(BF16)
    return jnp.stack([hi, lo])


def kernel(x_prompt, x_sample, cache_a_k, cache_a_v, cache_c_k, cache_c_v, state_b_fwd, state_b_bwd, c, c_ctx, w_mod, b_mod, norm1_g, norm2_g, w_in, a_sink, b_conv, b_a_log, b_dt_bias, b_norm_g, c_q_norm, c_k_norm, w_out, w_group, b_group, w_expert, b_expert, w1, w3, w2, final_norm_g):
    n_p, t_p, d = x_prompt.shape
    n_s, t_s, _ = x_sample.shape
    depth = w_in.shape[0]
    past = cache_a_k.shape[2]
    tok_p = n_p * t_p
    n_tok = tok_p + n_s * t_s
    assert d == D_MODEL and tok_p % t_s == 0 and t_s % 512 == 0 and t_p % 256 == 0

    ab0 = ZA_W + ZB_W
    w_in_p = jnp.concatenate(
        [w_in[:, :, :ab0], w_in[:, :, ab0 + N_AB:], w_in[:, :, ab0:ab0 + N_AB],
         jnp.zeros((depth, d, ZAB_W - N_AB), F32)], axis=-1).astype(BF16)
    w_abt = jnp.swapaxes(w_in[:, :, ab0:ab0 + N_AB], 1, 2).astype(BF16)
    w_out16 = w_out.astype(BF16)
    w116, w316, w216 = w1.astype(BF16), w3.astype(BF16), w2.astype(BF16)
    w_r = jnp.concatenate([w_group, w_expert, jnp.zeros((depth, d, LANES - N_GROUPS - N_EXPERTS), F32)], -1)
    w_r_hi = w_r.astype(BF16)
    w_r_lo = (w_r - w_r_hi.astype(F32)).astype(BF16)
    w_r2 = jnp.stack([w_r_hi, w_r_lo], axis=1)
    b_r = jnp.concatenate([b_group, b_expert, jnp.zeros((depth, LANES - N_GROUPS - N_EXPERTS), F32)], -1)
    b_r = b_r[:, None, :]
    cqn = jnp.tile(c_q_norm, (1, 4))[:, None, :]
    ckn = jnp.tile(c_k_norm, (1, 2))[:, None, :]
    gate_prm = jnp.stack([b_a_log.reshape(depth, 8), b_dt_bias.reshape(depth, 8)], 1)
    prmc = jnp.pad(gate_prm, ((0, 0), (0, 0), (0, LANES - 8)))
    prmr = jnp.broadcast_to(jnp.pad(gate_prm, ((0, 0), (0, 0), (0, N_AB - 8)))[..., None],
                            (depth, 2, N_AB, LANES))
    cos_t, sin_t = _rope_tables(t_s)
    masks, cum = _delta_tables()
    seg = _segment_mean_table()

    cond = jnp.concatenate([c_ctx[None, :], c], axis=0)
    cond_b = jnp.broadcast_to(cond[:, :, None], cond.shape + (LANES,))
    mods_all = _mods_call(cond_b, w_mod, b_mod).reshape(depth, SUBLANES, 6, d)

    def slot_fn(tm):
        per_s = t_s // tm
        first = tok_p // tm
        return lambda i: jnp.where(i < first, 0, 1 + (i - first) // per_s)

    x = jnp.concatenate([x_prompt.reshape(tok_p, d), x_sample.reshape(n_s * t_s, d)], axis=0)
    blk_s = tok_p // t_s
    ctx = tuple(a.reshape(n_s, depth, past, LANES) for a in (cache_a_k, cache_a_v, cache_c_k, cache_c_v))
    s0 = tuple(a.reshape(n_s, depth, B_HEADS * B_DIM, B_DIM) for a in (state_b_fwd, state_b_bwd))
    g1, g2, bng = norm1_g[:, None, :], norm2_g[:, None, :], b_norm_g[:, None, :]
    caches = None
    states = None
    tm = 512
    for l in range(depth):
        za, zb, zc, zab, zabt = _inproj_call(l, x, mods_all, g1, w_in_p, w_abt, slot_fn(tm), tm)

        ao, co, *caches = _attn_call(False, t_p, n_p, 0, l, za, zc, a_sink, cqn, ckn, seg, prev=caches)
        ao, co = _attn_call(True, t_s, n_s, blk_s, l, za, zc, a_sink, cqn, ckn, seg, prev=(ao, co),
                            rope=(cos_t, sin_t), ctx=ctx)

        bo, *states = _delta_call(False, t_p, n_p, 0, l, zb, zab, zabt, b_conv, prmc, prmr, bng, cum, masks,
                                  prev=states)
        (bo,) = _delta_call(True, t_s, n_s, blk_s, l, zb, zab, zabt, b_conv, prmc, prmr, bng, cum, masks,
                            prev=(bo,), s0=s0)

        x = _ffn_call(l, x, ao, bo, co, mods_all, g2, w_out16, w_r2, b_r, w116, w316, w216, slot_fn(1024), 1024, 512)

    y_prompt = _final_norm_call(x, final_norm_g[None], tm, 0, tok_p).reshape(n_p, t_p, d)
    y_sample = _final_norm_call(x, final_norm_g[None], tm, tok_p, n_s * t_s).reshape(n_s, t_s, d)
    new_ak, new_av, new_ck, new_cv = (a.reshape(n_p, depth, t_p, 2, HEAD_DIM) for a in caches)
    new_sf, new_sb = (a.reshape(n_p, depth, B_HEADS, B_DIM, B_DIM) for a in states)
    return (y_prompt, y_sample, new_ak, new_av, new_ck, new_cv, new_sf, new_sb)
```

```python
import functools

import jax
import jax.numpy as jnp
import numpy as np
from jax import lax
from jax.experimental import pallas as pl
from jax.experimental.pallas import tpu as pltpu

F32 = jnp.float32
BF16 = jnp.bfloat16

D_MODEL = 1024
DEPTH = 4
GRID_W = 64
EPS = 1e-6
NEG_INF = -1e30
ROPE_THETA = 10000.0
HEAD_DIM = 64
N_Q_HEADS = 4
WINDOW = 128
Q_BLOCK = 128
B_HEADS = 4
B_DIM = 128
CHUNK = 64
BD = B_HEADS * CHUNK
PAIR = 2 * CHUNK
N_LEVELS = 6
PREP_UNROLL = 4
N_GROUPS = 4
EXPERTS_PER_GROUP = 4
N_EXPERTS = 16
D_EXPERT = 256
EXPERT_ROW0 = 8

LANES = 128
SUBLANES = 8
VMEM_LIMIT = 56 * 1024 * 1024

ZA_W, ZB_W, ZC_W, ZAB_W = 512, 2048, 512, 128
N_AB = 16
Z_W = ZA_W + ZB_W + ZC_W + ZAB_W


def _sigmoid(x):
    return 1.0 / (1.0 + jnp.exp(-x))


def _silu(x):
    return x * _sigmoid(x)


def _softplus(x):
    return jnp.maximum(x, 0.0) + jnp.log1p(jnp.exp(-jnp.abs(x)))


def _dot(a, b):
    return jnp.dot(a, b, preferred_element_type=F32)


def _dot_nt(a, b):
    return lax.dot_general(a, b, (((1,), (1,)), ((), ())), preferred_element_type=F32)


def _dot_tn(a, b):
    return lax.dot_general(a, b, (((0,), (0,)), ((), ())), preferred_element_type=F32)


def _split2(x):
    hi = x.astype(BF16)
    lo = (x - hi.astype(F32)).astype(BF16)
    return hi, lo


def _params(sem=None):
    return pltpu.CompilerParams(dimension_semantics=sem, vmem_limit_bytes=VMEM_LIMIT)


def _mods_kernel(cond_ref, w_ref, b_ref, o_ref, act_s):
    n_cond = cond_ref.shape[0]
    tn = w_ref.shape[2]
    reps = tn // LANES

    @pl.when((pl.program_id(0) == 0) & (pl.program_id(1) == 0))
    def _():
        act_s[...] = _silu(cond_ref[...])

    def body(kb, accs):
        r = pl.multiple_of(kb * SUBLANES, SUBLANES)
        w = w_ref[0, pl.ds(r, SUBLANES), :]
        return tuple(acc + jnp.tile(act_s[m, pl.ds(r, SUBLANES), :], (1, reps)) * w for m, acc in enumerate(accs))

    zero = jnp.zeros((SUBLANES, tn), F32)
    accs = lax.fori_loop(0, w_ref.shape[1] // SUBLANES, body, (zero,) * n_cond, unroll=4)
    rows = [jnp.sum(a, axis=0, keepdims=True) + b_ref[0] for a in accs]
    rows.append(jnp.zeros((SUBLANES - n_cond, tn), F32))
    o_ref[0] = jnp.concatenate(rows, axis=0)


def _mods_call(cond_b, w_mod, b_mod):
    depth, d, n = w_mod.shape
    tn = 1536
    n_cond = cond_b.shape[0]
    return pl.pallas_call(
        _mods_kernel,
        grid=(depth, n // tn),
        in_specs=[
            pl.BlockSpec((n_cond, d, LANES), lambda l, j: (0, 0, 0)),
            pl.BlockSpec((1, d, tn), lambda l, j: (l, 0, j)),
            pl.BlockSpec((1, 1, tn), lambda l, j: (l, 0, j)),
        ],
        out_specs=pl.BlockSpec((1, SUBLANES, tn), lambda l, j: (l, 0, j)),
        out_shape=jax.ShapeDtypeStruct((depth, SUBLANES, n), F32),
        scratch_shapes=[pltpu.VMEM((n_cond, d, LANES), F32)],
        compiler_params=_params(("arbitrary", "arbitrary")),
        name="mods",
    )(cond_b, w_mod, b_mod.reshape(depth, 1, n))


def _modulated_norm(x, g, shift, scale):
    ms = jnp.mean(x * x, axis=-1, keepdims=True)
    y = x * lax.rsqrt(ms + EPS) * g
    return y * (1.0 + scale) + shift


def _inproj_kernel(x_ref, mod_ref, g_ref, w_ref, za_ref, zb_ref, zc_ref, zab_ref, zabt_ref):
    m = mod_ref[0, 0]
    h = _modulated_norm(x_ref[...], g_ref[0], m[0:1], m[1:2]).astype(BF16)
    za_ref[...] = _dot(h, w_ref[0, :, 0:ZA_W])
    step = 512
    for j in range(ZB_W // step):
        zb_ref[:, j * step:(j + 1) * step] = _dot(h, w_ref[0, :, ZA_W + j * step:ZA_W + (j + 1) * step])
    zc_ref[...] = _dot(h, w_ref[0, :, ZA_W + ZB_W:ZA_W + ZB_W + ZC_W])
    zab = _dot(h, w_ref[0, :, ZA_W + ZB_W + ZC_W:Z_W])
    zab_ref[...] = zab
    zabt_ref[...] = zab.T[:N_AB]


def _inproj_call(layer, x, mods, g, w, slot_fn, tm):
    n_tok = x.shape[0]
    n_ab = N_AB
    return pl.pallas_call(
        _inproj_kernel,
        grid=(n_tok // tm,),
        in_specs=[
            pl.BlockSpec((tm, D_MODEL), lambda i: (i, 0)),
            pl.BlockSpec((1, 1, 6, D_MODEL), lambda i: (layer, slot_fn(i), 0, 0)),
            pl.BlockSpec((1, 1, D_MODEL), lambda i: (layer, 0, 0)),
            pl.BlockSpec((1, D_MODEL, Z_W), lambda i: (layer, 0, 0)),
        ],
        out_specs=[
            pl.BlockSpec((tm, ZA_W), lambda i: (i, 0)),
            pl.BlockSpec((tm, ZB_W), lambda i: (i, 0)),
            pl.BlockSpec((tm, ZC_W), lambda i: (i, 0)),
            pl.BlockSpec((tm, ZAB_W), lambda i: (i, 0)),
            pl.BlockSpec((n_ab, tm), lambda i: (0, i)),
        ],
        out_shape=[
            jax.ShapeDtypeStruct((n_tok, ZA_W), F32),
            jax.ShapeDtypeStruct((n_tok, ZB_W), F32),
            jax.ShapeDtypeStruct((n_tok, ZC_W), F32),
            jax.ShapeDtypeStruct((n_tok, ZAB_W), F32),
            jax.ShapeDtypeStruct((n_ab, n_tok), F32),
        ],
        compiler_params=_params(("arbitrary",)),
        name="inproj",
    )(x, mods, g, w)


def _lane_lo(shape):
    return lax.broadcasted_iota(jnp.int32, shape, len(shape) - 1) % LANES < HEAD_DIM


def _store_kdup(dst_ref, off, k):
    n = k.shape[0]
    r = pltpu.roll(k, HEAD_DIM, 1)
    lo = _lane_lo(k.shape)
    dst_ref[0, off:off + n, :] = jnp.where(lo, k, r).astype(BF16)
    dst_ref[1, off:off + n, :] = jnp.where(lo, r, k).astype(BF16)


def _store_vsplit(dst_ref, off, v):
    n = v.shape[0]
    r = pltpu.roll(v, HEAD_DIM, 1)
    lo = _lane_lo(v.shape)
    z = jnp.zeros_like(v)
    dst_ref[0, off:off + n, :] = jnp.where(lo, v, z).astype(BF16)
    dst_ref[1, off:off + n, :] = jnp.where(lo, z, r).astype(BF16)
    dst_ref[2, off:off + n, :] = jnp.where(lo, r, z).astype(BF16)
    dst_ref[3, off:off + n, :] = jnp.where(lo, z, v).astype(BF16)


def _rope(x, cos, sin):
    first = (lax.broadcasted_iota(jnp.int32, x.shape, 1) // (HEAD_DIM // 2)) % 2 == 0
    partner = jnp.where(first, pltpu.roll(x, LANES - HEAD_DIM // 2, 1), pltpu.roll(x, HEAD_DIM // 2, 1))
    return x * cos + partner * sin


def _head_rmsnorm(x, g, seg_hi, seg_lo):
    hi, lo = _split2(x * x)
    ms = _dot(hi, seg_hi) + _dot(lo, seg_hi) + _dot(hi, seg_lo)
    return x * lax.rsqrt(ms + EPS) * g


def _attend(qt, segs, sink_pair):
    qb = qt.shape[0]
    lo = _lane_lo(qt.shape)
    z = jnp.zeros_like(qt)
    qs = jnp.concatenate([jnp.where(lo, qt, z), jnp.where(lo, z, qt)], axis=0).astype(BF16)
    scores = []
    for kdup, _, _, mask in segs:
        s = _dot_nt(qs, kdup)
        if mask is not None:
            s = jnp.where(mask, s, NEG_INF)
        scores.append(s)
    m = scores[0].max(axis=1, keepdims=True)
    for s in scores[1:]:
        m = jnp.maximum(m, s.max(axis=1, keepdims=True))
    if sink_pair is not None:
        row_a = lax.broadcasted_iota(jnp.int32, (2 * qb, 1), 0) < qb
        sink = jnp.where(row_a, sink_pair[0], sink_pair[1])
        m = jnp.maximum(m, sink)
        denom = jnp.exp(sink - m)
    else:
        denom = jnp.zeros((2 * qb, 1), F32)
    acc = jnp.zeros((qb, LANES), F32)
    for s, (_, vlo, vhi, _) in zip(scores, segs):
        p = jnp.exp(s - m)
        denom = denom + p.sum(axis=1, keepdims=True)
        pb = p.astype(BF16)
        acc = acc + _dot(pb[:qb], vlo) + _dot(pb[qb:], vhi)
    inv = 1.0 / denom
    return acc * jnp.where(lo, inv[:qb], inv[qb:])


def _attn_kernel(has_ctx, t, layer, *refs):
    if has_ctx:
        (sink_ref, za_ref, zc_ref, cqn_ref, ckn_ref, seg_ref, cos_ref, sin_ref,
         cak_ref, cav_ref, cck_ref, ccv_ref,
         ao_ref, co_ref,
         ka_s, va_s, kc_s, vc_s, kctx_s, vctx_s, qa_s, qc_s) = refs
    else:
        (sink_ref, za_ref, zc_ref, cqn_ref, ckn_ref, seg_ref,
         ao_ref, co_ref, nak_ref, nav_ref, nck_ref, ncv_ref,
         ka_s, va_s, kc_s, vc_s, qa_s, qc_s) = refs
    scale = HEAD_DIM ** -0.5
    seg_hi = seg_ref[0]
    seg_lo = seg_ref[1]
    piece = 256
    n_ctx = cak_ref.shape[2] if has_ctx else 0

    for p0 in range(0, t, piece):
        rows = slice(p0, p0 + piece)
        ak = za_ref[rows, 256:384]
        av = za_ref[rows, 384:512]
        ck = _head_rmsnorm(zc_ref[rows, 256:384], ckn_ref[0], seg_hi, seg_lo)
        cv = zc_ref[rows, 384:512]
        if has_ctx:
            cos = cos_ref[rows, :]
            sin = sin_ref[rows, :]
            ak = _rope(ak, cos, sin)
            ck = _rope(ck, cos, sin)
            _store_kdup(ka_s, WINDOW + p0, ak)
            _store_vsplit(va_s, WINDOW + p0, av)
            _store_kdup(kc_s, n_ctx + p0, ck)
            _store_vsplit(vc_s, n_ctx + p0, cv)
        else:
            nak_ref[0, 0, rows, :] = ak
            nav_ref[0, 0, rows, :] = av
            nck_ref[0, 0, rows, :] = ck
            ncv_ref[0, 0, rows, :] = cv
            _store_kdup(ka_s, p0, ak)
            _store_vsplit(va_s, p0, av)
            _store_kdup(kc_s, p0, ck)
            _store_vsplit(vc_s, p0, cv)
        for hk in range(2):
            cols = slice(hk * LANES, (hk + 1) * LANES)
            aq = za_ref[rows, cols]
            cq = _head_rmsnorm(zc_ref[rows, cols], cqn_ref[0, :, cols], seg_hi, seg_lo)
            if has_ctx:
                aq = _rope(aq, cos, sin)
                cq = _rope(cq, cos, sin)
            qa_s[rows, cols] = aq * scale
            qc_s[rows, cols] = cq * scale

    if has_ctx:
        zpad = jnp.zeros((WINDOW, LANES), BF16)
        for i in range(2):
            ka_s[i, 0:WINDOW, :] = zpad
            ka_s[i, WINDOW + t:2 * WINDOW + t, :] = zpad
        for i in range(4):
            va_s[i, 0:WINDOW, :] = zpad
            va_s[i, WINDOW + t:2 * WINDOW + t, :] = zpad
        for p0 in range(0, n_ctx, piece):
            rows = slice(p0, p0 + piece)
            _store_kdup(kctx_s, p0, cak_ref[0, 0, rows, :])
            _store_vsplit(vctx_s, p0, cav_ref[0, 0, rows, :])
            _store_kdup(kc_s, p0, cck_ref[0, 0, rows, :])
            _store_vsplit(vc_s, p0, ccv_ref[0, 0, rows, :])

        qb = Q_BLOCK
        span = 3 * qb
        qi = lax.broadcasted_iota(jnp.int32, (2 * qb, span), 0) % qb
        kj = lax.broadcasted_iota(jnp.int32, (2 * qb, span), 1)
        band = jnp.abs(kj - qb - qi) <= WINDOW

        def block(b, carry):
            r0 = pl.multiple_of(b * qb, qb)
            kpos = kj + (r0 - qb)
            mask = band & (kpos >= 0) & (kpos < t)
            for hk in range(2):
                cols = slice(hk * LANES, (hk + 1) * LANES)
                segs_a = [
                    (kctx_s[hk], vctx_s[2 * hk], vctx_s[2 * hk + 1], None),
                    (ka_s[hk, pl.ds(r0, span), :], va_s[2 * hk, pl.ds(r0, span), :],
                     va_s[2 * hk + 1, pl.ds(r0, span), :], mask),
                ]
                sinks = (sink_ref[layer, 2 * hk], sink_ref[layer, 2 * hk + 1])
                ao_ref[pl.ds(r0, qb), cols] = _attend(qa_s[pl.ds(r0, qb), cols], segs_a, sinks)
                segs_c = [(kc_s[hk], vc_s[2 * hk], vc_s[2 * hk + 1], None)]
                co_ref[pl.ds(r0, qb), cols] = _attend(qc_s[pl.ds(r0, qb), cols], segs_c, None)
            return carry

        lax.fori_loop(0, t // qb, block, 0)
    else:
        for hk in range(2):
            cols = slice(hk * LANES, (hk + 1) * LANES)
            sinks = (sink_ref[layer, 2 * hk], sink_ref[layer, 2 * hk + 1])
            segs_a = [(ka_s[hk], va_s[2 * hk], va_s[2 * hk + 1], None)]
            ao_ref[:, cols] = _attend(qa_s[:, cols], segs_a, sinks)
            segs_c = [(kc_s[hk], vc_s[2 * hk], vc_s[2 * hk + 1], None)]
            co_ref[:, cols] = _attend(qc_s[:, cols], segs_c, None)


def _attn_call(has_ctx, t, n_batch, row_block0, layer, za, zc, sink, cqn, ckn, seg, prev=None, rope=None, ctx=None):
    n_tok = za.shape[0]
    depth = sink.shape[0]
    tok_spec = lambda w: pl.BlockSpec((t, w), lambda b, *_: (row_block0 + b, 0))
    const = lambda shape: pl.BlockSpec(shape, lambda b, *_: (0,) * len(shape))
    layer_spec = lambda shape: pl.BlockSpec((1,) + shape, lambda b, *_: (layer,) + (0,) * len(shape))
    in_specs = [tok_spec(ZA_W), tok_spec(ZC_W), layer_spec((1, 256)), layer_spec((1, 128)), const((2, LANES, LANES))]
    args = [za, zc, cqn, ckn, seg]
    out_specs = [tok_spec(256), tok_spec(256)]
    out_shape = [jax.ShapeDtypeStruct((n_tok, 256), F32), jax.ShapeDtypeStruct((n_tok, 256), F32)]
    if has_ctx:
        n_ctx = ctx[0].shape[2]
        in_specs += [const((t, LANES)), const((t, LANES))]
        args += list(rope)
        in_specs += [pl.BlockSpec((1, 1, n_ctx, LANES), lambda b, *_: (b, layer, 0, 0))] * 4
        args += list(ctx)
        scratch = [
            pltpu.VMEM((2, t + 2 * WINDOW, LANES), BF16), pltpu.VMEM((4, t + 2 * WINDOW, LANES), BF16),
            pltpu.VMEM((2, n_ctx + t, LANES), BF16), pltpu.VMEM((4, n_ctx + t, LANES), BF16),
            pltpu.VMEM((2, n_ctx, LANES), BF16), pltpu.VMEM((4, n_ctx, LANES), BF16),
            pltpu.VMEM((t, 256), F32), pltpu.VMEM((t, 256), F32),
        ]
    else:
        cache_spec = pl.BlockSpec((1, 1, t, LANES), lambda b, *_: (b, layer, 0, 0))
        out_specs += [cache_spec] * 4
        out_shape += [jax.ShapeDtypeStruct((n_batch, depth, t, LANES), F32)] * 4
        scratch = [
            pltpu.VMEM((2, t, LANES), BF16), pltpu.VMEM((4, t, LANES), BF16),
            pltpu.VMEM((2, t, LANES), BF16), pltpu.VMEM((4, t, LANES), BF16),
            pltpu.VMEM((t, 256), F32), pltpu.VMEM((t, 256), F32),
        ]
    n_real = len(args)
    aliases = {}
    if prev is not None:
        first_out = 0 if has_ctx else 2
        for k, arr in enumerate(prev):
            in_specs.append(pl.BlockSpec(memory_space=pl.ANY))
            args.append(arr)
            aliases[1 + n_real + k] = first_out + k

    def body(*refs):
        ins = refs[:1 + n_real]
        rest = refs[1 + len(args):]
        _attn_kernel(has_ctx, t, layer, *ins, *rest)

    return pl.pallas_call(
        body,
        grid_spec=pltpu.PrefetchScalarGridSpec(
            num_scalar_prefetch=1, grid=(n_batch,), in_specs=in_specs, out_specs=out_specs,
            scratch_shapes=scratch),
        out_shape=out_shape,
        input_output_aliases=aliases,
        compiler_params=_params(("arbitrary",)),
        name="attn_latent" if has_ctx else "attn_prompt",
    )(sink, *args)


def _stack_pair(x, p):
    return jnp.concatenate([x[:, (2 * p + hl) * B_DIM:(2 * p + hl + 1) * B_DIM] for hl in range(2)], axis=0)


def _delta_kernel(t, has_s0, *refs):
    if has_s0:
        (zb_ref, abc_ref, abt_ref, conv_ref, prmr_ref, bng_ref, mask_ref,
         s0f_ref, s0b_ref, o_ref, qkv_s, of_s, ob_s, sf_s, sb_s, u_s, wq_s, at_s, kd_s, eg_s,
         pre_s, suf_s, prec_s, sufc_s) = refs
    else:
        (zb_ref, abc_ref, abt_ref, conv_ref, prmr_ref, bng_ref, mask_ref,
         o_ref, sfo_ref, sbo_ref, qkv_s, of_s, ob_s, sf_s, sb_s, u_s, wq_s, at_s, kd_s, eg_s,
         pre_s, suf_s, prec_s, sufc_s) = refs
    n_chunks = t // CHUNK
    qk_w = B_HEADS * B_DIM

    row = lax.broadcasted_iota(jnp.int32, (t, LANES), 0)
    for j in range(3 * B_HEADS):
        cols = slice(j * LANES, (j + 1) * LANES)
        x = zb_ref[:, cols]
        prev = jnp.where(row == 0, 0.0, pltpu.roll(x, 1, 0))
        nxt = jnp.where(row == t - 1, 0.0, pltpu.roll(x, t - 1, 0))
        y = _silu(prev * conv_ref[0, 0:1, cols] + x * conv_ref[0, 1:2, cols] + nxt * conv_ref[0, 2:3, cols])
        if j < 2 * B_HEADS:
            y = y * lax.rsqrt(jnp.sum(y * y, axis=-1, keepdims=True) + EPS)
        if j < B_HEADS:
            y = y * (B_DIM ** -0.5)
        qkv_s[:, cols] = y

    if has_s0:
        sf_s[...] = s0f_ref[0, 0]
        sb_s[...] = s0b_ref[0, 0]
    else:
        sf_s[...] = jnp.zeros_like(sf_s)
        sb_s[...] = jnp.zeros_like(sb_s)

    reps = t // LANES
    gr = -jnp.tile(jnp.exp(prmr_ref[0, 0]), (1, reps)) * _softplus(abt_ref[...] + jnp.tile(prmr_ref[0, 1], (1, reps)))
    seg_lane = lax.broadcasted_iota(jnp.int32, gr.shape, 1) % CHUNK
    pre, suf = gr, gr
    for s in (1, 2, 4, 8, 16, 32):
        pre = pre + jnp.where(seg_lane >= s, pltpu.roll(pre, s, 1), 0.0)
        suf = suf + jnp.where(seg_lane < CHUNK - s, pltpu.roll(suf, t - s, 1), 0.0)
    pre_s[...] = pre
    suf_s[...] = suf
    zrows = jnp.zeros((LANES - pre.shape[0], LANES), F32)
    for j in range(reps):
        tile = slice(j * LANES, (j + 1) * LANES)
        prec_s[tile, :] = jnp.concatenate([pre[:, tile], zrows], axis=0).T
        sufc_s[tile, :] = jnp.concatenate([suf[:, tile], zrows], axis=0).T
    lane_lo = lax.broadcasted_iota(jnp.int32, (1, LANES), 1) < CHUNK

    def prepare(cc, carry):
        chains = []
        for k in range(PREP_UNROLL):
            c = cc * PREP_UNROLL + k
            r0 = pl.multiple_of(c * CHUNK, CHUNK)
            b_all = _sigmoid(abc_ref[pl.ds(r0, CHUNK), :])
            run_c = (prec_s[pl.ds(r0, CHUNK), :], sufc_s[pl.ds(r0, CHUNK), :])
            tile0 = pl.multiple_of((cc * PREP_UNROLL + k - k % 2) * CHUNK, LANES)
            run = (pre_s[:, pl.ds(tile0, LANES)], suf_s[:, pl.ds(tile0, LANES)])
            run_r = tuple(pltpu.roll(x, CHUNK, 1) for x in run)
            for p in range(B_HEADS // 2):
                kst = _stack_pair(qkv_s[pl.ds(r0, CHUNK), qk_w:2 * qk_w], p)
                qst = _stack_pair(qkv_s[pl.ds(r0, CHUNK), 0:qk_w], p)
                vst = _stack_pair(qkv_s[pl.ds(r0, CHUNK), 2 * qk_w:3 * qk_w], p)
                kq = _dot_nt(jnp.concatenate([kst, qst], axis=0).astype(BF16), kst.astype(BF16))
                for d in range(2):
                    cg = 4 * d + 2 * p
                    edge = CHUNK - 1 if d == 0 else 0
                    rep_col = lambda x, col: jnp.broadcast_to(x[:, col:col + 1], (CHUNK, LANES))
                    b_rep = jnp.concatenate([rep_col(b_all, 8 + cg + hl) for hl in range(2)], axis=0)
                    gcol = jnp.concatenate([rep_col(run_c[d], cg + hl) for hl in range(2)], axis=0)
                    gtot = jnp.concatenate([rep_col(run_c[d][edge:edge + 1], cg + hl) for hl in range(2)], axis=0)
                    ra = cg
                    if k % 2 == 0:
                        grow = jnp.where(lane_lo, run[d][ra:ra + 1], run_r[d][ra + 1:ra + 2])
                    else:
                        grow = jnp.where(lane_lo, run_r[d][ra:ra + 1], run[d][ra + 1:ra + 2])
                    chains.append(dict(c=c, p=p, d=d, kst=kst, qst=qst, vst=vst, kq=kq, b_st=b_rep,
                                       gcol=gcol, gtot=gtot, grow=grow))

        for ch in chains:
            d, b_st, kq, gcol = ch["d"], ch["b_st"], ch.pop("kq"), ch["gcol"]
            decay = jnp.exp(jnp.minimum(gcol - ch.pop("grow"), 0.0))
            ch["a_mat"] = (b_st * kq[:PAIR]) * (decay * mask_ref[2 * d + 1])
            ch["attn"] = (kq[PAIR:] * (decay * mask_ref[2 * d])).astype(BF16)
            ch["t_inv"] = mask_ref[4] - ch["a_mat"] * mask_ref[5]
        for lvl in range(N_LEVELS - 1):
            for ch in chains:
                ch["t16"] = ch["t_inv"].astype(BF16)
                ch["et"] = _dot((ch["a_mat"] * mask_ref[6 + lvl]).astype(BF16), ch["t16"])
            for ch in chains:
                ch["t_inv"] = ch["t_inv"] - _dot(ch.pop("t16"), ch.pop("et").astype(BF16))
        for ch in chains:
            egc = jnp.exp(ch["gcol"])
            rk = jnp.concatenate([ch["b_st"] * ch["vst"], (ch["b_st"] * egc) * ch["kst"]], axis=1)
            ch["rk"] = _dot(ch.pop("t_inv").astype(BF16), rk.astype(BF16))
            ch["qp16"] = (ch["qst"] * egc).astype(BF16)
        for ch in chains:
            c, p, d, rk, qp16 = ch["c"], ch["p"], ch["d"], ch["rk"], ch["qp16"]
            pair_rows = slice(p * PAIR, (p + 1) * PAIR)
            w16 = rk[:, B_DIM:].astype(BF16)
            u_s[d, c, pair_rows, :] = rk[:, :B_DIM]
            at_s[d, c, p] = ch["attn"]
            kd_s[d, c, pair_rows, :] = (ch["kst"] * jnp.exp(ch["gtot"] - ch["gcol"])).astype(BF16)
            eg = jnp.exp(ch["gtot"])
            for hl in range(2):
                h = 2 * p + hl
                rows = slice(hl * CHUNK, (hl + 1) * CHUNK)
                wq_s[d, c, h * 2 * CHUNK:h * 2 * CHUNK + CHUNK, :] = w16[rows]
                wq_s[d, c, h * 2 * CHUNK + CHUNK:(h + 1) * 2 * CHUNK, :] = qp16[rows]
                eg_s[d, c, h * SUBLANES:(h + 1) * SUBLANES, :] = eg[hl * CHUNK:hl * CHUNK + SUBLANES, :]
        return carry

    lax.fori_loop(0, n_chunks // PREP_UNROLL, prepare, 0)

    def scan_step(i, carry):
        for d, s_ref, o_s in ((0, sf_s, of_s), (1, sb_s, ob_s)):
            c = i if d == 0 else n_chunks - 1 - i
            r0 = pl.multiple_of(c * CHUNK, CHUNK)
            v_new, bots = [], []
            for h in range(B_HEADS):
                sh = s_ref[h * B_DIM:(h + 1) * B_DIM, :].astype(BF16)
                x = _dot(wq_s[d, c, h * 2 * CHUNK:(h + 1) * 2 * CHUNK, :], sh)
                v_new.append(u_s[d, c, h * CHUNK:(h + 1) * CHUNK, :] - x[:CHUNK])
                bots.append(x[CHUNK:])
            for p in range(B_HEADS // 2):
                vp16 = jnp.concatenate(v_new[2 * p:2 * p + 2], axis=0).astype(BF16)
                o = jnp.concatenate(bots[2 * p:2 * p + 2], axis=0) + _dot(at_s[d, c, p], vp16)
                for hl in range(2):
                    h = 2 * p + hl
                    rows = slice(hl * CHUNK, (hl + 1) * CHUNK)
                    srows = slice(h * B_DIM, (h + 1) * B_DIM)
                    upd = _dot_tn(kd_s[d, c, h * CHUNK:(h + 1) * CHUNK, :], vp16[rows])
                    eg = jnp.tile(eg_s[d, c, h * SUBLANES:(h + 1) * SUBLANES, :], (B_DIM // SUBLANES, 1))
                    s_ref[srows, :] = s_ref[srows, :] * eg + upd
                    o_s[pl.ds(r0, CHUNK), h * B_DIM:(h + 1) * B_DIM] = o[rows]
        return carry

    lax.fori_loop(0, n_chunks, scan_step, 0)

    if not has_s0:
        sfo_ref[0, 0] = sf_s[...]
        sbo_ref[0, 0] = sb_s[...]

    for h in range(B_HEADS):
        cols = slice(h * B_DIM, (h + 1) * B_DIM)
        x = of_s[:, cols] + ob_s[:, cols]
        yn = x * lax.rsqrt(jnp.mean(x * x, axis=-1, keepdims=True) + EPS) * bng_ref[0]
        o_ref[:, cols] = yn * _silu(zb_ref[:, 3 * qk_w + h * B_DIM:3 * qk_w + (h + 1) * B_DIM])


def _delta_call(has_s0, t, n_batch, row_block0, layer, zb, zab, zabt, conv, prmr, bng, masks,
                prev=None, s0=None):
    n_tok = zb.shape[0]
    depth = conv.shape[0]
    n_chunks = t // CHUNK
    assert n_chunks % PREP_UNROLL == 0 and PREP_UNROLL % 2 == 0
    tok_spec = lambda w: pl.BlockSpec((t, w), lambda b: (row_block0 + b, 0))
    const = lambda shape: pl.BlockSpec(shape, lambda b: (0,) * len(shape))
    layer_spec = lambda shape: pl.BlockSpec((1,) + shape, lambda b: (layer,) + (0,) * len(shape))
    s_shape = (B_HEADS * B_DIM, B_DIM)
    s_spec = pl.BlockSpec((1, 1) + s_shape, lambda b: (b, layer, 0, 0))
    n_ab = zabt.shape[0]
    in_specs = [
        tok_spec(ZB_W), tok_spec(ZAB_W),
        pl.BlockSpec((n_ab, t), lambda b: (0, row_block0 + b)),
        layer_spec((3, 3 * B_HEADS * B_DIM)), layer_spec((2, n_ab, LANES)),
        layer_spec((1, B_DIM)),
        const((5 + N_LEVELS, PAIR, PAIR)),
    ]
    args = [zb, zab, zabt, conv, prmr, bng, masks]
    out_specs = [tok_spec(B_HEADS * B_DIM)]
    out_shape = [jax.ShapeDtypeStruct((n_tok, B_HEADS * B_DIM), F32)]
    if has_s0:
        in_specs += [s_spec, s_spec]
        args += [s0[0], s0[1]]
    else:
        out_specs += [s_spec, s_spec]
        out_shape += [jax.ShapeDtypeStruct((n_batch, depth) + s_shape, F32)] * 2
    n_real = len(args)
    aliases = {}
    if prev is not None:
        first_out = 0 if has_s0 else 1
        for k, arr in enumerate(prev):
            in_specs.append(pl.BlockSpec(memory_space=pl.ANY))
            args.append(arr)
            aliases[n_real + k] = first_out + k
    scratch = [
        pltpu.VMEM((t, 3 * B_HEADS * B_DIM), F32),
        pltpu.VMEM((t, B_HEADS * B_DIM), F32), pltpu.VMEM((t, B_HEADS * B_DIM), F32),
        pltpu.VMEM(s_shape, F32), pltpu.VMEM(s_shape, F32),
        pltpu.VMEM((2, n_chunks, BD, B_DIM), F32),
        pltpu.VMEM((2, n_chunks, 2 * BD, B_DIM), BF16),
        pltpu.VMEM((2, n_chunks, B_HEADS // 2, PAIR, PAIR), BF16),
        pltpu.VMEM((2, n_chunks, BD, B_DIM), BF16),
        pltpu.VMEM((2, n_chunks, B_HEADS * SUBLANES, LANES), F32),
        pltpu.VMEM((n_ab, t), F32), pltpu.VMEM((n_ab, t), F32),
        pltpu.VMEM((t, LANES), F32), pltpu.VMEM((t, LANES), F32),
    ]

    def body(*refs):
        _delta_kernel(t, has_s0, *refs[:n_real], *refs[len(args):])

    return pl.pallas_call(
        body,
        grid=(n_batch,),
        in_specs=in_specs,
        out_specs=out_specs,
        out_shape=out_shape,
        scratch_shapes=scratch,
        input_output_aliases=aliases,
        compiler_params=_params(("arbitrary",)),
        name="delta_latent" if has_s0 else "delta_prompt",
    )(*args)


def _outproj_router(x, ma, mb, mc, m, g, wo_ref, wr_ref, br):
    n = x.shape[0]
    y = (_dot(ma.astype(BF16), wo_ref[0, 0:256, :])
         + _dot(mb.astype(BF16), wo_ref[0, 256:768, :])
         + _dot(mc.astype(BF16), wo_ref[0, 768:1024, :]))
    x1 = x + m[2:3] * y
    h2 = _modulated_norm(x1, g, m[3:4], m[4:5])
    hi, lo = _split2(h2)

    hw = _dot(hi, wr_ref[0])
    logits = (hw[:, :LANES] + hw[:, LANES:] + _dot(lo, wr_ref[0, :, :LANES]) + br).T
    gl = logits[0:N_GROUPS]
    grow = lax.broadcasted_iota(jnp.int32, gl.shape, 0)
    gmax = gl.max(axis=0, keepdims=True)
    g_sel = jnp.where(gl == gmax, grow, N_GROUPS).min(axis=0, keepdims=True)
    g_w = 1.0 / jnp.exp(gl - gmax).sum(axis=0, keepdims=True)
    el = logits[EXPERT_ROW0:EXPERT_ROW0 + N_EXPERTS]
    e_idx = lax.broadcasted_iota(jnp.int32, el.shape, 0)
    el = jnp.where((e_idx // EXPERTS_PER_GROUP) == g_sel, el, -jnp.inf)
    m1 = el.max(axis=0, keepdims=True)
    i1 = jnp.where(el == m1, e_idx, N_EXPERTS).min(axis=0, keepdims=True)
    el2 = jnp.where(e_idx == i1, -jnp.inf, el)
    m2 = el2.max(axis=0, keepdims=True)
    i2 = jnp.where(el2 == m2, e_idx, N_EXPERTS).min(axis=0, keepdims=True)
    tt = jnp.exp(m2 - m1)
    w1 = g_w / (1.0 + tt)
    w2 = w1 * tt
    gate_t = jnp.where(e_idx == i1, w1, 0.0) + jnp.where(e_idx == i2, w2, 0.0)
    gate = jnp.concatenate([gate_t, jnp.zeros((LANES - N_EXPERTS, n), F32)], axis=0).T
    return x1, hi, gate


def _ffn_kernel(x_ref, ma_ref, mb_ref, mc_ref, mod_ref, g_ref, wo_ref, wr_ref, br_ref, w1_ref, w3_ref, w2_ref,
                o_ref, x1_s, h_s, gate_s, acc_s):
    j = pl.program_id(1)
    tm = x_ref.shape[0]
    th = w1_ref.shape[2]
    m = mod_ref[0, 0]

    @pl.when(j == 0)
    def _():
        sub = 1024
        for r0 in range(0, tm, sub):
            rows = slice(r0, r0 + sub)
            x1, hi, gate = _outproj_router(x_ref[rows, :], ma_ref[rows, :], mb_ref[rows, :], mc_ref[rows, :],
                                           m, g_ref[0], wo_ref, wr_ref, br_ref[0])
            x1_s[rows, :] = x1
            h_s[rows, :] = hi
            gate_s[rows, :] = gate
        acc_s[...] = jnp.zeros_like(acc_s)

    h = h_s[...]
    hid = _silu(_dot(h, w1_ref[0])) * _dot(h, w3_ref[0])
    gate = gate_s[...]
    lane = lax.broadcasted_iota(jnp.int32, gate.shape, 1)
    n_e = th // D_EXPERT
    col = lax.broadcasted_iota(jnp.int32, hid.shape, 1) // D_EXPERT
    gmat = jnp.zeros(hid.shape, F32)
    for e in range(n_e):
        ge = jnp.where(lane == j * n_e + e, gate, 0.0).sum(axis=1, keepdims=True)
        gmat = jnp.where(col == e, ge, gmat)
    acc_s[...] += _dot((hid * gmat).astype(BF16), w2_ref[0])

    @pl.when(j == pl.num_programs(1) - 1)
    def _():
        o_ref[...] = x1_s[...] + m[5:6] * acc_s[...]


def _ffn_call(layer, x, ma, mb, mc, mods, g, wo, wr, br, w1, w3, w2, slot_fn, tm, th):
    n_tok = x.shape[0]
    ef = w1.shape[2]
    tok = lambda w: pl.BlockSpec((tm, w), lambda i, j: (i, 0))
    layer_spec = lambda shape: pl.BlockSpec((1,) + shape, lambda i, j: (layer,) + (0,) * len(shape))
    return pl.pallas_call(
        _ffn_kernel,
        grid=(n_tok // tm, ef // th),
        in_specs=[tok(D_MODEL), tok(256), tok(512), tok(256),
                  pl.BlockSpec((1, 1, 6, D_MODEL), lambda i, j: (layer, slot_fn(i), 0, 0)),
                  layer_spec((1, D_MODEL)), layer_spec((D_MODEL, D_MODEL)), layer_spec((D_MODEL, 2 * LANES)),
                  layer_spec((1, LANES)),
                  pl.BlockSpec((1, D_MODEL, th), lambda i, j: (layer, 0, j)),
                  pl.BlockSpec((1, D_MODEL, th), lambda i, j: (layer, 0, j)),
                  pl.BlockSpec((1, th, D_MODEL), lambda i, j: (layer, j, 0))],
        out_specs=tok(D_MODEL),
        out_shape=jax.ShapeDtypeStruct((n_tok, D_MODEL), F32),
        scratch_shapes=[pltpu.VMEM((tm, D_MODEL), F32), pltpu.VMEM((tm, D_MODEL), BF16),
                        pltpu.VMEM((tm, LANES), F32), pltpu.VMEM((tm, D_MODEL), F32)],
        compiler_params=_params(("arbitrary", "arbitrary")),
        name="ffn",
    )(x, ma, mb, mc, mods, g, wo, wr, br, w1, w3, w2)


def _final_norm_kernel(x_ref, g_ref, o_ref):
    x = x_ref[...]
    o_ref[...] = x * lax.rsqrt(jnp.mean(x * x, axis=-1, keepdims=True) + EPS) * g_ref[...]


def _final_norm_call(x, g, tm, row0, n_rows):
    blk0 = row0 // tm
    return pl.pallas_call(
        _final_norm_kernel,
        grid=(n_rows // tm,),
        in_specs=[pl.BlockSpec((tm, D_MODEL), lambda i: (blk0 + i, 0)), pl.BlockSpec((1, D_MODEL), lambda i: (0, 0))],
        out_specs=pl.BlockSpec((tm, D_MODEL), lambda i: (i, 0)),
        out_shape=jax.ShapeDtypeStruct((n_rows, D_MODEL), F32),
        compiler_params=_params(("arbitrary",)),
        name="final_norm",
    )(x, g)


def _rope_tables(t):
    pos = np.arange(t)
    n_freq = HEAD_DIM // 4
    inv_freq = ROPE_THETA ** (-jnp.arange(n_freq, dtype=F32) / n_freq)
    row = jnp.asarray(pos // GRID_W, F32)
    col = jnp.asarray(pos % GRID_W, F32)
    ang = jnp.concatenate([row[:, None] * inv_freq, col[:, None] * inv_freq], -1)
    cos, sin = jnp.cos(ang), jnp.sin(ang)
    cos_t = jnp.tile(jnp.concatenate([cos, cos], -1), (1, LANES // HEAD_DIM))
    sin_t = jnp.tile(jnp.concatenate([-sin, sin], -1), (1, LANES // HEAD_DIM))
    return cos_t, sin_t


def _delta_tables():
    r = np.arange(PAIR)
    same = (r[:, None] // CHUNK) == (r[None, :] // CHUNK)
    low = same & (r[:, None] >= r[None, :])
    low_s = same & (r[:, None] > r[None, :])
    up = same & (r[:, None] <= r[None, :])
    up_s = same & (r[:, None] < r[None, :])
    levels = []
    for k in range(N_LEVELS):
        s = 1 << k
        levels.append(((r[:, None] // (2 * s)) == (r[None, :] // (2 * s))) & ((r[:, None] // s) != (r[None, :] // s)))
    masks = jnp.asarray(np.stack([low, low_s, up, up_s, np.eye(PAIR, dtype=bool)] + levels).astype(np.float32))
    return masks


def _segment_mean_table():
    r = np.arange(LANES)
    seg = ((r[:, None] // HEAD_DIM) == (r[None, :] // HEAD_DIM)).astype(np.float32) / HEAD_DIM
    hi = jnp.asarray(seg, BF16)
    lo = (jnp.asarray(seg) - hi.astype(F32)).astype(BF16)
    return jnp.stack([hi, lo])


def kernel(x_prompt, x_sample, cache_a_k, cache_a_v, cache_c_k, cache_c_v, state_b_fwd, state_b_bwd, c, c_ctx, w_mod, b_mod, norm1_g, norm2_g, w_in, a_sink, b_conv, b_a_log, b_dt_bias, b_norm_g, c_q_norm, c_k_norm, w_out, w_group, b_group, w_expert, b_expert, w1, w3, w2, final_norm_g):
    n_p, t_p, d = x_prompt.shape
    n_s, t_s, _ = x_sample.shape
    depth = w_in.shape[0]
    past = cache_a_k.shape[2]
    tok_p = n_p * t_p
    n_tok = tok_p + n_s * t_s
    assert d == D_MODEL and tok_p % t_s == 0 and t_s % 512 == 0 and t_p % 256 == 0

    ab0 = ZA_W + ZB_W
    w_in_p = jnp.concatenate(
        [w_in[:, :, :ab0], w_in[:, :, ab0 + N_AB:], w_in[:, :, ab0:ab0 + N_AB],
         jnp.zeros((depth, d, ZAB_W - N_AB), F32)], axis=-1).astype(BF16)
    w_out16 = w_out.astype(BF16)
    w116, w316, w216 = w1.astype(BF16), w3.astype(BF16), w2.astype(BF16)
    pad_g = jnp.zeros((depth, d, EXPERT_ROW0 - N_GROUPS), F32)
    pad_e = jnp.zeros((depth, d, LANES - EXPERT_ROW0 - N_EXPERTS), F32)
    w_r = jnp.concatenate([w_group, pad_g, w_expert, pad_e], -1)
    w_r_hi = w_r.astype(BF16)
    w_r2 = jnp.concatenate([w_r_hi, (w_r - w_r_hi.astype(F32)).astype(BF16)], axis=-1)
    b_r = jnp.concatenate([b_group, pad_g[:, 0], b_expert, pad_e[:, 0]], -1)[:, None, :]
    cqn = jnp.tile(c_q_norm, (1, 4))[:, None, :]
    ckn = jnp.tile(c_k_norm, (1, 2))[:, None, :]
    gate_prm = jnp.stack([b_a_log.reshape(depth, 8), b_dt_bias.reshape(depth, 8)], 1)
    prmr = jnp.broadcast_to(jnp.pad(gate_prm, ((0, 0), (0, 0), (0, N_AB - 8)))[..., None],
                            (depth, 2, N_AB, LANES))
    cos_t, sin_t = _rope_tables(t_s)
    masks = _delta_tables()
    seg = _segment_mean_table()

    cond = jnp.concatenate([c_ctx[None, :], c], axis=0)
    cond_b = jnp.broadcast_to(cond[:, :, None], cond.shape + (LANES,))
    mods_all = _mods_call(cond_b, w_mod, b_mod).reshape(depth, SUBLANES, 6, d)

    def slot_fn(tm):
        per_s = t_s // tm
        first = tok_p // tm
        return lambda i: jnp.where(i < first, 0, 1 + (i - first) // per_s)

    x = jnp.concatenate([x_prompt.reshape(tok_p, d), x_sample.reshape(n_s * t_s, d)], axis=0)
    blk_s = tok_p // t_s
    ctx = tuple(a.reshape(n_s, depth, past, LANES) for a in (cache_a_k, cache_a_v, cache_c_k, cache_c_v))
    s0 = tuple(a.reshape(n_s, depth, B_HEADS * B_DIM, B_DIM) for a in (state_b_fwd, state_b_bwd))
    g1, g2, bng = norm1_g[:, None, :], norm2_g[:, None, :], b_norm_g[:, None, :]
    caches = None
    states = None
    tm = 512
    for l in range(depth):
        za, zb, zc, zab, zabt = _inproj_call(l, x, mods_all, g1, w_in_p, slot_fn(tm), tm)

        ao, co, *caches = _attn_call(False, t_p, n_p, 0, l, za, zc, a_sink, cqn, ckn, seg, prev=caches)
        ao, co = _attn_call(True, t_s, n_s, blk_s, l, za, zc, a_sink, cqn, ckn, seg, prev=(ao, co),
                            rope=(cos_t, sin_t), ctx=ctx)

        bo, *states = _delta_call(False, t_p, n_p, 0, l, zb, zab, zabt, b_conv, prmr, bng, masks,
                                  prev=states)
        (bo,) = _delta_call(True, t_s, n_s, blk_s, l, zb, zab, zabt, b_conv, prmr, bng, masks,
                            prev=(bo,), s0=s0)

        x = _ffn_call(l, x, ao, bo, co, mods_all, g2, w_out16, w_r2, b_r, w116, w316, w216, slot_fn(1024), 1024, 512)

    y_prompt = _final_norm_call(x, final_norm_g[None], tm, 0, tok_p).reshape(n_p, t_p, d)
    y_sample = _final_norm_call(x, final_norm_g[None], tm, tok_p, n_s * t_s).reshape(n_s, t_s, d)
    new_ak, new_av, new_ck, new_cv = (a.reshape(n_p, depth, t_p, 2, HEAD_DIM) for a in caches)
    new_sf, new_sb = (a.reshape(n_p, depth, B_HEADS, B_DIM, B_DIM) for a in states)
    return (y_prompt, y_sample, new_ak, new_av, new_ck, new_cv, new_sf, new_sb)
```

```python
import functools

import jax
import jax.numpy as jnp
import numpy as np
from jax import lax
from jax.experimental import pallas as pl
from jax.experimental.pallas import tpu as pltpu

F32 = jnp.float32
BF16 = jnp.bfloat16

D_MODEL = 1024
DEPTH = 4
GRID_W = 64
EPS = 1e-6
NEG_INF = -1e30
ROPE_THETA = 10000.0
HEAD_DIM = 64
N_Q_HEADS = 4
WINDOW = 128
Q_BLOCK = 128
B_HEADS = 4
B_DIM = 128
CHUNK = 64
BD = B_HEADS * CHUNK
PAIR = 2 * CHUNK
N_LEVELS = 6
PREP_UNROLL = 4
DELTA_NSEQ = 2
N_GROUPS = 4
EXPERTS_PER_GROUP = 4
N_EXPERTS = 16
D_EXPERT = 256
EXPERT_ROW0 = 8

LANES = 128
SUBLANES = 8
VMEM_LIMIT = 60000 * 1024

ZA_W, ZB_W, ZC_W, ZAB_W = 512, 2048, 512, 128
N_AB = 16
Z_W = ZA_W + ZB_W + ZC_W + ZAB_W


def _sigmoid(x):
    return 1.0 / (1.0 + jnp.exp(-x))


def _silu(x):
    return x * _sigmoid(x)


def _softplus(x):
    return jnp.maximum(x, 0.0) + jnp.log1p(jnp.exp(-jnp.abs(x)))


def _dot(a, b):
    return jnp.dot(a, b, preferred_element_type=F32)


def _dot_nt(a, b):
    return lax.dot_general(a, b, (((1,), (1,)), ((), ())), preferred_element_type=F32)


def _dot_tn(a, b):
    return lax.dot_general(a, b, (((0,), (0,)), ((), ())), preferred_element_type=F32)


def _split2(x):
    hi = x.astype(BF16)
    lo = (x - hi.astype(F32)).astype(BF16)
    return hi, lo


def _params(sem=None):
    return pltpu.CompilerParams(dimension_semantics=sem, vmem_limit_bytes=VMEM_LIMIT)


def _mods_kernel(cond_ref, w_ref, b_ref, o_ref, act_s):
    n_cond = cond_ref.shape[0]
    tn = w_ref.shape[2]
    reps = tn // LANES

    @pl.when((pl.program_id(0) == 0) & (pl.program_id(1) == 0))
    def _():
        act_s[...] = _silu(cond_ref[...])

    def body(kb, accs):
        r = pl.multiple_of(kb * SUBLANES, SUBLANES)
        w = w_ref[0, pl.ds(r, SUBLANES), :]
        return tuple(acc + jnp.tile(act_s[m, pl.ds(r, SUBLANES), :], (1, reps)) * w for m, acc in enumerate(accs))

    zero = jnp.zeros((SUBLANES, tn), F32)
    accs = lax.fori_loop(0, w_ref.shape[1] // SUBLANES, body, (zero,) * n_cond, unroll=4)
    rows = [jnp.sum(a, axis=0, keepdims=True) + b_ref[0] for a in accs]
    rows.append(jnp.zeros((SUBLANES - n_cond, tn), F32))
    o_ref[0] = jnp.concatenate(rows, axis=0)


def _mods_call(cond_b, w_mod, b_mod):
    depth, d, n = w_mod.shape
    tn = 1536
    n_cond = cond_b.shape[0]
    return pl.pallas_call(
        _mods_kernel,
        grid=(depth, n // tn),
        in_specs=[
            pl.BlockSpec((n_cond, d, LANES), lambda l, j: (0, 0, 0)),
            pl.BlockSpec((1, d, tn), lambda l, j: (l, 0, j)),
            pl.BlockSpec((1, 1, tn), lambda l, j: (l, 0, j)),
        ],
        out_specs=pl.BlockSpec((1, SUBLANES, tn), lambda l, j: (l, 0, j)),
        out_shape=jax.ShapeDtypeStruct((depth, SUBLANES, n), F32),
        scratch_shapes=[pltpu.VMEM((n_cond, d, LANES), F32)],
        compiler_params=_params(("arbitrary", "arbitrary")),
        name="mods",
    )(cond_b, w_mod, b_mod.reshape(depth, 1, n))


def _modulated_norm(x, g, shift, scale):
    ms = jnp.mean(x * x, axis=-1, keepdims=True)
    y = x * lax.rsqrt(ms + EPS) * g
    return y * (1.0 + scale) + shift


def _inproj_kernel(x_ref, mod_ref, g_ref, w_ref, za_ref, zb_ref, zc_ref, zab_ref, zabt_ref, w_s):
    @pl.when(pl.program_id(0) == 0)
    def _():
        ab0 = ZA_W + ZB_W
        rows_per = 256
        for r0 in range(0, D_MODEL, rows_per):
            rows = slice(r0, r0 + rows_per)
            w_s[rows, 0:ab0] = w_ref[0, rows, 0:ab0].astype(BF16)
            w_s[rows, ab0:ab0 + ZC_W] = w_ref[0, rows, ab0 + N_AB:ab0 + N_AB + ZC_W].astype(BF16)
            gates = jnp.concatenate([w_ref[0, rows, ab0:ab0 + N_AB], jnp.zeros((rows_per, ZAB_W - N_AB), F32)], axis=1)
            w_s[rows, ab0 + ZC_W:Z_W] = gates.astype(BF16)

    m = mod_ref[0, 0]
    h = _modulated_norm(x_ref[...], g_ref[0], m[0:1], m[1:2]).astype(BF16)
    za_ref[...] = _dot(h, w_s[:, 0:ZA_W])
    step = 512
    for j in range(ZB_W // step):
        zb_ref[:, j * step:(j + 1) * step] = _dot(h, w_s[:, ZA_W + j * step:ZA_W + (j + 1) * step])
    zc_ref[...] = _dot(h, w_s[:, ZA_W + ZB_W:ZA_W + ZB_W + ZC_W])
    zab = _dot(h, w_s[:, ZA_W + ZB_W + ZC_W:Z_W])
    zab_ref[...] = zab
    zabt_ref[...] = zab.T[:N_AB]


def _inproj_call(layer, x, mods, g, w, slot_fn, tm):
    n_tok = x.shape[0]
    n_ab = N_AB
    return pl.pallas_call(
        _inproj_kernel,
        grid=(n_tok // tm,),
        in_specs=[
            pl.BlockSpec((tm, D_MODEL), lambda i: (i, 0)),
            pl.BlockSpec((1, 1, 6, D_MODEL), lambda i: (layer, slot_fn(i), 0, 0)),
            pl.BlockSpec((1, 1, D_MODEL), lambda i: (layer, 0, 0)),
            pl.BlockSpec((1, D_MODEL, w.shape[2]), lambda i: (layer, 0, 0)),
        ],
        out_specs=[
            pl.BlockSpec((tm, ZA_W), lambda i: (i, 0)),
            pl.BlockSpec((tm, ZB_W), lambda i: (i, 0)),
            pl.BlockSpec((tm, ZC_W), lambda i: (i, 0)),
            pl.BlockSpec((tm, ZAB_W), lambda i: (i, 0)),
            pl.BlockSpec((n_ab, tm), lambda i: (0, i)),
        ],
        out_shape=[
            jax.ShapeDtypeStruct((n_tok, ZA_W), F32),
            jax.ShapeDtypeStruct((n_tok, ZB_W), F32),
            jax.ShapeDtypeStruct((n_tok, ZC_W), F32),
            jax.ShapeDtypeStruct((n_tok, ZAB_W), F32),
            jax.ShapeDtypeStruct((n_ab, n_tok), F32),
        ],
        scratch_shapes=[pltpu.VMEM((D_MODEL, Z_W), BF16)],
        compiler_params=_params(("arbitrary",)),
        name="inproj",
    )(x, mods, g, w)


def _lane_lo(shape):
    return lax.broadcasted_iota(jnp.int32, shape, len(shape) - 1) % LANES < HEAD_DIM


def _store_kdup(dst_ref, off, k):
    n = k.shape[0]
    r = pltpu.roll(k, HEAD_DIM, 1)
    lo = _lane_lo(k.shape)
    dst_ref[0, off:off + n, :] = jnp.where(lo, k, r).astype(BF16)
    dst_ref[1, off:off + n, :] = jnp.where(lo, r, k).astype(BF16)


def _store_vsplit(dst_ref, off, v):
    n = v.shape[0]
    r = pltpu.roll(v, HEAD_DIM, 1)
    lo = _lane_lo(v.shape)
    z = jnp.zeros_like(v)
    dst_ref[0, off:off + n, :] = jnp.where(lo, v, z).astype(BF16)
    dst_ref[1, off:off + n, :] = jnp.where(lo, z, r).astype(BF16)
    dst_ref[2, off:off + n, :] = jnp.where(lo, r, z).astype(BF16)
    dst_ref[3, off:off + n, :] = jnp.where(lo, z, v).astype(BF16)


def _rope(x, cos, sin):
    first = (lax.broadcasted_iota(jnp.int32, x.shape, 1) // (HEAD_DIM // 2)) % 2 == 0
    partner = jnp.where(first, pltpu.roll(x, LANES - HEAD_DIM // 2, 1), pltpu.roll(x, HEAD_DIM // 2, 1))
    return x * cos + partner * sin


def _head_rmsnorm(x, g, seg_hi, seg_lo):
    hi, lo = _split2(x * x)
    ms = _dot(hi, seg_hi) + _dot(lo, seg_hi) + _dot(hi, seg_lo)
    return x * lax.rsqrt(ms + EPS) * g


def _attend_many(units):
    qb = units[0][0].shape[0]
    lo = _lane_lo(units[0][0].shape)
    all_scores = []
    for qt, segs, _ in units:
        z = jnp.zeros_like(qt)
        qs = jnp.concatenate([jnp.where(lo, qt, z), jnp.where(lo, z, qt)], axis=0).astype(BF16)
        scores = []
        for kdup, _, _, mask in segs:
            s = _dot_nt(qs, kdup)
            if mask is not None:
                s = jnp.where(mask, s, NEG_INF)
            scores.append(s)
        all_scores.append(scores)
    probs = []
    for (qt, segs, sink_pair), scores in zip(units, all_scores):
        m = scores[0].max(axis=1, keepdims=True)
        for s in scores[1:]:
            m = jnp.maximum(m, s.max(axis=1, keepdims=True))
        if sink_pair is not None:
            row_a = lax.broadcasted_iota(jnp.int32, (2 * qb, 1), 0) < qb
            sink = jnp.where(row_a, sink_pair[0], sink_pair[1])
            m = jnp.maximum(m, sink)
            denom = jnp.exp(sink - m)
        else:
            denom = jnp.zeros((2 * qb, 1), F32)
        ps = []
        for s in scores:
            p = jnp.exp(s - m)
            denom = denom + p.sum(axis=1, keepdims=True)
            ps.append(p.astype(BF16))
        probs.append((ps, 1.0 / denom))
    outs = []
    for (qt, segs, _), (ps, inv) in zip(units, probs):
        acc = jnp.zeros((qb, LANES), F32)
        for pb, (_, vlo, vhi, _) in zip(ps, segs):
            acc = acc + _dot(pb[:qb], vlo) + _dot(pb[qb:], vhi)
        outs.append(acc * jnp.where(lo, inv[:qb], inv[qb:]))
    return outs


def _attn_kernel(has_ctx, t, layer, *refs):
    if has_ctx:
        (sink_ref, za_ref, zc_ref, cqn_ref, ckn_ref, seg_ref, cos_ref, sin_ref,
         cak_ref, cav_ref, cck_ref, ccv_ref,
         ao_ref, co_ref,
         ka_s, va_s, kc_s, vc_s, kctx_s, vctx_s, qa_s, qc_s) = refs
    else:
        (sink_ref, za_ref, zc_ref, cqn_ref, ckn_ref, seg_ref,
         ao_ref, co_ref, nak_ref, nav_ref, nck_ref, ncv_ref,
         ka_s, va_s, kc_s, vc_s, qa_s, qc_s) = refs
    scale = HEAD_DIM ** -0.5
    seg_hi = seg_ref[0]
    seg_lo = seg_ref[1]
    piece = 256
    n_ctx = cak_ref.shape[2] if has_ctx else 0

    for p0 in range(0, t, piece):
        rows = slice(p0, p0 + piece)
        ak = za_ref[rows, 256:384]
        av = za_ref[rows, 384:512]
        ck = _head_rmsnorm(zc_ref[rows, 256:384], ckn_ref[0], seg_hi, seg_lo)
        cv = zc_ref[rows, 384:512]
        if has_ctx:
            cos = cos_ref[rows, :]
            sin = sin_ref[rows, :]
            ak = _rope(ak, cos, sin)
            ck = _rope(ck, cos, sin)
            _store_kdup(ka_s, WINDOW + p0, ak)
            _store_vsplit(va_s, WINDOW + p0, av)
            _store_kdup(kc_s, n_ctx + p0, ck)
            _store_vsplit(vc_s, n_ctx + p0, cv)
        else:
            nak_ref[0, 0, rows, :] = ak
            nav_ref[0, 0, rows, :] = av
            nck_ref[0, 0, rows, :] = ck
            ncv_ref[0, 0, rows, :] = cv
            _store_kdup(ka_s, p0, ak)
            _store_vsplit(va_s, p0, av)
            _store_kdup(kc_s, p0, ck)
            _store_vsplit(vc_s, p0, cv)
        for hk in range(2):
            cols = slice(hk * LANES, (hk + 1) * LANES)
            aq = za_ref[rows, cols]
            cq = _head_rmsnorm(zc_ref[rows, cols], cqn_ref[0, :, cols], seg_hi, seg_lo)
            if has_ctx:
                aq = _rope(aq, cos, sin)
                cq = _rope(cq, cos, sin)
            qa_s[rows, cols] = aq * scale
            qc_s[rows, cols] = cq * scale

    if has_ctx:
        zpad = jnp.zeros((WINDOW, LANES), BF16)
        for i in range(2):
            ka_s[i, 0:WINDOW, :] = zpad
            ka_s[i, WINDOW + t:2 * WINDOW + t, :] = zpad
        for i in range(4):
            va_s[i, 0:WINDOW, :] = zpad
            va_s[i, WINDOW + t:2 * WINDOW + t, :] = zpad
        for p0 in range(0, n_ctx, piece):
            rows = slice(p0, p0 + piece)
            _store_kdup(kctx_s, p0, cak_ref[0, 0, rows, :])
            _store_vsplit(vctx_s, p0, cav_ref[0, 0, rows, :])
            _store_kdup(kc_s, p0, cck_ref[0, 0, rows, :])
            _store_vsplit(vc_s, p0, ccv_ref[0, 0, rows, :])

        qb = Q_BLOCK
        span = 3 * qb
        qi = lax.broadcasted_iota(jnp.int32, (2 * qb, span), 0) % qb
        kj = lax.broadcasted_iota(jnp.int32, (2 * qb, span), 1)
        band = jnp.abs(kj - qb - qi) <= WINDOW

        def block(b, carry):
            r0 = pl.multiple_of(b * qb, qb)
            kpos = kj + (r0 - qb)
            mask = band & (kpos >= 0) & (kpos < t)
            units = []
            for hk in range(2):
                cols = slice(hk * LANES, (hk + 1) * LANES)
                segs_a = [
                    (kctx_s[hk], vctx_s[2 * hk], vctx_s[2 * hk + 1], None),
                    (ka_s[hk, pl.ds(r0, span), :], va_s[2 * hk, pl.ds(r0, span), :],
                     va_s[2 * hk + 1, pl.ds(r0, span), :], mask),
                ]
                sinks = (sink_ref[layer, 2 * hk], sink_ref[layer, 2 * hk + 1])
                units.append((qa_s[pl.ds(r0, qb), cols], segs_a, sinks))
                segs_c = [(kc_s[hk], vc_s[2 * hk], vc_s[2 * hk + 1], None)]
                units.append((qc_s[pl.ds(r0, qb), cols], segs_c, None))
            outs = _attend_many(units)
            for hk in range(2):
                cols = slice(hk * LANES, (hk + 1) * LANES)
                ao_ref[pl.ds(r0, qb), cols] = outs[2 * hk]
                co_ref[pl.ds(r0, qb), cols] = outs[2 * hk + 1]
            return carry

        lax.fori_loop(0, t // qb, block, 0)
    else:
        units = []
        for hk in range(2):
            cols = slice(hk * LANES, (hk + 1) * LANES)
            sinks = (sink_ref[layer, 2 * hk], sink_ref[layer, 2 * hk + 1])
            units.append((qa_s[:, cols], [(ka_s[hk], va_s[2 * hk], va_s[2 * hk + 1], None)], sinks))
            units.append((qc_s[:, cols], [(kc_s[hk], vc_s[2 * hk], vc_s[2 * hk + 1], None)], None))
        outs = _attend_many(units)
        for hk in range(2):
            cols = slice(hk * LANES, (hk + 1) * LANES)
            ao_ref[:, cols] = outs[2 * hk]
            co_ref[:, cols] = outs[2 * hk + 1]


def _attn_call(has_ctx, t, n_batch, row_block0, layer, za, zc, sink, cqn, ckn, seg, prev=None, rope=None, ctx=None):
    n_tok = za.shape[0]
    depth = sink.shape[0]
    tok_spec = lambda w: pl.BlockSpec((t, w), lambda b, *_: (row_block0 + b, 0))
    const = lambda shape: pl.BlockSpec(shape, lambda b, *_: (0,) * len(shape))
    layer_spec = lambda shape: pl.BlockSpec((1,) + shape, lambda b, *_: (layer,) + (0,) * len(shape))
    in_specs = [tok_spec(ZA_W), tok_spec(ZC_W), layer_spec((1, 256)), layer_spec((1, 128)), const((2, LANES, LANES))]
    args = [za, zc, cqn, ckn, seg]
    out_specs = [tok_spec(256), tok_spec(256)]
    out_shape = [jax.ShapeDtypeStruct((n_tok, 256), F32), jax.ShapeDtypeStruct((n_tok, 256), F32)]
    if has_ctx:
        n_ctx = ctx[0].shape[2]
        in_specs += [const((t, LANES)), const((t, LANES))]
        args += list(rope)
        in_specs += [pl.BlockSpec((1, 1, n_ctx, LANES), lambda b, *_: (b, layer, 0, 0))] * 4
        args += list(ctx)
        scratch = [
            pltpu.VMEM((2, t + 2 * WINDOW, LANES), BF16), pltpu.VMEM((4, t + 2 * WINDOW, LANES), BF16),
            pltpu.VMEM((2, n_ctx + t, LANES), BF16), pltpu.VMEM((4, n_ctx + t, LANES), BF16),
            pltpu.VMEM((2, n_ctx, LANES), BF16), pltpu.VMEM((4, n_ctx, LANES), BF16),
            pltpu.VMEM((t, 256), F32), pltpu.VMEM((t, 256), F32),
        ]
    else:
        cache_spec = pl.BlockSpec((1, 1, t, LANES), lambda b, *_: (b, layer, 0, 0))
        out_specs += [cache_spec] * 4
        out_shape += [jax.ShapeDtypeStruct((n_batch, depth, t, LANES), F32)] * 4
        scratch = [
            pltpu.VMEM((2, t, LANES), BF16), pltpu.VMEM((4, t, LANES), BF16),
            pltpu.VMEM((2, t, LANES), BF16), pltpu.VMEM((4, t, LANES), BF16),
            pltpu.VMEM((t, 256), F32), pltpu.VMEM((t, 256), F32),
        ]
    n_real = len(args)
    aliases = {}
    if prev is not None:
        first_out = 0 if has_ctx else 2
        for k, arr in enumerate(prev):
            in_specs.append(pl.BlockSpec(memory_space=pl.ANY))
            args.append(arr)
            aliases[1 + n_real + k] = first_out + k

    def body(*refs):
        ins = refs[:1 + n_real]
        rest = refs[1 + len(args):]
        _attn_kernel(has_ctx, t, layer, *ins, *rest)

    return pl.pallas_call(
        body,
        grid_spec=pltpu.PrefetchScalarGridSpec(
            num_scalar_prefetch=1, grid=(n_batch,), in_specs=in_specs, out_specs=out_specs,
            scratch_shapes=scratch),
        out_shape=out_shape,
        input_output_aliases=aliases,
        compiler_params=_params(("arbitrary",)),
        name="attn_latent" if has_ctx else "attn_prompt",
    )(sink, *args)


def _stack_pair(x, p):
    return jnp.concatenate([x[:, (2 * p + hl) * B_DIM:(2 * p + hl + 1) * B_DIM] for hl in range(2)], axis=0)


def _delta_kernel(t, nseq, has_s0, *refs):
    if has_s0:
        (zb_ref, abc_ref, abt_ref, conv_ref, prmr_ref, bng_ref, mask_ref,
         s0f_ref, s0b_ref, o_ref, qkv_s, of_s, ob_s, sf_s, sb_s, u_s, wq_s, at_s, kd_s, eg_s,
         pre_s, suf_s, prec_s, sufc_s) = refs
    else:
        (zb_ref, abc_ref, abt_ref, conv_ref, prmr_ref, bng_ref, mask_ref,
         o_ref, sfo_ref, sbo_ref, qkv_s, of_s, ob_s, sf_s, sb_s, u_s, wq_s, at_s, kd_s, eg_s,
         pre_s, suf_s, prec_s, sufc_s) = refs
    n_chunks = t // CHUNK
    n_total = nseq * n_chunks
    s_rows = B_HEADS * B_DIM
    qk_w = B_HEADS * B_DIM

    row = lax.broadcasted_iota(jnp.int32, (t, LANES), 0)
    for q in range(nseq):
        seq = slice(q * t, (q + 1) * t)
        for j in range(3 * B_HEADS):
            cols = slice(j * LANES, (j + 1) * LANES)
            x = zb_ref[seq, cols]
            prev = jnp.where(row == 0, 0.0, pltpu.roll(x, 1, 0))
            nxt = jnp.where(row == t - 1, 0.0, pltpu.roll(x, t - 1, 0))
            y = _silu(prev * conv_ref[0, 0:1, cols] + x * conv_ref[0, 1:2, cols] + nxt * conv_ref[0, 2:3, cols])
            if j < 2 * B_HEADS:
                y = y * lax.rsqrt(jnp.sum(y * y, axis=-1, keepdims=True) + EPS)
            if j < B_HEADS:
                y = y * (B_DIM ** -0.5)
            qkv_s[seq, cols] = y

    if has_s0:
        for q in range(nseq):
            sf_s[q * s_rows:(q + 1) * s_rows, :] = s0f_ref[q, 0]
            sb_s[q * s_rows:(q + 1) * s_rows, :] = s0b_ref[q, 0]
    else:
        sf_s[...] = jnp.zeros_like(sf_s)
        sb_s[...] = jnp.zeros_like(sb_s)

    reps = nseq * t // LANES
    gr = -jnp.tile(jnp.exp(prmr_ref[0, 0]), (1, reps)) * _softplus(abt_ref[...] + jnp.tile(prmr_ref[0, 1], (1, reps)))
    seg_lane = lax.broadcasted_iota(jnp.int32, gr.shape, 1) % CHUNK
    pre, suf = gr, gr
    for s in (1, 2, 4, 8, 16, 32):
        pre = pre + jnp.where(seg_lane >= s, pltpu.roll(pre, s, 1), 0.0)
        suf = suf + jnp.where(seg_lane < CHUNK - s, pltpu.roll(suf, nseq * t - s, 1), 0.0)
    pre_s[...] = pre
    suf_s[...] = suf
    zrows = jnp.zeros((LANES - pre.shape[0], LANES), F32)
    for j in range(reps):
        tile = slice(j * LANES, (j + 1) * LANES)
        prec_s[tile, :] = jnp.concatenate([pre[:, tile], zrows], axis=0).T
        sufc_s[tile, :] = jnp.concatenate([suf[:, tile], zrows], axis=0).T
    lane_lo = lax.broadcasted_iota(jnp.int32, (1, LANES), 1) < CHUNK

    def prepare(cc, carry):
        chains = []
        for k in range(PREP_UNROLL):
            c = cc * PREP_UNROLL + k
            r0 = pl.multiple_of(c * CHUNK, CHUNK)
            b_all = _sigmoid(abc_ref[pl.ds(r0, CHUNK), :])
            run_c = (prec_s[pl.ds(r0, CHUNK), :], sufc_s[pl.ds(r0, CHUNK), :])
            tile0 = pl.multiple_of((cc * PREP_UNROLL + k - k % 2) * CHUNK, LANES)
            run = (pre_s[:, pl.ds(tile0, LANES)], suf_s[:, pl.ds(tile0, LANES)])
            run_r = tuple(pltpu.roll(x, CHUNK, 1) for x in run)
            for p in range(B_HEADS // 2):
                kst = _stack_pair(qkv_s[pl.ds(r0, CHUNK), qk_w:2 * qk_w], p)
                qst = _stack_pair(qkv_s[pl.ds(r0, CHUNK), 0:qk_w], p)
                vst = _stack_pair(qkv_s[pl.ds(r0, CHUNK), 2 * qk_w:3 * qk_w], p)
                kq = _dot_nt(jnp.concatenate([kst, qst], axis=0).astype(BF16), kst.astype(BF16))
                for d in range(2):
                    cg = 4 * d + 2 * p
                    edge = CHUNK - 1 if d == 0 else 0
                    rep_col = lambda x, col: jnp.broadcast_to(x[:, col:col + 1], (CHUNK, LANES))
                    b_rep = jnp.concatenate([rep_col(b_all, 8 + cg + hl) for hl in range(2)], axis=0)
                    gcol = jnp.concatenate([rep_col(run_c[d], cg + hl) for hl in range(2)], axis=0)
                    gtot = jnp.concatenate([rep_col(run_c[d][edge:edge + 1], cg + hl) for hl in range(2)], axis=0)
                    ra = cg
                    if k % 2 == 0:
                        grow = jnp.where(lane_lo, run[d][ra:ra + 1], run_r[d][ra + 1:ra + 2])
                    else:
                        grow = jnp.where(lane_lo, run_r[d][ra:ra + 1], run[d][ra + 1:ra + 2])
                    chains.append(dict(c=c, p=p, d=d, kst=kst, qst=qst, vst=vst, kq=kq, b_st=b_rep,
                                       gcol=gcol, gtot=gtot, grow=grow))

        for ch in chains:
            d, b_st, kq, gcol = ch["d"], ch["b_st"], ch.pop("kq"), ch["gcol"]
            decay = jnp.exp(jnp.minimum(gcol - ch.pop("grow"), 0.0))
            ch["a_mat"] = (b_st * kq[:PAIR]) * (decay * mask_ref[2 * d + 1])
            ch["attn"] = (kq[PAIR:] * (decay * mask_ref[2 * d])).astype(BF16)
            ch["t_inv"] = mask_ref[4] - ch["a_mat"] * mask_ref[5]
        for lvl in range(N_LEVELS - 1):
            for ch in chains:
                ch["t16"] = ch["t_inv"].astype(BF16)
                ch["et"] = _dot((ch["a_mat"] * mask_ref[6 + lvl]).astype(BF16), ch["t16"])
            for ch in chains:
                ch["t_inv"] = ch["t_inv"] - _dot(ch.pop("t16"), ch.pop("et").astype(BF16))
        for ch in chains:
            egc = jnp.exp(ch["gcol"])
            rk = jnp.concatenate([ch["b_st"] * ch["vst"], (ch["b_st"] * egc) * ch["kst"]], axis=1)
            ch["rk"] = _dot(ch.pop("t_inv").astype(BF16), rk.astype(BF16))
            ch["qp16"] = (ch["qst"] * egc).astype(BF16)
        for ch in chains:
            c, p, d, rk, qp16 = ch["c"], ch["p"], ch["d"], ch["rk"], ch["qp16"]
            pair_rows = slice(p * PAIR, (p + 1) * PAIR)
            w16 = rk[:, B_DIM:].astype(BF16)
            u_s[d, c, pair_rows, :] = rk[:, :B_DIM]
            at_s[d, c, p] = ch["attn"]
            kd_s[d, c, pair_rows, :] = (ch["kst"] * jnp.exp(ch["gtot"] - ch["gcol"])).astype(BF16)
            eg = jnp.exp(ch["gtot"])
            for hl in range(2):
                h = 2 * p + hl
                rows = slice(hl * CHUNK, (hl + 1) * CHUNK)
                wq_s[d, c, h * 2 * CHUNK:h * 2 * CHUNK + CHUNK, :] = w16[rows]
                wq_s[d, c, h * 2 * CHUNK + CHUNK:(h + 1) * 2 * CHUNK, :] = qp16[rows]
                eg_s[d, c, h * SUBLANES:(h + 1) * SUBLANES, :] = eg[hl * CHUNK:hl * CHUNK + SUBLANES, :]
        return carry

    lax.fori_loop(0, n_total // PREP_UNROLL, prepare, 0)

    def scan_step(i, carry):
        units = []
        for q in range(nseq):
            for d, s_ref, o_s in ((0, sf_s, of_s), (1, sb_s, ob_s)):
                c = q * n_chunks + (i if d == 0 else n_chunks - 1 - i)
                units.append(dict(q=q, d=d, c=c, s_ref=s_ref, o_s=o_s, r0=pl.multiple_of(c * CHUNK, CHUNK)))
        for un in units:
            q, d, c, s_ref = un["q"], un["d"], un["c"], un["s_ref"]
            un["x"] = []
            for h in range(B_HEADS):
                srows = slice(q * s_rows + h * B_DIM, q * s_rows + (h + 1) * B_DIM)
                un["x"].append(_dot(wq_s[d, c, h * 2 * CHUNK:(h + 1) * 2 * CHUNK, :], s_ref[srows, :].astype(BF16)))
        for un in units:
            d, c = un["d"], un["c"]
            un["vp16"], un["o"] = [], []
            for p in range(B_HEADS // 2):
                xs = un["x"][2 * p:2 * p + 2]
                v_new = jnp.concatenate(
                    [u_s[d, c, (2 * p + hl) * CHUNK:(2 * p + hl + 1) * CHUNK, :] - xs[hl][:CHUNK] for hl in range(2)],
                    axis=0)
                vp16 = v_new.astype(BF16)
                un["vp16"].append(vp16)
                un["o"].append(jnp.concatenate([xs[hl][CHUNK:] for hl in range(2)], axis=0)
                               + _dot(at_s[d, c, p], vp16))
        for un in units:
            q, d, c, s_ref, o_s, r0 = un["q"], un["d"], un["c"], un["s_ref"], un["o_s"], un["r0"]
            for h in range(B_HEADS):
                p, hl = divmod(h, 2)
                rows = slice(hl * CHUNK, (hl + 1) * CHUNK)
                srows = slice(q * s_rows + h * B_DIM, q * s_rows + (h + 1) * B_DIM)
                upd = _dot_tn(kd_s[d, c, h * CHUNK:(h + 1) * CHUNK, :], un["vp16"][p][rows])
                eg = jnp.tile(eg_s[d, c, h * SUBLANES:(h + 1) * SUBLANES, :], (B_DIM // SUBLANES, 1))
                s_ref[srows, :] = s_ref[srows, :] * eg + upd
                o_s[pl.ds(r0, CHUNK), h * B_DIM:(h + 1) * B_DIM] = un["o"][p][rows]
        return carry

    lax.fori_loop(0, n_chunks, scan_step, 0)

    if not has_s0:
        for q in range(nseq):
            sfo_ref[q, 0] = sf_s[q * s_rows:(q + 1) * s_rows, :]
            sbo_ref[q, 0] = sb_s[q * s_rows:(q + 1) * s_rows, :]

    for h in range(B_HEADS):
        cols = slice(h * B_DIM, (h + 1) * B_DIM)
        x = of_s[:, cols] + ob_s[:, cols]
        yn = x * lax.rsqrt(jnp.mean(x * x, axis=-1, keepdims=True) + EPS) * bng_ref[0]
        o_ref[:, cols] = yn * _silu(zb_ref[:, 3 * qk_w + h * B_DIM:3 * qk_w + (h + 1) * B_DIM])


def _delta_call(has_s0, t, nseq, n_batch, row_block0, layer, zb, zab, zabt, conv, prmr, bng, masks,
                prev=None, s0=None):
    n_tok = zb.shape[0]
    depth = conv.shape[0]
    n_chunks = nseq * (t // CHUNK)
    assert (t // CHUNK) % PREP_UNROLL == 0 and PREP_UNROLL % 2 == 0 and n_batch % nseq == 0
    tok_spec = lambda w: pl.BlockSpec((nseq * t, w), lambda b: (row_block0 + b, 0))
    const = lambda shape: pl.BlockSpec(shape, lambda b: (0,) * len(shape))
    layer_spec = lambda shape: pl.BlockSpec((1,) + shape, lambda b: (layer,) + (0,) * len(shape))
    s_shape = (B_HEADS * B_DIM, B_DIM)
    s_spec = pl.BlockSpec((nseq, 1) + s_shape, lambda b: (b, layer, 0, 0))
    n_ab = zabt.shape[0]
    in_specs = [
        tok_spec(ZB_W), tok_spec(ZAB_W),
        pl.BlockSpec((n_ab, nseq * t), lambda b: (0, row_block0 + b)),
        layer_spec((3, 3 * B_HEADS * B_DIM)), layer_spec((2, n_ab, LANES)),
        layer_spec((1, B_DIM)),
        const((5 + N_LEVELS, PAIR, PAIR)),
    ]
    args = [zb, zab, zabt, conv, prmr, bng, masks]
    out_specs = [tok_spec(B_HEADS * B_DIM)]
    out_shape = [jax.ShapeDtypeStruct((n_tok, B_HEADS * B_DIM), F32)]
    if has_s0:
        in_specs += [s_spec, s_spec]
        args += [s0[0], s0[1]]
    else:
        out_specs += [s_spec, s_spec]
        out_shape += [jax.ShapeDtypeStruct((n_batch, depth) + s_shape, F32)] * 2
    n_real = len(args)
    aliases = {}
    if prev is not None:
        first_out = 0 if has_s0 else 1
        for k, arr in enumerate(prev):
            in_specs.append(pl.BlockSpec(memory_space=pl.ANY))
            args.append(arr)
            aliases[n_real + k] = first_out + k
    rows = nseq * t
    scratch = [
        pltpu.VMEM((rows, 3 * B_HEADS * B_DIM), F32),
        pltpu.VMEM((rows, B_HEADS * B_DIM), F32), pltpu.VMEM((rows, B_HEADS * B_DIM), F32),
        pltpu.VMEM((nseq * s_shape[0], B_DIM), F32), pltpu.VMEM((nseq * s_shape[0], B_DIM), F32),
        pltpu.VMEM((2, n_chunks, BD, B_DIM), F32),
        pltpu.VMEM((2, n_chunks, 2 * BD, B_DIM), BF16),
        pltpu.VMEM((2, n_chunks, B_HEADS // 2, PAIR, PAIR), BF16),
        pltpu.VMEM((2, n_chunks, BD, B_DIM), BF16),
        pltpu.VMEM((2, n_chunks, B_HEADS * SUBLANES, LANES), F32),
        pltpu.VMEM((n_ab, rows), F32), pltpu.VMEM((n_ab, rows), F32),
        pltpu.VMEM((rows, LANES), F32), pltpu.VMEM((rows, LANES), F32),
    ]

    def body(*refs):
        _delta_kernel(t, nseq, has_s0, *refs[:n_real], *refs[len(args):])

    return pl.pallas_call(
        body,
        grid=(n_batch // nseq,),
        in_specs=in_specs,
        out_specs=out_specs,
        out_shape=out_shape,
        scratch_shapes=scratch,
        input_output_aliases=aliases,
        compiler_params=_params(("arbitrary",)),
        name="delta_latent" if has_s0 else "delta_prompt",
    )(*args)


def _outproj_router(x, ma, mb, mc, m, g, wo_ref, wr_ref, br):
    n = x.shape[0]
    y = (_dot(ma.astype(BF16), wo_ref[0, 0:256, :])
         + _dot(mb.astype(BF16), wo_ref[0, 256:768, :])
         + _dot(mc.astype(BF16), wo_ref[0, 768:1024, :]))
    x1 = x + m[2:3] * y
    h2 = _modulated_norm(x1, g, m[3:4], m[4:5])
    hi, lo = _split2(h2)

    hw = _dot(hi, wr_ref[0])
    logits = (hw[:, :LANES] + hw[:, LANES:] + _dot(lo, wr_ref[0, :, :LANES]) + br).T
    gl = logits[0:N_GROUPS]
    grow = lax.broadcasted_iota(jnp.int32, gl.shape, 0)
    gmax = gl.max(axis=0, keepdims=True)
    g_sel = jnp.where(gl == gmax, grow, N_GROUPS).min(axis=0, keepdims=True)
    g_w = 1.0 / jnp.exp(gl - gmax).sum(axis=0, keepdims=True)
    el = logits[EXPERT_ROW0:EXPERT_ROW0 + N_EXPERTS]
    e_idx = lax.broadcasted_iota(jnp.int32, el.shape, 0)
    el = jnp.where((e_idx // EXPERTS_PER_GROUP) == g_sel, el, -jnp.inf)
    m1 = el.max(axis=0, keepdims=True)
    i1 = jnp.where(el == m1, e_idx, N_EXPERTS).min(axis=0, keepdims=True)
    el2 = jnp.where(e_idx == i1, -jnp.inf, el)
    m2 = el2.max(axis=0, keepdims=True)
    i2 = jnp.where(el2 == m2, e_idx, N_EXPERTS).min(axis=0, keepdims=True)
    tt = jnp.exp(m2 - m1)
    w1 = g_w / (1.0 + tt)
    w2 = w1 * tt
    gate_t = jnp.where(e_idx == i1, w1, 0.0) + jnp.where(e_idx == i2, w2, 0.0)
    gate = jnp.concatenate([gate_t, jnp.zeros((LANES - N_EXPERTS, n), F32)], axis=0).T
    return x1, hi, gate


def _ffn_kernel(x_ref, ma_ref, mb_ref, mc_ref, mod_ref, g_ref, wo_ref, wr_ref, br_ref, w1_ref, w3_ref, w2_ref,
                o_ref, x1_s, h_s, gate_s, acc_s):
    j = pl.program_id(1)
    tm = x_ref.shape[0]
    th = w1_ref.shape[2]
    m = mod_ref[0, 0]

    @pl.when(j == 0)
    def _():
        sub = 1024
        for r0 in range(0, tm, sub):
            rows = slice(r0, r0 + sub)
            x1, hi, gate = _outproj_router(x_ref[rows, :], ma_ref[rows, :], mb_ref[rows, :], mc_ref[rows, :],
                                           m, g_ref[0], wo_ref, wr_ref, br_ref[0])
            x1_s[rows, :] = x1
            h_s[rows, :] = hi
            gate_s[rows, :] = gate
        acc_s[...] = jnp.zeros_like(acc_s)

    h = h_s[...]
    hid = _silu(_dot(h, w1_ref[0].astype(BF16))) * _dot(h, w3_ref[0].astype(BF16))
    gate = gate_s[...]
    lane = lax.broadcasted_iota(jnp.int32, gate.shape, 1)
    n_e = th // D_EXPERT
    col = lax.broadcasted_iota(jnp.int32, hid.shape, 1) // D_EXPERT
    gmat = jnp.zeros(hid.shape, F32)
    for e in range(n_e):
        ge = jnp.where(lane == j * n_e + e, gate, 0.0).sum(axis=1, keepdims=True)
        gmat = jnp.where(col == e, ge, gmat)
    acc_s[...] += _dot((hid * gmat).astype(BF16), w2_ref[0].astype(BF16))

    @pl.when(j == pl.num_programs(1) - 1)
    def _():
        o_ref[...] = x1_s[...] + m[5:6] * acc_s[...]


def _ffn_call(layer, x, ma, mb, mc, mods, g, wo, wr, br, w1, w3, w2, slot_fn, tm, th):
    n_tok = x.shape[0]
    ef = w1.shape[2]
    tok = lambda w: pl.BlockSpec((tm, w), lambda i, j: (i, 0))
    layer_spec = lambda shape: pl.BlockSpec((1,) + shape, lambda i, j: (layer,) + (0,) * len(shape))
    return pl.pallas_call(
        _ffn_kernel,
        grid=(n_tok // tm, ef // th),
        in_specs=[tok(D_MODEL), tok(256), tok(512), tok(256),
                  pl.BlockSpec((1, 1, 6, D_MODEL), lambda i, j: (layer, slot_fn(i), 0, 0)),
                  layer_spec((1, D_MODEL)), layer_spec((D_MODEL, D_MODEL)), layer_spec((D_MODEL, 2 * LANES)),
                  layer_spec((1, LANES)),
                  pl.BlockSpec((1, D_MODEL, th), lambda i, j: (layer, 0, j)),
                  pl.BlockSpec((1, D_MODEL, th), lambda i, j: (layer, 0, j)),
                  pl.BlockSpec((1, th, D_MODEL), lambda i, j: (layer, j, 0))],
        out_specs=tok(D_MODEL),
        out_shape=jax.ShapeDtypeStruct((n_tok, D_MODEL), F32),
        scratch_shapes=[pltpu.VMEM((tm, D_MODEL), F32), pltpu.VMEM((tm, D_MODEL), BF16),
                        pltpu.VMEM((tm, LANES), F32), pltpu.VMEM((tm, D_MODEL), F32)],
        compiler_params=_params(("arbitrary", "arbitrary")),
        name="ffn",
    )(x, ma, mb, mc, mods, g, wo, wr, br, w1, w3, w2)


def _final_norm_kernel(x_ref, g_ref, o_ref):
    x = x_ref[...]
    o_ref[...] = x * lax.rsqrt(jnp.mean(x * x, axis=-1, keepdims=True) + EPS) * g_ref[...]


def _final_norm_call(x, g, tm, row0, n_rows):
    blk0 = row0 // tm
    return pl.pallas_call(
        _final_norm_kernel,
        grid=(n_rows // tm,),
        in_specs=[pl.BlockSpec((tm, D_MODEL), lambda i: (blk0 + i, 0)), pl.BlockSpec((1, D_MODEL), lambda i: (0, 0))],
        out_specs=pl.BlockSpec((tm, D_MODEL), lambda i: (i, 0)),
        out_shape=jax.ShapeDtypeStruct((n_rows, D_MODEL), F32),
        compiler_params=_params(("arbitrary",)),
        name="final_norm",
    )(x, g)


def _rope_tables(t):
    pos = np.arange(t)
    n_freq = HEAD_DIM // 4
    inv_freq = ROPE_THETA ** (-jnp.arange(n_freq, dtype=F32) / n_freq)
    row = jnp.asarray(pos // GRID_W, F32)
    col = jnp.asarray(pos % GRID_W, F32)
    ang = jnp.concatenate([row[:, None] * inv_freq, col[:, None] * inv_freq], -1)
    cos, sin = jnp.cos(ang), jnp.sin(ang)
    cos_t = jnp.tile(jnp.concatenate([cos, cos], -1), (1, LANES // HEAD_DIM))
    sin_t = jnp.tile(jnp.concatenate([-sin, sin], -1), (1, LANES // HEAD_DIM))
    return cos_t, sin_t


def _delta_tables():
    r = np.arange(PAIR)
    same = (r[:, None] // CHUNK) == (r[None, :] // CHUNK)
    low = same & (r[:, None] >= r[None, :])
    low_s = same & (r[:, None] > r[None, :])
    up = same & (r[:, None] <= r[None, :])
    up_s = same & (r[:, None] < r[None, :])
    levels = []
    for k in range(N_LEVELS):
        s = 1 << k
        levels.append(((r[:, None] // (2 * s)) == (r[None, :] // (2 * s))) & ((r[:, None] // s) != (r[None, :] // s)))
    masks = jnp.asarray(np.stack([low, low_s, up, up_s, np.eye(PAIR, dtype=bool)] + levels).astype(np.float32))
    return masks


def _segment_mean_table():
    r = np.arange(LANES)
    seg = ((r[:, None] // HEAD_DIM) == (r[None, :] // HEAD_DIM)).astype(np.float32) / HEAD_DIM
    hi = jnp.asarray(seg, BF16)
    lo = (jnp.asarray(seg) - hi.astype(F32)).astype(BF16)
    return jnp.stack([hi, lo])


def kernel(x_prompt, x_sample, cache_a_k, cache_a_v, cache_c_k, cache_c_v, state_b_fwd, state_b_bwd, c, c_ctx, w_mod, b_mod, norm1_g, norm2_g, w_in, a_sink, b_conv, b_a_log, b_dt_bias, b_norm_g, c_q_norm, c_k_norm, w_out, w_group, b_group, w_expert, b_expert, w1, w3, w2, final_norm_g):
    n_p, t_p, d = x_prompt.shape
    n_s, t_s, _ = x_sample.shape
    depth = w_in.shape[0]
    past = cache_a_k.shape[2]
    tok_p = n_p * t_p
    n_tok = tok_p + n_s * t_s
    assert d == D_MODEL and tok_p % t_s == 0 and t_s % 512 == 0 and t_p % 256 == 0

    w_out16 = w_out.astype(BF16)
    pad_g = jnp.zeros((depth, d, EXPERT_ROW0 - N_GROUPS), F32)
    pad_e = jnp.zeros((depth, d, LANES - EXPERT_ROW0 - N_EXPERTS), F32)
    w_r = jnp.concatenate([w_group, pad_g, w_expert, pad_e], -1)
    w_r_hi = w_r.astype(BF16)
    w_r2 = jnp.concatenate([w_r_hi, (w_r - w_r_hi.astype(F32)).astype(BF16)], axis=-1)
    b_r = jnp.concatenate([b_group, pad_g[:, 0], b_expert, pad_e[:, 0]], -1)[:, None, :]
    cqn = jnp.tile(c_q_norm, (1, 4))[:, None, :]
    ckn = jnp.tile(c_k_norm, (1, 2))[:, None, :]
    gate_prm = jnp.stack([b_a_log.reshape(depth, 8), b_dt_bias.reshape(depth, 8)], 1)
    prmr = jnp.broadcast_to(jnp.pad(gate_prm, ((0, 0), (0, 0), (0, N_AB - 8)))[..., None],
                            (depth, 2, N_AB, LANES))
    cos_t, sin_t = _rope_tables(t_s)
    masks = _delta_tables()
    seg = _segment_mean_table()

    cond = jnp.concatenate([c_ctx[None, :], c], axis=0)
    cond_b = jnp.broadcast_to(cond[:, :, None], cond.shape + (LANES,))
    mods_all = _mods_call(cond_b, w_mod, b_mod).reshape(depth, SUBLANES, 6, d)

    def slot_fn(tm):
        per_s = t_s // tm
        first = tok_p // tm
        return lambda i: jnp.where(i < first, 0, 1 + (i - first) // per_s)

    x = jnp.concatenate([x_prompt.reshape(tok_p, d), x_sample.reshape(n_s * t_s, d)], axis=0)
    blk_s = tok_p // t_s
    ctx = tuple(a.reshape(n_s, depth, past, LANES) for a in (cache_a_k, cache_a_v, cache_c_k, cache_c_v))
    s0 = tuple(a.reshape(n_s, depth, B_HEADS * B_DIM, B_DIM) for a in (state_b_fwd, state_b_bwd))
    g1, g2, bng = norm1_g[:, None, :], norm2_g[:, None, :], b_norm_g[:, None, :]
    caches = None
    states = None
    tm = 512
    for l in range(depth):
        za, zb, zc, zab, zabt = _inproj_call(l, x, mods_all, g1, w_in, slot_fn(tm), tm)

        ao, co, *caches = _attn_call(False, t_p, n_p, 0, l, za, zc, a_sink, cqn, ckn, seg, prev=caches)
        ao, co = _attn_call(True, t_s, n_s, blk_s, l, za, zc, a_sink, cqn, ckn, seg, prev=(ao, co),
                            rope=(cos_t, sin_t), ctx=ctx)

        bo, *states = _delta_call(False, t_p, DELTA_NSEQ, n_p, 0, l, zb, zab, zabt, b_conv, prmr, bng, masks,
                                  prev=states)
        (bo,) = _delta_call(True, t_s, 1, n_s, blk_s, l, zb, zab, zabt, b_conv, prmr, bng, masks,
                            prev=(bo,), s0=s0)

        x = _ffn_call(l, x, ao, bo, co, mods_all, g2, w_out16, w_r2, b_r, w1, w3, w2, slot_fn(1024), 1024, 512)

    y_prompt = _final_norm_call(x, final_norm_g[None], tm, 0, tok_p).reshape(n_p, t_p, d)
    y_sample = _final_norm_call(x, final_norm_g[None], tm, tok_p, n_s * t_s).reshape(n_s, t_s, d)
    new_ak, new_av, new_ck, new_cv = (a.reshape(n_p, depth, t_p, 2, HEAD_DIM) for a in caches)
    new_sf, new_sb = (a.reshape(n_p, depth, B_HEADS, B_DIM, B_DIM) for a in states)
    return (y_prompt, y_sample, new_ak, new_av, new_ck, new_cv, new_sf, new_sb)
```

```python
import functools

import jax
import jax.numpy as jnp
import numpy as np
from jax import lax
from jax.experimental import pallas as pl
from jax.experimental.pallas import tpu as pltpu

F32 = jnp.float32
BF16 = jnp.bfloat16

D_MODEL = 1024
DEPTH = 4
GRID_W = 64
EPS = 1e-6
NEG_INF = -1e30
ROPE_THETA = 10000.0
HEAD_DIM = 64
N_Q_HEADS = 4
WINDOW = 128
Q_BLOCK = 128
B_HEADS = 4
B_DIM = 128
CHUNK = 64
BD = B_HEADS * CHUNK
PAIR = 2 * CHUNK
N_LEVELS = 6
PREP_UNROLL = 4
DELTA_NSEQ = 2
ATTN_NSEQ = 2
N_GROUPS = 4
EXPERTS_PER_GROUP = 4
N_EXPERTS = 16
D_EXPERT = 256
EXPERT_ROW0 = 8

LANES = 128
SUBLANES = 8
VMEM_LIMIT = 60000 * 1024

ZA_W, ZB_W, ZC_W, ZAB_W = 512, 2048, 512, 128
N_AB = 16
Z_W = ZA_W + ZB_W + ZC_W + ZAB_W


def _sigmoid(x):
    return 1.0 / (1.0 + jnp.exp(-x))


def _silu(x):
    return x * _sigmoid(x)


def _softplus(x):
    return jnp.maximum(x, 0.0) + jnp.log1p(jnp.exp(-jnp.abs(x)))


def _dot(a, b):
    return jnp.dot(a, b, preferred_element_type=F32)


def _dot_nt(a, b):
    return lax.dot_general(a, b, (((1,), (1,)), ((), ())), preferred_element_type=F32)


def _dot_tn(a, b):
    return lax.dot_general(a, b, (((0,), (0,)), ((), ())), preferred_element_type=F32)


def _split2(x):
    hi = x.astype(BF16)
    lo = (x - hi.astype(F32)).astype(BF16)
    return hi, lo


def _params(sem=None):
    return pltpu.CompilerParams(dimension_semantics=sem, vmem_limit_bytes=VMEM_LIMIT)


def _mods_kernel(cond_ref, w_ref, b_ref, o_ref, act_s):
    n_cond = cond_ref.shape[0]
    tn = w_ref.shape[2]
    reps = tn // LANES

    @pl.when((pl.program_id(0) == 0) & (pl.program_id(1) == 0))
    def _():
        act_s[...] = _silu(cond_ref[...])

    def body(kb, accs):
        r = pl.multiple_of(kb * SUBLANES, SUBLANES)
        w = w_ref[0, pl.ds(r, SUBLANES), :]
        return tuple(acc + jnp.tile(act_s[m, pl.ds(r, SUBLANES), :], (1, reps)) * w for m, acc in enumerate(accs))

    zero = jnp.zeros((SUBLANES, tn), F32)
    accs = lax.fori_loop(0, w_ref.shape[1] // SUBLANES, body, (zero,) * n_cond, unroll=4)
    rows = [jnp.sum(a, axis=0, keepdims=True) + b_ref[0] for a in accs]
    rows.append(jnp.zeros((SUBLANES - n_cond, tn), F32))
    o_ref[0] = jnp.concatenate(rows, axis=0)


def _mods_call(cond_b, w_mod, b_mod):
    depth, d, n = w_mod.shape
    tn = 1536
    n_cond = cond_b.shape[0]
    return pl.pallas_call(
        _mods_kernel,
        grid=(depth, n // tn),
        in_specs=[
            pl.BlockSpec((n_cond, d, LANES), lambda l, j: (0, 0, 0)),
            pl.BlockSpec((1, d, tn), lambda l, j: (l, 0, j)),
            pl.BlockSpec((1, 1, tn), lambda l, j: (l, 0, j)),
        ],
        out_specs=pl.BlockSpec((1, SUBLANES, tn), lambda l, j: (l, 0, j)),
        out_shape=jax.ShapeDtypeStruct((depth, SUBLANES, n), F32),
        scratch_shapes=[pltpu.VMEM((n_cond, d, LANES), F32)],
        compiler_params=_params(("arbitrary", "arbitrary")),
        name="mods",
    )(cond_b, w_mod, b_mod.reshape(depth, 1, n))


def _x_specs(xs, tm):
    if len(xs) == 1:
        return [pl.BlockSpec((tm, D_MODEL), lambda i, *_: (i, 0))]
    first = xs[0].shape[0] // tm
    return [pl.BlockSpec((tm, D_MODEL), lambda i, *_: (jnp.minimum(i, first - 1), 0)),
            pl.BlockSpec((tm, D_MODEL), lambda i, *_: (jnp.maximum(i - first, 0), 0))]


def _x_tile(x_refs, first):
    if len(x_refs) == 1:
        return x_refs[0][...]
    return jnp.where(pl.program_id(0) < first, x_refs[0][...], x_refs[1][...])


def _modulated_norm(x, g, shift, scale):
    ms = jnp.mean(x * x, axis=-1, keepdims=True)
    y = x * lax.rsqrt(ms + EPS) * g
    return y * (1.0 + scale) + shift


def _inproj_kernel(n_x, first, *refs):
    x_refs = refs[:n_x]
    mod_ref, g_ref, wt_ref, za_ref, zb_ref, zc_ref, zab_ref, zabt_ref = refs[n_x:n_x + 8]
    w_s = refs[-1]
    @pl.when(pl.program_id(0) == 0)
    def _():
        ab0 = ZA_W + ZB_W
        w_s[0:ab0, :] = wt_ref[0, 0:ab0, :].astype(BF16)
        w_s[ab0:ab0 + ZC_W, :] = wt_ref[0, ab0 + N_AB:ab0 + N_AB + ZC_W, :].astype(BF16)
        w_s[ab0 + ZC_W:ab0 + ZC_W + N_AB, :] = wt_ref[0, ab0:ab0 + N_AB, :].astype(BF16)
        w_s[ab0 + ZC_W + N_AB:Z_W, :] = jnp.zeros((ZAB_W - N_AB, D_MODEL), BF16)

    m = mod_ref[0, 0]
    x = _x_tile(x_refs, first)
    if n_x > 1:
        refs[n_x + 8][...] = x
    h = _modulated_norm(x, g_ref[0], m[0:1], m[1:2]).astype(BF16)
    za_ref[...] = _dot_nt(h, w_s[0:ZA_W, :])
    step = 512
    for j in range(ZB_W // step):
        zb_ref[:, j * step:(j + 1) * step] = _dot_nt(h, w_s[ZA_W + j * step:ZA_W + (j + 1) * step, :])
    zc_ref[...] = _dot_nt(h, w_s[ZA_W + ZB_W:ZA_W + ZB_W + ZC_W, :])
    zab = _dot_nt(h, w_s[ZA_W + ZB_W + ZC_W:Z_W, :])
    zab_ref[...] = zab
    zabt_ref[...] = zab.T[:N_AB]


def _inproj_call(layer, xs, mods, g, w, slot_fn, tm):
    n_tok = sum(a.shape[0] for a in xs)
    n_ab = N_AB
    return pl.pallas_call(
        functools.partial(_inproj_kernel, len(xs), xs[0].shape[0] // tm),
        grid=(n_tok // tm,),
        in_specs=_x_specs(xs, tm) + [
            pl.BlockSpec((1, 1, 6, D_MODEL), lambda i: (layer, slot_fn(i), 0, 0)),
            pl.BlockSpec((1, 1, D_MODEL), lambda i: (layer, 0, 0)),
            pl.BlockSpec((1, w.shape[1], D_MODEL), lambda i: (layer, 0, 0)),
        ],
        out_specs=[
            pl.BlockSpec((tm, ZA_W), lambda i: (i, 0)),
            pl.BlockSpec((tm, ZB_W), lambda i: (i, 0)),
            pl.BlockSpec((tm, ZC_W), lambda i: (i, 0)),
            pl.BlockSpec((tm, ZAB_W), lambda i: (i, 0)),
            pl.BlockSpec((n_ab, tm), lambda i: (0, i)),
        ] + ([pl.BlockSpec((tm, D_MODEL), lambda i: (i, 0))] if len(xs) > 1 else []),
        out_shape=[
            jax.ShapeDtypeStruct((n_tok, ZA_W), F32),
            jax.ShapeDtypeStruct((n_tok, ZB_W), F32),
            jax.ShapeDtypeStruct((n_tok, ZC_W), F32),
            jax.ShapeDtypeStruct((n_tok, ZAB_W), F32),
            jax.ShapeDtypeStruct((n_ab, n_tok), F32),
        ] + ([jax.ShapeDtypeStruct((n_tok, D_MODEL), F32)] if len(xs) > 1 else []),
        scratch_shapes=[pltpu.VMEM((Z_W, D_MODEL), BF16)],
        compiler_params=_params(("arbitrary",)),
        name="inproj",
    )(*xs, mods, g, w)


def _lane_lo(shape):
    return lax.broadcasted_iota(jnp.int32, shape, len(shape) - 1) % LANES < HEAD_DIM


def _store_kdup(dst_ref, off, k):
    n = k.shape[0]
    r = pltpu.roll(k, HEAD_DIM, 1)
    lo = _lane_lo(k.shape)
    dst_ref[0, off:off + n, :] = jnp.where(lo, k, r).astype(BF16)
    dst_ref[1, off:off + n, :] = jnp.where(lo, r, k).astype(BF16)


def _store_vsplit(dst_ref, off, v):
    n = v.shape[0]
    r = pltpu.roll(v, HEAD_DIM, 1)
    lo = _lane_lo(v.shape)
    z = jnp.zeros_like(v)
    dst_ref[0, off:off + n, :] = jnp.where(lo, v, z).astype(BF16)
    dst_ref[1, off:off + n, :] = jnp.where(lo, z, r).astype(BF16)
    dst_ref[2, off:off + n, :] = jnp.where(lo, r, z).astype(BF16)
    dst_ref[3, off:off + n, :] = jnp.where(lo, z, v).astype(BF16)


def _rope(x, cos, sin):
    first = (lax.broadcasted_iota(jnp.int32, x.shape, 1) // (HEAD_DIM // 2)) % 2 == 0
    partner = jnp.where(first, pltpu.roll(x, LANES - HEAD_DIM // 2, 1), pltpu.roll(x, HEAD_DIM // 2, 1))
    return x * cos + partner * sin


def _head_rmsnorm(x, g, seg_hi, seg_lo):
    hi, lo = _split2(x * x)
    ms = _dot(hi, seg_hi) + _dot(lo, seg_hi) + _dot(hi, seg_lo)
    return x * lax.rsqrt(ms + EPS) * g


def _attend_many(units):
    qb = units[0][0].shape[0]
    lo = _lane_lo(units[0][0].shape)
    all_scores = []
    for qt, segs, _ in units:
        z = jnp.zeros_like(qt)
        qs = jnp.concatenate([jnp.where(lo, qt, z), jnp.where(lo, z, qt)], axis=0).astype(BF16)
        scores = []
        for kdup, _, _, mask in segs:
            s = _dot_nt(qs, kdup)
            if mask is not None:
                s = jnp.where(mask, s, NEG_INF)
            scores.append(s)
        all_scores.append(scores)
    probs = []
    for (qt, segs, sink_pair), scores in zip(units, all_scores):
        m = scores[0].max(axis=1, keepdims=True)
        for s in scores[1:]:
            m = jnp.maximum(m, s.max(axis=1, keepdims=True))
        if sink_pair is not None:
            row_a = lax.broadcasted_iota(jnp.int32, (2 * qb, 1), 0) < qb
            sink = jnp.where(row_a, sink_pair[0], sink_pair[1])
            m = jnp.maximum(m, sink)
            denom = jnp.exp(sink - m)
        else:
            denom = jnp.zeros((2 * qb, 1), F32)
        ps = []
        for s in scores:
            p = jnp.exp(s - m)
            denom = denom + p.sum(axis=1, keepdims=True)
            ps.append(p.astype(BF16))
        probs.append((ps, 1.0 / denom))
    outs = []
    for (qt, segs, _), (ps, inv) in zip(units, probs):
        acc = jnp.zeros((qb, LANES), F32)
        for pb, (_, vlo, vhi, _) in zip(ps, segs):
            acc = acc + _dot(pb[:qb], vlo) + _dot(pb[qb:], vhi)
        outs.append(acc * jnp.where(lo, inv[:qb], inv[qb:]))
    return outs


def _attn_kernel(has_ctx, t, nseq, layer, *refs):
    if has_ctx:
        (sink_ref, za_ref, zc_ref, cqn_ref, ckn_ref, seg_ref, cos_ref, sin_ref,
         cak_ref, cav_ref, cck_ref, ccv_ref,
         ao_ref, co_ref,
         ka_s, va_s, kc_s, vc_s, kctx_s, vctx_s, qa_s, qc_s) = refs
    else:
        (sink_ref, za_ref, zc_ref, cqn_ref, ckn_ref, seg_ref,
         ao_ref, co_ref, nak_ref, nav_ref, nck_ref, ncv_ref,
         ka_s, va_s, kc_s, vc_s, qa_s, qc_s) = refs
    scale = HEAD_DIM ** -0.5
    seg_hi = seg_ref[0]
    seg_lo = seg_ref[1]
    piece = 256
    n_ctx = cak_ref.shape[2] if has_ctx else 0

    for p0 in range(0, nseq * t, piece):
        rows = slice(p0, p0 + piece)
        ak = za_ref[rows, 256:384]
        av = za_ref[rows, 384:512]
        ck = _head_rmsnorm(zc_ref[rows, 256:384], ckn_ref[0], seg_hi, seg_lo)
        cv = zc_ref[rows, 384:512]
        if has_ctx:
            cos = cos_ref[rows, :]
            sin = sin_ref[rows, :]
            ak = _rope(ak, cos, sin)
            ck = _rope(ck, cos, sin)
            _store_kdup(ka_s, WINDOW + p0, ak)
            _store_vsplit(va_s, WINDOW + p0, av)
            _store_kdup(kc_s, n_ctx + p0, ck)
            _store_vsplit(vc_s, n_ctx + p0, cv)
        else:
            crow = slice(p0 % t, p0 % t + piece)
            nak_ref[p0 // t, 0, crow, :] = ak
            nav_ref[p0 // t, 0, crow, :] = av
            nck_ref[p0 // t, 0, crow, :] = ck
            ncv_ref[p0 // t, 0, crow, :] = cv
            _store_kdup(ka_s, p0, ak)
            _store_vsplit(va_s, p0, av)
            _store_kdup(kc_s, p0, ck)
            _store_vsplit(vc_s, p0, cv)
        for hk in range(2):
            cols = slice(hk * LANES, (hk + 1) * LANES)
            aq = za_ref[rows, cols]
            cq = _head_rmsnorm(zc_ref[rows, cols], cqn_ref[0, :, cols], seg_hi, seg_lo)
            if has_ctx:
                aq = _rope(aq, cos, sin)
                cq = _rope(cq, cos, sin)
            qa_s[rows, cols] = aq * scale
            qc_s[rows, cols] = cq * scale

    if has_ctx:
        zpad = jnp.zeros((WINDOW, LANES), BF16)
        for i in range(2):
            ka_s[i, 0:WINDOW, :] = zpad
            ka_s[i, WINDOW + t:2 * WINDOW + t, :] = zpad
        for i in range(4):
            va_s[i, 0:WINDOW, :] = zpad
            va_s[i, WINDOW + t:2 * WINDOW + t, :] = zpad
        for p0 in range(0, n_ctx, piece):
            rows = slice(p0, p0 + piece)
            _store_kdup(kctx_s, p0, cak_ref[0, 0, rows, :])
            _store_vsplit(vctx_s, p0, cav_ref[0, 0, rows, :])
            _store_kdup(kc_s, p0, cck_ref[0, 0, rows, :])
            _store_vsplit(vc_s, p0, ccv_ref[0, 0, rows, :])

        qb = Q_BLOCK
        span = 3 * qb
        qi = lax.broadcasted_iota(jnp.int32, (2 * qb, span), 0) % qb
        kj = lax.broadcasted_iota(jnp.int32, (2 * qb, span), 1)
        band = jnp.abs(kj - qb - qi) <= WINDOW

        def block(b, carry):
            r0 = pl.multiple_of(b * qb, qb)
            kpos = kj + (r0 - qb)
            mask = band & (kpos >= 0) & (kpos < t)
            units = []
            for hk in range(2):
                cols = slice(hk * LANES, (hk + 1) * LANES)
                segs_a = [
                    (kctx_s[hk], vctx_s[2 * hk], vctx_s[2 * hk + 1], None),
                    (ka_s[hk, pl.ds(r0, span), :], va_s[2 * hk, pl.ds(r0, span), :],
                     va_s[2 * hk + 1, pl.ds(r0, span), :], mask),
                ]
                sinks = (sink_ref[layer, 2 * hk], sink_ref[layer, 2 * hk + 1])
                units.append((qa_s[pl.ds(r0, qb), cols], segs_a, sinks))
                segs_c = [(kc_s[hk], vc_s[2 * hk], vc_s[2 * hk + 1], None)]
                units.append((qc_s[pl.ds(r0, qb), cols], segs_c, None))
            outs = _attend_many(units)
            for hk in range(2):
                cols = slice(hk * LANES, (hk + 1) * LANES)
                ao_ref[pl.ds(r0, qb), cols] = outs[2 * hk]
                co_ref[pl.ds(r0, qb), cols] = outs[2 * hk + 1]
            return carry

        lax.fori_loop(0, t // qb, block, 0)
    else:
        units = []
        for q in range(nseq):
            seq = slice(q * t, (q + 1) * t)
            for hk in range(2):
                cols = slice(hk * LANES, (hk + 1) * LANES)
                sinks = (sink_ref[layer, 2 * hk], sink_ref[layer, 2 * hk + 1])
                units.append((qa_s[seq, cols],
                              [(ka_s[hk, seq, :], va_s[2 * hk, seq, :], va_s[2 * hk + 1, seq, :], None)], sinks))
                units.append((qc_s[seq, cols],
                              [(kc_s[hk, seq, :], vc_s[2 * hk, seq, :], vc_s[2 * hk + 1, seq, :], None)], None))
        outs = _attend_many(units)
        for q in range(nseq):
            seq = slice(q * t, (q + 1) * t)
            for hk in range(2):
                cols = slice(hk * LANES, (hk + 1) * LANES)
                ao_ref[seq, cols] = outs[4 * q + 2 * hk]
                co_ref[seq, cols] = outs[4 * q + 2 * hk + 1]


def _attn_call(has_ctx, t, nseq, n_batch, row_block0, layer, za, zc, sink, cqn, ckn, seg, prev=None, rope=None,
               ctx=None):
    n_tok = za.shape[0]
    depth = sink.shape[0]
    assert n_batch % nseq == 0 and (nseq == 1 or not has_ctx)
    tok_spec = lambda w: pl.BlockSpec((nseq * t, w), lambda b, *_: (row_block0 + b, 0))
    const = lambda shape: pl.BlockSpec(shape, lambda b, *_: (0,) * len(shape))
    layer_spec = lambda shape: pl.BlockSpec((1,) + shape, lambda b, *_: (layer,) + (0,) * len(shape))
    in_specs = [tok_spec(ZA_W), tok_spec(ZC_W), layer_spec((1, 256)), layer_spec((1, 128)), const((2, LANES, LANES))]
    args = [za, zc, cqn, ckn, seg]
    out_specs = [tok_spec(256), tok_spec(256)]
    out_shape = [jax.ShapeDtypeStruct((n_tok, 256), F32), jax.ShapeDtypeStruct((n_tok, 256), F32)]
    if has_ctx:
        n_ctx = ctx[0].shape[2]
        in_specs += [const((t, LANES)), const((t, LANES))]
        args += list(rope)
        in_specs += [pl.BlockSpec((1, 1, n_ctx, LANES), lambda b, *_: (b, layer, 0, 0))] * 4
        args += list(ctx)
        scratch = [
            pltpu.VMEM((2, t + 2 * WINDOW, LANES), BF16), pltpu.VMEM((4, t + 2 * WINDOW, LANES), BF16),
            pltpu.VMEM((2, n_ctx + t, LANES), BF16), pltpu.VMEM((4, n_ctx + t, LANES), BF16),
            pltpu.VMEM((2, n_ctx, LANES), BF16), pltpu.VMEM((4, n_ctx, LANES), BF16),
            pltpu.VMEM((t, 256), F32), pltpu.VMEM((t, 256), F32),
        ]
    else:
        cache_spec = pl.BlockSpec((nseq, 1, t, LANES), lambda b, *_: (b, layer, 0, 0))
        out_specs += [cache_spec] * 4
        out_shape += [jax.ShapeDtypeStruct((n_batch, depth, t, LANES), F32)] * 4
        rows = nseq * t
        scratch = [
            pltpu.VMEM((2, rows, LANES), BF16), pltpu.VMEM((4, rows, LANES), BF16),
            pltpu.VMEM((2, rows, LANES), BF16), pltpu.VMEM((4, rows, LANES), BF16),
            pltpu.VMEM((rows, 256), F32), pltpu.VMEM((rows, 256), F32),
        ]
    n_real = len(args)
    aliases = {}
    if prev is not None:
        first_out = 0 if has_ctx else 2
        for k, arr in enumerate(prev):
            in_specs.append(pl.BlockSpec(memory_space=pl.ANY))
            args.append(arr)
            aliases[1 + n_real + k] = first_out + k

    def body(*refs):
        ins = refs[:1 + n_real]
        rest = refs[1 + len(args):]
        _attn_kernel(has_ctx, t, nseq, layer, *ins, *rest)

    return pl.pallas_call(
        body,
        grid_spec=pltpu.PrefetchScalarGridSpec(
            num_scalar_prefetch=1, grid=(n_batch // nseq,), in_specs=in_specs, out_specs=out_specs,
            scratch_shapes=scratch),
        out_shape=out_shape,
        input_output_aliases=aliases,
        compiler_params=_params(("arbitrary",)),
        name="attn_latent" if has_ctx else "attn_prompt",
    )(sink, *args)


def _stack_pair(x, p):
    return jnp.concatenate([x[:, (2 * p + hl) * B_DIM:(2 * p + hl + 1) * B_DIM] for hl in range(2)], axis=0)


def _delta_kernel(t, nseq, has_s0, *refs):
    if has_s0:
        (zb_ref, abc_ref, abt_ref, conv_ref, prmr_ref, bng_ref, mask_ref,
         s0f_ref, s0b_ref, o_ref, qkv_s, of_s, ob_s, sf_s, sb_s, u_s, wq_s, at_s, kd_s, eg_s,
         pre_s, suf_s, prec_s, sufc_s) = refs
    else:
        (zb_ref, abc_ref, abt_ref, conv_ref, prmr_ref, bng_ref, mask_ref,
         o_ref, sfo_ref, sbo_ref, qkv_s, of_s, ob_s, sf_s, sb_s, u_s, wq_s, at_s, kd_s, eg_s,
         pre_s, suf_s, prec_s, sufc_s) = refs
    n_chunks = t // CHUNK
    n_total = nseq * n_chunks
    s_rows = B_HEADS * B_DIM
    qk_w = B_HEADS * B_DIM

    row = lax.broadcasted_iota(jnp.int32, (t, LANES), 0)
    for q in range(nseq):
        seq = slice(q * t, (q + 1) * t)
        for j in range(3 * B_HEADS):
            cols = slice(j * LANES, (j + 1) * LANES)
            x = zb_ref[seq, cols]
            prev = jnp.where(row == 0, 0.0, pltpu.roll(x, 1, 0))
            nxt = jnp.where(row == t - 1, 0.0, pltpu.roll(x, t - 1, 0))
            y = _silu(prev * conv_ref[0, 0:1, cols] + x * conv_ref[0, 1:2, cols] + nxt * conv_ref[0, 2:3, cols])
            if j < 2 * B_HEADS:
                y = y * lax.rsqrt(jnp.sum(y * y, axis=-1, keepdims=True) + EPS)
            if j < B_HEADS:
                y = y * (B_DIM ** -0.5)
            qkv_s[seq, cols] = y

    if has_s0:
        for q in range(nseq):
            sf_s[q * s_rows:(q + 1) * s_rows, :] = s0f_ref[q, 0]
            sb_s[q * s_rows:(q + 1) * s_rows, :] = s0b_ref[q, 0]
    else:
        sf_s[...] = jnp.zeros_like(sf_s)
        sb_s[...] = jnp.zeros_like(sb_s)

    reps = nseq * t // LANES
    gr = -jnp.tile(jnp.exp(prmr_ref[0, 0]), (1, reps)) * _softplus(abt_ref[...] + jnp.tile(prmr_ref[0, 1], (1, reps)))
    seg_lane = lax.broadcasted_iota(jnp.int32, gr.shape, 1) % CHUNK
    pre, suf = gr, gr
    for s in (1, 2, 4, 8, 16, 32):
        pre = pre + jnp.where(seg_lane >= s, pltpu.roll(pre, s, 1), 0.0)
        suf = suf + jnp.where(seg_lane < CHUNK - s, pltpu.roll(suf, nseq * t - s, 1), 0.0)
    pre_s[...] = pre
    suf_s[...] = suf
    zrows = jnp.zeros((LANES - pre.shape[0], LANES), F32)
    for j in range(reps):
        tile = slice(j * LANES, (j + 1) * LANES)
        prec_s[tile, :] = jnp.concatenate([pre[:, tile], zrows], axis=0).T
        sufc_s[tile, :] = jnp.concatenate([suf[:, tile], zrows], axis=0).T
    lane_lo = lax.broadcasted_iota(jnp.int32, (1, LANES), 1) < CHUNK

    def prepare(cc, carry):
        chains = []
        for k in range(PREP_UNROLL):
            c = cc * PREP_UNROLL + k
            r0 = pl.multiple_of(c * CHUNK, CHUNK)
            b_all = _sigmoid(abc_ref[pl.ds(r0, CHUNK), :])
            run_c = (prec_s[pl.ds(r0, CHUNK), :], sufc_s[pl.ds(r0, CHUNK), :])
            tile0 = pl.multiple_of((cc * PREP_UNROLL + k - k % 2) * CHUNK, LANES)
            run = (pre_s[:, pl.ds(tile0, LANES)], suf_s[:, pl.ds(tile0, LANES)])
            run_r = tuple(pltpu.roll(x, CHUNK, 1) for x in run)
            for p in range(B_HEADS // 2):
                kst = _stack_pair(qkv_s[pl.ds(r0, CHUNK), qk_w:2 * qk_w], p)
                qst = _stack_pair(qkv_s[pl.ds(r0, CHUNK), 0:qk_w], p)
                vst = _stack_pair(qkv_s[pl.ds(r0, CHUNK), 2 * qk_w:3 * qk_w], p)
                kq = _dot_nt(jnp.concatenate([kst, qst], axis=0).astype(BF16), kst.astype(BF16))
                for d in range(2):
                    cg = 4 * d + 2 * p
                    edge = CHUNK - 1 if d == 0 else 0
                    rep_col = lambda x, col: jnp.broadcast_to(x[:, col:col + 1], (CHUNK, LANES))
                    b_rep = jnp.concatenate([rep_col(b_all, 8 + cg + hl) for hl in range(2)], axis=0)
                    gcol = jnp.concatenate([rep_col(run_c[d], cg + hl) for hl in range(2)], axis=0)
                    gtot = jnp.concatenate([rep_col(run_c[d][edge:edge + 1], cg + hl) for hl in range(2)], axis=0)
                    ra = cg
                    if k % 2 == 0:
                        grow = jnp.where(lane_lo, run[d][ra:ra + 1], run_r[d][ra + 1:ra + 2])
                    else:
                        grow = jnp.where(lane_lo, run_r[d][ra:ra + 1], run[d][ra + 1:ra + 2])
                    chains.append(dict(c=c, p=p, d=d, kst=kst, qst=qst, vst=vst, kq=kq, b_st=b_rep,
                                       gcol=gcol, gtot=gtot, grow=grow))

        for ch in chains:
            d, b_st, kq, gcol = ch["d"], ch["b_st"], ch.pop("kq"), ch["gcol"]
            decay = jnp.exp(jnp.minimum(gcol - ch.pop("grow"), 0.0))
            ch["a_mat"] = (b_st * kq[:PAIR]) * (decay * mask_ref[2 * d + 1])
            ch["attn"] = (kq[PAIR:] * (decay * mask_ref[2 * d])).astype(BF16)
            ch["t_inv"] = mask_ref[4] - ch["a_mat"] * mask_ref[5]
        for lvl in range(N_LEVELS - 1):
            for ch in chains:
                ch["t16"] = ch["t_inv"].astype(BF16)
                ch["et"] = _dot((ch["a_mat"] * mask_ref[6 + lvl]).astype(BF16), ch["t16"])
            for ch in chains:
                ch["t_inv"] = ch["t_inv"] - _dot(ch.pop("t16"), ch.pop("et").astype(BF16))
        for ch in chains:
            egc = jnp.exp(ch["gcol"])
            rk = jnp.concatenate([ch["b_st"] * ch["vst"], (ch["b_st"] * egc) * ch["kst"]], axis=1)
            ch["rk"] = _dot(ch.pop("t_inv").astype(BF16), rk.astype(BF16))
            ch["qp16"] = (ch["qst"] * egc).astype(BF16)
        for ch in chains:
            c, p, d, rk, qp16 = ch["c"], ch["p"], ch["d"], ch["rk"], ch["qp16"]
            pair_rows = slice(p * PAIR, (p + 1) * PAIR)
            w16 = rk[:, B_DIM:].astype(BF16)
            u_s[d, c, pair_rows, :] = rk[:, :B_DIM]
            at_s[d, c, p] = ch["attn"]
            kd_s[d, c, pair_rows, :] = (ch["kst"] * jnp.exp(ch["gtot"] - ch["gcol"])).astype(BF16)
            eg = jnp.exp(ch["gtot"])
            for hl in range(2):
                h = 2 * p + hl
                rows = slice(hl * CHUNK, (hl + 1) * CHUNK)
                wq_s[d, c, h * 2 * CHUNK:h * 2 * CHUNK + CHUNK, :] = w16[rows]
                wq_s[d, c, h * 2 * CHUNK + CHUNK:(h + 1) * 2 * CHUNK, :] = qp16[rows]
                eg_s[d, c, h * SUBLANES:(h + 1) * SUBLANES, :] = eg[hl * CHUNK:hl * CHUNK + SUBLANES, :]
        return carry

    lax.fori_loop(0, n_total // PREP_UNROLL, prepare, 0)

    def scan_step(i, carry):
        units = []
        for q in range(nseq):
            for d, s_ref, o_s in ((0, sf_s, of_s), (1, sb_s, ob_s)):
                c = q * n_chunks + (i if d == 0 else n_chunks - 1 - i)
                units.append(dict(q=q, d=d, c=c, s_ref=s_ref, o_s=o_s, r0=pl.multiple_of(c * CHUNK, CHUNK)))
        for un in units:
            q, d, c, s_ref = un["q"], un["d"], un["c"], un["s_ref"]
            un["x"] = []
            for h in range(B_HEADS):
                srows = slice(q * s_rows + h * B_DIM, q * s_rows + (h + 1) * B_DIM)
                un["x"].append(_dot(wq_s[d, c, h * 2 * CHUNK:(h + 1) * 2 * CHUNK, :], s_ref[srows, :].astype(BF16)))
        for un in units:
            d, c = un["d"], un["c"]
            un["vp16"], un["o"] = [], []
            for p in range(B_HEADS // 2):
                xs = un["x"][2 * p:2 * p + 2]
                v_new = jnp.concatenate(
                    [u_s[d, c, (2 * p + hl) * CHUNK:(2 * p + hl + 1) * CHUNK, :] - xs[hl][:CHUNK] for hl in range(2)],
                    axis=0)
                vp16 = v_new.astype(BF16)
                un["vp16"].append(vp16)
                un["o"].append(jnp.concatenate([xs[hl][CHUNK:] for hl in range(2)], axis=0)
                               + _dot(at_s[d, c, p], vp16))
        for un in units:
            q, d, c, s_ref, o_s, r0 = un["q"], un["d"], un["c"], un["s_ref"], un["o_s"], un["r0"]
            for h in range(B_HEADS):
                p, hl = divmod(h, 2)
                rows = slice(hl * CHUNK, (hl + 1) * CHUNK)
                srows = slice(q * s_rows + h * B_DIM, q * s_rows + (h + 1) * B_DIM)
                upd = _dot_tn(kd_s[d, c, h * CHUNK:(h + 1) * CHUNK, :], un["vp16"][p][rows])
                eg = jnp.tile(eg_s[d, c, h * SUBLANES:(h + 1) * SUBLANES, :], (B_DIM // SUBLANES, 1))
                s_ref[srows, :] = s_ref[srows, :] * eg + upd
                o_s[pl.ds(r0, CHUNK), h * B_DIM:(h + 1) * B_DIM] = un["o"][p][rows]
        return carry

    lax.fori_loop(0, n_chunks, scan_step, 0)

    if not has_s0:
        for q in range(nseq):
            sfo_ref[q, 0] = sf_s[q * s_rows:(q + 1) * s_rows, :]
            sbo_ref[q, 0] = sb_s[q * s_rows:(q + 1) * s_rows, :]

    for h in range(B_HEADS):
        cols = slice(h * B_DIM, (h + 1) * B_DIM)
        x = of_s[:, cols] + ob_s[:, cols]
        yn = x * lax.rsqrt(jnp.mean(x * x, axis=-1, keepdims=True) + EPS) * bng_ref[0]
        o_ref[:, cols] = yn * _silu(zb_ref[:, 3 * qk_w + h * B_DIM:3 * qk_w + (h + 1) * B_DIM])


def _delta_call(has_s0, t, nseq, n_batch, row_block0, layer, zb, zab, zabt, conv, prmr, bng, masks,
                prev=None, s0=None):
    n_tok = zb.shape[0]
    depth = conv.shape[0]
    n_chunks = nseq * (t // CHUNK)
    assert (t // CHUNK) % PREP_UNROLL == 0 and PREP_UNROLL % 2 == 0 and n_batch % nseq == 0
    tok_spec = lambda w: pl.BlockSpec((nseq * t, w), lambda b: (row_block0 + b, 0))
    const = lambda shape: pl.BlockSpec(shape, lambda b: (0,) * len(shape))
    layer_spec = lambda shape: pl.BlockSpec((1,) + shape, lambda b: (layer,) + (0,) * len(shape))
    s_shape = (B_HEADS * B_DIM, B_DIM)
    s_spec = pl.BlockSpec((nseq, 1) + s_shape, lambda b: (b, layer, 0, 0))
    n_ab = zabt.shape[0]
    in_specs = [
        tok_spec(ZB_W), tok_spec(ZAB_W),
        pl.BlockSpec((n_ab, nseq * t), lambda b: (0, row_block0 + b)),
        layer_spec((3, 3 * B_HEADS * B_DIM)), layer_spec((2, n_ab, LANES)),
        layer_spec((1, B_DIM)),
        const((5 + N_LEVELS, PAIR, PAIR)),
    ]
    args = [zb, zab, zabt, conv, prmr, bng, masks]
    out_specs = [tok_spec(B_HEADS * B_DIM)]
    out_shape = [jax.ShapeDtypeStruct((n_tok, B_HEADS * B_DIM), F32)]
    if has_s0:
        in_specs += [s_spec, s_spec]
        args += [s0[0], s0[1]]
    else:
        out_specs += [s_spec, s_spec]
        out_shape += [jax.ShapeDtypeStruct((n_batch, depth) + s_shape, F32)] * 2
    n_real = len(args)
    aliases = {}
    if prev is not None:
        first_out = 0 if has_s0 else 1
        for k, arr in enumerate(prev):
            in_specs.append(pl.BlockSpec(memory_space=pl.ANY))
            args.append(arr)
            aliases[n_real + k] = first_out + k
    rows = nseq * t
    scratch = [
        pltpu.VMEM((rows, 3 * B_HEADS * B_DIM), F32),
        pltpu.VMEM((rows, B_HEADS * B_DIM), F32), pltpu.VMEM((rows, B_HEADS * B_DIM), F32),
        pltpu.VMEM((nseq * s_shape[0], B_DIM), F32), pltpu.VMEM((nseq * s_shape[0], B_DIM), F32),
        pltpu.VMEM((2, n_chunks, BD, B_DIM), F32),
        pltpu.VMEM((2, n_chunks, 2 * BD, B_DIM), BF16),
        pltpu.VMEM((2, n_chunks, B_HEADS // 2, PAIR, PAIR), BF16),
        pltpu.VMEM((2, n_chunks, BD, B_DIM), BF16),
        pltpu.VMEM((2, n_chunks, B_HEADS * SUBLANES, LANES), F32),
        pltpu.VMEM((n_ab, rows), F32), pltpu.VMEM((n_ab, rows), F32),
        pltpu.VMEM((rows, LANES), F32), pltpu.VMEM((rows, LANES), F32),
    ]

    def body(*refs):
        _delta_kernel(t, nseq, has_s0, *refs[:n_real], *refs[len(args):])

    return pl.pallas_call(
        body,
        grid=(n_batch // nseq,),
        in_specs=in_specs,
        out_specs=out_specs,
        out_shape=out_shape,
        scratch_shapes=scratch,
        input_output_aliases=aliases,
        compiler_params=_params(("arbitrary",)),
        name="delta_latent" if has_s0 else "delta_prompt",
    )(*args)


def _outproj_router(x, ma, mb, mc, m, g, wo_ref, wr_ref, br):
    n = x.shape[0]
    y = (_dot(ma.astype(BF16), wo_ref[0, 0:256, :])
         + _dot(mb.astype(BF16), wo_ref[0, 256:768, :])
         + _dot(mc.astype(BF16), wo_ref[0, 768:1024, :]))
    x1 = x + m[2:3] * y
    h2 = _modulated_norm(x1, g, m[3:4], m[4:5])
    hi, lo = _split2(h2)

    hw = _dot(hi, wr_ref[0])
    logits = (hw[:, :LANES] + hw[:, LANES:] + _dot(lo, wr_ref[0, :, :LANES]) + br).T
    gl = logits[0:N_GROUPS]
    grow = lax.broadcasted_iota(jnp.int32, gl.shape, 0)
    gmax = gl.max(axis=0, keepdims=True)
    g_sel = jnp.where(gl == gmax, grow, N_GROUPS).min(axis=0, keepdims=True)
    g_w = 1.0 / jnp.exp(gl - gmax).sum(axis=0, keepdims=True)
    el = logits[EXPERT_ROW0:EXPERT_ROW0 + N_EXPERTS]
    e_idx = lax.broadcasted_iota(jnp.int32, el.shape, 0)
    el = jnp.where((e_idx // EXPERTS_PER_GROUP) == g_sel, el, -jnp.inf)
    m1 = el.max(axis=0, keepdims=True)
    i1 = jnp.where(el == m1, e_idx, N_EXPERTS).min(axis=0, keepdims=True)
    el2 = jnp.where(e_idx == i1, -jnp.inf, el)
    m2 = el2.max(axis=0, keepdims=True)
    i2 = jnp.where(el2 == m2, e_idx, N_EXPERTS).min(axis=0, keepdims=True)
    tt = jnp.exp(m2 - m1)
    w1 = g_w / (1.0 + tt)
    w2 = w1 * tt
    gate_t = jnp.where(e_idx == i1, w1, 0.0) + jnp.where(e_idx == i2, w2, 0.0)
    gate = jnp.concatenate([gate_t, jnp.zeros((LANES - N_EXPERTS, n), F32)], axis=0).T
    return x1, hi, gate


def _ffn_kernel(x_ref, ma_ref, mb_ref, mc_ref, mod_ref, g_ref, wo_ref, wr_ref, br_ref, w1_ref, w3_ref, w2_ref,
                o_ref, x1_s, h_s, gate_s, acc_s):
    j = pl.program_id(1)
    tm = x_ref.shape[0]
    th = w1_ref.shape[2]
    m = mod_ref[0, 0]

    @pl.when(j == 0)
    def _():
        sub = 1024
        for r0 in range(0, tm, sub):
            rows = slice(r0, r0 + sub)
            x1, hi, gate = _outproj_router(x_ref[rows, :], ma_ref[rows, :], mb_ref[rows, :], mc_ref[rows, :],
                                           m, g_ref[0], wo_ref, wr_ref, br_ref[0])
            x1_s[rows, :] = x1
            h_s[rows, :] = hi
            gate_s[rows, :] = gate
        acc_s[...] = jnp.zeros_like(acc_s)

    h = h_s[...]
    hid = _silu(_dot(h, w1_ref[0].astype(BF16))) * _dot(h, w3_ref[0].astype(BF16))
    gate = gate_s[...]
    lane = lax.broadcasted_iota(jnp.int32, gate.shape, 1)
    n_e = th // D_EXPERT
    col = lax.broadcasted_iota(jnp.int32, hid.shape, 1) // D_EXPERT
    gmat = jnp.zeros(hid.shape, F32)
    for e in range(n_e):
        ge = jnp.where(lane == j * n_e + e, gate, 0.0).sum(axis=1, keepdims=True)
        gmat = jnp.where(col == e, ge, gmat)
    acc_s[...] += _dot((hid * gmat).astype(BF16), w2_ref[0].astype(BF16))

    @pl.when(j == pl.num_programs(1) - 1)
    def _():
        o_ref[...] = x1_s[...] + m[5:6] * acc_s[...]


def _ffn_call(layer, x, ma, mb, mc, mods, g, wo, wr, br, w1, w3, w2, slot_fn, tm, th):
    n_tok = x.shape[0]
    ef = w1.shape[2]
    tok = lambda w: pl.BlockSpec((tm, w), lambda i, j: (i, 0))
    layer_spec = lambda shape: pl.BlockSpec((1,) + shape, lambda i, j: (layer,) + (0,) * len(shape))
    return pl.pallas_call(
        _ffn_kernel,
        grid=(n_tok // tm, ef // th),
        in_specs=[tok(D_MODEL), tok(256), tok(512), tok(256),
                  pl.BlockSpec((1, 1, 6, D_MODEL), lambda i, j: (layer, slot_fn(i), 0, 0)),
                  layer_spec((1, D_MODEL)), layer_spec((D_MODEL, D_MODEL)), layer_spec((D_MODEL, 2 * LANES)),
                  layer_spec((1, LANES)),
                  pl.BlockSpec((1, D_MODEL, th), lambda i, j: (layer, 0, j)),
                  pl.BlockSpec((1, D_MODEL, th), lambda i, j: (layer, 0, j)),
                  pl.BlockSpec((1, th, D_MODEL), lambda i, j: (layer, j, 0))],
        out_specs=tok(D_MODEL),
        out_shape=jax.ShapeDtypeStruct((n_tok, D_MODEL), F32),
        scratch_shapes=[pltpu.VMEM((tm, D_MODEL), F32), pltpu.VMEM((tm, D_MODEL), BF16),
                        pltpu.VMEM((tm, LANES), F32), pltpu.VMEM((tm, D_MODEL), F32)],
        compiler_params=_params(("arbitrary", "arbitrary")),
        name="ffn",
    )(x, ma, mb, mc, mods, g, wo, wr, br, w1, w3, w2)


def _final_norm_kernel(x_ref, g_ref, o_ref):
    x = x_ref[...]
    o_ref[...] = x * lax.rsqrt(jnp.mean(x * x, axis=-1, keepdims=True) + EPS) * g_ref[...]


def _final_norm_call(x, g, tm, row0, n_rows):
    blk0 = row0 // tm
    return pl.pallas_call(
        _final_norm_kernel,
        grid=(n_rows // tm,),
        in_specs=[pl.BlockSpec((tm, D_MODEL), lambda i: (blk0 + i, 0)), pl.BlockSpec((1, D_MODEL), lambda i: (0, 0))],
        out_specs=pl.BlockSpec((tm, D_MODEL), lambda i: (i, 0)),
        out_shape=jax.ShapeDtypeStruct((n_rows, D_MODEL), F32),
        compiler_params=_params(("arbitrary",)),
        name="final_norm",
    )(x, g)


def _rope_tables(t):
    pos = np.arange(t)
    n_freq = HEAD_DIM // 4
    inv_freq = ROPE_THETA ** (-jnp.arange(n_freq, dtype=F32) / n_freq)
    row = jnp.asarray(pos // GRID_W, F32)
    col = jnp.asarray(pos % GRID_W, F32)
    ang = jnp.concatenate([row[:, None] * inv_freq, col[:, None] * inv_freq], -1)
    cos, sin = jnp.cos(ang), jnp.sin(ang)
    cos_t = jnp.tile(jnp.concatenate([cos, cos], -1), (1, LANES // HEAD_DIM))
    sin_t = jnp.tile(jnp.concatenate([-sin, sin], -1), (1, LANES // HEAD_DIM))
    return cos_t, sin_t


def _delta_tables():
    r = np.arange(PAIR)
    same = (r[:, None] // CHUNK) == (r[None, :] // CHUNK)
    low = same & (r[:, None] >= r[None, :])
    low_s = same & (r[:, None] > r[None, :])
    up = same & (r[:, None] <= r[None, :])
    up_s = same & (r[:, None] < r[None, :])
    levels = []
    for k in range(N_LEVELS):
        s = 1 << k
        levels.append(((r[:, None] // (2 * s)) == (r[None, :] // (2 * s))) & ((r[:, None] // s) != (r[None, :] // s)))
    masks = jnp.asarray(np.stack([low, low_s, up, up_s, np.eye(PAIR, dtype=bool)] + levels).astype(np.float32))
    return masks


def _segment_mean_table():
    r = np.arange(LANES)
    seg = ((r[:, None] // HEAD_DIM) == (r[None, :] // HEAD_DIM)).astype(np.float32) / HEAD_DIM
    hi = jnp.asarray(seg, BF16)
    lo = (jnp.asarray(seg) - hi.astype(F32)).astype(BF16)
    return jnp.stack([hi, lo])


def kernel(x_prompt, x_sample, cache_a_k, cache_a_v, cache_c_k, cache_c_v, state_b_fwd, state_b_bwd, c, c_ctx, w_mod, b_mod, norm1_g, norm2_g, w_in, a_sink, b_conv, b_a_log, b_dt_bias, b_norm_g, c_q_norm, c_k_norm, w_out, w_group, b_group, w_expert, b_expert, w1, w3, w2, final_norm_g):
    n_p, t_p, d = x_prompt.shape
    n_s, t_s, _ = x_sample.shape
    depth = w_in.shape[0]
    past = cache_a_k.shape[2]
    tok_p = n_p * t_p
    n_tok = tok_p + n_s * t_s
    assert d == D_MODEL and tok_p % t_s == 0 and t_s % 512 == 0 and t_p % 256 == 0

    w_in_t = jnp.swapaxes(w_in, 1, 2)
    w_out16 = w_out.astype(BF16)
    pad_g = jnp.zeros((depth, d, EXPERT_ROW0 - N_GROUPS), F32)
    pad_e = jnp.zeros((depth, d, LANES - EXPERT_ROW0 - N_EXPERTS), F32)
    w_r = jnp.concatenate([w_group, pad_g, w_expert, pad_e], -1)
    w_r_hi = w_r.astype(BF16)
    w_r2 = jnp.concatenate([w_r_hi, (w_r - w_r_hi.astype(F32)).astype(BF16)], axis=-1)
    b_r = jnp.concatenate([b_group, pad_g[:, 0], b_expert, pad_e[:, 0]], -1)[:, None, :]
    cqn = jnp.tile(c_q_norm, (1, 4))[:, None, :]
    ckn = jnp.tile(c_k_norm, (1, 2))[:, None, :]
    gate_prm = jnp.stack([b_a_log.reshape(depth, 8), b_dt_bias.reshape(depth, 8)], 1)
    prmr = jnp.broadcast_to(jnp.pad(gate_prm, ((0, 0), (0, 0), (0, N_AB - 8)))[..., None],
                            (depth, 2, N_AB, LANES))
    cos_t, sin_t = _rope_tables(t_s)
    masks = _delta_tables()
    seg = _segment_mean_table()

    cond = jnp.concatenate([c_ctx[None, :], c], axis=0)
    cond_b = jnp.broadcast_to(cond[:, :, None], cond.shape + (LANES,))
    mods_all = _mods_call(cond_b, w_mod, b_mod).reshape(depth, SUBLANES, 6, d)

    def slot_fn(tm):
        per_s = t_s // tm
        first = tok_p // tm
        return lambda i: jnp.where(i < first, 0, 1 + (i - first) // per_s)

    xs = (x_prompt.reshape(tok_p, d), x_sample.reshape(n_s * t_s, d))
    blk_s = tok_p // t_s
    ctx = tuple(a.reshape(n_s, depth, past, LANES) for a in (cache_a_k, cache_a_v, cache_c_k, cache_c_v))
    s0 = tuple(a.reshape(n_s, depth, B_HEADS * B_DIM, B_DIM) for a in (state_b_fwd, state_b_bwd))
    g1, g2, bng = norm1_g[:, None, :], norm2_g[:, None, :], b_norm_g[:, None, :]
    caches = None
    states = None
    tm = 512
    for l in range(depth):
        za, zb, zc, zab, zabt, *slab = _inproj_call(l, xs, mods_all, g1, w_in_t, slot_fn(tm), tm)
        x = slab[0] if slab else xs[0]

        ao, co, *caches = _attn_call(False, t_p, ATTN_NSEQ, n_p, 0, l, za, zc, a_sink, cqn, ckn, seg, prev=caches)
        ao, co = _attn_call(True, t_s, 1, n_s, blk_s, l, za, zc, a_sink, cqn, ckn, seg, prev=(ao, co),
                            rope=(cos_t, sin_t), ctx=ctx)

        bo, *states = _delta_call(False, t_p, DELTA_NSEQ, n_p, 0, l, zb, zab, zabt, b_conv, prmr, bng, masks,
                                  prev=states)
        (bo,) = _delta_call(True, t_s, 1, n_s, blk_s, l, zb, zab, zabt, b_conv, prmr, bng, masks,
                            prev=(bo,), s0=s0)

        x = _ffn_call(l, x, ao, bo, co, mods_all, g2, w_out16, w_r2, b_r, w1, w3, w2, slot_fn(1024), 1024, 512)
        xs = (x,)

    y_prompt = _final_norm_call(x, final_norm_g[None], tm, 0, tok_p).reshape(n_p, t_p, d)
    y_sample = _final_norm_call(x, final_norm_g[None], tm, tok_p, n_s * t_s).reshape(n_s, t_s, d)
    new_ak, new_av, new_ck, new_cv = (a.reshape(n_p, depth, t_p, 2, HEAD_DIM) for a in caches)
    new_sf, new_sb = (a.reshape(n_p, depth, B_HEADS, B_DIM, B_DIM) for a in states)
    return (y_prompt, y_sample, new_ak, new_av, new_ck, new_cv, new_sf, new_sb)
```

```python
import functools

import jax
import jax.numpy as jnp
import numpy as np
from jax import lax
from jax.experimental import pallas as pl
from jax.experimental.pallas import tpu as pltpu

F32 = jnp.float32
BF16 = jnp.bfloat16

D_MODEL = 1024
GRID_W = 64
EPS = 1e-6
NEG_INF = -1e30
ROPE_THETA = 10000.0
HEAD_DIM = 64
Q_W = 256
KV_W = 128
WINDOW = 128
Q_BLOCK = 128
B_HEADS = 4
B_DIM = 128
CHUNK = 64
BD = B_HEADS * CHUNK
PAIR = 2 * CHUNK
N_LEVELS = 6
PREP_UNROLL = 4
DELTA_NSEQ = 4
ATTN_NSEQ = 4
N_GROUPS = 4
EXPERTS_PER_GROUP = 4
N_EXPERTS = 16
D_EXPERT = 256
EXPERT_ROW0 = 8

LANES = 128
SUBLANES = 8
VMEM_LIMIT = 60000 * 1024

TM_PROJ = 512
TM_FFN, TH_FFN = 1024, 512
TM_NORM = 512
MODS_TN = 1536

ZA_W, ZB_W, ZC_W, ZAB_W = 512, 2048, 512, 128
N_AB = 16
Z_W = ZA_W + ZB_W + ZC_W + ZAB_W


def _sigmoid(x):
    return 1.0 / (1.0 + jnp.exp(-x))


def _silu(x):
    return x * _sigmoid(x)


def _softplus(x):
    return jnp.maximum(x, 0.0) + jnp.log1p(jnp.exp(-jnp.abs(x)))


def _dot(a, b):
    return jnp.dot(a, b, preferred_element_type=F32)


def _dot_nt(a, b):
    return lax.dot_general(a, b, (((1,), (1,)), ((), ())), preferred_element_type=F32)


def _dot_tn(a, b):
    return lax.dot_general(a, b, (((0,), (0,)), ((), ())), preferred_element_type=F32)


def _split2(x):
    hi = x.astype(BF16)
    lo = (x - hi.astype(F32)).astype(BF16)
    return hi, lo


def _params(sem=None):
    return pltpu.CompilerParams(dimension_semantics=sem, vmem_limit_bytes=VMEM_LIMIT)


def _mods_kernel(cond_ref, w_ref, b_ref, o_ref, act_s):
    n_cond = cond_ref.shape[0]
    tn = w_ref.shape[2]
    reps = tn // LANES

    @pl.when((pl.program_id(0) == 0) & (pl.program_id(1) == 0))
    def _():
        act_s[...] = _silu(cond_ref[...])

    def body(kb, accs):
        r = pl.multiple_of(kb * SUBLANES, SUBLANES)
        w = w_ref[0, pl.ds(r, SUBLANES), :]
        return tuple(acc + jnp.tile(act_s[m, pl.ds(r, SUBLANES), :], (1, reps)) * w for m, acc in enumerate(accs))

    zero = jnp.zeros((SUBLANES, tn), F32)
    accs = lax.fori_loop(0, w_ref.shape[1] // SUBLANES, body, (zero,) * n_cond, unroll=4)
    rows = [jnp.sum(a, axis=0, keepdims=True) + b_ref[0] for a in accs]
    rows.append(jnp.zeros((SUBLANES - n_cond, tn), F32))
    o_ref[0] = jnp.concatenate(rows, axis=0)


def _mods_call(cond_b, w_mod, b_mod):
    depth, d, n = w_mod.shape
    tn = MODS_TN
    n_cond = cond_b.shape[0]
    return pl.pallas_call(
        _mods_kernel,
        grid=(depth, n // tn),
        in_specs=[
            pl.BlockSpec((n_cond, d, LANES), lambda l, j: (0, 0, 0)),
            pl.BlockSpec((1, d, tn), lambda l, j: (l, 0, j)),
            pl.BlockSpec((1, 1, tn), lambda l, j: (l, 0, j)),
        ],
        out_specs=pl.BlockSpec((1, SUBLANES, tn), lambda l, j: (l, 0, j)),
        out_shape=jax.ShapeDtypeStruct((depth, SUBLANES, n), F32),
        scratch_shapes=[pltpu.VMEM((n_cond, d, LANES), F32)],
        compiler_params=_params(("arbitrary", "arbitrary")),
        name="mods",
    )(cond_b, w_mod, b_mod.reshape(depth, 1, n))


def _x_specs(xs, tm):
    if len(xs) == 1:
        return [pl.BlockSpec((tm, D_MODEL), lambda i, *_: (i, 0))]
    first = xs[0].shape[0] // tm
    return [pl.BlockSpec((tm, D_MODEL), lambda i, *_: (jnp.minimum(i, first - 1), 0)),
            pl.BlockSpec((tm, D_MODEL), lambda i, *_: (jnp.maximum(i - first, 0), 0))]


def _x_tile(x_refs, first):
    if len(x_refs) == 1:
        return x_refs[0][...]
    return jnp.where(pl.program_id(0) < first, x_refs[0][...], x_refs[1][...])


def _modulated_norm(x, g, shift, scale):
    ms = jnp.mean(x * x, axis=-1, keepdims=True)
    y = x * lax.rsqrt(ms + EPS) * g
    return y * (1.0 + scale) + shift


def _inproj_kernel(n_x, first, *refs):
    x_refs = refs[:n_x]
    mod_ref, g_ref, wt_ref, za_ref, zb_ref, zc_ref, zab_ref, zabt_ref = refs[n_x:n_x + 8]
    w_s = refs[-1]
    @pl.when(pl.program_id(0) == 0)
    def _():
        ab0 = ZA_W + ZB_W
        w_s[0:ab0, :] = wt_ref[0, 0:ab0, :].astype(BF16)
        w_s[ab0:ab0 + ZC_W, :] = wt_ref[0, ab0 + N_AB:ab0 + N_AB + ZC_W, :].astype(BF16)
        w_s[ab0 + ZC_W:ab0 + ZC_W + N_AB, :] = wt_ref[0, ab0:ab0 + N_AB, :].astype(BF16)
        w_s[ab0 + ZC_W + N_AB:Z_W, :] = jnp.zeros((ZAB_W - N_AB, D_MODEL), BF16)

    m = mod_ref[0, 0]
    x = _x_tile(x_refs, first)
    if n_x > 1:
        refs[n_x + 8][...] = x
    h = _modulated_norm(x, g_ref[0], m[0:1], m[1:2]).astype(BF16)
    za_ref[...] = _dot_nt(h, w_s[0:ZA_W, :])
    step = 512
    for j in range(ZB_W // step):
        zb_ref[:, j * step:(j + 1) * step] = _dot_nt(h, w_s[ZA_W + j * step:ZA_W + (j + 1) * step, :])
    zc_ref[...] = _dot_nt(h, w_s[ZA_W + ZB_W:ZA_W + ZB_W + ZC_W, :])
    zab = _dot_nt(h, w_s[ZA_W + ZB_W + ZC_W:Z_W, :])
    zab_ref[...] = zab
    zabt_ref[...] = zab.T[:N_AB]


def _inproj_call(layer, xs, mods, g, w, slot_fn, tm):
    n_tok = sum(a.shape[0] for a in xs)
    n_ab = N_AB
    return pl.pallas_call(
        functools.partial(_inproj_kernel, len(xs), xs[0].shape[0] // tm),
        grid=(n_tok // tm,),
        in_specs=_x_specs(xs, tm) + [
            pl.BlockSpec((1, 1, 6, D_MODEL), lambda i: (layer, slot_fn(i), 0, 0)),
            pl.BlockSpec((1, 1, D_MODEL), lambda i: (layer, 0, 0)),
            pl.BlockSpec((1, w.shape[1], D_MODEL), lambda i: (layer, 0, 0)),
        ],
        out_specs=[
            pl.BlockSpec((tm, ZA_W), lambda i: (i, 0)),
            pl.BlockSpec((tm, ZB_W), lambda i: (i, 0)),
            pl.BlockSpec((tm, ZC_W), lambda i: (i, 0)),
            pl.BlockSpec((tm, ZAB_W), lambda i: (i, 0)),
            pl.BlockSpec((n_ab, tm), lambda i: (0, i)),
        ] + ([pl.BlockSpec((tm, D_MODEL), lambda i: (i, 0))] if len(xs) > 1 else []),
        out_shape=[
            jax.ShapeDtypeStruct((n_tok, ZA_W), F32),
            jax.ShapeDtypeStruct((n_tok, ZB_W), F32),
            jax.ShapeDtypeStruct((n_tok, ZC_W), F32),
            jax.ShapeDtypeStruct((n_tok, ZAB_W), F32),
            jax.ShapeDtypeStruct((n_ab, n_tok), F32),
        ] + ([jax.ShapeDtypeStruct((n_tok, D_MODEL), F32)] if len(xs) > 1 else []),
        scratch_shapes=[pltpu.VMEM((Z_W, D_MODEL), BF16)],
        compiler_params=_params(("arbitrary",)),
        name="inproj",
    )(*xs, mods, g, w)


def _lane_lo(shape):
    return lax.broadcasted_iota(jnp.int32, shape, len(shape) - 1) % LANES < HEAD_DIM


def _store_kdup(dst_ref, off, k):
    n = k.shape[0]
    r = pltpu.roll(k, HEAD_DIM, 1)
    lo = _lane_lo(k.shape)
    dst_ref[0, off:off + n, :] = jnp.where(lo, k, r).astype(BF16)
    dst_ref[1, off:off + n, :] = jnp.where(lo, r, k).astype(BF16)


def _store_vsplit(dst_ref, off, v):
    n = v.shape[0]
    r = pltpu.roll(v, HEAD_DIM, 1)
    lo = _lane_lo(v.shape)
    z = jnp.zeros_like(v)
    dst_ref[0, off:off + n, :] = jnp.where(lo, v, z).astype(BF16)
    dst_ref[1, off:off + n, :] = jnp.where(lo, z, r).astype(BF16)
    dst_ref[2, off:off + n, :] = jnp.where(lo, r, z).astype(BF16)
    dst_ref[3, off:off + n, :] = jnp.where(lo, z, v).astype(BF16)


def _rope(x, cos, sin):
    first = (lax.broadcasted_iota(jnp.int32, x.shape, 1) // (HEAD_DIM // 2)) % 2 == 0
    partner = jnp.where(first, pltpu.roll(x, LANES - HEAD_DIM // 2, 1), pltpu.roll(x, HEAD_DIM // 2, 1))
    return x * cos + partner * sin


def _head_rmsnorm(x, g, seg_hi, seg_lo):
    hi, lo = _split2(x * x)
    ms = _dot(hi, seg_hi) + _dot(lo, seg_hi) + _dot(hi, seg_lo)
    return x * lax.rsqrt(ms + EPS) * g


def _attend_many(units):
    qb = units[0][0].shape[0]
    lo = _lane_lo(units[0][0].shape)
    all_scores = []
    for qt, segs, _ in units:
        z = jnp.zeros_like(qt)
        qs = jnp.concatenate([jnp.where(lo, qt, z), jnp.where(lo, z, qt)], axis=0).astype(BF16)
        scores = []
        for kdup, _, _, mask in segs:
            s = _dot_nt(qs, kdup)
            if mask is not None:
                s = jnp.where(mask, s, NEG_INF)
            scores.append(s)
        all_scores.append(scores)
    probs = []
    for (qt, segs, sink_pair), scores in zip(units, all_scores):
        m = scores[0].max(axis=1, keepdims=True)
        for s in scores[1:]:
            m = jnp.maximum(m, s.max(axis=1, keepdims=True))
        if sink_pair is not None:
            row_a = lax.broadcasted_iota(jnp.int32, (2 * qb, 1), 0) < qb
            sink = jnp.where(row_a, sink_pair[0], sink_pair[1])
            m = jnp.maximum(m, sink)
            denom = jnp.exp(sink - m)
        else:
            denom = jnp.zeros((2 * qb, 1), F32)
        ps = []
        for s in scores:
            p = jnp.exp(s - m)
            denom = denom + p.sum(axis=1, keepdims=True)
            ps.append(p.astype(BF16))
        probs.append((ps, 1.0 / denom))
    outs = []
    for (qt, segs, _), (ps, inv) in zip(units, probs):
        acc = jnp.zeros((qb, LANES), F32)
        for pb, (_, vlo, vhi, _) in zip(ps, segs):
            acc = acc + _dot(pb[:qb], vlo) + _dot(pb[qb:], vhi)
        outs.append(acc * jnp.where(lo, inv[:qb], inv[qb:]))
    return outs


def _attn_kernel(has_ctx, t, nseq, layer, *refs):
    if has_ctx:
        (sink_ref, za_ref, zc_ref, cqn_ref, ckn_ref, seg_ref, cos_ref, sin_ref,
         cak_ref, cav_ref, cck_ref, ccv_ref,
         ao_ref, co_ref,
         ka_s, va_s, kc_s, vc_s, kctx_s, vctx_s, qa_s, qc_s) = refs
    else:
        (sink_ref, za_ref, zc_ref, cqn_ref, ckn_ref, seg_ref,
         ao_ref, co_ref, nak_ref, nav_ref, nck_ref, ncv_ref,
         ka_s, va_s, kc_s, vc_s, qa_s, qc_s) = refs
    scale = HEAD_DIM ** -0.5
    seg_hi = seg_ref[0]
    seg_lo = seg_ref[1]
    piece = 256
    n_ctx = cak_ref.shape[2] if has_ctx else 0

    for p0 in range(0, nseq * t, piece):
        rows = slice(p0, p0 + piece)
        ak = za_ref[rows, Q_W:Q_W + KV_W]
        av = za_ref[rows, Q_W + KV_W:Q_W + 2 * KV_W]
        ck = _head_rmsnorm(zc_ref[rows, Q_W:Q_W + KV_W], ckn_ref[0], seg_hi, seg_lo)
        cv = zc_ref[rows, Q_W + KV_W:Q_W + 2 * KV_W]
        if has_ctx:
            cos = cos_ref[rows, :]
            sin = sin_ref[rows, :]
            ak = _rope(ak, cos, sin)
            ck = _rope(ck, cos, sin)
            _store_kdup(ka_s, WINDOW + p0, ak)
            _store_vsplit(va_s, WINDOW + p0, av)
            _store_kdup(kc_s, n_ctx + p0, ck)
            _store_vsplit(vc_s, n_ctx + p0, cv)
        else:
            crow = slice(p0 % t, p0 % t + piece)
            nak_ref[p0 // t, 0, crow, :] = ak
            nav_ref[p0 // t, 0, crow, :] = av
            nck_ref[p0 // t, 0, crow, :] = ck
            ncv_ref[p0 // t, 0, crow, :] = cv
            _store_kdup(ka_s, p0, ak)
            _store_vsplit(va_s, p0, av)
            _store_kdup(kc_s, p0, ck)
            _store_vsplit(vc_s, p0, cv)
        for hk in range(2):
            cols = slice(hk * LANES, (hk + 1) * LANES)
            aq = za_ref[rows, cols]
            cq = _head_rmsnorm(zc_ref[rows, cols], cqn_ref[0, :, cols], seg_hi, seg_lo)
            if has_ctx:
                aq = _rope(aq, cos, sin)
                cq = _rope(cq, cos, sin)
            qa_s[rows, cols] = aq * scale
            qc_s[rows, cols] = cq * scale

    if has_ctx:
        zpad = jnp.zeros((WINDOW, LANES), BF16)
        for i in range(2):
            ka_s[i, 0:WINDOW, :] = zpad
            ka_s[i, WINDOW + t:2 * WINDOW + t, :] = zpad
        for i in range(4):
            va_s[i, 0:WINDOW, :] = zpad
            va_s[i, WINDOW + t:2 * WINDOW + t, :] = zpad
        for p0 in range(0, n_ctx, piece):
            rows = slice(p0, p0 + piece)
            _store_kdup(kctx_s, p0, cak_ref[0, 0, rows, :])
            _store_vsplit(vctx_s, p0, cav_ref[0, 0, rows, :])
            _store_kdup(kc_s, p0, cck_ref[0, 0, rows, :])
            _store_vsplit(vc_s, p0, ccv_ref[0, 0, rows, :])

        qb = Q_BLOCK
        span = qb + 2 * WINDOW
        qi = lax.broadcasted_iota(jnp.int32, (2 * qb, span), 0) % qb
        kj = lax.broadcasted_iota(jnp.int32, (2 * qb, span), 1)
        band = jnp.abs(kj - WINDOW - qi) <= WINDOW

        def block(b, carry):
            r0 = pl.multiple_of(b * qb, qb)
            kpos = kj + (r0 - WINDOW)
            mask = band & (kpos >= 0) & (kpos < t)
            units = []
            for hk in range(2):
                cols = slice(hk * LANES, (hk + 1) * LANES)
                segs_a = [
                    (kctx_s[hk], vctx_s[2 * hk], vctx_s[2 * hk + 1], None),
                    (ka_s[hk, pl.ds(r0, span), :], va_s[2 * hk, pl.ds(r0, span), :],
                     va_s[2 * hk + 1, pl.ds(r0, span), :], mask),
                ]
                sinks = (sink_ref[layer, 2 * hk], sink_ref[layer, 2 * hk + 1])
                units.append((qa_s[pl.ds(r0, qb), cols], segs_a, sinks))
                segs_c = [(kc_s[hk], vc_s[2 * hk], vc_s[2 * hk + 1], None)]
                units.append((qc_s[pl.ds(r0, qb), cols], segs_c, None))
            outs = _attend_many(units)
            for hk in range(2):
                cols = slice(hk * LANES, (hk + 1) * LANES)
                ao_ref[pl.ds(r0, qb), cols] = outs[2 * hk]
                co_ref[pl.ds(r0, qb), cols] = outs[2 * hk + 1]
            return carry

        lax.fori_loop(0, t // qb, block, 0)
    else:
        units = []
        for q in range(nseq):
            seq = slice(q * t, (q + 1) * t)
            for hk in range(2):
                cols = slice(hk * LANES, (hk + 1) * LANES)
                sinks = (sink_ref[layer, 2 * hk], sink_ref[layer, 2 * hk + 1])
                units.append((qa_s[seq, cols],
                              [(ka_s[hk, seq, :], va_s[2 * hk, seq, :], va_s[2 * hk + 1, seq, :], None)], sinks))
                units.append((qc_s[seq, cols],
                              [(kc_s[hk, seq, :], vc_s[2 * hk, seq, :], vc_s[2 * hk + 1, seq, :], None)], None))
        outs = _attend_many(units)
        for q in range(nseq):
            seq = slice(q * t, (q + 1) * t)
            for hk in range(2):
                cols = slice(hk * LANES, (hk + 1) * LANES)
                ao_ref[seq, cols] = outs[4 * q + 2 * hk]
                co_ref[seq, cols] = outs[4 * q + 2 * hk + 1]


def _attn_call(has_ctx, t, nseq, n_batch, row_block0, layer, za, zc, sink, cqn, ckn, seg, prev=None, rope=None,
               ctx=None):
    n_tok = za.shape[0]
    depth = sink.shape[0]
    assert n_batch % nseq == 0 and (nseq == 1 or not has_ctx)
    tok_spec = lambda w: pl.BlockSpec((nseq * t, w), lambda b, *_: (row_block0 + b, 0))
    const = lambda shape: pl.BlockSpec(shape, lambda b, *_: (0,) * len(shape))
    layer_spec = lambda shape: pl.BlockSpec((1,) + shape, lambda b, *_: (layer,) + (0,) * len(shape))
    in_specs = [tok_spec(ZA_W), tok_spec(ZC_W), layer_spec((1, Q_W)), layer_spec((1, KV_W)), const((2, LANES, LANES))]
    args = [za, zc, cqn, ckn, seg]
    out_specs = [tok_spec(Q_W), tok_spec(Q_W)]
    out_shape = [jax.ShapeDtypeStruct((n_tok, Q_W), F32), jax.ShapeDtypeStruct((n_tok, Q_W), F32)]
    if has_ctx:
        n_ctx = ctx[0].shape[2]
        in_specs += [const((t, LANES)), const((t, LANES))]
        args += list(rope)
        in_specs += [pl.BlockSpec((1, 1, n_ctx, LANES), lambda b, *_: (b, layer, 0, 0))] * 4
        args += list(ctx)
        scratch = [
            pltpu.VMEM((2, t + 2 * WINDOW, LANES), BF16), pltpu.VMEM((4, t + 2 * WINDOW, LANES), BF16),
            pltpu.VMEM((2, n_ctx + t, LANES), BF16), pltpu.VMEM((4, n_ctx + t, LANES), BF16),
            pltpu.VMEM((2, n_ctx, LANES), BF16), pltpu.VMEM((4, n_ctx, LANES), BF16),
            pltpu.VMEM((t, Q_W), F32), pltpu.VMEM((t, Q_W), F32),
        ]
    else:
        cache_spec = pl.BlockSpec((nseq, 1, t, LANES), lambda b, *_: (b, layer, 0, 0))
        out_specs += [cache_spec] * 4
        out_shape += [jax.ShapeDtypeStruct((n_batch, depth, t, LANES), F32)] * 4
        rows = nseq * t
        scratch = [
            pltpu.VMEM((2, rows, LANES), BF16), pltpu.VMEM((4, rows, LANES), BF16),
            pltpu.VMEM((2, rows, LANES), BF16), pltpu.VMEM((4, rows, LANES), BF16),
            pltpu.VMEM((rows, Q_W), F32), pltpu.VMEM((rows, Q_W), F32),
        ]
    n_real = len(args)
    aliases = {}
    if prev is not None:
        first_out = 0 if has_ctx else 2
        for k, arr in enumerate(prev):
            in_specs.append(pl.BlockSpec(memory_space=pl.ANY))
            args.append(arr)
            aliases[1 + n_real + k] = first_out + k

    def body(*refs):
        ins = refs[:1 + n_real]
        rest = refs[1 + len(args):]
        _attn_kernel(has_ctx, t, nseq, layer, *ins, *rest)

    return pl.pallas_call(
        body,
        grid_spec=pltpu.PrefetchScalarGridSpec(
            num_scalar_prefetch=1, grid=(n_batch // nseq,), in_specs=in_specs, out_specs=out_specs,
            scratch_shapes=scratch),
        out_shape=out_shape,
        input_output_aliases=aliases,
        compiler_params=_params(("arbitrary",)),
        name="attn_latent" if has_ctx else "attn_prompt",
    )(sink, *args)


def _stack_pair(x, p):
    return jnp.concatenate([x[:, (2 * p + hl) * B_DIM:(2 * p + hl + 1) * B_DIM] for hl in range(2)], axis=0)


def _delta_kernel(t, nseq, has_s0, *refs):
    if has_s0:
        (zb_ref, abc_ref, abt_ref, conv_ref, prmr_ref, bng_ref, mask_ref,
         s0f_ref, s0b_ref, o_ref, qkv_s, of_s, ob_s, sf_s, sb_s, u_s, wq_s, at_s, kd_s, eg_s,
         pre_s, suf_s, prec_s, sufc_s) = refs
    else:
        (zb_ref, abc_ref, abt_ref, conv_ref, prmr_ref, bng_ref, mask_ref,
         o_ref, sfo_ref, sbo_ref, qkv_s, of_s, ob_s, sf_s, sb_s, u_s, wq_s, at_s, kd_s, eg_s,
         pre_s, suf_s, prec_s, sufc_s) = refs
    n_chunks = t // CHUNK
    n_total = nseq * n_chunks
    s_rows = B_HEADS * B_DIM
    qk_w = B_HEADS * B_DIM

    row = lax.broadcasted_iota(jnp.int32, (t, LANES), 0)
    for q in range(nseq):
        seq = slice(q * t, (q + 1) * t)
        for j in range(3 * B_HEADS):
            cols = slice(j * LANES, (j + 1) * LANES)
            x = zb_ref[seq, cols]
            prev = jnp.where(row == 0, 0.0, pltpu.roll(x, 1, 0))
            nxt = jnp.where(row == t - 1, 0.0, pltpu.roll(x, t - 1, 0))
            y = _silu(prev * conv_ref[0, 0:1, cols] + x * conv_ref[0, 1:2, cols] + nxt * conv_ref[0, 2:3, cols])
            if j < 2 * B_HEADS:
                y = y * lax.rsqrt(jnp.sum(y * y, axis=-1, keepdims=True) + EPS)
            if j < B_HEADS:
                y = y * (B_DIM ** -0.5)
            qkv_s[seq, cols] = y

    if has_s0:
        for q in range(nseq):
            sf_s[q * s_rows:(q + 1) * s_rows, :] = s0f_ref[q, 0]
            sb_s[q * s_rows:(q + 1) * s_rows, :] = s0b_ref[q, 0]
    else:
        sf_s[...] = jnp.zeros_like(sf_s)
        sb_s[...] = jnp.zeros_like(sb_s)

    reps = nseq * t // LANES
    gr = -jnp.tile(jnp.exp(prmr_ref[0, 0]), (1, reps)) * _softplus(abt_ref[...] + jnp.tile(prmr_ref[0, 1], (1, reps)))
    seg_lane = lax.broadcasted_iota(jnp.int32, gr.shape, 1) % CHUNK
    pre, suf = gr, gr
    for s in (1, 2, 4, 8, 16, 32):
        pre = pre + jnp.where(seg_lane >= s, pltpu.roll(pre, s, 1), 0.0)
        suf = suf + jnp.where(seg_lane < CHUNK - s, pltpu.roll(suf, nseq * t - s, 1), 0.0)
    pre_s[...] = pre
    suf_s[...] = suf
    zrows = jnp.zeros((LANES - pre.shape[0], LANES), F32)
    for j in range(reps):
        tile = slice(j * LANES, (j + 1) * LANES)
        prec_s[tile, :] = jnp.concatenate([pre[:, tile], zrows], axis=0).T
        sufc_s[tile, :] = jnp.concatenate([suf[:, tile], zrows], axis=0).T
    lane_lo = lax.broadcasted_iota(jnp.int32, (1, LANES), 1) < CHUNK

    def prepare(cc, carry):
        chains = []
        for k in range(PREP_UNROLL):
            c = cc * PREP_UNROLL + k
            r0 = pl.multiple_of(c * CHUNK, CHUNK)
            b_all = _sigmoid(abc_ref[pl.ds(r0, CHUNK), :])
            run_c = (prec_s[pl.ds(r0, CHUNK), :], sufc_s[pl.ds(r0, CHUNK), :])
            tile0 = pl.multiple_of((cc * PREP_UNROLL + k - k % 2) * CHUNK, LANES)
            run = (pre_s[:, pl.ds(tile0, LANES)], suf_s[:, pl.ds(tile0, LANES)])
            run_r = tuple(pltpu.roll(x, CHUNK, 1) for x in run)
            for p in range(B_HEADS // 2):
                kst = _stack_pair(qkv_s[pl.ds(r0, CHUNK), qk_w:2 * qk_w], p)
                qst = _stack_pair(qkv_s[pl.ds(r0, CHUNK), 0:qk_w], p)
                vst = _stack_pair(qkv_s[pl.ds(r0, CHUNK), 2 * qk_w:3 * qk_w], p)
                kq = _dot_nt(jnp.concatenate([kst, qst], axis=0).astype(BF16), kst.astype(BF16))
                for d in range(2):
                    cg = 4 * d + 2 * p
                    edge = CHUNK - 1 if d == 0 else 0
                    rep_col = lambda x, col: jnp.broadcast_to(x[:, col:col + 1], (CHUNK, LANES))
                    b_rep = jnp.concatenate([rep_col(b_all, 8 + cg + hl) for hl in range(2)], axis=0)
                    gcol = jnp.concatenate([rep_col(run_c[d], cg + hl) for hl in range(2)], axis=0)
                    gtot = jnp.concatenate([rep_col(run_c[d][edge:edge + 1], cg + hl) for hl in range(2)], axis=0)
                    ra = cg
                    if k % 2 == 0:
                        grow = jnp.where(lane_lo, run[d][ra:ra + 1], run_r[d][ra + 1:ra + 2])
                    else:
                        grow = jnp.where(lane_lo, run_r[d][ra:ra + 1], run[d][ra + 1:ra + 2])
                    chains.append(dict(c=c, p=p, d=d, kst=kst, qst=qst, vst=vst, kq=kq, b_st=b_rep,
                                       gcol=gcol, gtot=gtot, grow=grow))

        for ch in chains:
            d, b_st, kq, gcol = ch["d"], ch["b_st"], ch.pop("kq"), ch["gcol"]
            decay = jnp.exp(jnp.minimum(gcol - ch.pop("grow"), 0.0))
            ch["a_mat"] = (b_st * kq[:PAIR]) * (decay * mask_ref[2 * d + 1])
            ch["attn"] = (kq[PAIR:] * (decay * mask_ref[2 * d])).astype(BF16)
            ch["t_inv"] = mask_ref[4] - ch["a_mat"] * mask_ref[5]
        for lvl in range(N_LEVELS - 1):
            for ch in chains:
                ch["t16"] = ch["t_inv"].astype(BF16)
                ch["et"] = _dot((ch["a_mat"] * mask_ref[6 + lvl]).astype(BF16), ch["t16"])
            for ch in chains:
                ch["t_inv"] = ch["t_inv"] - _dot(ch.pop("t16"), ch.pop("et").astype(BF16))
        for ch in chains:
            egc = jnp.exp(ch["gcol"])
            rk = jnp.concatenate([ch["b_st"] * ch["vst"], (ch["b_st"] * egc) * ch["kst"]], axis=1)
            ch["rk"] = _dot(ch.pop("t_inv").astype(BF16), rk.astype(BF16))
            ch["qp16"] = (ch["qst"] * egc).astype(BF16)
        for ch in chains:
            c, p, d, rk, qp16 = ch["c"], ch["p"], ch["d"], ch["rk"], ch["qp16"]
            pair_rows = slice(p * PAIR, (p + 1) * PAIR)
            w16 = rk[:, B_DIM:].astype(BF16)
            u_s[d, c, pair_rows, :] = rk[:, :B_DIM]
            at_s[d, c, p] = ch["attn"]
            kd_s[d, c, pair_rows, :] = (ch["kst"] * jnp.exp(ch["gtot"] - ch["gcol"])).astype(BF16)
            eg = jnp.exp(ch["gtot"])
            for hl in range(2):
                h = 2 * p + hl
                rows = slice(hl * CHUNK, (hl + 1) * CHUNK)
                wq_s[d, c, h * 2 * CHUNK:h * 2 * CHUNK + CHUNK, :] = w16[rows]
                wq_s[d, c, h * 2 * CHUNK + CHUNK:(h + 1) * 2 * CHUNK, :] = qp16[rows]
                eg_s[d, c, h * SUBLANES:(h + 1) * SUBLANES, :] = eg[hl * CHUNK:hl * CHUNK + SUBLANES, :]
        return carry

    lax.fori_loop(0, n_total // PREP_UNROLL, prepare, 0)

    def scan_step(i, carry):
        units = []
        for q in range(nseq):
            for d, s_ref, o_s in ((0, sf_s, of_s), (1, sb_s, ob_s)):
                c = q * n_chunks + (i if d == 0 else n_chunks - 1 - i)
                units.append(dict(q=q, d=d, c=c, s_ref=s_ref, o_s=o_s, r0=pl.multiple_of(c * CHUNK, CHUNK)))
        for un in units:
            q, d, c, s_ref = un["q"], un["d"], un["c"], un["s_ref"]
            un["x"] = []
            for h in range(B_HEADS):
                srows = slice(q * s_rows + h * B_DIM, q * s_rows + (h + 1) * B_DIM)
                un["x"].append(_dot(wq_s[d, c, h * 2 * CHUNK:(h + 1) * 2 * CHUNK, :], s_ref[srows, :].astype(BF16)))
        for un in units:
            d, c = un["d"], un["c"]
            un["vp16"], un["o"] = [], []
            for p in range(B_HEADS // 2):
                xs = un["x"][2 * p:2 * p + 2]
                v_new = jnp.concatenate(
                    [u_s[d, c, (2 * p + hl) * CHUNK:(2 * p + hl + 1) * CHUNK, :] - xs[hl][:CHUNK] for hl in range(2)],
                    axis=0)
                vp16 = v_new.astype(BF16)
                un["vp16"].append(vp16)
                un["o"].append(jnp.concatenate([xs[hl][CHUNK:] for hl in range(2)], axis=0)
                               + _dot(at_s[d, c, p], vp16))
        for un in units:
            q, d, c, s_ref, o_s, r0 = un["q"], un["d"], un["c"], un["s_ref"], un["o_s"], un["r0"]
            for h in range(B_HEADS):
                p, hl = divmod(h, 2)
                rows = slice(hl * CHUNK, (hl + 1) * CHUNK)
                srows = slice(q * s_rows + h * B_DIM, q * s_rows + (h + 1) * B_DIM)
                upd = _dot_tn(kd_s[d, c, h * CHUNK:(h + 1) * CHUNK, :], un["vp16"][p][rows])
                eg = jnp.tile(eg_s[d, c, h * SUBLANES:(h + 1) * SUBLANES, :], (B_DIM // SUBLANES, 1))
                s_ref[srows, :] = s_ref[srows, :] * eg + upd
                o_s[pl.ds(r0, CHUNK), h * B_DIM:(h + 1) * B_DIM] = un["o"][p][rows]
        return carry

    lax.fori_loop(0, n_chunks, scan_step, 0)

    if not has_s0:
        for q in range(nseq):
            sfo_ref[q, 0] = sf_s[q * s_rows:(q + 1) * s_rows, :]
            sbo_ref[q, 0] = sb_s[q * s_rows:(q + 1) * s_rows, :]

    for h in range(B_HEADS):
        cols = slice(h * B_DIM, (h + 1) * B_DIM)
        x = of_s[:, cols] + ob_s[:, cols]
        yn = x * lax.rsqrt(jnp.mean(x * x, axis=-1, keepdims=True) + EPS) * bng_ref[0]
        o_ref[:, cols] = yn * _silu(zb_ref[:, 3 * qk_w + h * B_DIM:3 * qk_w + (h + 1) * B_DIM])


def _delta_call(has_s0, t, nseq, n_batch, row_block0, layer, zb, zab, zabt, conv, prmr, bng, masks,
                prev=None, s0=None):
    n_tok = zb.shape[0]
    depth = conv.shape[0]
    n_chunks = nseq * (t // CHUNK)
    assert n_chunks % PREP_UNROLL == 0 and PREP_UNROLL % 2 == 0 and n_batch % nseq == 0
    tok_spec = lambda w: pl.BlockSpec((nseq * t, w), lambda b: (row_block0 + b, 0))
    const = lambda shape: pl.BlockSpec(shape, lambda b: (0,) * len(shape))
    layer_spec = lambda shape: pl.BlockSpec((1,) + shape, lambda b: (layer,) + (0,) * len(shape))
    s_shape = (B_HEADS * B_DIM, B_DIM)
    s_spec = pl.BlockSpec((nseq, 1) + s_shape, lambda b: (b, layer, 0, 0))
    n_ab = zabt.shape[0]
    in_specs = [
        tok_spec(ZB_W), tok_spec(ZAB_W),
        pl.BlockSpec((n_ab, nseq * t), lambda b: (0, row_block0 + b)),
        layer_spec((3, 3 * B_HEADS * B_DIM)), layer_spec((2, n_ab, LANES)),
        layer_spec((1, B_DIM)),
        const((5 + N_LEVELS, PAIR, PAIR)),
    ]
    args = [zb, zab, zabt, conv, prmr, bng, masks]
    out_specs = [tok_spec(B_HEADS * B_DIM)]
    out_shape = [jax.ShapeDtypeStruct((n_tok, B_HEADS * B_DIM), F32)]
    if has_s0:
        in_specs += [s_spec, s_spec]
        args += [s0[0], s0[1]]
    else:
        out_specs += [s_spec, s_spec]
        out_shape += [jax.ShapeDtypeStruct((n_batch, depth) + s_shape, F32)] * 2
    n_real = len(args)
    aliases = {}
    if prev is not None:
        first_out = 0 if has_s0 else 1
        for k, arr in enumerate(prev):
            in_specs.append(pl.BlockSpec(memory_space=pl.ANY))
            args.append(arr)
            aliases[n_real + k] = first_out + k
    rows = nseq * t
    scratch = [
        pltpu.VMEM((rows, 3 * B_HEADS * B_DIM), F32),
        pltpu.VMEM((rows, B_HEADS * B_DIM), F32), pltpu.VMEM((rows, B_HEADS * B_DIM), F32),
        pltpu.VMEM((nseq * s_shape[0], B_DIM), F32), pltpu.VMEM((nseq * s_shape[0], B_DIM), F32),
        pltpu.VMEM((2, n_chunks, BD, B_DIM), F32),
        pltpu.VMEM((2, n_chunks, 2 * BD, B_DIM), BF16),
        pltpu.VMEM((2, n_chunks, B_HEADS // 2, PAIR, PAIR), BF16),
        pltpu.VMEM((2, n_chunks, BD, B_DIM), BF16),
        pltpu.VMEM((2, n_chunks, B_HEADS * SUBLANES, LANES), F32),
        pltpu.VMEM((n_ab, rows), F32), pltpu.VMEM((n_ab, rows), F32),
        pltpu.VMEM((rows, LANES), F32), pltpu.VMEM((rows, LANES), F32),
    ]

    def body(*refs):
        _delta_kernel(t, nseq, has_s0, *refs[:n_real], *refs[len(args):])

    return pl.pallas_call(
        body,
        grid=(n_batch // nseq,),
        in_specs=in_specs,
        out_specs=out_specs,
        out_shape=out_shape,
        scratch_shapes=scratch,
        input_output_aliases=aliases,
        compiler_params=_params(("arbitrary",)),
        name="delta_latent" if has_s0 else "delta_prompt",
    )(*args)


def _outproj_router(x, ma, mb, mc, m, g, wo_ref, wr_ref, br):
    n = x.shape[0]
    b0, c0 = Q_W, Q_W + B_HEADS * B_DIM
    y = (_dot(ma.astype(BF16), wo_ref[0, 0:b0, :])
         + _dot(mb.astype(BF16), wo_ref[0, b0:c0, :])
         + _dot(mc.astype(BF16), wo_ref[0, c0:c0 + Q_W, :]))
    x1 = x + m[2:3] * y
    h2 = _modulated_norm(x1, g, m[3:4], m[4:5])
    hi, lo = _split2(h2)

    hw = _dot(hi, wr_ref[0])
    logits = (hw[:, :LANES] + hw[:, LANES:] + _dot(lo, wr_ref[0, :, :LANES]) + br).T
    gl = logits[0:N_GROUPS]
    grow = lax.broadcasted_iota(jnp.int32, gl.shape, 0)
    gmax = gl.max(axis=0, keepdims=True)
    g_sel = jnp.where(gl == gmax, grow, N_GROUPS).min(axis=0, keepdims=True)
    g_w = 1.0 / jnp.exp(gl - gmax).sum(axis=0, keepdims=True)
    el = logits[EXPERT_ROW0:EXPERT_ROW0 + N_EXPERTS]
    e_idx = lax.broadcasted_iota(jnp.int32, el.shape, 0)
    el = jnp.where((e_idx // EXPERTS_PER_GROUP) == g_sel, el, -jnp.inf)
    m1 = el.max(axis=0, keepdims=True)
    i1 = jnp.where(el == m1, e_idx, N_EXPERTS).min(axis=0, keepdims=True)
    el2 = jnp.where(e_idx == i1, -jnp.inf, el)
    m2 = el2.max(axis=0, keepdims=True)
    i2 = jnp.where(el2 == m2, e_idx, N_EXPERTS).min(axis=0, keepdims=True)
    tt = jnp.exp(m2 - m1)
    w1 = g_w / (1.0 + tt)
    w2 = w1 * tt
    gate_t = jnp.where(e_idx == i1, w1, 0.0) + jnp.where(e_idx == i2, w2, 0.0)
    gate = jnp.concatenate([gate_t, jnp.zeros((LANES - N_EXPERTS, n), F32)], axis=0).T
    return x1, hi, gate


def _ffn_kernel(x_ref, ma_ref, mb_ref, mc_ref, mod_ref, g_ref, wo_ref, wr_ref, br_ref, w1_ref, w3_ref, w2_ref,
                o_ref, x1_s, h_s, gate_s, acc_s):
    j = pl.program_id(1)
    tm = x_ref.shape[0]
    th = w1_ref.shape[2]
    m = mod_ref[0, 0]

    @pl.when(j == 0)
    def _():
        sub = 1024
        for r0 in range(0, tm, sub):
            rows = slice(r0, r0 + sub)
            x1, hi, gate = _outproj_router(x_ref[rows, :], ma_ref[rows, :], mb_ref[rows, :], mc_ref[rows, :],
                                           m, g_ref[0], wo_ref, wr_ref, br_ref[0])
            x1_s[rows, :] = x1
            h_s[rows, :] = hi
            gate_s[rows, :] = gate
        acc_s[...] = jnp.zeros_like(acc_s)

    h = h_s[...]
    hid = _silu(_dot(h, w1_ref[0].astype(BF16))) * _dot(h, w3_ref[0].astype(BF16))
    gate = gate_s[...]
    lane = lax.broadcasted_iota(jnp.int32, gate.shape, 1)
    n_e = th // D_EXPERT
    col = lax.broadcasted_iota(jnp.int32, hid.shape, 1) // D_EXPERT
    gmat = jnp.zeros(hid.shape, F32)
    for e in range(n_e):
        ge = jnp.where(lane == j * n_e + e, gate, 0.0).sum(axis=1, keepdims=True)
        gmat = jnp.where(col == e, ge, gmat)
    acc_s[...] += _dot((hid * gmat).astype(BF16), w2_ref[0].astype(BF16))

    @pl.when(j == pl.num_programs(1) - 1)
    def _():
        o_ref[...] = x1_s[...] + m[5:6] * acc_s[...]


def _ffn_call(layer, x, ma, mb, mc, mods, g, wo, wr, br, w1, w3, w2, slot_fn, tm, th):
    n_tok = x.shape[0]
    ef = w1.shape[2]
    tok = lambda w: pl.BlockSpec((tm, w), lambda i, j: (i, 0))
    layer_spec = lambda shape: pl.BlockSpec((1,) + shape, lambda i, j: (layer,) + (0,) * len(shape))
    return pl.pallas_call(
        _ffn_kernel,
        grid=(n_tok // tm, ef // th),
        in_specs=[tok(D_MODEL), tok(Q_W), tok(B_HEADS * B_DIM), tok(Q_W),
                  pl.BlockSpec((1, 1, 6, D_MODEL), lambda i, j: (layer, slot_fn(i), 0, 0)),
                  layer_spec((1, D_MODEL)), layer_spec((D_MODEL, D_MODEL)), layer_spec((D_MODEL, 2 * LANES)),
                  layer_spec((1, LANES)),
                  pl.BlockSpec((1, D_MODEL, th), lambda i, j: (layer, 0, j)),
                  pl.BlockSpec((1, D_MODEL, th), lambda i, j: (layer, 0, j)),
                  pl.BlockSpec((1, th, D_MODEL), lambda i, j: (layer, j, 0))],
        out_specs=tok(D_MODEL),
        out_shape=jax.ShapeDtypeStruct((n_tok, D_MODEL), F32),
        scratch_shapes=[pltpu.VMEM((tm, D_MODEL), F32), pltpu.VMEM((tm, D_MODEL), BF16),
                        pltpu.VMEM((tm, LANES), F32), pltpu.VMEM((tm, D_MODEL), F32)],
        compiler_params=_params(("arbitrary", "arbitrary")),
        name="ffn",
    )(x, ma, mb, mc, mods, g, wo, wr, br, w1, w3, w2)


def _final_norm_kernel(x_ref, g_ref, o_ref):
    x = x_ref[...]
    o_ref[...] = x * lax.rsqrt(jnp.mean(x * x, axis=-1, keepdims=True) + EPS) * g_ref[...]


def _final_norm_call(x, g, tm, row0, n_rows):
    blk0 = row0 // tm
    return pl.pallas_call(
        _final_norm_kernel,
        grid=(n_rows // tm,),
        in_specs=[pl.BlockSpec((tm, D_MODEL), lambda i: (blk0 + i, 0)), pl.BlockSpec((1, D_MODEL), lambda i: (0, 0))],
        out_specs=pl.BlockSpec((tm, D_MODEL), lambda i: (i, 0)),
        out_shape=jax.ShapeDtypeStruct((n_rows, D_MODEL), F32),
        compiler_params=_params(("arbitrary",)),
        name="final_norm",
    )(x, g)


def _rope_tables(t):
    pos = np.arange(t)
    n_freq = HEAD_DIM // 4
    inv_freq = ROPE_THETA ** (-jnp.arange(n_freq, dtype=F32) / n_freq)
    row = jnp.asarray(pos // GRID_W, F32)
    col = jnp.asarray(pos % GRID_W, F32)
    ang = jnp.concatenate([row[:, None] * inv_freq, col[:, None] * inv_freq], -1)
    cos, sin = jnp.cos(ang), jnp.sin(ang)
    cos_t = jnp.tile(jnp.concatenate([cos, cos], -1), (1, LANES // HEAD_DIM))
    sin_t = jnp.tile(jnp.concatenate([-sin, sin], -1), (1, LANES // HEAD_DIM))
    return cos_t, sin_t


def _delta_tables():
    r = np.arange(PAIR)
    same = (r[:, None] // CHUNK) == (r[None, :] // CHUNK)
    low = same & (r[:, None] >= r[None, :])
    low_s = same & (r[:, None] > r[None, :])
    up = same & (r[:, None] <= r[None, :])
    up_s = same & (r[:, None] < r[None, :])
    levels = []
    for k in range(N_LEVELS):
        s = 1 << k
        levels.append(((r[:, None] // (2 * s)) == (r[None, :] // (2 * s))) & ((r[:, None] // s) != (r[None, :] // s)))
    masks = jnp.asarray(np.stack([low, low_s, up, up_s, np.eye(PAIR, dtype=bool)] + levels).astype(np.float32))
    return masks


def _segment_mean_table():
    r = np.arange(LANES)
    seg = ((r[:, None] // HEAD_DIM) == (r[None, :] // HEAD_DIM)).astype(np.float32) / HEAD_DIM
    hi = jnp.asarray(seg, BF16)
    lo = (jnp.asarray(seg) - hi.astype(F32)).astype(BF16)
    return jnp.stack([hi, lo])


def kernel(x_prompt, x_sample, cache_a_k, cache_a_v, cache_c_k, cache_c_v, state_b_fwd, state_b_bwd, c, c_ctx, w_mod, b_mod, norm1_g, norm2_g, w_in, a_sink, b_conv, b_a_log, b_dt_bias, b_norm_g, c_q_norm, c_k_norm, w_out, w_group, b_group, w_expert, b_expert, w1, w3, w2, final_norm_g):
    n_p, t_p, d = x_prompt.shape
    n_s, t_s, _ = x_sample.shape
    depth = w_in.shape[0]
    past = cache_a_k.shape[2]
    tok_p = n_p * t_p
    n_tok = tok_p + n_s * t_s
    assert d == D_MODEL and tok_p % t_s == 0 and t_s % max(TM_PROJ, TM_FFN) == 0 and t_p % 256 == 0

    w_in_t = jnp.swapaxes(w_in, 1, 2)
    w_out16 = w_out.astype(BF16)
    pad_g = jnp.zeros((depth, d, EXPERT_ROW0 - N_GROUPS), F32)
    pad_e = jnp.zeros((depth, d, LANES - EXPERT_ROW0 - N_EXPERTS), F32)
    w_r = jnp.concatenate([w_group, pad_g, w_expert, pad_e], -1)
    w_r_hi = w_r.astype(BF16)
    w_r2 = jnp.concatenate([w_r_hi, (w_r - w_r_hi.astype(F32)).astype(BF16)], axis=-1)
    b_r = jnp.concatenate([b_group, pad_g[:, 0], b_expert, pad_e[:, 0]], -1)[:, None, :]
    cqn = jnp.tile(c_q_norm, (1, 4))[:, None, :]
    ckn = jnp.tile(c_k_norm, (1, 2))[:, None, :]
    gate_prm = jnp.stack([b_a_log.reshape(depth, 8), b_dt_bias.reshape(depth, 8)], 1)
    prmr = jnp.broadcast_to(jnp.pad(gate_prm, ((0, 0), (0, 0), (0, N_AB - 8)))[..., None],
                            (depth, 2, N_AB, LANES))
    cos_t, sin_t = _rope_tables(t_s)
    masks = _delta_tables()
    seg = _segment_mean_table()

    cond = jnp.concatenate([c_ctx[None, :], c], axis=0)
    cond_b = jnp.broadcast_to(cond[:, :, None], cond.shape + (LANES,))
    mods_all = _mods_call(cond_b, w_mod, b_mod).reshape(depth, SUBLANES, 6, d)

    def slot_fn(tm):
        per_s = t_s // tm
        first = tok_p // tm
        return lambda i: jnp.where(i < first, 0, 1 + (i - first) // per_s)

    xs = (x_prompt.reshape(tok_p, d), x_sample.reshape(n_s * t_s, d))
    blk_s = tok_p // t_s
    ctx = tuple(a.reshape(n_s, depth, past, LANES) for a in (cache_a_k, cache_a_v, cache_c_k, cache_c_v))
    s0 = tuple(a.reshape(n_s, depth, B_HEADS * B_DIM, B_DIM) for a in (state_b_fwd, state_b_bwd))
    g1, g2, bng = norm1_g[:, None, :], norm2_g[:, None, :], b_norm_g[:, None, :]
    caches = None
    states = None
    for l in range(depth):
        za, zb, zc, zab, zabt, *slab = _inproj_call(l, xs, mods_all, g1, w_in_t, slot_fn(TM_PROJ), TM_PROJ)
        x = slab[0] if slab else xs[0]

        ao, co, *caches = _attn_call(False, t_p, ATTN_NSEQ, n_p, 0, l, za, zc, a_sink, cqn, ckn, seg, prev=caches)
        ao, co = _attn_call(True, t_s, 1, n_s, blk_s, l, za, zc, a_sink, cqn, ckn, seg, prev=(ao, co),
                            rope=(cos_t, sin_t), ctx=ctx)

        bo, *states = _delta_call(False, t_p, DELTA_NSEQ, n_p, 0, l, zb, zab, zabt, b_conv, prmr, bng, masks,
                                  prev=states)
        (bo,) = _delta_call(True, t_s, 1, n_s, blk_s, l, zb, zab, zabt, b_conv, prmr, bng, masks,
                            prev=(bo,), s0=s0)

        x = _ffn_call(l, x, ao, bo, co, mods_all, g2, w_out16, w_r2, b_r, w1, w3, w2, slot_fn(TM_FFN), TM_FFN, TH_FFN)
        xs = (x,)

    y_prompt = _final_norm_call(x, final_norm_g[None], TM_NORM, 0, tok_p).reshape(n_p, t_p, d)
    y_sample = _final_norm_call(x, final_norm_g[None], TM_NORM, tok_p, n_s * t_s).reshape(n_s, t_s, d)
    new_ak, new_av, new_ck, new_cv = (a.reshape(n_p, depth, t_p, 2, HEAD_DIM) for a in caches)
    new_sf, new_sb = (a.reshape(n_p, depth, B_HEADS, B_DIM, B_DIM) for a in states)
    return (y_prompt, y_sample, new_ak, new_av, new_ck, new_cv, new_sf, new_sb)
```

```python
import functools

import jax
import jax.numpy as jnp
import numpy as np
from jax import lax
from jax.experimental import pallas as pl
from jax.experimental.pallas import tpu as pltpu

F32 = jnp.float32
BF16 = jnp.bfloat16

D_MODEL = 1024
GRID_W = 64
EPS = 1e-6
NEG_INF = -1e30
ROPE_THETA = 10000.0
HEAD_DIM = 64
Q_W = 256
KV_W = 128
WINDOW = 128
Q_BLOCK = 128
B_HEADS = 4
B_DIM = 128
CHUNK = 64
BD = B_HEADS * CHUNK
PAIR = 2 * CHUNK
N_LEVELS = 6
PREP_UNROLL = 4
DELTA_NSEQ = 4
ATTN_NSEQ = 4
N_GROUPS = 4
EXPERTS_PER_GROUP = 4
N_EXPERTS = 16
D_EXPERT = 256
EXPERT_ROW0 = 8

LANES = 128
SUBLANES = 8
VMEM_LIMIT = 60000 * 1024

TM_PROJ = 512
TM_FFN, TH_FFN = 1024, 512
TM_NORM = 512
MODS_TN = 1536

ZA_W, ZB_W, ZC_W, ZAB_W = 512, 2048, 512, 128
N_AB = 16
Z_W = ZA_W + ZB_W + ZC_W + ZAB_W


def _sigmoid(x):
    return 1.0 / (1.0 + jnp.exp(-x))


def _silu(x):
    return x * _sigmoid(x)


def _softplus(x):
    return jnp.maximum(x, 0.0) + jnp.log1p(jnp.exp(-jnp.abs(x)))


def _dot(a, b):
    return jnp.dot(a, b, preferred_element_type=F32)


def _dot_nt(a, b):
    return lax.dot_general(a, b, (((1,), (1,)), ((), ())), preferred_element_type=F32)


def _dot_tn(a, b):
    return lax.dot_general(a, b, (((0,), (0,)), ((), ())), preferred_element_type=F32)


def _split2(x):
    hi = x.astype(BF16)
    lo = (x - hi.astype(F32)).astype(BF16)
    return hi, lo


def _params(sem=None):
    return pltpu.CompilerParams(dimension_semantics=sem, vmem_limit_bytes=VMEM_LIMIT)


def _mods_kernel(cond_ref, w_ref, b_ref, o_ref, act_s):
    n_cond = cond_ref.shape[0]
    tn = w_ref.shape[2]
    reps = tn // LANES

    @pl.when((pl.program_id(0) == 0) & (pl.program_id(1) == 0))
    def _():
        act_s[...] = _silu(cond_ref[...])

    def body(kb, accs):
        r = pl.multiple_of(kb * SUBLANES, SUBLANES)
        w = w_ref[0, pl.ds(r, SUBLANES), :]
        return tuple(acc + jnp.tile(act_s[m, pl.ds(r, SUBLANES), :], (1, reps)) * w for m, acc in enumerate(accs))

    zero = jnp.zeros((SUBLANES, tn), F32)
    accs = lax.fori_loop(0, w_ref.shape[1] // SUBLANES, body, (zero,) * n_cond, unroll=4)
    rows = [jnp.sum(a, axis=0, keepdims=True) + b_ref[0] for a in accs]
    rows.append(jnp.zeros((SUBLANES - n_cond, tn), F32))
    o_ref[0] = jnp.concatenate(rows, axis=0)


def _mods_call(cond_b, w_mod, b_mod):
    depth, d, n = w_mod.shape
    tn = MODS_TN
    n_cond = cond_b.shape[0]
    return pl.pallas_call(
        _mods_kernel,
        grid=(depth, n // tn),
        in_specs=[
            pl.BlockSpec((n_cond, d, LANES), lambda l, j: (0, 0, 0)),
            pl.BlockSpec((1, d, tn), lambda l, j: (l, 0, j)),
            pl.BlockSpec((1, 1, tn), lambda l, j: (l, 0, j)),
        ],
        out_specs=pl.BlockSpec((1, SUBLANES, tn), lambda l, j: (l, 0, j)),
        out_shape=jax.ShapeDtypeStruct((depth, SUBLANES, n), F32),
        scratch_shapes=[pltpu.VMEM((n_cond, d, LANES), F32)],
        compiler_params=_params(("arbitrary", "arbitrary")),
        name="mods",
    )(cond_b, w_mod, b_mod.reshape(depth, 1, n))


def _x_specs(xs, tm):
    if len(xs) == 1:
        return [pl.BlockSpec((tm, D_MODEL), lambda i, *_: (i, 0))]
    first = xs[0].shape[0] // tm
    return [pl.BlockSpec((tm, D_MODEL), lambda i, *_: (jnp.minimum(i, first - 1), 0)),
            pl.BlockSpec((tm, D_MODEL), lambda i, *_: (jnp.maximum(i - first, 0), 0))]


def _x_tile(x_refs, first):
    if len(x_refs) == 1:
        return x_refs[0][...]
    return jnp.where(pl.program_id(0) < first, x_refs[0][...], x_refs[1][...])


def _modulated_norm(x, g, shift, scale):
    ms = jnp.mean(x * x, axis=-1, keepdims=True)
    y = x * lax.rsqrt(ms + EPS) * g
    return y * (1.0 + scale) + shift


def _inproj_kernel(n_x, first, *refs):
    x_refs = refs[:n_x]
    mod_ref, g_ref, wt_ref, za_ref, zb_ref, zc_ref, zab_ref, zabt_ref = refs[n_x:n_x + 8]
    w_s = refs[-1]
    @pl.when(pl.program_id(0) == 0)
    def _():
        ab0 = ZA_W + ZB_W
        w_s[0:ab0, :] = wt_ref[0, 0:ab0, :].astype(BF16)
        w_s[ab0:ab0 + ZC_W, :] = wt_ref[0, ab0 + N_AB:ab0 + N_AB + ZC_W, :].astype(BF16)
        w_s[ab0 + ZC_W:ab0 + ZC_W + N_AB, :] = wt_ref[0, ab0:ab0 + N_AB, :].astype(BF16)
        w_s[ab0 + ZC_W + N_AB:Z_W, :] = jnp.zeros((ZAB_W - N_AB, D_MODEL), BF16)

    m = mod_ref[0, 0]
    x = _x_tile(x_refs, first)
    if n_x > 1:
        refs[n_x + 8][...] = x
    h = _modulated_norm(x, g_ref[0], m[0:1], m[1:2]).astype(BF16)
    za_ref[...] = _dot_nt(h, w_s[0:ZA_W, :])
    step = 512
    for j in range(ZB_W // step):
        zb_ref[:, j * step:(j + 1) * step] = _dot_nt(h, w_s[ZA_W + j * step:ZA_W + (j + 1) * step, :])
    zc_ref[...] = _dot_nt(h, w_s[ZA_W + ZB_W:ZA_W + ZB_W + ZC_W, :])
    zab = _dot_nt(h, w_s[ZA_W + ZB_W + ZC_W:Z_W, :])
    zab_ref[...] = zab
    zabt_ref[...] = zab.T[:N_AB]


def _inproj_call(layer, xs, mods, g, w, slot_fn, tm):
    n_tok = sum(a.shape[0] for a in xs)
    n_ab = N_AB
    return pl.pallas_call(
        functools.partial(_inproj_kernel, len(xs), xs[0].shape[0] // tm),
        grid=(n_tok // tm,),
        in_specs=_x_specs(xs, tm) + [
            pl.BlockSpec((1, 1, 6, D_MODEL), lambda i: (layer, slot_fn(i), 0, 0)),
            pl.BlockSpec((1, 1, D_MODEL), lambda i: (layer, 0, 0)),
            pl.BlockSpec((1, w.shape[1], D_MODEL), lambda i: (layer, 0, 0)),
        ],
        out_specs=[
            pl.BlockSpec((tm, ZA_W), lambda i: (i, 0)),
            pl.BlockSpec((tm, ZB_W), lambda i: (i, 0)),
            pl.BlockSpec((tm, ZC_W), lambda i: (i, 0)),
            pl.BlockSpec((tm, ZAB_W), lambda i: (i, 0)),
            pl.BlockSpec((n_ab, tm), lambda i: (0, i)),
        ] + ([pl.BlockSpec((tm, D_MODEL), lambda i: (i, 0))] if len(xs) > 1 else []),
        out_shape=[
            jax.ShapeDtypeStruct((n_tok, ZA_W), F32),
            jax.ShapeDtypeStruct((n_tok, ZB_W), F32),
            jax.ShapeDtypeStruct((n_tok, ZC_W), F32),
            jax.ShapeDtypeStruct((n_tok, ZAB_W), F32),
            jax.ShapeDtypeStruct((n_ab, n_tok), F32),
        ] + ([jax.ShapeDtypeStruct((n_tok, D_MODEL), F32)] if len(xs) > 1 else []),
        scratch_shapes=[pltpu.VMEM((Z_W, D_MODEL), BF16)],
        compiler_params=_params(("arbitrary",)),
        name="inproj",
    )(*xs, mods, g, w)


def _lane_lo(shape):
    return lax.broadcasted_iota(jnp.int32, shape, len(shape) - 1) % LANES < HEAD_DIM


def _store_kdup(dst_ref, off, k):
    n = k.shape[0]
    r = pltpu.roll(k, HEAD_DIM, 1)
    lo = _lane_lo(k.shape)
    dst_ref[0, off:off + n, :] = jnp.where(lo, k, r).astype(BF16)
    dst_ref[1, off:off + n, :] = jnp.where(lo, r, k).astype(BF16)


def _store_vsplit(dst_ref, off, v):
    n = v.shape[0]
    r = pltpu.roll(v, HEAD_DIM, 1)
    lo = _lane_lo(v.shape)
    z = jnp.zeros_like(v)
    dst_ref[0, off:off + n, :] = jnp.where(lo, v, z).astype(BF16)
    dst_ref[1, off:off + n, :] = jnp.where(lo, z, r).astype(BF16)
    dst_ref[2, off:off + n, :] = jnp.where(lo, r, z).astype(BF16)
    dst_ref[3, off:off + n, :] = jnp.where(lo, z, v).astype(BF16)


def _rope(x, cos, sin):
    first = (lax.broadcasted_iota(jnp.int32, x.shape, 1) // (HEAD_DIM // 2)) % 2 == 0
    partner = jnp.where(first, pltpu.roll(x, LANES - HEAD_DIM // 2, 1), pltpu.roll(x, HEAD_DIM // 2, 1))
    return x * cos + partner * sin


def _head_rmsnorm(x, g, seg_hi, seg_lo):
    hi, lo = _split2(x * x)
    ms = _dot(hi, seg_hi) + _dot(lo, seg_hi) + _dot(hi, seg_lo)
    return x * lax.rsqrt(ms + EPS) * g


def _attend_many(units):
    qb = units[0][0].shape[0]
    lo = _lane_lo(units[0][0].shape)
    all_scores = []
    for qt, segs, _ in units:
        z = jnp.zeros_like(qt)
        qs = jnp.concatenate([jnp.where(lo, qt, z), jnp.where(lo, z, qt)], axis=0).astype(BF16)
        scores = []
        for kdup, _, _, mask in segs:
            s = _dot_nt(qs, kdup)
            if mask is not None:
                s = jnp.where(mask, s, NEG_INF)
            scores.append(s)
        all_scores.append(scores)
    probs = []
    for (qt, segs, sink_pair), scores in zip(units, all_scores):
        m = scores[0].max(axis=1, keepdims=True)
        for s in scores[1:]:
            m = jnp.maximum(m, s.max(axis=1, keepdims=True))
        if sink_pair is not None:
            row_a = lax.broadcasted_iota(jnp.int32, (2 * qb, 1), 0) < qb
            sink = jnp.where(row_a, sink_pair[0], sink_pair[1])
            m = jnp.maximum(m, sink)
            denom = jnp.exp(sink - m)
        else:
            denom = jnp.zeros((2 * qb, 1), F32)
        ps = []
        for s in scores:
            p = jnp.exp(s - m)
            denom = denom + p.sum(axis=1, keepdims=True)
            ps.append(p.astype(BF16))
        probs.append((ps, 1.0 / denom))
    outs = []
    for (qt, segs, _), (ps, inv) in zip(units, probs):
        acc = jnp.zeros((qb, LANES), F32)
        for pb, (_, vlo, vhi, _) in zip(ps, segs):
            acc = acc + _dot(pb[:qb], vlo) + _dot(pb[qb:], vhi)
        outs.append(acc * jnp.where(lo, inv[:qb], inv[qb:]))
    return outs


def _attn_kernel(has_ctx, t, nseq, layer, *refs):
    if has_ctx:
        (sink_ref, za_ref, zc_ref, cqn_ref, ckn_ref, seg_ref, cos_ref, sin_ref,
         cak_ref, cav_ref, cck_ref, ccv_ref,
         ao_ref, co_ref,
         ka_s, va_s, kc_s, vc_s, kctx_s, vctx_s, qa_s, qc_s) = refs
    else:
        (sink_ref, za_ref, zc_ref, cqn_ref, ckn_ref, seg_ref,
         ao_ref, co_ref, nak_ref, nav_ref, nck_ref, ncv_ref,
         ka_s, va_s, kc_s, vc_s, qa_s, qc_s) = refs
    scale = HEAD_DIM ** -0.5
    seg_hi = seg_ref[0]
    seg_lo = seg_ref[1]
    piece = 256
    n_ctx = cak_ref.shape[2] if has_ctx else 0

    for p0 in range(0, nseq * t, piece):
        rows = slice(p0, p0 + piece)
        ak = za_ref[rows, Q_W:Q_W + KV_W]
        av = za_ref[rows, Q_W + KV_W:Q_W + 2 * KV_W]
        ck = _head_rmsnorm(zc_ref[rows, Q_W:Q_W + KV_W], ckn_ref[0], seg_hi, seg_lo)
        cv = zc_ref[rows, Q_W + KV_W:Q_W + 2 * KV_W]
        if has_ctx:
            cos = cos_ref[rows, :]
            sin = sin_ref[rows, :]
            ak = _rope(ak, cos, sin)
            ck = _rope(ck, cos, sin)
            _store_kdup(ka_s, WINDOW + p0, ak)
            _store_vsplit(va_s, WINDOW + p0, av)
            _store_kdup(kc_s, n_ctx + p0, ck)
            _store_vsplit(vc_s, n_ctx + p0, cv)
        else:
            crow = slice(p0 % t, p0 % t + piece)
            nak_ref[p0 // t, 0, crow, :] = ak
            nav_ref[p0 // t, 0, crow, :] = av
            nck_ref[p0 // t, 0, crow, :] = ck
            ncv_ref[p0 // t, 0, crow, :] = cv
            _store_kdup(ka_s, p0, ak)
            _store_vsplit(va_s, p0, av)
            _store_kdup(kc_s, p0, ck)
            _store_vsplit(vc_s, p0, cv)
        for hk in range(2):
            cols = slice(hk * LANES, (hk + 1) * LANES)
            aq = za_ref[rows, cols]
            cq = _head_rmsnorm(zc_ref[rows, cols], cqn_ref[0, :, cols], seg_hi, seg_lo)
            if has_ctx:
                aq = _rope(aq, cos, sin)
                cq = _rope(cq, cos, sin)
            qa_s[rows, cols] = aq * scale
            qc_s[rows, cols] = cq * scale

    if has_ctx:
        zpad = jnp.zeros((WINDOW, LANES), BF16)
        for i in range(2):
            ka_s[i, 0:WINDOW, :] = zpad
            ka_s[i, WINDOW + t:2 * WINDOW + t, :] = zpad
        for i in range(4):
            va_s[i, 0:WINDOW, :] = zpad
            va_s[i, WINDOW + t:2 * WINDOW + t, :] = zpad
        for p0 in range(0, n_ctx, piece):
            rows = slice(p0, p0 + piece)
            _store_kdup(kctx_s, p0, cak_ref[0, 0, rows, :])
            _store_vsplit(vctx_s, p0, cav_ref[0, 0, rows, :])
            _store_kdup(kc_s, p0, cck_ref[0, 0, rows, :])
            _store_vsplit(vc_s, p0, ccv_ref[0, 0, rows, :])

        qb = Q_BLOCK
        span = qb + 2 * WINDOW
        qi = lax.broadcasted_iota(jnp.int32, (2 * qb, span), 0) % qb
        kj = lax.broadcasted_iota(jnp.int32, (2 * qb, span), 1)
        band = jnp.abs(kj - WINDOW - qi) <= WINDOW

        def block(b, carry):
            r0 = pl.multiple_of(b * qb, qb)
            kpos = kj + (r0 - WINDOW)
            mask = band & (kpos >= 0) & (kpos < t)
            units = []
            for hk in range(2):
                cols = slice(hk * LANES, (hk + 1) * LANES)
                segs_a = [
                    (kctx_s[hk], vctx_s[2 * hk], vctx_s[2 * hk + 1], None),
                    (ka_s[hk, pl.ds(r0, span), :], va_s[2 * hk, pl.ds(r0, span), :],
                     va_s[2 * hk + 1, pl.ds(r0, span), :], mask),
                ]
                sinks = (sink_ref[layer, 2 * hk], sink_ref[layer, 2 * hk + 1])
                units.append((qa_s[pl.ds(r0, qb), cols], segs_a, sinks))
                segs_c = [(kc_s[hk], vc_s[2 * hk], vc_s[2 * hk + 1], None)]
                units.append((qc_s[pl.ds(r0, qb), cols], segs_c, None))
            outs = _attend_many(units)
            for hk in range(2):
                cols = slice(hk * LANES, (hk + 1) * LANES)
                ao_ref[pl.ds(r0, qb), cols] = outs[2 * hk]
                co_ref[pl.ds(r0, qb), cols] = outs[2 * hk + 1]
            return carry

        lax.fori_loop(0, t // qb, block, 0)
    else:
        units = []
        for q in range(nseq):
            seq = slice(q * t, (q + 1) * t)
            for hk in range(2):
                cols = slice(hk * LANES, (hk + 1) * LANES)
                sinks = (sink_ref[layer, 2 * hk], sink_ref[layer, 2 * hk + 1])
                units.append((qa_s[seq, cols],
                              [(ka_s[hk, seq, :], va_s[2 * hk, seq, :], va_s[2 * hk + 1, seq, :], None)], sinks))
                units.append((qc_s[seq, cols],
                              [(kc_s[hk, seq, :], vc_s[2 * hk, seq, :], vc_s[2 * hk + 1, seq, :], None)], None))
        outs = _attend_many(units)
        for q in range(nseq):
            seq = slice(q * t, (q + 1) * t)
            for hk in range(2):
                cols = slice(hk * LANES, (hk + 1) * LANES)
                ao_ref[seq, cols] = outs[4 * q + 2 * hk]
                co_ref[seq, cols] = outs[4 * q + 2 * hk + 1]


def _attn_call(has_ctx, t, nseq, n_batch, row_block0, layer, za, zc, sink, cqn, ckn, seg, prev=None, rope=None,
               ctx=None):
    n_tok = za.shape[0]
    depth = sink.shape[0]
    assert n_batch % nseq == 0 and (nseq == 1 or not has_ctx)
    tok_spec = lambda w: pl.BlockSpec((nseq * t, w), lambda b, *_: (row_block0 + b, 0))
    const = lambda shape: pl.BlockSpec(shape, lambda b, *_: (0,) * len(shape))
    layer_spec = lambda shape: pl.BlockSpec((1,) + shape, lambda b, *_: (layer,) + (0,) * len(shape))
    in_specs = [tok_spec(ZA_W), tok_spec(ZC_W), layer_spec((1, Q_W)), layer_spec((1, KV_W)), const((2, LANES, LANES))]
    args = [za, zc, cqn, ckn, seg]
    out_specs = [tok_spec(Q_W), tok_spec(Q_W)]
    out_shape = [jax.ShapeDtypeStruct((n_tok, Q_W), F32), jax.ShapeDtypeStruct((n_tok, Q_W), F32)]
    if has_ctx:
        n_ctx = ctx[0].shape[2]
        in_specs += [const((t, LANES)), const((t, LANES))]
        args += list(rope)
        in_specs += [pl.BlockSpec((1, 1, n_ctx, LANES), lambda b, *_: (b, layer, 0, 0))] * 4
        args += list(ctx)
        scratch = [
            pltpu.VMEM((2, t + 2 * WINDOW, LANES), BF16), pltpu.VMEM((4, t + 2 * WINDOW, LANES), BF16),
            pltpu.VMEM((2, n_ctx + t, LANES), BF16), pltpu.VMEM((4, n_ctx + t, LANES), BF16),
            pltpu.VMEM((2, n_ctx, LANES), BF16), pltpu.VMEM((4, n_ctx, LANES), BF16),
            pltpu.VMEM((t, Q_W), F32), pltpu.VMEM((t, Q_W), F32),
        ]
    else:
        cache_spec = pl.BlockSpec((nseq, 1, t, LANES), lambda b, *_: (b, layer, 0, 0))
        out_specs += [cache_spec] * 4
        out_shape += [jax.ShapeDtypeStruct((n_batch, depth, t, LANES), F32)] * 4
        rows = nseq * t
        scratch = [
            pltpu.VMEM((2, rows, LANES), BF16), pltpu.VMEM((4, rows, LANES), BF16),
            pltpu.VMEM((2, rows, LANES), BF16), pltpu.VMEM((4, rows, LANES), BF16),
            pltpu.VMEM((rows, Q_W), F32), pltpu.VMEM((rows, Q_W), F32),
        ]
    n_real = len(args)
    aliases = {}
    if prev is not None:
        first_out = 0 if has_ctx else 2
        for k, arr in enumerate(prev):
            in_specs.append(pl.BlockSpec(memory_space=pl.ANY))
            args.append(arr)
            aliases[1 + n_real + k] = first_out + k

    def body(*refs):
        ins = refs[:1 + n_real]
        rest = refs[1 + len(args):]
        _attn_kernel(has_ctx, t, nseq, layer, *ins, *rest)

    return pl.pallas_call(
        body,
        grid_spec=pltpu.PrefetchScalarGridSpec(
            num_scalar_prefetch=1, grid=(n_batch // nseq,), in_specs=in_specs, out_specs=out_specs,
            scratch_shapes=scratch),
        out_shape=out_shape,
        input_output_aliases=aliases,
        compiler_params=_params(("arbitrary",)),
        name="attn_latent" if has_ctx else "attn_prompt",
    )(sink, *args)


def _stack_pair(x, p):
    return jnp.concatenate([x[:, (2 * p + hl) * B_DIM:(2 * p + hl + 1) * B_DIM] for hl in range(2)], axis=0)


def _delta_kernel(t, nseq, has_s0, *refs):
    if has_s0:
        (zb_ref, abc_ref, abt_ref, conv_ref, prmr_ref, bng_ref, mask_ref,
         s0f_ref, s0b_ref, o_ref, qkv_s, of_s, ob_s, sf_s, sb_s, u_s, wq_s, at_s, kd_s, eg_s,
         pre_s, suf_s, prec_s, sufc_s) = refs
    else:
        (zb_ref, abc_ref, abt_ref, conv_ref, prmr_ref, bng_ref, mask_ref,
         o_ref, sfo_ref, sbo_ref, qkv_s, of_s, ob_s, sf_s, sb_s, u_s, wq_s, at_s, kd_s, eg_s,
         pre_s, suf_s, prec_s, sufc_s) = refs
    n_chunks = t // CHUNK
    n_total = nseq * n_chunks
    s_rows = B_HEADS * B_DIM
    qk_w = B_HEADS * B_DIM

    row = lax.broadcasted_iota(jnp.int32, (t, LANES), 0)
    for q in range(nseq):
        seq = slice(q * t, (q + 1) * t)
        for j in range(3 * B_HEADS):
            cols = slice(j * LANES, (j + 1) * LANES)
            x = zb_ref[seq, cols]
            prev = jnp.where(row == 0, 0.0, pltpu.roll(x, 1, 0))
            nxt = jnp.where(row == t - 1, 0.0, pltpu.roll(x, t - 1, 0))
            y = _silu(prev * conv_ref[0, 0:1, cols] + x * conv_ref[0, 1:2, cols] + nxt * conv_ref[0, 2:3, cols])
            if j < 2 * B_HEADS:
                y = y * lax.rsqrt(jnp.sum(y * y, axis=-1, keepdims=True) + EPS)
            if j < B_HEADS:
                y = y * (B_DIM ** -0.5)
            qkv_s[seq, cols] = y

    if has_s0:
        for q in range(nseq):
            sf_s[q * s_rows:(q + 1) * s_rows, :] = s0f_ref[q, 0]
            sb_s[q * s_rows:(q + 1) * s_rows, :] = s0b_ref[q, 0]
    else:
        sf_s[...] = jnp.zeros_like(sf_s)
        sb_s[...] = jnp.zeros_like(sb_s)

    reps = nseq * t // LANES
    gr = -jnp.tile(jnp.exp(prmr_ref[0, 0]), (1, reps)) * _softplus(abt_ref[...] + jnp.tile(prmr_ref[0, 1], (1, reps)))
    seg_lane = lax.broadcasted_iota(jnp.int32, gr.shape, 1) % CHUNK
    pre, suf = gr, gr
    for s in (1, 2, 4, 8, 16, 32):
        pre = pre + jnp.where(seg_lane >= s, pltpu.roll(pre, s, 1), 0.0)
        suf = suf + jnp.where(seg_lane < CHUNK - s, pltpu.roll(suf, nseq * t - s, 1), 0.0)
    pre_s[...] = pre
    suf_s[...] = suf
    zrows = jnp.zeros((LANES - pre.shape[0], LANES), F32)
    for j in range(reps):
        tile = slice(j * LANES, (j + 1) * LANES)
        prec_s[tile, :] = jnp.concatenate([pre[:, tile], zrows], axis=0).T
        sufc_s[tile, :] = jnp.concatenate([suf[:, tile], zrows], axis=0).T
    lane_lo = lax.broadcasted_iota(jnp.int32, (1, LANES), 1) < CHUNK

    def prepare(cc, carry):
        chains = []
        for k in range(PREP_UNROLL):
            c = cc * PREP_UNROLL + k
            r0 = pl.multiple_of(c * CHUNK, CHUNK)
            b_all = _sigmoid(abc_ref[pl.ds(r0, CHUNK), :])
            run_c = (prec_s[pl.ds(r0, CHUNK), :], sufc_s[pl.ds(r0, CHUNK), :])
            tile0 = pl.multiple_of((cc * PREP_UNROLL + k - k % 2) * CHUNK, LANES)
            run = (pre_s[:, pl.ds(tile0, LANES)], suf_s[:, pl.ds(tile0, LANES)])
            run_r = tuple(pltpu.roll(x, CHUNK, 1) for x in run)
            for p in range(B_HEADS // 2):
                kst = _stack_pair(qkv_s[pl.ds(r0, CHUNK), qk_w:2 * qk_w], p)
                qst = _stack_pair(qkv_s[pl.ds(r0, CHUNK), 0:qk_w], p)
                vst = _stack_pair(qkv_s[pl.ds(r0, CHUNK), 2 * qk_w:3 * qk_w], p)
                kq = _dot_nt(jnp.concatenate([kst, qst], axis=0).astype(BF16), kst.astype(BF16))
                for d in range(2):
                    cg = 4 * d + 2 * p
                    edge = CHUNK - 1 if d == 0 else 0
                    rep_col = lambda x, col: jnp.broadcast_to(x[:, col:col + 1], (CHUNK, LANES))
                    b_rep = jnp.concatenate([rep_col(b_all, 8 + cg + hl) for hl in range(2)], axis=0)
                    gcol = jnp.concatenate([rep_col(run_c[d], cg + hl) for hl in range(2)], axis=0)
                    gtot = jnp.concatenate([rep_col(run_c[d][edge:edge + 1], cg + hl) for hl in range(2)], axis=0)
                    ra = cg
                    if k % 2 == 0:
                        grow = jnp.where(lane_lo, run[d][ra:ra + 1], run_r[d][ra + 1:ra + 2])
                    else:
                        grow = jnp.where(lane_lo, run_r[d][ra:ra + 1], run[d][ra + 1:ra + 2])
                    chains.append(dict(c=c, p=p, d=d, kst=kst, qst=qst, vst=vst, kq=kq, b_st=b_rep,
                                       gcol=gcol, gtot=gtot, grow=grow))

        for ch in chains:
            d, b_st, kq, gcol = ch["d"], ch["b_st"], ch.pop("kq"), ch["gcol"]
            decay = jnp.exp(jnp.minimum(gcol - ch.pop("grow"), 0.0))
            ch["a_mat"] = (b_st * kq[:PAIR]) * (decay * mask_ref[2 * d + 1])
            ch["attn"] = (kq[PAIR:] * (decay * mask_ref[2 * d])).astype(BF16)
            ch["t_inv"] = mask_ref[4] - ch["a_mat"] * mask_ref[5]
        for lvl in range(N_LEVELS - 1):
            for ch in chains:
                ch["t16"] = ch["t_inv"].astype(BF16)
                ch["et"] = _dot((ch["a_mat"] * mask_ref[6 + lvl]).astype(BF16), ch["t16"])
            for ch in chains:
                ch["t_inv"] = ch["t_inv"] - _dot(ch.pop("t16"), ch.pop("et").astype(BF16))
        for ch in chains:
            egc = jnp.exp(ch["gcol"])
            rk = jnp.concatenate([ch["b_st"] * ch["vst"], (ch["b_st"] * egc) * ch["kst"]], axis=1)
            ch["rk"] = _dot(ch.pop("t_inv").astype(BF16), rk.astype(BF16))
            ch["qp16"] = (ch["qst"] * egc).astype(BF16)
        for ch in chains:
            c, p, d, rk, qp16 = ch["c"], ch["p"], ch["d"], ch["rk"], ch["qp16"]
            pair_rows = slice(p * PAIR, (p + 1) * PAIR)
            w16 = rk[:, B_DIM:].astype(BF16)
            u_s[d, c, pair_rows, :] = rk[:, :B_DIM]
            at_s[d, c, p] = ch["attn"]
            kd_s[d, c, pair_rows, :] = (ch["kst"] * jnp.exp(ch["gtot"] - ch["gcol"])).astype(BF16)
            eg = jnp.exp(ch["gtot"])
            for hl in range(2):
                h = 2 * p + hl
                rows = slice(hl * CHUNK, (hl + 1) * CHUNK)
                wq_s[d, c, h * 2 * CHUNK:h * 2 * CHUNK + CHUNK, :] = w16[rows]
                wq_s[d, c, h * 2 * CHUNK + CHUNK:(h + 1) * 2 * CHUNK, :] = qp16[rows]
                eg_s[d, c, h * SUBLANES:(h + 1) * SUBLANES, :] = eg[hl * CHUNK:hl * CHUNK + SUBLANES, :]
        return carry

    lax.fori_loop(0, n_total // PREP_UNROLL, prepare, 0)

    def scan_step(i, carry):
        units = []
        for q in range(nseq):
            for d, s_ref, o_s in ((0, sf_s, of_s), (1, sb_s, ob_s)):
                c = q * n_chunks + (i if d == 0 else n_chunks - 1 - i)
                units.append(dict(q=q, d=d, c=c, s_ref=s_ref, o_s=o_s, r0=pl.multiple_of(c * CHUNK, CHUNK)))
        for un in units:
            q, d, c, s_ref = un["q"], un["d"], un["c"], un["s_ref"]
            un["x"] = []
            for h in range(B_HEADS):
                srows = slice(q * s_rows + h * B_DIM, q * s_rows + (h + 1) * B_DIM)
                un["x"].append(_dot(wq_s[d, c, h * 2 * CHUNK:(h + 1) * 2 * CHUNK, :], s_ref[srows, :].astype(BF16)))
        for un in units:
            d, c = un["d"], un["c"]
            un["vp16"], un["o"] = [], []
            for p in range(B_HEADS // 2):
                xs = un["x"][2 * p:2 * p + 2]
                v_new = jnp.concatenate(
                    [u_s[d, c, (2 * p + hl) * CHUNK:(2 * p + hl + 1) * CHUNK, :] - xs[hl][:CHUNK] for hl in range(2)],
                    axis=0)
                vp16 = v_new.astype(BF16)
                un["vp16"].append(vp16)
                un["o"].append(jnp.concatenate([xs[hl][CHUNK:] for hl in range(2)], axis=0)
                               + _dot(at_s[d, c, p], vp16))
        for un in units:
            q, d, c, s_ref, o_s, r0 = un["q"], un["d"], un["c"], un["s_ref"], un["o_s"], un["r0"]
            for h in range(B_HEADS):
                p, hl = divmod(h, 2)
                rows = slice(hl * CHUNK, (hl + 1) * CHUNK)
                srows = slice(q * s_rows + h * B_DIM, q * s_rows + (h + 1) * B_DIM)
                upd = _dot_tn(kd_s[d, c, h * CHUNK:(h + 1) * CHUNK, :], un["vp16"][p][rows])
                eg = jnp.tile(eg_s[d, c, h * SUBLANES:(h + 1) * SUBLANES, :], (B_DIM // SUBLANES, 1))
                s_ref[srows, :] = s_ref[srows, :] * eg + upd
                o_s[pl.ds(r0, CHUNK), h * B_DIM:(h + 1) * B_DIM] = un["o"][p][rows]
        return carry

    lax.fori_loop(0, n_chunks, scan_step, 0)

    if not has_s0:
        for q in range(nseq):
            sfo_ref[q, 0] = sf_s[q * s_rows:(q + 1) * s_rows, :]
            sbo_ref[q, 0] = sb_s[q * s_rows:(q + 1) * s_rows, :]

    for h in range(B_HEADS):
        cols = slice(h * B_DIM, (h + 1) * B_DIM)
        x = of_s[:, cols] + ob_s[:, cols]
        yn = x * lax.rsqrt(jnp.mean(x * x, axis=-1, keepdims=True) + EPS) * bng_ref[0]
        o_ref[:, cols] = yn * _silu(zb_ref[:, 3 * qk_w + h * B_DIM:3 * qk_w + (h + 1) * B_DIM])


def _delta_call(has_s0, t, nseq, n_batch, row_block0, layer, zb, zab, zabt, conv, prmr, bng, masks,
                prev=None, s0=None):
    n_tok = zb.shape[0]
    depth = conv.shape[0]
    n_chunks = nseq * (t // CHUNK)
    assert n_chunks % PREP_UNROLL == 0 and PREP_UNROLL % 2 == 0 and n_batch % nseq == 0
    tok_spec = lambda w: pl.BlockSpec((nseq * t, w), lambda b: (row_block0 + b, 0))
    const = lambda shape: pl.BlockSpec(shape, lambda b: (0,) * len(shape))
    layer_spec = lambda shape: pl.BlockSpec((1,) + shape, lambda b: (layer,) + (0,) * len(shape))
    s_shape = (B_HEADS * B_DIM, B_DIM)
    s_spec = pl.BlockSpec((nseq, 1) + s_shape, lambda b: (b, layer, 0, 0))
    n_ab = zabt.shape[0]
    in_specs = [
        tok_spec(ZB_W), tok_spec(ZAB_W),
        pl.BlockSpec((n_ab, nseq * t), lambda b: (0, row_block0 + b)),
        layer_spec((3, 3 * B_HEADS * B_DIM)), layer_spec((2, n_ab, LANES)),
        layer_spec((1, B_DIM)),
        const((5 + N_LEVELS, PAIR, PAIR)),
    ]
    args = [zb, zab, zabt, conv, prmr, bng, masks]
    out_specs = [tok_spec(B_HEADS * B_DIM)]
    out_shape = [jax.ShapeDtypeStruct((n_tok, B_HEADS * B_DIM), F32)]
    if has_s0:
        in_specs += [s_spec, s_spec]
        args += [s0[0], s0[1]]
    else:
        out_specs += [s_spec, s_spec]
        out_shape += [jax.ShapeDtypeStruct((n_batch, depth) + s_shape, F32)] * 2
    n_real = len(args)
    aliases = {}
    if prev is not None:
        first_out = 0 if has_s0 else 1
        for k, arr in enumerate(prev):
            in_specs.append(pl.BlockSpec(memory_space=pl.ANY))
            args.append(arr)
            aliases[n_real + k] = first_out + k
    rows = nseq * t
    scratch = [
        pltpu.VMEM((rows, 3 * B_HEADS * B_DIM), F32),
        pltpu.VMEM((rows, B_HEADS * B_DIM), F32), pltpu.VMEM((rows, B_HEADS * B_DIM), F32),
        pltpu.VMEM((nseq * s_shape[0], B_DIM), F32), pltpu.VMEM((nseq * s_shape[0], B_DIM), F32),
        pltpu.VMEM((2, n_chunks, BD, B_DIM), F32),
        pltpu.VMEM((2, n_chunks, 2 * BD, B_DIM), BF16),
        pltpu.VMEM((2, n_chunks, B_HEADS // 2, PAIR, PAIR), BF16),
        pltpu.VMEM((2, n_chunks, BD, B_DIM), BF16),
        pltpu.VMEM((2, n_chunks, B_HEADS * SUBLANES, LANES), F32),
        pltpu.VMEM((n_ab, rows), F32), pltpu.VMEM((n_ab, rows), F32),
        pltpu.VMEM((rows, LANES), F32), pltpu.VMEM((rows, LANES), F32),
    ]

    def body(*refs):
        _delta_kernel(t, nseq, has_s0, *refs[:n_real], *refs[len(args):])

    return pl.pallas_call(
        body,
        grid=(n_batch // nseq,),
        in_specs=in_specs,
        out_specs=out_specs,
        out_shape=out_shape,
        scratch_shapes=scratch,
        input_output_aliases=aliases,
        compiler_params=_params(("arbitrary",)),
        name="delta_latent" if has_s0 else "delta_prompt",
    )(*args)


def _outproj_router(x, ma, mb, mc, m, g, wo_ref, wr_ref, br):
    n = x.shape[0]
    b0, c0 = Q_W, Q_W + B_HEADS * B_DIM
    y = (_dot(ma.astype(BF16), wo_ref[0, 0:b0, :])
         + _dot(mb.astype(BF16), wo_ref[0, b0:c0, :])
         + _dot(mc.astype(BF16), wo_ref[0, c0:c0 + Q_W, :]))
    x1 = x + m[2:3] * y
    h2 = _modulated_norm(x1, g, m[3:4], m[4:5])
    hi, lo = _split2(h2)

    hw = _dot(hi, wr_ref[0])
    logits = (hw[:, :LANES] + hw[:, LANES:] + _dot(lo, wr_ref[0, :, :LANES]) + br).T
    gl = logits[0:N_GROUPS]
    grow = lax.broadcasted_iota(jnp.int32, gl.shape, 0)
    gmax = gl.max(axis=0, keepdims=True)
    g_sel = jnp.where(gl == gmax, grow, N_GROUPS).min(axis=0, keepdims=True)
    g_w = 1.0 / jnp.exp(gl - gmax).sum(axis=0, keepdims=True)
    el = logits[EXPERT_ROW0:EXPERT_ROW0 + N_EXPERTS]
    e_idx = lax.broadcasted_iota(jnp.int32, el.shape, 0)
    el = jnp.where((e_idx // EXPERTS_PER_GROUP) == g_sel, el, -jnp.inf)
    m1 = el.max(axis=0, keepdims=True)
    i1 = jnp.where(el == m1, e_idx, N_EXPERTS).min(axis=0, keepdims=True)
    el2 = jnp.where(e_idx == i1, -jnp.inf, el)
    m2 = el2.max(axis=0, keepdims=True)
    i2 = jnp.where(el2 == m2, e_idx, N_EXPERTS).min(axis=0, keepdims=True)
    tt = jnp.exp(m2 - m1)
    w1 = g_w / (1.0 + tt)
    w2 = w1 * tt
    gate_t = jnp.where(e_idx == i1, w1, 0.0) + jnp.where(e_idx == i2, w2, 0.0)
    gate = jnp.concatenate([gate_t, jnp.zeros((LANES - N_EXPERTS, n), F32)], axis=0).T
    return x1, hi, gate


def _ffn_kernel(x_ref, ma_ref, mb_ref, mc_ref, mod_ref, g_ref, wo_ref, wr_ref, br_ref, w1_ref, w3_ref, w2_ref,
                o_ref, x1_s, h_s, gate_s, acc_s):
    j = pl.program_id(1)
    tm = x_ref.shape[0]
    th = w1_ref.shape[2]
    m = mod_ref[0, 0]

    @pl.when(j == 0)
    def _():
        x1, hi, gate = _outproj_router(x_ref[...], ma_ref[...], mb_ref[...], mc_ref[...],
                                       m, g_ref[0], wo_ref, wr_ref, br_ref[0])
        x1_s[...] = x1
        h_s[...] = hi
        gate_s[...] = gate
        acc_s[...] = jnp.zeros_like(acc_s)

    @pl.when(j > 0)
    def _():
        h = h_s[...]
        hid = _silu(_dot(h, w1_ref[0].astype(BF16))) * _dot(h, w3_ref[0].astype(BF16))
        gate = gate_s[...]
        lane = lax.broadcasted_iota(jnp.int32, gate.shape, 1)
        n_e = th // D_EXPERT
        col = lax.broadcasted_iota(jnp.int32, hid.shape, 1) // D_EXPERT
        gmat = jnp.zeros(hid.shape, F32)
        for e in range(n_e):
            ge = jnp.where(lane == (j - 1) * n_e + e, gate, 0.0).sum(axis=1, keepdims=True)
            gmat = jnp.where(col == e, ge, gmat)
        acc_s[...] += _dot((hid * gmat).astype(BF16), w2_ref[0].astype(BF16))

    @pl.when(j == pl.num_programs(1) - 1)
    def _():
        o_ref[...] = x1_s[...] + m[5:6] * acc_s[...]


def _ffn_call(layer, x, ma, mb, mc, mods, g, wo, wr, br, w1, w3, w2, slot_fn, tm, th):
    n_tok = x.shape[0]
    n_h = w1.shape[2] // th
    hidden = lambda j: jnp.where(j == 0, n_h - 1, j - 1)
    tok = lambda w: pl.BlockSpec((tm, w), lambda i, j: (i, 0))
    layer_spec = lambda shape: pl.BlockSpec((1,) + shape, lambda i, j: (layer,) + (0,) * len(shape))
    return pl.pallas_call(
        _ffn_kernel,
        grid=(n_tok // tm, n_h + 1),
        in_specs=[tok(D_MODEL), tok(Q_W), tok(B_HEADS * B_DIM), tok(Q_W),
                  pl.BlockSpec((1, 1, 6, D_MODEL), lambda i, j: (layer, slot_fn(i), 0, 0)),
                  layer_spec((1, D_MODEL)), layer_spec((D_MODEL, D_MODEL)), layer_spec((D_MODEL, 2 * LANES)),
                  layer_spec((1, LANES)),
                  pl.BlockSpec((1, D_MODEL, th), lambda i, j: (layer, 0, hidden(j))),
                  pl.BlockSpec((1, D_MODEL, th), lambda i, j: (layer, 0, hidden(j))),
                  pl.BlockSpec((1, th, D_MODEL), lambda i, j: (layer, hidden(j), 0))],
        out_specs=tok(D_MODEL),
        out_shape=jax.ShapeDtypeStruct((n_tok, D_MODEL), F32),
        scratch_shapes=[pltpu.VMEM((tm, D_MODEL), F32), pltpu.VMEM((tm, D_MODEL), BF16),
                        pltpu.VMEM((tm, LANES), F32), pltpu.VMEM((tm, D_MODEL), F32)],
        compiler_params=_params(("arbitrary", "arbitrary")),
        name="ffn",
    )(x, ma, mb, mc, mods, g, wo, wr, br, w1, w3, w2)


def _final_norm_kernel(x_ref, g_ref, o_ref):
    x = x_ref[...]
    o_ref[...] = x * lax.rsqrt(jnp.mean(x * x, axis=-1, keepdims=True) + EPS) * g_ref[...]


def _final_norm_call(x, g, tm, row0, n_rows):
    blk0 = row0 // tm
    return pl.pallas_call(
        _final_norm_kernel,
        grid=(n_rows // tm,),
        in_specs=[pl.BlockSpec((tm, D_MODEL), lambda i: (blk0 + i, 0)), pl.BlockSpec((1, D_MODEL), lambda i: (0, 0))],
        out_specs=pl.BlockSpec((tm, D_MODEL), lambda i: (i, 0)),
        out_shape=jax.ShapeDtypeStruct((n_rows, D_MODEL), F32),
        compiler_params=_params(("arbitrary",)),
        name="final_norm",
    )(x, g)


def _rope_tables(t):
    pos = np.arange(t)
    n_freq = HEAD_DIM // 4
    inv_freq = ROPE_THETA ** (-jnp.arange(n_freq, dtype=F32) / n_freq)
    row = jnp.asarray(pos // GRID_W, F32)
    col = jnp.asarray(pos % GRID_W, F32)
    ang = jnp.concatenate([row[:, None] * inv_freq, col[:, None] * inv_freq], -1)
    cos, sin = jnp.cos(ang), jnp.sin(ang)
    cos_t = jnp.tile(jnp.concatenate([cos, cos], -1), (1, LANES // HEAD_DIM))
    sin_t = jnp.tile(jnp.concatenate([-sin, sin], -1), (1, LANES // HEAD_DIM))
    return cos_t, sin_t


def _delta_tables():
    r = np.arange(PAIR)
    same = (r[:, None] // CHUNK) == (r[None, :] // CHUNK)
    low = same & (r[:, None] >= r[None, :])
    low_s = same & (r[:, None] > r[None, :])
    up = same & (r[:, None] <= r[None, :])
    up_s = same & (r[:, None] < r[None, :])
    levels = []
    for k in range(N_LEVELS):
        s = 1 << k
        levels.append(((r[:, None] // (2 * s)) == (r[None, :] // (2 * s))) & ((r[:, None] // s) != (r[None, :] // s)))
    masks = jnp.asarray(np.stack([low, low_s, up, up_s, np.eye(PAIR, dtype=bool)] + levels).astype(np.float32))
    return masks


def _segment_mean_table():
    r = np.arange(LANES)
    seg = ((r[:, None] // HEAD_DIM) == (r[None, :] // HEAD_DIM)).astype(np.float32) / HEAD_DIM
    hi = jnp.asarray(seg, BF16)
    lo = (jnp.asarray(seg) - hi.astype(F32)).astype(BF16)
    return jnp.stack([hi, lo])


def kernel(x_prompt, x_sample, cache_a_k, cache_a_v, cache_c_k, cache_c_v, state_b_fwd, state_b_bwd, c, c_ctx, w_mod, b_mod, norm1_g, norm2_g, w_in, a_sink, b_conv, b_a_log, b_dt_bias, b_norm_g, c_q_norm, c_k_norm, w_out, w_group, b_group, w_expert, b_expert, w1, w3, w2, final_norm_g):
    n_p, t_p, d = x_prompt.shape
    n_s, t_s, _ = x_sample.shape
    depth = w_in.shape[0]
    past = cache_a_k.shape[2]
    tok_p = n_p * t_p
    n_tok = tok_p + n_s * t_s
    assert d == D_MODEL and tok_p % t_s == 0 and t_s % max(TM_PROJ, TM_FFN) == 0 and t_p % 256 == 0

    w_in_t = jnp.swapaxes(w_in, 1, 2)
    w_out16 = w_out.astype(BF16)
    pad_g = jnp.zeros((depth, d, EXPERT_ROW0 - N_GROUPS), F32)
    pad_e = jnp.zeros((depth, d, LANES - EXPERT_ROW0 - N_EXPERTS), F32)
    w_r = jnp.concatenate([w_group, pad_g, w_expert, pad_e], -1)
    w_r_hi = w_r.astype(BF16)
    w_r2 = jnp.concatenate([w_r_hi, (w_r - w_r_hi.astype(F32)).astype(BF16)], axis=-1)
    b_r = jnp.concatenate([b_group, pad_g[:, 0], b_expert, pad_e[:, 0]], -1)[:, None, :]
    cqn = jnp.tile(c_q_norm, (1, 4))[:, None, :]
    ckn = jnp.tile(c_k_norm, (1, 2))[:, None, :]
    gate_prm = jnp.stack([b_a_log.reshape(depth, 8), b_dt_bias.reshape(depth, 8)], 1)
    prmr = jnp.broadcast_to(jnp.pad(gate_prm, ((0, 0), (0, 0), (0, N_AB - 8)))[..., None],
                            (depth, 2, N_AB, LANES))
    cos_t, sin_t = _rope_tables(t_s)
    masks = _delta_tables()
    seg = _segment_mean_table()

    cond = jnp.concatenate([c_ctx[None, :], c], axis=0)
    cond_b = jnp.broadcast_to(cond[:, :, None], cond.shape + (LANES,))
    mods_all = _mods_call(cond_b, w_mod, b_mod).reshape(depth, SUBLANES, 6, d)

    def slot_fn(tm):
        per_s = t_s // tm
        first = tok_p // tm
        return lambda i: jnp.where(i < first, 0, 1 + (i - first) // per_s)

    xs = (x_prompt.reshape(tok_p, d), x_sample.reshape(n_s * t_s, d))
    blk_s = tok_p // t_s
    ctx = tuple(a.reshape(n_s, depth, past, LANES) for a in (cache_a_k, cache_a_v, cache_c_k, cache_c_v))
    s0 = tuple(a.reshape(n_s, depth, B_HEADS * B_DIM, B_DIM) for a in (state_b_fwd, state_b_bwd))
    g1, g2, bng = norm1_g[:, None, :], norm2_g[:, None, :], b_norm_g[:, None, :]
    caches = None
    states = None
    for l in range(depth):
        za, zb, zc, zab, zabt, *slab = _inproj_call(l, xs, mods_all, g1, w_in_t, slot_fn(TM_PROJ), TM_PROJ)
        x = slab[0] if slab else xs[0]

        ao, co, *caches = _attn_call(False, t_p, ATTN_NSEQ, n_p, 0, l, za, zc, a_sink, cqn, ckn, seg, prev=caches)
        ao, co = _attn_call(True, t_s, 1, n_s, blk_s, l, za, zc, a_sink, cqn, ckn, seg, prev=(ao, co),
                            rope=(cos_t, sin_t), ctx=ctx)

        bo, *states = _delta_call(False, t_p, DELTA_NSEQ, n_p, 0, l, zb, zab, zabt, b_conv, prmr, bng, masks,
                                  prev=states)
        (bo,) = _delta_call(True, t_s, 1, n_s, blk_s, l, zb, zab, zabt, b_conv, prmr, bng, masks,
                            prev=(bo,), s0=s0)

        x = _ffn_call(l, x, ao, bo, co, mods_all, g2, w_out16, w_r2, b_r, w1, w3, w2, slot_fn(TM_FFN), TM_FFN, TH_FFN)
        xs = (x,)

    y_prompt = _final_norm_call(x, final_norm_g[None], TM_NORM, 0, tok_p).reshape(n_p, t_p, d)
    y_sample = _final_norm_call(x, final_norm_g[None], TM_NORM, tok_p, n_s * t_s).reshape(n_s, t_s, d)
    new_ak, new_av, new_ck, new_cv = (a.reshape(n_p, depth, t_p, 2, HEAD_DIM) for a in caches)
    new_sf, new_sb = (a.reshape(n_p, depth, B_HEADS, B_DIM, B_DIM) for a in states)
    return (y_prompt, y_sample, new_ak, new_av, new_ck, new_cv, new_sf, new_sb)
```

```python
import functools

import jax
import jax.numpy as jnp
import numpy as np
from jax import lax
from jax.experimental import pallas as pl
from jax.experimental.pallas import tpu as pltpu

F32 = jnp.float32
BF16 = jnp.bfloat16

D_MODEL = 1024
GRID_W = 64
EPS = 1e-6
NEG_INF = -1e30
ROPE_THETA = 10000.0
HEAD_DIM = 64
Q_W = 256
KV_W = 128
WINDOW = 128
Q_BLOCK = 128
B_HEADS = 4
B_DIM = 128
CHUNK = 64
BD = B_HEADS * CHUNK
PAIR = 2 * CHUNK
N_LEVELS = 6
PREP_UNROLL = 4
DELTA_NSEQ = 4
ATTN_NSEQ = 4
N_GROUPS = 4
EXPERTS_PER_GROUP = 4
N_EXPERTS = 16
D_EXPERT = 256
EXPERT_ROW0 = 8

LANES = 128
SUBLANES = 8
VMEM_LIMIT = 60000 * 1024

TM_PROJ = 512
TM_FFN, TH_FFN = 1024, 1024
TM_NORM = 512
MODS_TN = 1536

ZA_W, ZB_W, ZC_W, ZAB_W = 512, 2048, 512, 128
N_AB = 16
Z_W = ZA_W + ZB_W + ZC_W + ZAB_W


def _sigmoid(x):
    return 1.0 / (1.0 + jnp.exp(-x))


def _silu(x):
    return x * _sigmoid(x)


def _softplus(x):
    return jnp.maximum(x, 0.0) + jnp.log1p(jnp.exp(-jnp.abs(x)))


def _dot(a, b):
    return jnp.dot(a, b, preferred_element_type=F32)


def _dot_nt(a, b):
    return lax.dot_general(a, b, (((1,), (1,)), ((), ())), preferred_element_type=F32)


def _dot_tn(a, b):
    return lax.dot_general(a, b, (((0,), (0,)), ((), ())), preferred_element_type=F32)


def _split2(x):
    hi = x.astype(BF16)
    lo = (x - hi.astype(F32)).astype(BF16)
    return hi, lo


def _params(sem=None):
    return pltpu.CompilerParams(dimension_semantics=sem, vmem_limit_bytes=VMEM_LIMIT)


def _mods_kernel(cond_ref, w_ref, b_ref, o_ref, act_s):
    n_cond = cond_ref.shape[0]
    tn = w_ref.shape[2]
    reps = tn // LANES

    @pl.when((pl.program_id(0) == 0) & (pl.program_id(1) == 0))
    def _():
        act_s[...] = _silu(cond_ref[...])

    def body(kb, accs):
        r = pl.multiple_of(kb * SUBLANES, SUBLANES)
        w = w_ref[0, pl.ds(r, SUBLANES), :]
        return tuple(acc + jnp.tile(act_s[m, pl.ds(r, SUBLANES), :], (1, reps)) * w for m, acc in enumerate(accs))

    zero = jnp.zeros((SUBLANES, tn), F32)
    accs = lax.fori_loop(0, w_ref.shape[1] // SUBLANES, body, (zero,) * n_cond, unroll=4)
    rows = [jnp.sum(a, axis=0, keepdims=True) + b_ref[0] for a in accs]
    rows.append(jnp.zeros((SUBLANES - n_cond, tn), F32))
    o_ref[0] = jnp.concatenate(rows, axis=0)


def _mods_call(cond_b, w_mod, b_mod):
    depth, d, n = w_mod.shape
    tn = MODS_TN
    n_cond = cond_b.shape[0]
    return pl.pallas_call(
        _mods_kernel,
        grid=(depth, n // tn),
        in_specs=[
            pl.BlockSpec((n_cond, d, LANES), lambda l, j: (0, 0, 0)),
            pl.BlockSpec((1, d, tn), lambda l, j: (l, 0, j)),
            pl.BlockSpec((1, 1, tn), lambda l, j: (l, 0, j)),
        ],
        out_specs=pl.BlockSpec((1, SUBLANES, tn), lambda l, j: (l, 0, j)),
        out_shape=jax.ShapeDtypeStruct((depth, SUBLANES, n), F32),
        scratch_shapes=[pltpu.VMEM((n_cond, d, LANES), F32)],
        compiler_params=_params(("arbitrary", "arbitrary")),
        name="mods",
    )(cond_b, w_mod, b_mod.reshape(depth, 1, n))


def _x_specs(xs, tm):
    if len(xs) == 1:
        return [pl.BlockSpec((tm, D_MODEL), lambda i, *_: (i, 0))]
    first = xs[0].shape[0] // tm
    return [pl.BlockSpec((tm, D_MODEL), lambda i, *_: (jnp.minimum(i, first - 1), 0)),
            pl.BlockSpec((tm, D_MODEL), lambda i, *_: (jnp.maximum(i - first, 0), 0))]


def _x_tile(x_refs, first):
    if len(x_refs) == 1:
        return x_refs[0][...]
    return jnp.where(pl.program_id(0) < first, x_refs[0][...], x_refs[1][...])


def _modulated_norm(x, g, shift, scale):
    ms = jnp.mean(x * x, axis=-1, keepdims=True)
    y = x * lax.rsqrt(ms + EPS) * g
    return y * (1.0 + scale) + shift


def _inproj_kernel(n_x, first, *refs):
    x_refs = refs[:n_x]
    mod_ref, g_ref, wt_ref, za_ref, zb_ref, zc_ref, zab_ref, zabt_ref = refs[n_x:n_x + 8]
    w_s = refs[-1]
    @pl.when(pl.program_id(0) == 0)
    def _():
        ab0 = ZA_W + ZB_W
        w_s[0:ab0, :] = wt_ref[0, 0:ab0, :].astype(BF16)
        w_s[ab0:ab0 + ZC_W, :] = wt_ref[0, ab0 + N_AB:ab0 + N_AB + ZC_W, :].astype(BF16)
        w_s[ab0 + ZC_W:ab0 + ZC_W + N_AB, :] = wt_ref[0, ab0:ab0 + N_AB, :].astype(BF16)
        w_s[ab0 + ZC_W + N_AB:Z_W, :] = jnp.zeros((ZAB_W - N_AB, D_MODEL), BF16)

    m = mod_ref[0, 0]
    x = _x_tile(x_refs, first)
    if n_x > 1:
        refs[n_x + 8][...] = x
    h = _modulated_norm(x, g_ref[0], m[0:1], m[1:2]).astype(BF16)
    za_ref[...] = _dot_nt(h, w_s[0:ZA_W, :])
    step = 512
    for j in range(ZB_W // step):
        zb_ref[:, j * step:(j + 1) * step] = _dot_nt(h, w_s[ZA_W + j * step:ZA_W + (j + 1) * step, :])
    zc_ref[...] = _dot_nt(h, w_s[ZA_W + ZB_W:ZA_W + ZB_W + ZC_W, :])
    zab = _dot_nt(h, w_s[ZA_W + ZB_W + ZC_W:Z_W, :])
    zab_ref[...] = zab
    zabt_ref[...] = zab.T[:N_AB]


def _inproj_call(layer, xs, mods, g, w, slot_fn, tm):
    n_tok = sum(a.shape[0] for a in xs)
    n_ab = N_AB
    return pl.pallas_call(
        functools.partial(_inproj_kernel, len(xs), xs[0].shape[0] // tm),
        grid=(n_tok // tm,),
        in_specs=_x_specs(xs, tm) + [
            pl.BlockSpec((1, 1, 6, D_MODEL), lambda i: (layer, slot_fn(i), 0, 0)),
            pl.BlockSpec((1, 1, D_MODEL), lambda i: (layer, 0, 0)),
            pl.BlockSpec((1, w.shape[1], D_MODEL), lambda i: (layer, 0, 0)),
        ],
        out_specs=[
            pl.BlockSpec((tm, ZA_W), lambda i: (i, 0)),
            pl.BlockSpec((tm, ZB_W), lambda i: (i, 0)),
            pl.BlockSpec((tm, ZC_W), lambda i: (i, 0)),
            pl.BlockSpec((tm, ZAB_W), lambda i: (i, 0)),
            pl.BlockSpec((n_ab, tm), lambda i: (0, i)),
        ] + ([pl.BlockSpec((tm, D_MODEL), lambda i: (i, 0))] if len(xs) > 1 else []),
        out_shape=[
            jax.ShapeDtypeStruct((n_tok, ZA_W), F32),
            jax.ShapeDtypeStruct((n_tok, ZB_W), F32),
            jax.ShapeDtypeStruct((n_tok, ZC_W), F32),
            jax.ShapeDtypeStruct((n_tok, ZAB_W), F32),
            jax.ShapeDtypeStruct((n_ab, n_tok), F32),
        ] + ([jax.ShapeDtypeStruct((n_tok, D_MODEL), F32)] if len(xs) > 1 else []),
        scratch_shapes=[pltpu.VMEM((Z_W, D_MODEL), BF16)],
        compiler_params=_params(("arbitrary",)),
        name="inproj",
    )(*xs, mods, g, w)


def _lane_lo(shape):
    return lax.broadcasted_iota(jnp.int32, shape, len(shape) - 1) % LANES < HEAD_DIM


def _store_kdup(dst_ref, off, k):
    n = k.shape[0]
    r = pltpu.roll(k, HEAD_DIM, 1)
    lo = _lane_lo(k.shape)
    dst_ref[0, off:off + n, :] = jnp.where(lo, k, r).astype(BF16)
    dst_ref[1, off:off + n, :] = jnp.where(lo, r, k).astype(BF16)


def _store_vsplit(dst_ref, off, v):
    n = v.shape[0]
    r = pltpu.roll(v, HEAD_DIM, 1)
    lo = _lane_lo(v.shape)
    z = jnp.zeros_like(v)
    dst_ref[0, off:off + n, :] = jnp.where(lo, v, z).astype(BF16)
    dst_ref[1, off:off + n, :] = jnp.where(lo, z, r).astype(BF16)
    dst_ref[2, off:off + n, :] = jnp.where(lo, r, z).astype(BF16)
    dst_ref[3, off:off + n, :] = jnp.where(lo, z, v).astype(BF16)


def _rope(x, cos, sin):
    first = (lax.broadcasted_iota(jnp.int32, x.shape, 1) // (HEAD_DIM // 2)) % 2 == 0
    partner = jnp.where(first, pltpu.roll(x, LANES - HEAD_DIM // 2, 1), pltpu.roll(x, HEAD_DIM // 2, 1))
    return x * cos + partner * sin


def _head_rmsnorm(x, g, seg_hi, seg_lo):
    hi, lo = _split2(x * x)
    ms = _dot(hi, seg_hi) + _dot(lo, seg_hi) + _dot(hi, seg_lo)
    return x * lax.rsqrt(ms + EPS) * g


def _attend_many(units):
    qb = units[0][0].shape[0]
    lo = _lane_lo(units[0][0].shape)
    all_scores = []
    for qt, segs, _ in units:
        z = jnp.zeros_like(qt)
        qs = jnp.concatenate([jnp.where(lo, qt, z), jnp.where(lo, z, qt)], axis=0).astype(BF16)
        scores = []
        for kdup, _, _, mask in segs:
            s = _dot_nt(qs, kdup)
            if mask is not None:
                s = jnp.where(mask, s, NEG_INF)
            scores.append(s)
        all_scores.append(scores)
    probs = []
    for (qt, segs, sink_pair), scores in zip(units, all_scores):
        m = scores[0].max(axis=1, keepdims=True)
        for s in scores[1:]:
            m = jnp.maximum(m, s.max(axis=1, keepdims=True))
        if sink_pair is not None:
            row_a = lax.broadcasted_iota(jnp.int32, (2 * qb, 1), 0) < qb
            sink = jnp.where(row_a, sink_pair[0], sink_pair[1])
            m = jnp.maximum(m, sink)
            denom = jnp.exp(sink - m)
        else:
            denom = jnp.zeros((2 * qb, 1), F32)
        ps = []
        for s in scores:
            p = jnp.exp(s - m)
            denom = denom + p.sum(axis=1, keepdims=True)
            ps.append(p.astype(BF16))
        probs.append((ps, 1.0 / denom))
    outs = []
    for (qt, segs, _), (ps, inv) in zip(units, probs):
        acc = jnp.zeros((qb, LANES), F32)
        for pb, (_, vlo, vhi, _) in zip(ps, segs):
            acc = acc + _dot(pb[:qb], vlo) + _dot(pb[qb:], vhi)
        outs.append(acc * jnp.where(lo, inv[:qb], inv[qb:]))
    return outs


def _attn_kernel(has_ctx, t, nseq, layer, *refs):
    if has_ctx:
        (sink_ref, za_ref, zc_ref, cqn_ref, ckn_ref, seg_ref, cos_ref, sin_ref,
         cak_ref, cav_ref, cck_ref, ccv_ref,
         ao_ref, co_ref,
         ka_s, va_s, kc_s, vc_s, kctx_s, vctx_s, qa_s, qc_s) = refs
    else:
        (sink_ref, za_ref, zc_ref, cqn_ref, ckn_ref, seg_ref,
         ao_ref, co_ref, nak_ref, nav_ref, nck_ref, ncv_ref,
         ka_s, va_s, kc_s, vc_s, qa_s, qc_s) = refs
    scale = HEAD_DIM ** -0.5
    seg_hi = seg_ref[0]
    seg_lo = seg_ref[1]
    piece = 256
    n_ctx = cak_ref.shape[2] if has_ctx else 0

    for p0 in range(0, nseq * t, piece):
        rows = slice(p0, p0 + piece)
        ak = za_ref[rows, Q_W:Q_W + KV_W]
        av = za_ref[rows, Q_W + KV_W:Q_W + 2 * KV_W]
        ck = _head_rmsnorm(zc_ref[rows, Q_W:Q_W + KV_W], ckn_ref[0], seg_hi, seg_lo)
        cv = zc_ref[rows, Q_W + KV_W:Q_W + 2 * KV_W]
        if has_ctx:
            cos = cos_ref[rows, :]
            sin = sin_ref[rows, :]
            ak = _rope(ak, cos, sin)
            ck = _rope(ck, cos, sin)
            _store_kdup(ka_s, WINDOW + p0, ak)
            _store_vsplit(va_s, WINDOW + p0, av)
            _store_kdup(kc_s, n_ctx + p0, ck)
            _store_vsplit(vc_s, n_ctx + p0, cv)
        else:
            crow = slice(p0 % t, p0 % t + piece)
            nak_ref[p0 // t, 0, crow, :] = ak
            nav_ref[p0 // t, 0, crow, :] = av
            nck_ref[p0 // t, 0, crow, :] = ck
            ncv_ref[p0 // t, 0, crow, :] = cv
            _store_kdup(ka_s, p0, ak)
            _store_vsplit(va_s, p0, av)
            _store_kdup(kc_s, p0, ck)
            _store_vsplit(vc_s, p0, cv)
        for hk in range(2):
            cols = slice(hk * LANES, (hk + 1) * LANES)
            aq = za_ref[rows, cols]
            cq = _head_rmsnorm(zc_ref[rows, cols], cqn_ref[0, :, cols], seg_hi, seg_lo)
            if has_ctx:
                aq = _rope(aq, cos, sin)
                cq = _rope(cq, cos, sin)
            qa_s[rows, cols] = aq * scale
            qc_s[rows, cols] = cq * scale

    if has_ctx:
        zpad = jnp.zeros((WINDOW, LANES), BF16)
        for i in range(2):
            ka_s[i, 0:WINDOW, :] = zpad
            ka_s[i, WINDOW + t:2 * WINDOW + t, :] = zpad
        for i in range(4):
            va_s[i, 0:WINDOW, :] = zpad
            va_s[i, WINDOW + t:2 * WINDOW + t, :] = zpad
        for p0 in range(0, n_ctx, piece):
            rows = slice(p0, p0 + piece)
            _store_kdup(kctx_s, p0, cak_ref[0, 0, rows, :])
            _store_vsplit(vctx_s, p0, cav_ref[0, 0, rows, :])
            _store_kdup(kc_s, p0, cck_ref[0, 0, rows, :])
            _store_vsplit(vc_s, p0, ccv_ref[0, 0, rows, :])

        qb = Q_BLOCK
        span = qb + 2 * WINDOW
        qi = lax.broadcasted_iota(jnp.int32, (2 * qb, span), 0) % qb
        kj = lax.broadcasted_iota(jnp.int32, (2 * qb, span), 1)
        band = jnp.abs(kj - WINDOW - qi) <= WINDOW

        def block(b, carry):
            r0 = pl.multiple_of(b * qb, qb)
            kpos = kj + (r0 - WINDOW)
            mask = band & (kpos >= 0) & (kpos < t)
            units = []
            for hk in range(2):
                cols = slice(hk * LANES, (hk + 1) * LANES)
                segs_a = [
                    (kctx_s[hk], vctx_s[2 * hk], vctx_s[2 * hk + 1], None),
                    (ka_s[hk, pl.ds(r0, span), :], va_s[2 * hk, pl.ds(r0, span), :],
                     va_s[2 * hk + 1, pl.ds(r0, span), :], mask),
                ]
                sinks = (sink_ref[layer, 2 * hk], sink_ref[layer, 2 * hk + 1])
                units.append((qa_s[pl.ds(r0, qb), cols], segs_a, sinks))
                segs_c = [(kc_s[hk], vc_s[2 * hk], vc_s[2 * hk + 1], None)]
                units.append((qc_s[pl.ds(r0, qb), cols], segs_c, None))
            outs = _attend_many(units)
            for hk in range(2):
                cols = slice(hk * LANES, (hk + 1) * LANES)
                ao_ref[pl.ds(r0, qb), cols] = outs[2 * hk].astype(BF16)
                co_ref[pl.ds(r0, qb), cols] = outs[2 * hk + 1].astype(BF16)
            return carry

        lax.fori_loop(0, t // qb, block, 0)
    else:
        units = []
        for q in range(nseq):
            seq = slice(q * t, (q + 1) * t)
            for hk in range(2):
                cols = slice(hk * LANES, (hk + 1) * LANES)
                sinks = (sink_ref[layer, 2 * hk], sink_ref[layer, 2 * hk + 1])
                units.append((qa_s[seq, cols],
                              [(ka_s[hk, seq, :], va_s[2 * hk, seq, :], va_s[2 * hk + 1, seq, :], None)], sinks))
                units.append((qc_s[seq, cols],
                              [(kc_s[hk, seq, :], vc_s[2 * hk, seq, :], vc_s[2 * hk + 1, seq, :], None)], None))
        outs = _attend_many(units)
        for q in range(nseq):
            seq = slice(q * t, (q + 1) * t)
            for hk in range(2):
                cols = slice(hk * LANES, (hk + 1) * LANES)
                ao_ref[seq, cols] = outs[4 * q + 2 * hk].astype(BF16)
                co_ref[seq, cols] = outs[4 * q + 2 * hk + 1].astype(BF16)


def _attn_call(has_ctx, t, nseq, n_batch, row_block0, layer, za, zc, sink, cqn, ckn, seg, prev=None, rope=None,
               ctx=None):
    n_tok = za.shape[0]
    depth = sink.shape[0]
    assert n_batch % nseq == 0 and (nseq == 1 or not has_ctx)
    tok_spec = lambda w: pl.BlockSpec((nseq * t, w), lambda b, *_: (row_block0 + b, 0))
    const = lambda shape: pl.BlockSpec(shape, lambda b, *_: (0,) * len(shape))
    layer_spec = lambda shape: pl.BlockSpec((1,) + shape, lambda b, *_: (layer,) + (0,) * len(shape))
    in_specs = [tok_spec(ZA_W), tok_spec(ZC_W), layer_spec((1, Q_W)), layer_spec((1, KV_W)), const((2, LANES, LANES))]
    args = [za, zc, cqn, ckn, seg]
    out_specs = [tok_spec(Q_W), tok_spec(Q_W)]
    out_shape = [jax.ShapeDtypeStruct((n_tok, Q_W), BF16), jax.ShapeDtypeStruct((n_tok, Q_W), BF16)]
    if has_ctx:
        n_ctx = ctx[0].shape[2]
        in_specs += [const((t, LANES)), const((t, LANES))]
        args += list(rope)
        in_specs += [pl.BlockSpec((1, 1, n_ctx, LANES), lambda b, *_: (b, layer, 0, 0))] * 4
        args += list(ctx)
        scratch = [
            pltpu.VMEM((2, t + 2 * WINDOW, LANES), BF16), pltpu.VMEM((4, t + 2 * WINDOW, LANES), BF16),
            pltpu.VMEM((2, n_ctx + t, LANES), BF16), pltpu.VMEM((4, n_ctx + t, LANES), BF16),
            pltpu.VMEM((2, n_ctx, LANES), BF16), pltpu.VMEM((4, n_ctx, LANES), BF16),
            pltpu.VMEM((t, Q_W), F32), pltpu.VMEM((t, Q_W), F32),
        ]
    else:
        cache_spec = pl.BlockSpec((nseq, 1, t, LANES), lambda b, *_: (b, layer, 0, 0))
        out_specs += [cache_spec] * 4
        out_shape += [jax.ShapeDtypeStruct((n_batch, depth, t, LANES), F32)] * 4
        rows = nseq * t
        scratch = [
            pltpu.VMEM((2, rows, LANES), BF16), pltpu.VMEM((4, rows, LANES), BF16),
            pltpu.VMEM((2, rows, LANES), BF16), pltpu.VMEM((4, rows, LANES), BF16),
            pltpu.VMEM((rows, Q_W), F32), pltpu.VMEM((rows, Q_W), F32),
        ]
    n_real = len(args)
    aliases = {}
    if prev is not None:
        first_out = 0 if has_ctx else 2
        for k, arr in enumerate(prev):
            in_specs.append(pl.BlockSpec(memory_space=pl.ANY))
            args.append(arr)
            aliases[1 + n_real + k] = first_out + k

    def body(*refs):
        ins = refs[:1 + n_real]
        rest = refs[1 + len(args):]
        _attn_kernel(has_ctx, t, nseq, layer, *ins, *rest)

    return pl.pallas_call(
        body,
        grid_spec=pltpu.PrefetchScalarGridSpec(
            num_scalar_prefetch=1, grid=(n_batch // nseq,), in_specs=in_specs, out_specs=out_specs,
            scratch_shapes=scratch),
        out_shape=out_shape,
        input_output_aliases=aliases,
        compiler_params=_params(("arbitrary",)),
        name="attn_latent" if has_ctx else "attn_prompt",
    )(sink, *args)


def _stack_pair(x, p):
    return jnp.concatenate([x[:, (2 * p + hl) * B_DIM:(2 * p + hl + 1) * B_DIM] for hl in range(2)], axis=0)


def _delta_kernel(t, nseq, has_s0, *refs):
    if has_s0:
        (zb_ref, abc_ref, abt_ref, conv_ref, prmr_ref, bng_ref, mask_ref,
         s0f_ref, s0b_ref, o_ref, qkv_s, of_s, ob_s, sf_s, sb_s, u_s, wq_s, at_s, kd_s, eg_s,
         pre_s, suf_s, prec_s, sufc_s) = refs
    else:
        (zb_ref, abc_ref, abt_ref, conv_ref, prmr_ref, bng_ref, mask_ref,
         o_ref, sfo_ref, sbo_ref, qkv_s, of_s, ob_s, sf_s, sb_s, u_s, wq_s, at_s, kd_s, eg_s,
         pre_s, suf_s, prec_s, sufc_s) = refs
    n_chunks = t // CHUNK
    n_total = nseq * n_chunks
    s_rows = B_HEADS * B_DIM
    qk_w = B_HEADS * B_DIM

    row = lax.broadcasted_iota(jnp.int32, (t, LANES), 0)
    for q in range(nseq):
        seq = slice(q * t, (q + 1) * t)
        for j in range(3 * B_HEADS):
            cols = slice(j * LANES, (j + 1) * LANES)
            x = zb_ref[seq, cols]
            prev = jnp.where(row == 0, 0.0, pltpu.roll(x, 1, 0))
            nxt = jnp.where(row == t - 1, 0.0, pltpu.roll(x, t - 1, 0))
            y = _silu(prev * conv_ref[0, 0:1, cols] + x * conv_ref[0, 1:2, cols] + nxt * conv_ref[0, 2:3, cols])
            if j < 2 * B_HEADS:
                y = y * lax.rsqrt(jnp.sum(y * y, axis=-1, keepdims=True) + EPS)
            if j < B_HEADS:
                y = y * (B_DIM ** -0.5)
            qkv_s[seq, cols] = y

    if has_s0:
        for q in range(nseq):
            sf_s[q * s_rows:(q + 1) * s_rows, :] = s0f_ref[q, 0]
            sb_s[q * s_rows:(q + 1) * s_rows, :] = s0b_ref[q, 0]
    else:
        sf_s[...] = jnp.zeros_like(sf_s)
        sb_s[...] = jnp.zeros_like(sb_s)

    reps = nseq * t // LANES
    gr = -jnp.tile(jnp.exp(prmr_ref[0, 0]), (1, reps)) * _softplus(abt_ref[...] + jnp.tile(prmr_ref[0, 1], (1, reps)))
    seg_lane = lax.broadcasted_iota(jnp.int32, gr.shape, 1) % CHUNK
    pre, suf = gr, gr
    for s in (1, 2, 4, 8, 16, 32):
        pre = pre + jnp.where(seg_lane >= s, pltpu.roll(pre, s, 1), 0.0)
        suf = suf + jnp.where(seg_lane < CHUNK - s, pltpu.roll(suf, nseq * t - s, 1), 0.0)
    pre_s[...] = pre
    suf_s[...] = suf
    zrows = jnp.zeros((LANES - pre.shape[0], LANES), F32)
    for j in range(reps):
        tile = slice(j * LANES, (j + 1) * LANES)
        prec_s[tile, :] = jnp.concatenate([pre[:, tile], zrows], axis=0).T
        sufc_s[tile, :] = jnp.concatenate([suf[:, tile], zrows], axis=0).T
    lane_lo = lax.broadcasted_iota(jnp.int32, (1, LANES), 1) < CHUNK

    def prepare(cc, carry):
        chains = []
        for k in range(PREP_UNROLL):
            c = cc * PREP_UNROLL + k
            r0 = pl.multiple_of(c * CHUNK, CHUNK)
            b_all = _sigmoid(abc_ref[pl.ds(r0, CHUNK), :])
            run_c = (prec_s[pl.ds(r0, CHUNK), :], sufc_s[pl.ds(r0, CHUNK), :])
            tile0 = pl.multiple_of((cc * PREP_UNROLL + k - k % 2) * CHUNK, LANES)
            run = (pre_s[:, pl.ds(tile0, LANES)], suf_s[:, pl.ds(tile0, LANES)])
            run_r = tuple(pltpu.roll(x, CHUNK, 1) for x in run)
            for p in range(B_HEADS // 2):
                kst = _stack_pair(qkv_s[pl.ds(r0, CHUNK), qk_w:2 * qk_w], p)
                qst = _stack_pair(qkv_s[pl.ds(r0, CHUNK), 0:qk_w], p)
                vst = _stack_pair(qkv_s[pl.ds(r0, CHUNK), 2 * qk_w:3 * qk_w], p)
                kq = _dot_nt(jnp.concatenate([kst, qst], axis=0).astype(BF16), kst.astype(BF16))
                for d in range(2):
                    cg = 4 * d + 2 * p
                    edge = CHUNK - 1 if d == 0 else 0
                    rep_col = lambda x, col: jnp.broadcast_to(x[:, col:col + 1], (CHUNK, LANES))
                    b_rep = jnp.concatenate([rep_col(b_all, 8 + cg + hl) for hl in range(2)], axis=0)
                    gcol = jnp.concatenate([rep_col(run_c[d], cg + hl) for hl in range(2)], axis=0)
                    gtot = jnp.concatenate([rep_col(run_c[d][edge:edge + 1], cg + hl) for hl in range(2)], axis=0)
                    ra = cg
                    if k % 2 == 0:
                        grow = jnp.where(lane_lo, run[d][ra:ra + 1], run_r[d][ra + 1:ra + 2])
                    else:
                        grow = jnp.where(lane_lo, run_r[d][ra:ra + 1], run[d][ra + 1:ra + 2])
                    chains.append(dict(c=c, p=p, d=d, kst=kst, qst=qst, vst=vst, kq=kq, b_st=b_rep,
                                       gcol=gcol, gtot=gtot, grow=grow))

        for ch in chains:
            d, b_st, kq, gcol = ch["d"], ch["b_st"], ch.pop("kq"), ch["gcol"]
            decay = jnp.exp(jnp.minimum(gcol - ch.pop("grow"), 0.0))
            ch["a_mat"] = (b_st * kq[:PAIR]) * (decay * mask_ref[2 * d + 1])
            ch["attn"] = (kq[PAIR:] * (decay * mask_ref[2 * d])).astype(BF16)
            ch["t_inv"] = mask_ref[4] - ch["a_mat"] * mask_ref[5]
        for lvl in range(N_LEVELS - 1):
            for ch in chains:
                ch["t16"] = ch["t_inv"].astype(BF16)
                ch["et"] = _dot((ch["a_mat"] * mask_ref[6 + lvl]).astype(BF16), ch["t16"])
            for ch in chains:
                ch["t_inv"] = ch["t_inv"] - _dot(ch.pop("t16"), ch.pop("et").astype(BF16))
        for ch in chains:
            egc = jnp.exp(ch["gcol"])
            rk = jnp.concatenate([ch["b_st"] * ch["vst"], (ch["b_st"] * egc) * ch["kst"]], axis=1)
            ch["rk"] = _dot(ch.pop("t_inv").astype(BF16), rk.astype(BF16))
            ch["qp16"] = (ch["qst"] * egc).astype(BF16)
        for ch in chains:
            c, p, d, rk, qp16 = ch["c"], ch["p"], ch["d"], ch["rk"], ch["qp16"]
            pair_rows = slice(p * PAIR, (p + 1) * PAIR)
            w16 = rk[:, B_DIM:].astype(BF16)
            u_s[d, c, pair_rows, :] = rk[:, :B_DIM]
            at_s[d, c, p] = ch["attn"]
            kd_s[d, c, pair_rows, :] = (ch["kst"] * jnp.exp(ch["gtot"] - ch["gcol"])).astype(BF16)
            eg = jnp.exp(ch["gtot"])
            for hl in range(2):
                h = 2 * p + hl
                rows = slice(hl * CHUNK, (hl + 1) * CHUNK)
                wq_s[d, c, h * 2 * CHUNK:h * 2 * CHUNK + CHUNK, :] = w16[rows]
                wq_s[d, c, h * 2 * CHUNK + CHUNK:(h + 1) * 2 * CHUNK, :] = qp16[rows]
                eg_s[d, c, h * SUBLANES:(h + 1) * SUBLANES, :] = eg[hl * CHUNK:hl * CHUNK + SUBLANES, :]
        return carry

    lax.fori_loop(0, n_total // PREP_UNROLL, prepare, 0)

    def scan_step(i, carry):
        units = []
        for q in range(nseq):
            for d, s_ref, o_s in ((0, sf_s, of_s), (1, sb_s, ob_s)):
                c = q * n_chunks + (i if d == 0 else n_chunks - 1 - i)
                units.append(dict(q=q, d=d, c=c, s_ref=s_ref, o_s=o_s, r0=pl.multiple_of(c * CHUNK, CHUNK)))
        for un in units:
            q, d, c, s_ref = un["q"], un["d"], un["c"], un["s_ref"]
            un["x"] = []
            for h in range(B_HEADS):
                srows = slice(q * s_rows + h * B_DIM, q * s_rows + (h + 1) * B_DIM)
                un["x"].append(_dot(wq_s[d, c, h * 2 * CHUNK:(h + 1) * 2 * CHUNK, :], s_ref[srows, :].astype(BF16)))
        for un in units:
            d, c = un["d"], un["c"]
            un["vp16"], un["o"] = [], []
            for p in range(B_HEADS // 2):
                xs = un["x"][2 * p:2 * p + 2]
                v_new = jnp.concatenate(
                    [u_s[d, c, (2 * p + hl) * CHUNK:(2 * p + hl + 1) * CHUNK, :] - xs[hl][:CHUNK] for hl in range(2)],
                    axis=0)
                vp16 = v_new.astype(BF16)
                un["vp16"].append(vp16)
                un["o"].append(jnp.concatenate([xs[hl][CHUNK:] for hl in range(2)], axis=0)
                               + _dot(at_s[d, c, p], vp16))
        for un in units:
            q, d, c, s_ref, o_s, r0 = un["q"], un["d"], un["c"], un["s_ref"], un["o_s"], un["r0"]
            for h in range(B_HEADS):
                p, hl = divmod(h, 2)
                rows = slice(hl * CHUNK, (hl + 1) * CHUNK)
                srows = slice(q * s_rows + h * B_DIM, q * s_rows + (h + 1) * B_DIM)
                upd = _dot_tn(kd_s[d, c, h * CHUNK:(h + 1) * CHUNK, :], un["vp16"][p][rows])
                eg = jnp.tile(eg_s[d, c, h * SUBLANES:(h + 1) * SUBLANES, :], (B_DIM // SUBLANES, 1))
                s_ref[srows, :] = s_ref[srows, :] * eg + upd
                o_s[pl.ds(r0, CHUNK), h * B_DIM:(h + 1) * B_DIM] = un["o"][p][rows]
        return carry

    lax.fori_loop(0, n_chunks, scan_step, 0)

    if not has_s0:
        for q in range(nseq):
            sfo_ref[q, 0] = sf_s[q * s_rows:(q + 1) * s_rows, :]
            sbo_ref[q, 0] = sb_s[q * s_rows:(q + 1) * s_rows, :]

    for h in range(B_HEADS):
        cols = slice(h * B_DIM, (h + 1) * B_DIM)
        x = of_s[:, cols] + ob_s[:, cols]
        yn = x * lax.rsqrt(jnp.mean(x * x, axis=-1, keepdims=True) + EPS) * bng_ref[0]
        o_ref[:, cols] = (yn * _silu(zb_ref[:, 3 * qk_w + h * B_DIM:3 * qk_w + (h + 1) * B_DIM])).astype(BF16)


def _delta_call(has_s0, t, nseq, n_batch, row_block0, layer, zb, zab, zabt, conv, prmr, bng, masks,
                prev=None, s0=None):
    n_tok = zb.shape[0]
    depth = conv.shape[0]
    n_chunks = nseq * (t // CHUNK)
    assert n_chunks % PREP_UNROLL == 0 and PREP_UNROLL % 2 == 0 and n_batch % nseq == 0
    tok_spec = lambda w: pl.BlockSpec((nseq * t, w), lambda b: (row_block0 + b, 0))
    const = lambda shape: pl.BlockSpec(shape, lambda b: (0,) * len(shape))
    layer_spec = lambda shape: pl.BlockSpec((1,) + shape, lambda b: (layer,) + (0,) * len(shape))
    s_shape = (B_HEADS * B_DIM, B_DIM)
    s_spec = pl.BlockSpec((nseq, 1) + s_shape, lambda b: (b, layer, 0, 0))
    n_ab = zabt.shape[0]
    in_specs = [
        tok_spec(ZB_W), tok_spec(ZAB_W),
        pl.BlockSpec((n_ab, nseq * t), lambda b: (0, row_block0 + b)),
        layer_spec((3, 3 * B_HEADS * B_DIM)), layer_spec((2, n_ab, LANES)),
        layer_spec((1, B_DIM)),
        const((5 + N_LEVELS, PAIR, PAIR)),
    ]
    args = [zb, zab, zabt, conv, prmr, bng, masks]
    out_specs = [tok_spec(B_HEADS * B_DIM)]
    out_shape = [jax.ShapeDtypeStruct((n_tok, B_HEADS * B_DIM), BF16)]
    if has_s0:
        in_specs += [s_spec, s_spec]
        args += [s0[0], s0[1]]
    else:
        out_specs += [s_spec, s_spec]
        out_shape += [jax.ShapeDtypeStruct((n_batch, depth) + s_shape, F32)] * 2
    n_real = len(args)
    aliases = {}
    if prev is not None:
        first_out = 0 if has_s0 else 1
        for k, arr in enumerate(prev):
            in_specs.append(pl.BlockSpec(memory_space=pl.ANY))
            args.append(arr)
            aliases[n_real + k] = first_out + k
    rows = nseq * t
    scratch = [
        pltpu.VMEM((rows, 3 * B_HEADS * B_DIM), F32),
        pltpu.VMEM((rows, B_HEADS * B_DIM), F32), pltpu.VMEM((rows, B_HEADS * B_DIM), F32),
        pltpu.VMEM((nseq * s_shape[0], B_DIM), F32), pltpu.VMEM((nseq * s_shape[0], B_DIM), F32),
        pltpu.VMEM((2, n_chunks, BD, B_DIM), F32),
        pltpu.VMEM((2, n_chunks, 2 * BD, B_DIM), BF16),
        pltpu.VMEM((2, n_chunks, B_HEADS // 2, PAIR, PAIR), BF16),
        pltpu.VMEM((2, n_chunks, BD, B_DIM), BF16),
        pltpu.VMEM((2, n_chunks, B_HEADS * SUBLANES, LANES), F32),
        pltpu.VMEM((n_ab, rows), F32), pltpu.VMEM((n_ab, rows), F32),
        pltpu.VMEM((rows, LANES), F32), pltpu.VMEM((rows, LANES), F32),
    ]

    def body(*refs):
        _delta_kernel(t, nseq, has_s0, *refs[:n_real], *refs[len(args):])

    return pl.pallas_call(
        body,
        grid=(n_batch // nseq,),
        in_specs=in_specs,
        out_specs=out_specs,
        out_shape=out_shape,
        scratch_shapes=scratch,
        input_output_aliases=aliases,
        compiler_params=_params(("arbitrary",)),
        name="delta_latent" if has_s0 else "delta_prompt",
    )(*args)


def _outproj_router(x, ma, mb, mc, m, g, wo_ref, wr_ref, br):
    n = x.shape[0]
    b0, c0 = Q_W, Q_W + B_HEADS * B_DIM
    y = (_dot(ma.astype(BF16), wo_ref[0, 0:b0, :])
         + _dot(mb.astype(BF16), wo_ref[0, b0:c0, :])
         + _dot(mc.astype(BF16), wo_ref[0, c0:c0 + Q_W, :]))
    x1 = x + m[2:3] * y
    h2 = _modulated_norm(x1, g, m[3:4], m[4:5])
    hi, lo = _split2(h2)

    hw = _dot(hi, wr_ref[0])
    logits = (hw[:, :LANES] + hw[:, LANES:] + _dot(lo, wr_ref[0, :, :LANES]) + br).T
    gl = logits[0:N_GROUPS]
    grow = lax.broadcasted_iota(jnp.int32, gl.shape, 0)
    gmax = gl.max(axis=0, keepdims=True)
    g_sel = jnp.where(gl == gmax, grow, N_GROUPS).min(axis=0, keepdims=True)
    g_w = 1.0 / jnp.exp(gl - gmax).sum(axis=0, keepdims=True)
    el = logits[EXPERT_ROW0:EXPERT_ROW0 + N_EXPERTS]
    e_idx = lax.broadcasted_iota(jnp.int32, el.shape, 0)
    el = jnp.where((e_idx // EXPERTS_PER_GROUP) == g_sel, el, -jnp.inf)
    m1 = el.max(axis=0, keepdims=True)
    i1 = jnp.where(el == m1, e_idx, N_EXPERTS).min(axis=0, keepdims=True)
    el2 = jnp.where(e_idx == i1, -jnp.inf, el)
    m2 = el2.max(axis=0, keepdims=True)
    i2 = jnp.where(el2 == m2, e_idx, N_EXPERTS).min(axis=0, keepdims=True)
    tt = jnp.exp(m2 - m1)
    w1 = g_w / (1.0 + tt)
    w2 = w1 * tt
    gate_t = jnp.where(e_idx == i1, w1, 0.0) + jnp.where(e_idx == i2, w2, 0.0)
    gate = jnp.concatenate([gate_t, jnp.zeros((LANES - N_EXPERTS, n), F32)], axis=0).T
    return x1, hi, gate


def _ffn_kernel(x_ref, ma_ref, mb_ref, mc_ref, mod_ref, g_ref, wo_ref, wr_ref, br_ref, w1_ref, w3_ref, w2_ref,
                o_ref, h_s, gate_s):
    j = pl.program_id(1)
    tm = x_ref.shape[0]
    th = w1_ref.shape[2]
    m = mod_ref[0, 0]

    @pl.when(j == 0)
    def _():
        x1, hi, gate = _outproj_router(x_ref[...], ma_ref[...], mb_ref[...], mc_ref[...],
                                       m, g_ref[0], wo_ref, wr_ref, br_ref[0])
        o_ref[...] = x1
        h_s[...] = hi
        gate_s[...] = gate

    @pl.when(j > 0)
    def _():
        h = h_s[...]
        hid = _silu(_dot(h, w1_ref[0].astype(BF16))) * _dot(h, w3_ref[0].astype(BF16))
        gate = gate_s[...]
        lane = lax.broadcasted_iota(jnp.int32, gate.shape, 1)
        n_e = th // D_EXPERT
        col = lax.broadcasted_iota(jnp.int32, hid.shape, 1) // D_EXPERT
        gmat = jnp.zeros(hid.shape, F32)
        for e in range(n_e):
            ge = jnp.where(lane == (j - 1) * n_e + e, gate, 0.0).sum(axis=1, keepdims=True)
            gmat = jnp.where(col == e, ge, gmat)
        o_ref[...] += m[5:6] * _dot((hid * gmat).astype(BF16), w2_ref[0].astype(BF16))


def _ffn_call(layer, x, ma, mb, mc, mods, g, wo, wr, br, w1, w3, w2, slot_fn, tm, th):
    n_tok = x.shape[0]
    n_h = w1.shape[2] // th
    hidden = lambda j: jnp.where(j == 0, n_h - 1, j - 1)
    tok = lambda w: pl.BlockSpec((tm, w), lambda i, j: (i, 0))
    layer_spec = lambda shape: pl.BlockSpec((1,) + shape, lambda i, j: (layer,) + (0,) * len(shape))
    return pl.pallas_call(
        _ffn_kernel,
        grid=(n_tok // tm, n_h + 1),
        in_specs=[tok(D_MODEL), tok(Q_W), tok(B_HEADS * B_DIM), tok(Q_W),
                  pl.BlockSpec((1, 1, 6, D_MODEL), lambda i, j: (layer, slot_fn(i), 0, 0)),
                  layer_spec((1, D_MODEL)), layer_spec((D_MODEL, D_MODEL)), layer_spec((D_MODEL, 2 * LANES)),
                  layer_spec((1, LANES)),
                  pl.BlockSpec((1, D_MODEL, th), lambda i, j: (layer, 0, hidden(j))),
                  pl.BlockSpec((1, D_MODEL, th), lambda i, j: (layer, 0, hidden(j))),
                  pl.BlockSpec((1, th, D_MODEL), lambda i, j: (layer, hidden(j), 0))],
        out_specs=tok(D_MODEL),
        out_shape=jax.ShapeDtypeStruct((n_tok, D_MODEL), F32),
        scratch_shapes=[pltpu.VMEM((tm, D_MODEL), BF16), pltpu.VMEM((tm, LANES), F32)],
        compiler_params=_params(("arbitrary", "arbitrary")),
        name="ffn",
    )(x, ma, mb, mc, mods, g, wo, wr, br, w1, w3, w2)


def _final_norm_kernel(x_ref, g_ref, o_ref):
    x = x_ref[...]
    o_ref[...] = x * lax.rsqrt(jnp.mean(x * x, axis=-1, keepdims=True) + EPS) * g_ref[...]


def _final_norm_call(x, g, tm, row0, n_rows):
    blk0 = row0 // tm
    return pl.pallas_call(
        _final_norm_kernel,
        grid=(n_rows // tm,),
        in_specs=[pl.BlockSpec((tm, D_MODEL), lambda i: (blk0 + i, 0)), pl.BlockSpec((1, D_MODEL), lambda i: (0, 0))],
        out_specs=pl.BlockSpec((tm, D_MODEL), lambda i: (i, 0)),
        out_shape=jax.ShapeDtypeStruct((n_rows, D_MODEL), F32),
        compiler_params=_params(("arbitrary",)),
        name="final_norm",
    )(x, g)


def _rope_tables(t):
    pos = np.arange(t)
    n_freq = HEAD_DIM // 4
    inv_freq = ROPE_THETA ** (-jnp.arange(n_freq, dtype=F32) / n_freq)
    row = jnp.asarray(pos // GRID_W, F32)
    col = jnp.asarray(pos % GRID_W, F32)
    ang = jnp.concatenate([row[:, None] * inv_freq, col[:, None] * inv_freq], -1)
    cos, sin = jnp.cos(ang), jnp.sin(ang)
    cos_t = jnp.tile(jnp.concatenate([cos, cos], -1), (1, LANES // HEAD_DIM))
    sin_t = jnp.tile(jnp.concatenate([-sin, sin], -1), (1, LANES // HEAD_DIM))
    return cos_t, sin_t


def _delta_tables():
    r = np.arange(PAIR)
    same = (r[:, None] // CHUNK) == (r[None, :] // CHUNK)
    low = same & (r[:, None] >= r[None, :])
    low_s = same & (r[:, None] > r[None, :])
    up = same & (r[:, None] <= r[None, :])
    up_s = same & (r[:, None] < r[None, :])
    levels = []
    for k in range(N_LEVELS):
        s = 1 << k
        levels.append(((r[:, None] // (2 * s)) == (r[None, :] // (2 * s))) & ((r[:, None] // s) != (r[None, :] // s)))
    masks = jnp.asarray(np.stack([low, low_s, up, up_s, np.eye(PAIR, dtype=bool)] + levels).astype(np.float32))
    return masks


def _segment_mean_table():
    r = np.arange(LANES)
    seg = ((r[:, None] // HEAD_DIM) == (r[None, :] // HEAD_DIM)).astype(np.float32) / HEAD_DIM
    hi = jnp.asarray(seg, BF16)
    lo = (jnp.asarray(seg) - hi.astype(F32)).astype(BF16)
    return jnp.stack([hi, lo])


def kernel(x_prompt, x_sample, cache_a_k, cache_a_v, cache_c_k, cache_c_v, state_b_fwd, state_b_bwd, c, c_ctx, w_mod, b_mod, norm1_g, norm2_g, w_in, a_sink, b_conv, b_a_log, b_dt_bias, b_norm_g, c_q_norm, c_k_norm, w_out, w_group, b_group, w_expert, b_expert, w1, w3, w2, final_norm_g):
    n_p, t_p, d = x_prompt.shape
    n_s, t_s, _ = x_sample.shape
    depth = w_in.shape[0]
    past = cache_a_k.shape[2]
    tok_p = n_p * t_p
    n_tok = tok_p + n_s * t_s
    assert d == D_MODEL and tok_p % t_s == 0 and t_s % max(TM_PROJ, TM_FFN) == 0 and t_p % 256 == 0

    w_in_t = jnp.swapaxes(w_in, 1, 2)
    w_out16 = w_out.astype(BF16)
    pad_g = jnp.zeros((depth, d, EXPERT_ROW0 - N_GROUPS), F32)
    pad_e = jnp.zeros((depth, d, LANES - EXPERT_ROW0 - N_EXPERTS), F32)
    w_r = jnp.concatenate([w_group, pad_g, w_expert, pad_e], -1)
    w_r_hi = w_r.astype(BF16)
    w_r2 = jnp.concatenate([w_r_hi, (w_r - w_r_hi.astype(F32)).astype(BF16)], axis=-1)
    b_r = jnp.concatenate([b_group, pad_g[:, 0], b_expert, pad_e[:, 0]], -1)[:, None, :]
    cqn = jnp.tile(c_q_norm, (1, 4))[:, None, :]
    ckn = jnp.tile(c_k_norm, (1, 2))[:, None, :]
    gate_prm = jnp.stack([b_a_log.reshape(depth, 8), b_dt_bias.reshape(depth, 8)], 1)
    prmr = jnp.broadcast_to(jnp.pad(gate_prm, ((0, 0), (0, 0), (0, N_AB - 8)))[..., None],
                            (depth, 2, N_AB, LANES))
    cos_t, sin_t = _rope_tables(t_s)
    masks = _delta_tables()
    seg = _segment_mean_table()

    cond = jnp.concatenate([c_ctx[None, :], c], axis=0)
    cond_b = jnp.broadcast_to(cond[:, :, None], cond.shape + (LANES,))
    mods_all = _mods_call(cond_b, w_mod, b_mod).reshape(depth, SUBLANES, 6, d)

    def slot_fn(tm):
        per_s = t_s // tm
        first = tok_p // tm
        return lambda i: jnp.where(i < first, 0, 1 + (i - first) // per_s)

    xs = (x_prompt.reshape(tok_p, d), x_sample.reshape(n_s * t_s, d))
    blk_s = tok_p // t_s
    ctx = tuple(a.reshape(n_s, depth, past, LANES) for a in (cache_a_k, cache_a_v, cache_c_k, cache_c_v))
    s0 = tuple(a.reshape(n_s, depth, B_HEADS * B_DIM, B_DIM) for a in (state_b_fwd, state_b_bwd))
    g1, g2, bng = norm1_g[:, None, :], norm2_g[:, None, :], b_norm_g[:, None, :]
    caches = None
    states = None
    for l in range(depth):
        za, zb, zc, zab, zabt, *slab = _inproj_call(l, xs, mods_all, g1, w_in_t, slot_fn(TM_PROJ), TM_PROJ)
        x = slab[0] if slab else xs[0]

        ao, co, *caches = _attn_call(False, t_p, ATTN_NSEQ, n_p, 0, l, za, zc, a_sink, cqn, ckn, seg, prev=caches)
        ao, co = _attn_call(True, t_s, 1, n_s, blk_s, l, za, zc, a_sink, cqn, ckn, seg, prev=(ao, co),
                            rope=(cos_t, sin_t), ctx=ctx)

        bo, *states = _delta_call(False, t_p, DELTA_NSEQ, n_p, 0, l, zb, zab, zabt, b_conv, prmr, bng, masks,
                                  prev=states)
        (bo,) = _delta_call(True, t_s, 1, n_s, blk_s, l, zb, zab, zabt, b_conv, prmr, bng, masks,
                            prev=(bo,), s0=s0)

        x = _ffn_call(l, x, ao, bo, co, mods_all, g2, w_out16, w_r2, b_r, w1, w3, w2, slot_fn(TM_FFN), TM_FFN, TH_FFN)
        xs = (x,)

    y_prompt = _final_norm_call(x, final_norm_g[None], TM_NORM, 0, tok_p).reshape(n_p, t_p, d)
    y_sample = _final_norm_call(x, final_norm_g[None], TM_NORM, tok_p, n_s * t_s).reshape(n_s, t_s, d)
    new_ak, new_av, new_ck, new_cv = (a.reshape(n_p, depth, t_p, 2, HEAD_DIM) for a in caches)
    new_sf, new_sb = (a.reshape(n_p, depth, B_HEADS, B_DIM, B_DIM) for a in states)
    return (y_prompt, y_sample, new_ak, new_av, new_ck, new_cv, new_sf, new_sb)
```

```python
import functools

import jax
import jax.numpy as jnp
import numpy as np
from jax import lax
from jax.experimental import pallas as pl
from jax.experimental.pallas import tpu as pltpu

F32 = jnp.float32
BF16 = jnp.bfloat16

D_MODEL = 1024
GRID_W = 64
EPS = 1e-6
NEG_INF = -1e30
ROPE_THETA = 10000.0
HEAD_DIM = 64
Q_W = 256
KV_W = 128
WINDOW = 128
Q_BLOCK = 128
B_HEADS = 4
B_DIM = 128
CHUNK = 64
BD = B_HEADS * CHUNK
PAIR = 2 * CHUNK
N_LEVELS = 6
PREP_UNROLL = 4
DELTA_NSEQ = 4
ATTN_NSEQ = 4
N_GROUPS = 4
EXPERTS_PER_GROUP = 4
N_EXPERTS = 16
D_EXPERT = 256
EXPERT_ROW0 = 8

LANES = 128
SUBLANES = 8
VMEM_LIMIT = 60000 * 1024

TM_PROJ = 512
TM_FFN, TH_FFN = 1024, 1024
TM_NORM = 512
MODS_TN = 1536

ZA_W, ZB_W, ZC_W, ZAB_W = 512, 2048, 512, 128
N_AB = 16
Z_W = ZA_W + ZB_W + ZC_W + ZAB_W


def _sigmoid(x):
    return 1.0 / (1.0 + jnp.exp(-x))


def _silu(x):
    return x * _sigmoid(x)


def _softplus(x):
    return jnp.maximum(x, 0.0) + jnp.log1p(jnp.exp(-jnp.abs(x)))


def _dot(a, b):
    return jnp.dot(a, b, preferred_element_type=F32)


def _dot_nt(a, b):
    return lax.dot_general(a, b, (((1,), (1,)), ((), ())), preferred_element_type=F32)


def _dot_tn(a, b):
    return lax.dot_general(a, b, (((0,), (0,)), ((), ())), preferred_element_type=F32)


def _split2(x):
    hi = x.astype(BF16)
    lo = (x - hi.astype(F32)).astype(BF16)
    return hi, lo


def _params(sem=None):
    return pltpu.CompilerParams(dimension_semantics=sem, vmem_limit_bytes=VMEM_LIMIT)


def _mods_kernel(cond_ref, w_ref, b_ref, o_ref, act_s):
    n_cond = cond_ref.shape[0]
    tn = w_ref.shape[2]
    reps = tn // LANES

    @pl.when((pl.program_id(0) == 0) & (pl.program_id(1) == 0))
    def _():
        act_s[...] = _silu(cond_ref[...])

    def body(kb, accs):
        r = pl.multiple_of(kb * SUBLANES, SUBLANES)
        w = w_ref[0, pl.ds(r, SUBLANES), :]
        return tuple(acc + jnp.tile(act_s[m, pl.ds(r, SUBLANES), :], (1, reps)) * w for m, acc in enumerate(accs))

    zero = jnp.zeros((SUBLANES, tn), F32)
    accs = lax.fori_loop(0, w_ref.shape[1] // SUBLANES, body, (zero,) * n_cond, unroll=4)
    rows = [jnp.sum(a, axis=0, keepdims=True) + b_ref[0] for a in accs]
    rows.append(jnp.zeros((SUBLANES - n_cond, tn), F32))
    o_ref[0] = jnp.concatenate(rows, axis=0)


def _mods_call(cond_b, w_mod, b_mod):
    depth, d, n = w_mod.shape
    tn = MODS_TN
    n_cond = cond_b.shape[0]
    return pl.pallas_call(
        _mods_kernel,
        grid=(depth, n // tn),
        in_specs=[
            pl.BlockSpec((n_cond, d, LANES), lambda l, j: (0, 0, 0)),
            pl.BlockSpec((1, d, tn), lambda l, j: (l, 0, j)),
            pl.BlockSpec((1, 1, tn), lambda l, j: (l, 0, j)),
        ],
        out_specs=pl.BlockSpec((1, SUBLANES, tn), lambda l, j: (l, 0, j)),
        out_shape=jax.ShapeDtypeStruct((depth, SUBLANES, n), F32),
        scratch_shapes=[pltpu.VMEM((n_cond, d, LANES), F32)],
        compiler_params=_params(("arbitrary", "arbitrary")),
        name="mods",
    )(cond_b, w_mod, b_mod.reshape(depth, 1, n))


def _x_specs(xs, tm):
    if len(xs) == 1:
        return [pl.BlockSpec((tm, D_MODEL), lambda i, *_: (i, 0))]
    first = xs[0].shape[0] // tm
    return [pl.BlockSpec((tm, D_MODEL), lambda i, *_: (jnp.minimum(i, first - 1), 0)),
            pl.BlockSpec((tm, D_MODEL), lambda i, *_: (jnp.maximum(i - first, 0), 0))]


def _x_tile(x_refs, first):
    if len(x_refs) == 1:
        return x_refs[0][...]
    return jnp.where(pl.program_id(0) < first, x_refs[0][...], x_refs[1][...])


def _modulated_norm(x, g, shift, scale):
    ms = jnp.mean(x * x, axis=-1, keepdims=True)
    y = x * lax.rsqrt(ms + EPS) * g
    return y * (1.0 + scale) + shift


def _inproj_kernel(n_x, first, *refs):
    x_refs = refs[:n_x]
    mod_ref, g_ref, wt_ref, za_ref, zb_ref, zc_ref, zab_ref, zabt_ref = refs[n_x:n_x + 8]
    w_s = refs[-1]
    @pl.when(pl.program_id(0) == 0)
    def _():
        ab0 = ZA_W + ZB_W
        w_s[0:ab0, :] = wt_ref[0, 0:ab0, :].astype(BF16)
        w_s[ab0:ab0 + ZC_W, :] = wt_ref[0, ab0 + N_AB:ab0 + N_AB + ZC_W, :].astype(BF16)
        w_s[ab0 + ZC_W:ab0 + ZC_W + N_AB, :] = wt_ref[0, ab0:ab0 + N_AB, :].astype(BF16)
        w_s[ab0 + ZC_W + N_AB:Z_W, :] = jnp.zeros((ZAB_W - N_AB, D_MODEL), BF16)

    m = mod_ref[0, 0]
    x = _x_tile(x_refs, first)
    if n_x > 1:
        refs[n_x + 8][...] = x
    h = _modulated_norm(x, g_ref[0], m[0:1], m[1:2]).astype(BF16)
    za_ref[...] = _dot_nt(h, w_s[0:ZA_W, :])
    step = 512
    for j in range(ZB_W // step):
        zb_ref[:, j * step:(j + 1) * step] = _dot_nt(h, w_s[ZA_W + j * step:ZA_W + (j + 1) * step, :])
    zc_ref[...] = _dot_nt(h, w_s[ZA_W + ZB_W:ZA_W + ZB_W + ZC_W, :])
    zab = _dot_nt(h, w_s[ZA_W + ZB_W + ZC_W:Z_W, :])
    zab_ref[...] = zab
    zabt_ref[...] = zab.T[:N_AB]


def _inproj_call(layer, xs, mods, g, w, slot_fn, tm):
    n_tok = sum(a.shape[0] for a in xs)
    n_ab = N_AB
    return pl.pallas_call(
        functools.partial(_inproj_kernel, len(xs), xs[0].shape[0] // tm),
        grid=(n_tok // tm,),
        in_specs=_x_specs(xs, tm) + [
            pl.BlockSpec((1, 1, 6, D_MODEL), lambda i: (layer, slot_fn(i), 0, 0)),
            pl.BlockSpec((1, 1, D_MODEL), lambda i: (layer, 0, 0)),
            pl.BlockSpec((1, w.shape[1], D_MODEL), lambda i: (layer, 0, 0)),
        ],
        out_specs=[
            pl.BlockSpec((tm, ZA_W), lambda i: (i, 0)),
            pl.BlockSpec((tm, ZB_W), lambda i: (i, 0)),
            pl.BlockSpec((tm, ZC_W), lambda i: (i, 0)),
            pl.BlockSpec((tm, ZAB_W), lambda i: (i, 0)),
            pl.BlockSpec((n_ab, tm), lambda i: (0, i)),
        ] + ([pl.BlockSpec((tm, D_MODEL), lambda i: (i, 0))] if len(xs) > 1 else []),
        out_shape=[
            jax.ShapeDtypeStruct((n_tok, ZA_W), F32),
            jax.ShapeDtypeStruct((n_tok, ZB_W), F32),
            jax.ShapeDtypeStruct((n_tok, ZC_W), F32),
            jax.ShapeDtypeStruct((n_tok, ZAB_W), F32),
            jax.ShapeDtypeStruct((n_ab, n_tok), F32),
        ] + ([jax.ShapeDtypeStruct((n_tok, D_MODEL), F32)] if len(xs) > 1 else []),
        scratch_shapes=[pltpu.VMEM((Z_W, D_MODEL), BF16)],
        compiler_params=_params(("arbitrary",)),
        name="inproj",
    )(*xs, mods, g, w)


def _lane_lo(shape):
    return lax.broadcasted_iota(jnp.int32, shape, len(shape) - 1) % LANES < HEAD_DIM


def _store_kdup(dst_ref, off, k):
    n = k.shape[0]
    r = pltpu.roll(k, HEAD_DIM, 1)
    lo = _lane_lo(k.shape)
    dst_ref[0, off:off + n, :] = jnp.where(lo, k, r).astype(BF16)
    dst_ref[1, off:off + n, :] = jnp.where(lo, r, k).astype(BF16)


def _store_vsplit(dst_ref, off, v):
    n = v.shape[0]
    r = pltpu.roll(v, HEAD_DIM, 1)
    lo = _lane_lo(v.shape)
    z = jnp.zeros_like(v)
    dst_ref[0, off:off + n, :] = jnp.where(lo, v, z).astype(BF16)
    dst_ref[1, off:off + n, :] = jnp.where(lo, z, r).astype(BF16)
    dst_ref[2, off:off + n, :] = jnp.where(lo, r, z).astype(BF16)
    dst_ref[3, off:off + n, :] = jnp.where(lo, z, v).astype(BF16)


def _rope(x, cos, sin):
    first = (lax.broadcasted_iota(jnp.int32, x.shape, 1) // (HEAD_DIM // 2)) % 2 == 0
    partner = jnp.where(first, pltpu.roll(x, LANES - HEAD_DIM // 2, 1), pltpu.roll(x, HEAD_DIM // 2, 1))
    return x * cos + partner * sin


def _head_rmsnorm(x, g, seg_hi, seg_lo):
    hi, lo = _split2(x * x)
    ms = _dot(hi, seg_hi) + _dot(lo, seg_hi) + _dot(hi, seg_lo)
    return x * lax.rsqrt(ms + EPS) * g


def _attend_many(units):
    qb = units[0][0].shape[0]
    lo = _lane_lo(units[0][0].shape)
    all_scores = []
    for qt, segs, _ in units:
        z = jnp.zeros_like(qt)
        qs = jnp.concatenate([jnp.where(lo, qt, z), jnp.where(lo, z, qt)], axis=0).astype(BF16)
        scores = []
        for kdup, _, _, mask in segs:
            s = _dot_nt(qs, kdup)
            if mask is not None:
                s = jnp.where(mask, s, NEG_INF)
            scores.append(s)
        all_scores.append(scores)
    probs = []
    for (qt, segs, sink_pair), scores in zip(units, all_scores):
        m = scores[0].max(axis=1, keepdims=True)
        for s in scores[1:]:
            m = jnp.maximum(m, s.max(axis=1, keepdims=True))
        if sink_pair is not None:
            row_a = lax.broadcasted_iota(jnp.int32, (2 * qb, 1), 0) < qb
            sink = jnp.where(row_a, sink_pair[0], sink_pair[1])
            m = jnp.maximum(m, sink)
            denom = jnp.exp(sink - m)
        else:
            denom = jnp.zeros((2 * qb, 1), F32)
        ps = []
        for s in scores:
            p = jnp.exp(s - m)
            denom = denom + p.sum(axis=1, keepdims=True)
            ps.append(p.astype(BF16))
        probs.append((ps, 1.0 / denom))
    outs = []
    for (qt, segs, _), (ps, inv) in zip(units, probs):
        acc = jnp.zeros((qb, LANES), F32)
        for pb, (_, vlo, vhi, _) in zip(ps, segs):
            acc = acc + _dot(pb[:qb], vlo) + _dot(pb[qb:], vhi)
        outs.append(acc * jnp.where(lo, inv[:qb], inv[qb:]))
    return outs


def _attn_kernel(has_ctx, t, nseq, layer, *refs):
    if has_ctx:
        (sink_ref, za_ref, zc_ref, cqn_ref, ckn_ref, seg_ref, cos_ref, sin_ref,
         cak_ref, cav_ref, cck_ref, ccv_ref,
         ao_ref, co_ref,
         ka_s, va_s, kc_s, vc_s, kctx_s, vctx_s, qa_s, qc_s) = refs
    else:
        (sink_ref, za_ref, zc_ref, cqn_ref, ckn_ref, seg_ref, w1_ref, w3_ref, w2_ref,
         ao_ref, co_ref, nak_ref, nav_ref, nck_ref, ncv_ref, w1b_ref, w3b_ref, w2b_ref,
         ka_s, va_s, kc_s, vc_s, qa_s, qc_s) = refs
        w1b_ref[...] = w1_ref[0].astype(BF16)
        w3b_ref[...] = w3_ref[0].astype(BF16)
        w2b_ref[...] = w2_ref[0].astype(BF16)
    scale = HEAD_DIM ** -0.5
    seg_hi = seg_ref[0]
    seg_lo = seg_ref[1]
    piece = 256
    n_ctx = cak_ref.shape[2] if has_ctx else 0

    for p0 in range(0, nseq * t, piece):
        rows = slice(p0, p0 + piece)
        ak = za_ref[rows, Q_W:Q_W + KV_W]
        av = za_ref[rows, Q_W + KV_W:Q_W + 2 * KV_W]
        ck = _head_rmsnorm(zc_ref[rows, Q_W:Q_W + KV_W], ckn_ref[0], seg_hi, seg_lo)
        cv = zc_ref[rows, Q_W + KV_W:Q_W + 2 * KV_W]
        if has_ctx:
            cos = cos_ref[rows, :]
            sin = sin_ref[rows, :]
            ak = _rope(ak, cos, sin)
            ck = _rope(ck, cos, sin)
            _store_kdup(ka_s, WINDOW + p0, ak)
            _store_vsplit(va_s, WINDOW + p0, av)
            _store_kdup(kc_s, n_ctx + p0, ck)
            _store_vsplit(vc_s, n_ctx + p0, cv)
        else:
            crow = slice(p0 % t, p0 % t + piece)
            nak_ref[p0 // t, 0, crow, :] = ak
            nav_ref[p0 // t, 0, crow, :] = av
            nck_ref[p0 // t, 0, crow, :] = ck
            ncv_ref[p0 // t, 0, crow, :] = cv
            _store_kdup(ka_s, p0, ak)
            _store_vsplit(va_s, p0, av)
            _store_kdup(kc_s, p0, ck)
            _store_vsplit(vc_s, p0, cv)
        for hk in range(2):
            cols = slice(hk * LANES, (hk + 1) * LANES)
            aq = za_ref[rows, cols]
            cq = _head_rmsnorm(zc_ref[rows, cols], cqn_ref[0, :, cols], seg_hi, seg_lo)
            if has_ctx:
                aq = _rope(aq, cos, sin)
                cq = _rope(cq, cos, sin)
            qa_s[rows, cols] = aq * scale
            qc_s[rows, cols] = cq * scale

    if has_ctx:
        zpad = jnp.zeros((WINDOW, LANES), BF16)
        for i in range(2):
            ka_s[i, 0:WINDOW, :] = zpad
            ka_s[i, WINDOW + t:2 * WINDOW + t, :] = zpad
        for i in range(4):
            va_s[i, 0:WINDOW, :] = zpad
            va_s[i, WINDOW + t:2 * WINDOW + t, :] = zpad
        for p0 in range(0, n_ctx, piece):
            rows = slice(p0, p0 + piece)
            _store_kdup(kctx_s, p0, cak_ref[0, 0, rows, :])
            _store_vsplit(vctx_s, p0, cav_ref[0, 0, rows, :])
            _store_kdup(kc_s, p0, cck_ref[0, 0, rows, :])
            _store_vsplit(vc_s, p0, ccv_ref[0, 0, rows, :])

        qb = Q_BLOCK
        span = qb + 2 * WINDOW
        qi = lax.broadcasted_iota(jnp.int32, (2 * qb, span), 0) % qb
        kj = lax.broadcasted_iota(jnp.int32, (2 * qb, span), 1)
        band = jnp.abs(kj - WINDOW - qi) <= WINDOW

        def block(b, carry):
            r0 = pl.multiple_of(b * qb, qb)
            kpos = kj + (r0 - WINDOW)
            mask = band & (kpos >= 0) & (kpos < t)
            units = []
            for hk in range(2):
                cols = slice(hk * LANES, (hk + 1) * LANES)
                segs_a = [
                    (kctx_s[hk], vctx_s[2 * hk], vctx_s[2 * hk + 1], None),
                    (ka_s[hk, pl.ds(r0, span), :], va_s[2 * hk, pl.ds(r0, span), :],
                     va_s[2 * hk + 1, pl.ds(r0, span), :], mask),
                ]
                sinks = (sink_ref[layer, 2 * hk], sink_ref[layer, 2 * hk + 1])
                units.append((qa_s[pl.ds(r0, qb), cols], segs_a, sinks))
                segs_c = [(kc_s[hk], vc_s[2 * hk], vc_s[2 * hk + 1], None)]
                units.append((qc_s[pl.ds(r0, qb), cols], segs_c, None))
            outs = _attend_many(units)
            for hk in range(2):
                cols = slice(hk * LANES, (hk + 1) * LANES)
                ao_ref[pl.ds(r0, qb), cols] = outs[2 * hk].astype(BF16)
                co_ref[pl.ds(r0, qb), cols] = outs[2 * hk + 1].astype(BF16)
            return carry

        lax.fori_loop(0, t // qb, block, 0)
    else:
        units = []
        for q in range(nseq):
            seq = slice(q * t, (q + 1) * t)
            for hk in range(2):
                cols = slice(hk * LANES, (hk + 1) * LANES)
                sinks = (sink_ref[layer, 2 * hk], sink_ref[layer, 2 * hk + 1])
                units.append((qa_s[seq, cols],
                              [(ka_s[hk, seq, :], va_s[2 * hk, seq, :], va_s[2 * hk + 1, seq, :], None)], sinks))
                units.append((qc_s[seq, cols],
                              [(kc_s[hk, seq, :], vc_s[2 * hk, seq, :], vc_s[2 * hk + 1, seq, :], None)], None))
        outs = _attend_many(units)
        for q in range(nseq):
            seq = slice(q * t, (q + 1) * t)
            for hk in range(2):
                cols = slice(hk * LANES, (hk + 1) * LANES)
                ao_ref[seq, cols] = outs[4 * q + 2 * hk].astype(BF16)
                co_ref[seq, cols] = outs[4 * q + 2 * hk + 1].astype(BF16)


def _attn_call(has_ctx, t, nseq, n_batch, row_block0, layer, za, zc, sink, cqn, ckn, seg, prev=None, rope=None,
               ctx=None, experts=None):
    n_tok = za.shape[0]
    depth = sink.shape[0]
    assert n_batch % nseq == 0 and (nseq == 1 or not has_ctx)
    tok_spec = lambda w: pl.BlockSpec((nseq * t, w), lambda b, *_: (row_block0 + b, 0))
    const = lambda shape: pl.BlockSpec(shape, lambda b, *_: (0,) * len(shape))
    layer_spec = lambda shape: pl.BlockSpec((1,) + shape, lambda b, *_: (layer,) + (0,) * len(shape))
    in_specs = [tok_spec(ZA_W), tok_spec(ZC_W), layer_spec((1, Q_W)), layer_spec((1, KV_W)), const((2, LANES, LANES))]
    args = [za, zc, cqn, ckn, seg]
    out_specs = [tok_spec(Q_W), tok_spec(Q_W)]
    out_shape = [jax.ShapeDtypeStruct((n_tok, Q_W), BF16), jax.ShapeDtypeStruct((n_tok, Q_W), BF16)]
    if has_ctx:
        n_ctx = ctx[0].shape[2]
        in_specs += [const((t, LANES)), const((t, LANES))]
        args += list(rope)
        in_specs += [pl.BlockSpec((1, 1, n_ctx, LANES), lambda b, *_: (b, layer, 0, 0))] * 4
        args += list(ctx)
        scratch = [
            pltpu.VMEM((2, t + 2 * WINDOW, LANES), BF16), pltpu.VMEM((4, t + 2 * WINDOW, LANES), BF16),
            pltpu.VMEM((2, n_ctx + t, LANES), BF16), pltpu.VMEM((4, n_ctx + t, LANES), BF16),
            pltpu.VMEM((2, n_ctx, LANES), BF16), pltpu.VMEM((4, n_ctx, LANES), BF16),
            pltpu.VMEM((t, Q_W), F32), pltpu.VMEM((t, Q_W), F32),
        ]
    else:
        cache_spec = pl.BlockSpec((nseq, 1, t, LANES), lambda b, *_: (b, layer, 0, 0))
        out_specs += [cache_spec] * 4
        out_shape += [jax.ShapeDtypeStruct((n_batch, depth, t, LANES), F32)] * 4
        n_steps = n_batch // nseq
        for w in experts:
            rows_w = w.shape[1] // n_steps
            in_specs.append(pl.BlockSpec((1, rows_w, w.shape[2]), lambda b, *_: (layer, b, 0)))
            args.append(w)
            out_specs.append(pl.BlockSpec((rows_w, w.shape[2]), lambda b, *_: (b, 0)))
            out_shape.append(jax.ShapeDtypeStruct(w.shape[1:], BF16))
        rows = nseq * t
        scratch = [
            pltpu.VMEM((2, rows, LANES), BF16), pltpu.VMEM((4, rows, LANES), BF16),
            pltpu.VMEM((2, rows, LANES), BF16), pltpu.VMEM((4, rows, LANES), BF16),
            pltpu.VMEM((rows, Q_W), F32), pltpu.VMEM((rows, Q_W), F32),
        ]
    n_real = len(args)
    aliases = {}
    if prev is not None:
        first_out = 0 if has_ctx else 2
        for k, arr in enumerate(prev):
            in_specs.append(pl.BlockSpec(memory_space=pl.ANY))
            args.append(arr)
            aliases[1 + n_real + k] = first_out + k

    def body(*refs):
        ins = refs[:1 + n_real]
        rest = refs[1 + len(args):]
        _attn_kernel(has_ctx, t, nseq, layer, *ins, *rest)

    return pl.pallas_call(
        body,
        grid_spec=pltpu.PrefetchScalarGridSpec(
            num_scalar_prefetch=1, grid=(n_batch // nseq,), in_specs=in_specs, out_specs=out_specs,
            scratch_shapes=scratch),
        out_shape=out_shape,
        input_output_aliases=aliases,
        compiler_params=_params(("arbitrary",)),
        name="attn_latent" if has_ctx else "attn_prompt",
    )(sink, *args)


def _stack_pair(x, p):
    return jnp.concatenate([x[:, (2 * p + hl) * B_DIM:(2 * p + hl + 1) * B_DIM] for hl in range(2)], axis=0)


def _delta_kernel(t, nseq, has_s0, *refs):
    if has_s0:
        (zb_ref, abc_ref, abt_ref, conv_ref, prmr_ref, bng_ref, mask_ref,
         s0f_ref, s0b_ref, o_ref, qkv_s, of_s, ob_s, sf_s, sb_s, u_s, wq_s, at_s, kd_s, eg_s,
         pre_s, suf_s, prec_s, sufc_s) = refs
    else:
        (zb_ref, abc_ref, abt_ref, conv_ref, prmr_ref, bng_ref, mask_ref,
         o_ref, sfo_ref, sbo_ref, qkv_s, of_s, ob_s, sf_s, sb_s, u_s, wq_s, at_s, kd_s, eg_s,
         pre_s, suf_s, prec_s, sufc_s) = refs
    n_chunks = t // CHUNK
    n_total = nseq * n_chunks
    s_rows = B_HEADS * B_DIM
    qk_w = B_HEADS * B_DIM

    row = lax.broadcasted_iota(jnp.int32, (t, LANES), 0)
    for q in range(nseq):
        seq = slice(q * t, (q + 1) * t)
        for j in range(3 * B_HEADS):
            cols = slice(j * LANES, (j + 1) * LANES)
            x = zb_ref[seq, cols]
            prev = jnp.where(row == 0, 0.0, pltpu.roll(x, 1, 0))
            nxt = jnp.where(row == t - 1, 0.0, pltpu.roll(x, t - 1, 0))
            y = _silu(prev * conv_ref[0, 0:1, cols] + x * conv_ref[0, 1:2, cols] + nxt * conv_ref[0, 2:3, cols])
            if j < 2 * B_HEADS:
                y = y * lax.rsqrt(jnp.sum(y * y, axis=-1, keepdims=True) + EPS)
            if j < B_HEADS:
                y = y * (B_DIM ** -0.5)
            qkv_s[seq, cols] = y

    if has_s0:
        for q in range(nseq):
            sf_s[q * s_rows:(q + 1) * s_rows, :] = s0f_ref[q, 0]
            sb_s[q * s_rows:(q + 1) * s_rows, :] = s0b_ref[q, 0]
    else:
        sf_s[...] = jnp.zeros_like(sf_s)
        sb_s[...] = jnp.zeros_like(sb_s)

    reps = nseq * t // LANES
    gr = -jnp.tile(jnp.exp(prmr_ref[0, 0]), (1, reps)) * _softplus(abt_ref[...] + jnp.tile(prmr_ref[0, 1], (1, reps)))
    seg_lane = lax.broadcasted_iota(jnp.int32, gr.shape, 1) % CHUNK
    pre, suf = gr, gr
    for s in (1, 2, 4, 8, 16, 32):
        pre = pre + jnp.where(seg_lane >= s, pltpu.roll(pre, s, 1), 0.0)
        suf = suf + jnp.where(seg_lane < CHUNK - s, pltpu.roll(suf, nseq * t - s, 1), 0.0)
    pre_s[...] = pre
    suf_s[...] = suf
    zrows = jnp.zeros((LANES - pre.shape[0], LANES), F32)
    for j in range(reps):
        tile = slice(j * LANES, (j + 1) * LANES)
        prec_s[tile, :] = jnp.concatenate([pre[:, tile], zrows], axis=0).T
        sufc_s[tile, :] = jnp.concatenate([suf[:, tile], zrows], axis=0).T
    lane_lo = lax.broadcasted_iota(jnp.int32, (1, LANES), 1) < CHUNK

    def prepare(cc, carry):
        chains = []
        for k in range(PREP_UNROLL):
            c = cc * PREP_UNROLL + k
            r0 = pl.multiple_of(c * CHUNK, CHUNK)
            b_all = _sigmoid(abc_ref[pl.ds(r0, CHUNK), :])
            run_c = (prec_s[pl.ds(r0, CHUNK), :], sufc_s[pl.ds(r0, CHUNK), :])
            tile0 = pl.multiple_of((cc * PREP_UNROLL + k - k % 2) * CHUNK, LANES)
            run = (pre_s[:, pl.ds(tile0, LANES)], suf_s[:, pl.ds(tile0, LANES)])
            run_r = tuple(pltpu.roll(x, CHUNK, 1) for x in run)
            for p in range(B_HEADS // 2):
                kst = _stack_pair(qkv_s[pl.ds(r0, CHUNK), qk_w:2 * qk_w], p)
                qst = _stack_pair(qkv_s[pl.ds(r0, CHUNK), 0:qk_w], p)
                vst = _stack_pair(qkv_s[pl.ds(r0, CHUNK), 2 * qk_w:3 * qk_w], p)
                kq = _dot_nt(jnp.concatenate([kst, qst], axis=0).astype(BF16), kst.astype(BF16))
                for d in range(2):
                    cg = 4 * d + 2 * p
                    edge = CHUNK - 1 if d == 0 else 0
                    rep_col = lambda x, col: jnp.broadcast_to(x[:, col:col + 1], (CHUNK, LANES))
                    b_rep = jnp.concatenate([rep_col(b_all, 8 + cg + hl) for hl in range(2)], axis=0)
                    gcol = jnp.concatenate([rep_col(run_c[d], cg + hl) for hl in range(2)], axis=0)
                    gtot = jnp.concatenate([rep_col(run_c[d][edge:edge + 1], cg + hl) for hl in range(2)], axis=0)
                    ra = cg
                    if k % 2 == 0:
                        grow = jnp.where(lane_lo, run[d][ra:ra + 1], run_r[d][ra + 1:ra + 2])
                    else:
                        grow = jnp.where(lane_lo, run_r[d][ra:ra + 1], run[d][ra + 1:ra + 2])
                    chains.append(dict(c=c, p=p, d=d, kst=kst, qst=qst, vst=vst, kq=kq, b_st=b_rep,
                                       gcol=gcol, gtot=gtot, grow=grow))

        for ch in chains:
            d, b_st, kq, gcol = ch["d"], ch["b_st"], ch.pop("kq"), ch["gcol"]
            decay = jnp.exp(jnp.minimum(gcol - ch.pop("grow"), 0.0))
            ch["a_mat"] = (b_st * kq[:PAIR]) * (decay * mask_ref[2 * d + 1])
            ch["attn"] = (kq[PAIR:] * (decay * mask_ref[2 * d])).astype(BF16)
            ch["t_inv"] = mask_ref[4] - ch["a_mat"] * mask_ref[5]
        for lvl in range(N_LEVELS - 1):
            for ch in chains:
                ch["t16"] = ch["t_inv"].astype(BF16)
                ch["et"] = _dot((ch["a_mat"] * mask_ref[6 + lvl]).astype(BF16), ch["t16"])
            for ch in chains:
                ch["t_inv"] = ch["t_inv"] - _dot(ch.pop("t16"), ch.pop("et").astype(BF16))
        for ch in chains:
            egc = jnp.exp(ch["gcol"])
            rk = jnp.concatenate([ch["b_st"] * ch["vst"], (ch["b_st"] * egc) * ch["kst"]], axis=1)
            ch["rk"] = _dot(ch.pop("t_inv").astype(BF16), rk.astype(BF16))
            ch["qp16"] = (ch["qst"] * egc).astype(BF16)
        for ch in chains:
            c, p, d, rk, qp16 = ch["c"], ch["p"], ch["d"], ch["rk"], ch["qp16"]
            pair_rows = slice(p * PAIR, (p + 1) * PAIR)
            w16 = rk[:, B_DIM:].astype(BF16)
            u_s[d, c, pair_rows, :] = rk[:, :B_DIM]
            at_s[d, c, p] = ch["attn"]
            kd_s[d, c, pair_rows, :] = (ch["kst"] * jnp.exp(ch["gtot"] - ch["gcol"])).astype(BF16)
            eg = jnp.exp(ch["gtot"])
            for hl in range(2):
                h = 2 * p + hl
                rows = slice(hl * CHUNK, (hl + 1) * CHUNK)
                wq_s[d, c, h * 2 * CHUNK:h * 2 * CHUNK + CHUNK, :] = w16[rows]
                wq_s[d, c, h * 2 * CHUNK + CHUNK:(h + 1) * 2 * CHUNK, :] = qp16[rows]
                eg_s[d, c, h * SUBLANES:(h + 1) * SUBLANES, :] = eg[hl * CHUNK:hl * CHUNK + SUBLANES, :]
        return carry

    lax.fori_loop(0, n_total // PREP_UNROLL, prepare, 0)

    def scan_step(i, carry):
        units = []
        for q in range(nseq):
            for d, s_ref, o_s in ((0, sf_s, of_s), (1, sb_s, ob_s)):
                c = q * n_chunks + (i if d == 0 else n_chunks - 1 - i)
                units.append(dict(q=q, d=d, c=c, s_ref=s_ref, o_s=o_s, r0=pl.multiple_of(c * CHUNK, CHUNK)))
        for un in units:
            q, d, c, s_ref = un["q"], un["d"], un["c"], un["s_ref"]
            un["x"] = []
            for h in range(B_HEADS):
                srows = slice(q * s_rows + h * B_DIM, q * s_rows + (h + 1) * B_DIM)
                un["x"].append(_dot(wq_s[d, c, h * 2 * CHUNK:(h + 1) * 2 * CHUNK, :], s_ref[srows, :].astype(BF16)))
        for un in units:
            d, c = un["d"], un["c"]
            un["vp16"], un["o"] = [], []
            for p in range(B_HEADS // 2):
                xs = un["x"][2 * p:2 * p + 2]
                v_new = jnp.concatenate(
                    [u_s[d, c, (2 * p + hl) * CHUNK:(2 * p + hl + 1) * CHUNK, :] - xs[hl][:CHUNK] for hl in range(2)],
                    axis=0)
                vp16 = v_new.astype(BF16)
                un["vp16"].append(vp16)
                un["o"].append(jnp.concatenate([xs[hl][CHUNK:] for hl in range(2)], axis=0)
                               + _dot(at_s[d, c, p], vp16))
        for un in units:
            q, d, c, s_ref, o_s, r0 = un["q"], un["d"], un["c"], un["s_ref"], un["o_s"], un["r0"]
            for h in range(B_HEADS):
                p, hl = divmod(h, 2)
                rows = slice(hl * CHUNK, (hl + 1) * CHUNK)
                srows = slice(q * s_rows + h * B_DIM, q * s_rows + (h + 1) * B_DIM)
                upd = _dot_tn(kd_s[d, c, h * CHUNK:(h + 1) * CHUNK, :], un["vp16"][p][rows])
                eg = jnp.tile(eg_s[d, c, h * SUBLANES:(h + 1) * SUBLANES, :], (B_DIM // SUBLANES, 1))
                s_ref[srows, :] = s_ref[srows, :] * eg + upd
                o_s[pl.ds(r0, CHUNK), h * B_DIM:(h + 1) * B_DIM] = un["o"][p][rows]
        return carry

    lax.fori_loop(0, n_chunks, scan_step, 0)

    if not has_s0:
        for q in range(nseq):
            sfo_ref[q, 0] = sf_s[q * s_rows:(q + 1) * s_rows, :]
            sbo_ref[q, 0] = sb_s[q * s_rows:(q + 1) * s_rows, :]

    for h in range(B_HEADS):
        cols = slice(h * B_DIM, (h + 1) * B_DIM)
        x = of_s[:, cols] + ob_s[:, cols]
        yn = x * lax.rsqrt(jnp.mean(x * x, axis=-1, keepdims=True) + EPS) * bng_ref[0]
        o_ref[:, cols] = (yn * _silu(zb_ref[:, 3 * qk_w + h * B_DIM:3 * qk_w + (h + 1) * B_DIM])).astype(BF16)


def _delta_call(has_s0, t, nseq, n_batch, row_block0, layer, zb, zab, zabt, conv, prmr, bng, masks,
                prev=None, s0=None):
    n_tok = zb.shape[0]
    depth = conv.shape[0]
    n_chunks = nseq * (t // CHUNK)
    assert n_chunks % PREP_UNROLL == 0 and PREP_UNROLL % 2 == 0 and n_batch % nseq == 0
    tok_spec = lambda w: pl.BlockSpec((nseq * t, w), lambda b: (row_block0 + b, 0))
    const = lambda shape: pl.BlockSpec(shape, lambda b: (0,) * len(shape))
    layer_spec = lambda shape: pl.BlockSpec((1,) + shape, lambda b: (layer,) + (0,) * len(shape))
    s_shape = (B_HEADS * B_DIM, B_DIM)
    s_spec = pl.BlockSpec((nseq, 1) + s_shape, lambda b: (b, layer, 0, 0))
    n_ab = zabt.shape[0]
    in_specs = [
        tok_spec(ZB_W), tok_spec(ZAB_W),
        pl.BlockSpec((n_ab, nseq * t), lambda b: (0, row_block0 + b)),
        layer_spec((3, 3 * B_HEADS * B_DIM)), layer_spec((2, n_ab, LANES)),
        layer_spec((1, B_DIM)),
        const((5 + N_LEVELS, PAIR, PAIR)),
    ]
    args = [zb, zab, zabt, conv, prmr, bng, masks]
    out_specs = [tok_spec(B_HEADS * B_DIM)]
    out_shape = [jax.ShapeDtypeStruct((n_tok, B_HEADS * B_DIM), BF16)]
    if has_s0:
        in_specs += [s_spec, s_spec]
        args += [s0[0], s0[1]]
    else:
        out_specs += [s_spec, s_spec]
        out_shape += [jax.ShapeDtypeStruct((n_batch, depth) + s_shape, F32)] * 2
    n_real = len(args)
    aliases = {}
    if prev is not None:
        first_out = 0 if has_s0 else 1
        for k, arr in enumerate(prev):
            in_specs.append(pl.BlockSpec(memory_space=pl.ANY))
            args.append(arr)
            aliases[n_real + k] = first_out + k
    rows = nseq * t
    scratch = [
        pltpu.VMEM((rows, 3 * B_HEADS * B_DIM), F32),
        pltpu.VMEM((rows, B_HEADS * B_DIM), F32), pltpu.VMEM((rows, B_HEADS * B_DIM), F32),
        pltpu.VMEM((nseq * s_shape[0], B_DIM), F32), pltpu.VMEM((nseq * s_shape[0], B_DIM), F32),
        pltpu.VMEM((2, n_chunks, BD, B_DIM), F32),
        pltpu.VMEM((2, n_chunks, 2 * BD, B_DIM), BF16),
        pltpu.VMEM((2, n_chunks, B_HEADS // 2, PAIR, PAIR), BF16),
        pltpu.VMEM((2, n_chunks, BD, B_DIM), BF16),
        pltpu.VMEM((2, n_chunks, B_HEADS * SUBLANES, LANES), F32),
        pltpu.VMEM((n_ab, rows), F32), pltpu.VMEM((n_ab, rows), F32),
        pltpu.VMEM((rows, LANES), F32), pltpu.VMEM((rows, LANES), F32),
    ]

    def body(*refs):
        _delta_kernel(t, nseq, has_s0, *refs[:n_real], *refs[len(args):])

    return pl.pallas_call(
        body,
        grid=(n_batch // nseq,),
        in_specs=in_specs,
        out_specs=out_specs,
        out_shape=out_shape,
        scratch_shapes=scratch,
        input_output_aliases=aliases,
        compiler_params=_params(("arbitrary",)),
        name="delta_latent" if has_s0 else "delta_prompt",
    )(*args)


def _outproj_router(x, ma, mb, mc, m, g, wo_ref, wr_ref, br):
    n = x.shape[0]
    b0, c0 = Q_W, Q_W + B_HEADS * B_DIM
    y = (_dot(ma.astype(BF16), wo_ref[0, 0:b0, :])
         + _dot(mb.astype(BF16), wo_ref[0, b0:c0, :])
         + _dot(mc.astype(BF16), wo_ref[0, c0:c0 + Q_W, :]))
    x1 = x + m[2:3] * y
    h2 = _modulated_norm(x1, g, m[3:4], m[4:5])
    hi, lo = _split2(h2)

    hw = _dot(hi, wr_ref[0])
    logits = (hw[:, :LANES] + hw[:, LANES:] + _dot(lo, wr_ref[0, :, :LANES]) + br).T
    gl = logits[0:N_GROUPS]
    grow = lax.broadcasted_iota(jnp.int32, gl.shape, 0)
    gmax = gl.max(axis=0, keepdims=True)
    g_sel = jnp.where(gl == gmax, grow, N_GROUPS).min(axis=0, keepdims=True)
    g_w = 1.0 / jnp.exp(gl - gmax).sum(axis=0, keepdims=True)
    el = logits[EXPERT_ROW0:EXPERT_ROW0 + N_EXPERTS]
    e_idx = lax.broadcasted_iota(jnp.int32, el.shape, 0)
    el = jnp.where((e_idx // EXPERTS_PER_GROUP) == g_sel, el, -jnp.inf)
    m1 = el.max(axis=0, keepdims=True)
    i1 = jnp.where(el == m1, e_idx, N_EXPERTS).min(axis=0, keepdims=True)
    el2 = jnp.where(e_idx == i1, -jnp.inf, el)
    m2 = el2.max(axis=0, keepdims=True)
    i2 = jnp.where(el2 == m2, e_idx, N_EXPERTS).min(axis=0, keepdims=True)
    tt = jnp.exp(m2 - m1)
    w1 = g_w / (1.0 + tt)
    w2 = w1 * tt
    gate_t = jnp.where(e_idx == i1, w1, 0.0) + jnp.where(e_idx == i2, w2, 0.0)
    gate = jnp.concatenate([gate_t, jnp.zeros((LANES - N_EXPERTS, n), F32)], axis=0).T
    return x1, hi, gate


def _ffn_kernel(x_ref, ma_ref, mb_ref, mc_ref, mod_ref, g_ref, wo_ref, wr_ref, br_ref, w1_ref, w3_ref, w2_ref,
                o_ref, h_s, gate_s):
    j = pl.program_id(1)
    tm = x_ref.shape[0]
    th = w1_ref.shape[1]
    m = mod_ref[0, 0]

    @pl.when(j == 0)
    def _():
        x1, hi, gate = _outproj_router(x_ref[...], ma_ref[...], mb_ref[...], mc_ref[...],
                                       m, g_ref[0], wo_ref, wr_ref, br_ref[0])
        o_ref[...] = x1
        h_s[...] = hi
        gate_s[...] = gate

    @pl.when(j > 0)
    def _():
        h = h_s[...]
        hid = _silu(_dot(h, w1_ref[...])) * _dot(h, w3_ref[...])
        gate = gate_s[...]
        lane = lax.broadcasted_iota(jnp.int32, gate.shape, 1)
        n_e = th // D_EXPERT
        col = lax.broadcasted_iota(jnp.int32, hid.shape, 1) // D_EXPERT
        gmat = jnp.zeros(hid.shape, F32)
        for e in range(n_e):
            ge = jnp.where(lane == (j - 1) * n_e + e, gate, 0.0).sum(axis=1, keepdims=True)
            gmat = jnp.where(col == e, ge, gmat)
        o_ref[...] += m[5:6] * _dot((hid * gmat).astype(BF16), w2_ref[...])


def _ffn_call(layer, x, ma, mb, mc, mods, g, wo, wr, br, w1, w3, w2, slot_fn, tm, th):
    n_tok = x.shape[0]
    n_h = w1.shape[1] // th
    hidden = lambda j: jnp.where(j == 0, n_h - 1, j - 1)
    tok = lambda w: pl.BlockSpec((tm, w), lambda i, j: (i, 0))
    layer_spec = lambda shape: pl.BlockSpec((1,) + shape, lambda i, j: (layer,) + (0,) * len(shape))
    return pl.pallas_call(
        _ffn_kernel,
        grid=(n_tok // tm, n_h + 1),
        in_specs=[tok(D_MODEL), tok(Q_W), tok(B_HEADS * B_DIM), tok(Q_W),
                  pl.BlockSpec((1, 1, 6, D_MODEL), lambda i, j: (layer, slot_fn(i), 0, 0)),
                  layer_spec((1, D_MODEL)), layer_spec((D_MODEL, D_MODEL)), layer_spec((D_MODEL, 2 * LANES)),
                  layer_spec((1, LANES)),
                  pl.BlockSpec((D_MODEL, th), lambda i, j: (0, hidden(j))),
                  pl.BlockSpec((D_MODEL, th), lambda i, j: (0, hidden(j))),
                  pl.BlockSpec((th, D_MODEL), lambda i, j: (hidden(j), 0))],
        out_specs=tok(D_MODEL),
        out_shape=jax.ShapeDtypeStruct((n_tok, D_MODEL), F32),
        scratch_shapes=[pltpu.VMEM((tm, D_MODEL), BF16), pltpu.VMEM((tm, LANES), F32)],
        compiler_params=_params(("arbitrary", "arbitrary")),
        name="ffn",
    )(x, ma, mb, mc, mods, g, wo, wr, br, w1, w3, w2)


def _final_norm_kernel(x_ref, g_ref, o_ref):
    x = x_ref[...]
    o_ref[...] = x * lax.rsqrt(jnp.mean(x * x, axis=-1, keepdims=True) + EPS) * g_ref[...]


def _final_norm_call(x, g, tm, row0, n_rows):
    blk0 = row0 // tm
    return pl.pallas_call(
        _final_norm_kernel,
        grid=(n_rows // tm,),
        in_specs=[pl.BlockSpec((tm, D_MODEL), lambda i: (blk0 + i, 0)), pl.BlockSpec((1, D_MODEL), lambda i: (0, 0))],
        out_specs=pl.BlockSpec((tm, D_MODEL), lambda i: (i, 0)),
        out_shape=jax.ShapeDtypeStruct((n_rows, D_MODEL), F32),
        compiler_params=_params(("arbitrary",)),
        name="final_norm",
    )(x, g)


def _rope_tables(t):
    pos = np.arange(t)
    n_freq = HEAD_DIM // 4
    inv_freq = ROPE_THETA ** (-jnp.arange(n_freq, dtype=F32) / n_freq)
    row = jnp.asarray(pos // GRID_W, F32)
    col = jnp.asarray(pos % GRID_W, F32)
    ang = jnp.concatenate([row[:, None] * inv_freq, col[:, None] * inv_freq], -1)
    cos, sin = jnp.cos(ang), jnp.sin(ang)
    cos_t = jnp.tile(jnp.concatenate([cos, cos], -1), (1, LANES // HEAD_DIM))
    sin_t = jnp.tile(jnp.concatenate([-sin, sin], -1), (1, LANES // HEAD_DIM))
    return cos_t, sin_t


def _delta_tables():
    r = np.arange(PAIR)
    same = (r[:, None] // CHUNK) == (r[None, :] // CHUNK)
    low = same & (r[:, None] >= r[None, :])
    low_s = same & (r[:, None] > r[None, :])
    up = same & (r[:, None] <= r[None, :])
    up_s = same & (r[:, None] < r[None, :])
    levels = []
    for k in range(N_LEVELS):
        s = 1 << k
        levels.append(((r[:, None] // (2 * s)) == (r[None, :] // (2 * s))) & ((r[:, None] // s) != (r[None, :] // s)))
    masks = jnp.asarray(np.stack([low, low_s, up, up_s, np.eye(PAIR, dtype=bool)] + levels).astype(np.float32))
    return masks


def _segment_mean_table():
    r = np.arange(LANES)
    seg = ((r[:, None] // HEAD_DIM) == (r[None, :] // HEAD_DIM)).astype(np.float32) / HEAD_DIM
    hi = jnp.asarray(seg, BF16)
    lo = (jnp.asarray(seg) - hi.astype(F32)).astype(BF16)
    return jnp.stack([hi, lo])


def kernel(x_prompt, x_sample, cache_a_k, cache_a_v, cache_c_k, cache_c_v, state_b_fwd, state_b_bwd, c, c_ctx, w_mod, b_mod, norm1_g, norm2_g, w_in, a_sink, b_conv, b_a_log, b_dt_bias, b_norm_g, c_q_norm, c_k_norm, w_out, w_group, b_group, w_expert, b_expert, w1, w3, w2, final_norm_g):
    n_p, t_p, d = x_prompt.shape
    n_s, t_s, _ = x_sample.shape
    depth = w_in.shape[0]
    past = cache_a_k.shape[2]
    tok_p = n_p * t_p
    n_tok = tok_p + n_s * t_s
    assert d == D_MODEL and tok_p % t_s == 0 and t_s % max(TM_PROJ, TM_FFN) == 0 and t_p % 256 == 0

    w_in_t = jnp.swapaxes(w_in, 1, 2)
    w_out16 = w_out.astype(BF16)
    pad_g = jnp.zeros((depth, d, EXPERT_ROW0 - N_GROUPS), F32)
    pad_e = jnp.zeros((depth, d, LANES - EXPERT_ROW0 - N_EXPERTS), F32)
    w_r = jnp.concatenate([w_group, pad_g, w_expert, pad_e], -1)
    w_r_hi = w_r.astype(BF16)
    w_r2 = jnp.concatenate([w_r_hi, (w_r - w_r_hi.astype(F32)).astype(BF16)], axis=-1)
    b_r = jnp.concatenate([b_group, pad_g[:, 0], b_expert, pad_e[:, 0]], -1)[:, None, :]
    cqn = jnp.tile(c_q_norm, (1, 4))[:, None, :]
    ckn = jnp.tile(c_k_norm, (1, 2))[:, None, :]
    gate_prm = jnp.stack([b_a_log.reshape(depth, 8), b_dt_bias.reshape(depth, 8)], 1)
    prmr = jnp.broadcast_to(jnp.pad(gate_prm, ((0, 0), (0, 0), (0, N_AB - 8)))[..., None],
                            (depth, 2, N_AB, LANES))
    cos_t, sin_t = _rope_tables(t_s)
    masks = _delta_tables()
    seg = _segment_mean_table()

    cond = jnp.concatenate([c_ctx[None, :], c], axis=0)
    cond_b = jnp.broadcast_to(cond[:, :, None], cond.shape + (LANES,))
    mods_all = _mods_call(cond_b, w_mod, b_mod).reshape(depth, SUBLANES, 6, d)

    def slot_fn(tm):
        per_s = t_s // tm
        first = tok_p // tm
        return lambda i: jnp.where(i < first, 0, 1 + (i - first) // per_s)

    xs = (x_prompt.reshape(tok_p, d), x_sample.reshape(n_s * t_s, d))
    blk_s = tok_p // t_s
    ctx = tuple(a.reshape(n_s, depth, past, LANES) for a in (cache_a_k, cache_a_v, cache_c_k, cache_c_v))
    s0 = tuple(a.reshape(n_s, depth, B_HEADS * B_DIM, B_DIM) for a in (state_b_fwd, state_b_bwd))
    g1, g2, bng = norm1_g[:, None, :], norm2_g[:, None, :], b_norm_g[:, None, :]
    caches = None
    states = None
    for l in range(depth):
        za, zb, zc, zab, zabt, *slab = _inproj_call(l, xs, mods_all, g1, w_in_t, slot_fn(TM_PROJ), TM_PROJ)
        x = slab[0] if slab else xs[0]

        ao, co, *rest = _attn_call(False, t_p, ATTN_NSEQ, n_p, 0, l, za, zc, a_sink, cqn, ckn, seg, prev=caches,
                                   experts=(w1, w3, w2))
        caches, (w1b, w3b, w2b) = rest[:4], rest[4:]
        ao, co = _attn_call(True, t_s, 1, n_s, blk_s, l, za, zc, a_sink, cqn, ckn, seg, prev=(ao, co),
                            rope=(cos_t, sin_t), ctx=ctx)

        bo, *states = _delta_call(False, t_p, DELTA_NSEQ, n_p, 0, l, zb, zab, zabt, b_conv, prmr, bng, masks,
                                  prev=states)
        (bo,) = _delta_call(True, t_s, 1, n_s, blk_s, l, zb, zab, zabt, b_conv, prmr, bng, masks,
                            prev=(bo,), s0=s0)

        x = _ffn_call(l, x, ao, bo, co, mods_all, g2, w_out16, w_r2, b_r, w1b, w3b, w2b, slot_fn(TM_FFN), TM_FFN, TH_FFN)
        xs = (x,)

    y_prompt = _final_norm_call(x, final_norm_g[None], TM_NORM, 0, tok_p).reshape(n_p, t_p, d)
    y_sample = _final_norm_call(x, final_norm_g[None], TM_NORM, tok_p, n_s * t_s).reshape(n_s, t_s, d)
    new_ak, new_av, new_ck, new_cv = (a.reshape(n_p, depth, t_p, 2, HEAD_DIM) for a in caches)
    new_sf, new_sb = (a.reshape(n_p, depth, B_HEADS, B_DIM, B_DIM) for a in states)
    return (y_prompt, y_sample, new_ak, new_av, new_ck, new_cv, new_sf, new_sb)
```

```python
import functools

import jax
import jax.numpy as jnp
import numpy as np
from jax import lax
from jax.experimental import pallas as pl
from jax.experimental.pallas import tpu as pltpu

F32 = jnp.float32
BF16 = jnp.bfloat16

D_MODEL = 1024
GRID_W = 64
EPS = 1e-6
NEG_INF = -1e30
ROPE_THETA = 10000.0
HEAD_DIM = 64
Q_W = 256
KV_W = 128
WINDOW = 128
Q_BLOCK = 128
B_HEADS = 4
B_DIM = 128
CHUNK = 64
BD = B_HEADS * CHUNK
PAIR = 2 * CHUNK
N_LEVELS = 6
PREP_UNROLL = 4
DELTA_NSEQ = 4
ATTN_NSEQ = 2
N_GROUPS = 4
EXPERTS_PER_GROUP = 4
N_EXPERTS = 16
D_EXPERT = 256
EXPERT_ROW0 = 8

LANES = 128
SUBLANES = 8
VMEM_LIMIT = 60000 * 1024

TM_PROJ = 512
TM_FFN, TH_FFN = 1024, 1024
TM_NORM = 512
MODS_TN = 1536

ZA_W, ZB_W, ZC_W, ZAB_W = 512, 2048, 512, 128
N_AB = 16
Z_W = ZA_W + ZB_W + ZC_W + ZAB_W


def _sigmoid(x):
    return 1.0 / (1.0 + jnp.exp(-x))


def _silu(x):
    return x * _sigmoid(x)


def _softplus(x):
    return jnp.maximum(x, 0.0) + jnp.log1p(jnp.exp(-jnp.abs(x)))


def _dot(a, b):
    return jnp.dot(a, b, preferred_element_type=F32)


def _dot_nt(a, b):
    return lax.dot_general(a, b, (((1,), (1,)), ((), ())), preferred_element_type=F32)


def _dot_tn(a, b):
    return lax.dot_general(a, b, (((0,), (0,)), ((), ())), preferred_element_type=F32)


def _split2(x):
    hi = x.astype(BF16)
    lo = (x - hi.astype(F32)).astype(BF16)
    return hi, lo


def _params(sem=None):
    return pltpu.CompilerParams(dimension_semantics=sem, vmem_limit_bytes=VMEM_LIMIT)


def _mods_kernel(cond_ref, w_ref, b_ref, o_ref, act_s):
    n_cond = cond_ref.shape[0]
    tn = w_ref.shape[2]
    reps = tn // LANES

    @pl.when((pl.program_id(0) == 0) & (pl.program_id(1) == 0))
    def _():
        act_s[...] = _silu(cond_ref[...])

    def body(kb, accs):
        r = pl.multiple_of(kb * SUBLANES, SUBLANES)
        w = w_ref[0, pl.ds(r, SUBLANES), :]
        return tuple(acc + jnp.tile(act_s[m, pl.ds(r, SUBLANES), :], (1, reps)) * w for m, acc in enumerate(accs))

    zero = jnp.zeros((SUBLANES, tn), F32)
    accs = lax.fori_loop(0, w_ref.shape[1] // SUBLANES, body, (zero,) * n_cond, unroll=4)
    rows = [jnp.sum(a, axis=0, keepdims=True) + b_ref[0] for a in accs]
    rows.append(jnp.zeros((SUBLANES - n_cond, tn), F32))
    o_ref[0] = jnp.concatenate(rows, axis=0)


def _mods_call(cond_b, w_mod, b_mod):
    depth, d, n = w_mod.shape
    tn = MODS_TN
    n_cond = cond_b.shape[0]
    return pl.pallas_call(
        _mods_kernel,
        grid=(depth, n // tn),
        in_specs=[
            pl.BlockSpec((n_cond, d, LANES), lambda l, j: (0, 0, 0)),
            pl.BlockSpec((1, d, tn), lambda l, j: (l, 0, j)),
            pl.BlockSpec((1, 1, tn), lambda l, j: (l, 0, j)),
        ],
        out_specs=pl.BlockSpec((1, SUBLANES, tn), lambda l, j: (l, 0, j)),
        out_shape=jax.ShapeDtypeStruct((depth, SUBLANES, n), F32),
        scratch_shapes=[pltpu.VMEM((n_cond, d, LANES), F32)],
        compiler_params=_params(("arbitrary", "arbitrary")),
        name="mods",
    )(cond_b, w_mod, b_mod.reshape(depth, 1, n))


def _x_specs(xs, tm):
    if len(xs) == 1:
        return [pl.BlockSpec((tm, D_MODEL), lambda i, *_: (i, 0))]
    first = xs[0].shape[0] // tm
    return [pl.BlockSpec((tm, D_MODEL), lambda i, *_: (jnp.minimum(i, first - 1), 0)),
            pl.BlockSpec((tm, D_MODEL), lambda i, *_: (jnp.maximum(i - first, 0), 0))]


def _x_tile(x_refs, first):
    if len(x_refs) == 1:
        return x_refs[0][...]
    return jnp.where(pl.program_id(0) < first, x_refs[0][...], x_refs[1][...])


def _modulated_norm(x, g, shift, scale):
    ms = jnp.mean(x * x, axis=-1, keepdims=True)
    y = x * lax.rsqrt(ms + EPS) * g
    return y * (1.0 + scale) + shift


def _inproj_kernel(n_x, first, *refs):
    x_refs = refs[:n_x]
    mod_ref, g_ref, wt_ref, za_ref, zb_ref, zc_ref, zab_ref, zabt_ref = refs[n_x:n_x + 8]
    w_s = refs[-1]
    @pl.when(pl.program_id(0) == 0)
    def _():
        ab0 = ZA_W + ZB_W
        w_s[0:ab0, :] = wt_ref[0, 0:ab0, :].astype(BF16)
        w_s[ab0:ab0 + ZC_W, :] = wt_ref[0, ab0 + N_AB:ab0 + N_AB + ZC_W, :].astype(BF16)
        w_s[ab0 + ZC_W:ab0 + ZC_W + N_AB, :] = wt_ref[0, ab0:ab0 + N_AB, :].astype(BF16)
        w_s[ab0 + ZC_W + N_AB:Z_W, :] = jnp.zeros((ZAB_W - N_AB, D_MODEL), BF16)

    m = mod_ref[0, 0]
    x = _x_tile(x_refs, first)
    if n_x > 1:
        refs[n_x + 8][...] = x
    h = _modulated_norm(x, g_ref[0], m[0:1], m[1:2]).astype(BF16)
    za_ref[...] = _dot_nt(h, w_s[0:ZA_W, :])
    step = 512
    for j in range(ZB_W // step):
        zb_ref[:, j * step:(j + 1) * step] = _dot_nt(h, w_s[ZA_W + j * step:ZA_W + (j + 1) * step, :])
    zc_ref[...] = _dot_nt(h, w_s[ZA_W + ZB_W:ZA_W + ZB_W + ZC_W, :])
    zab = _dot_nt(h, w_s[ZA_W + ZB_W + ZC_W:Z_W, :])
    zab_ref[...] = zab
    zabt_ref[...] = zab.T[:N_AB]


def _inproj_call(layer, xs, mods, g, w, slot_fn, tm):
    n_tok = sum(a.shape[0] for a in xs)
    n_ab = N_AB
    return pl.pallas_call(
        functools.partial(_inproj_kernel, len(xs), xs[0].shape[0] // tm),
        grid=(n_tok // tm,),
        in_specs=_x_specs(xs, tm) + [
            pl.BlockSpec((1, 1, 6, D_MODEL), lambda i: (layer, slot_fn(i), 0, 0)),
            pl.BlockSpec((1, 1, D_MODEL), lambda i: (layer, 0, 0)),
            pl.BlockSpec((1, w.shape[1], D_MODEL), lambda i: (layer, 0, 0)),
        ],
        out_specs=[
            pl.BlockSpec((tm, ZA_W), lambda i: (i, 0)),
            pl.BlockSpec((tm, ZB_W), lambda i: (i, 0)),
            pl.BlockSpec((tm, ZC_W), lambda i: (i, 0)),
            pl.BlockSpec((tm, ZAB_W), lambda i: (i, 0)),
            pl.BlockSpec((n_ab, tm), lambda i: (0, i)),
        ] + ([pl.BlockSpec((tm, D_MODEL), lambda i: (i, 0))] if len(xs) > 1 else []),
        out_shape=[
            jax.ShapeDtypeStruct((n_tok, ZA_W), F32),
            jax.ShapeDtypeStruct((n_tok, ZB_W), F32),
            jax.ShapeDtypeStruct((n_tok, ZC_W), F32),
            jax.ShapeDtypeStruct((n_tok, ZAB_W), F32),
            jax.ShapeDtypeStruct((n_ab, n_tok), F32),
        ] + ([jax.ShapeDtypeStruct((n_tok, D_MODEL), F32)] if len(xs) > 1 else []),
        scratch_shapes=[pltpu.VMEM((Z_W, D_MODEL), BF16)],
        compiler_params=_params(("arbitrary",)),
        name="inproj",
    )(*xs, mods, g, w)


def _lane_lo(shape):
    return lax.broadcasted_iota(jnp.int32, shape, len(shape) - 1) % LANES < HEAD_DIM


def _store_kdup(dst_ref, off, k):
    n = k.shape[0]
    r = pltpu.roll(k, HEAD_DIM, 1)
    lo = _lane_lo(k.shape)
    dst_ref[0, off:off + n, :] = jnp.where(lo, k, r).astype(BF16)
    dst_ref[1, off:off + n, :] = jnp.where(lo, r, k).astype(BF16)


def _store_vsplit(dst_ref, off, v):
    n = v.shape[0]
    r = pltpu.roll(v, HEAD_DIM, 1)
    lo = _lane_lo(v.shape)
    z = jnp.zeros_like(v)
    dst_ref[0, off:off + n, :] = jnp.where(lo, v, z).astype(BF16)
    dst_ref[1, off:off + n, :] = jnp.where(lo, z, r).astype(BF16)
    dst_ref[2, off:off + n, :] = jnp.where(lo, r, z).astype(BF16)
    dst_ref[3, off:off + n, :] = jnp.where(lo, z, v).astype(BF16)


def _rope(x, cos, sin):
    first = (lax.broadcasted_iota(jnp.int32, x.shape, 1) // (HEAD_DIM // 2)) % 2 == 0
    partner = jnp.where(first, pltpu.roll(x, LANES - HEAD_DIM // 2, 1), pltpu.roll(x, HEAD_DIM // 2, 1))
    return x * cos + partner * sin


def _head_rmsnorm(x, g, seg_hi, seg_lo):
    hi, lo = _split2(x * x)
    ms = _dot(hi, seg_hi) + _dot(lo, seg_hi) + _dot(hi, seg_lo)
    return x * lax.rsqrt(ms + EPS) * g


def _attend_many(units):
    qb = units[0][0].shape[0]
    lo = _lane_lo(units[0][0].shape)
    all_scores = []
    for qt, segs, _ in units:
        z = jnp.zeros_like(qt)
        qs = jnp.concatenate([jnp.where(lo, qt, z), jnp.where(lo, z, qt)], axis=0).astype(BF16)
        scores = []
        for kdup, _, _, mask in segs:
            s = _dot_nt(qs, kdup)
            if mask is not None:
                s = jnp.where(mask, s, NEG_INF)
            scores.append(s)
        all_scores.append(scores)
    probs = []
    for (qt, segs, sink_pair), scores in zip(units, all_scores):
        m = scores[0].max(axis=1, keepdims=True)
        for s in scores[1:]:
            m = jnp.maximum(m, s.max(axis=1, keepdims=True))
        if sink_pair is not None:
            row_a = lax.broadcasted_iota(jnp.int32, (2 * qb, 1), 0) < qb
            sink = jnp.where(row_a, sink_pair[0], sink_pair[1])
            m = jnp.maximum(m, sink)
            denom = jnp.exp(sink - m)
        else:
            denom = jnp.zeros((2 * qb, 1), F32)
        ps = []
        for s in scores:
            p = jnp.exp(s - m)
            denom = denom + p.sum(axis=1, keepdims=True)
            ps.append(p.astype(BF16))
        probs.append((ps, 1.0 / denom))
    outs = []
    for (qt, segs, _), (ps, inv) in zip(units, probs):
        acc = jnp.zeros((qb, LANES), F32)
        for pb, (_, vlo, vhi, _) in zip(ps, segs):
            acc = acc + _dot(pb[:qb], vlo) + _dot(pb[qb:], vhi)
        outs.append(acc * jnp.where(lo, inv[:qb], inv[qb:]))
    return outs


def _attn_kernel(has_ctx, t, nseq, layer, *refs):
    if has_ctx:
        (sink_ref, za_ref, zc_ref, cqn_ref, ckn_ref, seg_ref, cos_ref, sin_ref,
         cak_ref, cav_ref, cck_ref, ccv_ref,
         ao_ref, co_ref,
         ka_s, va_s, kc_s, vc_s, kctx_s, vctx_s, qa_s, qc_s) = refs
    else:
        (sink_ref, za_ref, zc_ref, cqn_ref, ckn_ref, seg_ref, w1_ref, w3_ref, w2_ref,
         ao_ref, co_ref, nak_ref, nav_ref, nck_ref, ncv_ref, w1b_ref, w3b_ref, w2b_ref,
         ka_s, va_s, kc_s, vc_s, qa_s, qc_s) = refs
        w1b_ref[...] = w1_ref[0].astype(BF16)
        w3b_ref[...] = w3_ref[0].astype(BF16)
        w2b_ref[...] = w2_ref[0].astype(BF16)
    scale = HEAD_DIM ** -0.5
    seg_hi = seg_ref[0]
    seg_lo = seg_ref[1]
    piece = 256
    n_ctx = cak_ref.shape[2] if has_ctx else 0

    for p0 in range(0, nseq * t, piece):
        rows = slice(p0, p0 + piece)
        ak = za_ref[rows, Q_W:Q_W + KV_W]
        av = za_ref[rows, Q_W + KV_W:Q_W + 2 * KV_W]
        ck = _head_rmsnorm(zc_ref[rows, Q_W:Q_W + KV_W], ckn_ref[0], seg_hi, seg_lo)
        cv = zc_ref[rows, Q_W + KV_W:Q_W + 2 * KV_W]
        if has_ctx:
            cos = cos_ref[rows, :]
            sin = sin_ref[rows, :]
            ak = _rope(ak, cos, sin)
            ck = _rope(ck, cos, sin)
            _store_kdup(ka_s, WINDOW + p0, ak)
            _store_vsplit(va_s, WINDOW + p0, av)
            _store_kdup(kc_s, n_ctx + p0, ck)
            _store_vsplit(vc_s, n_ctx + p0, cv)
        else:
            crow = slice(p0 % t, p0 % t + piece)
            nak_ref[p0 // t, 0, crow, :] = ak
            nav_ref[p0 // t, 0, crow, :] = av
            nck_ref[p0 // t, 0, crow, :] = ck
            ncv_ref[p0 // t, 0, crow, :] = cv
            _store_kdup(ka_s, p0, ak)
            _store_vsplit(va_s, p0, av)
            _store_kdup(kc_s, p0, ck)
            _store_vsplit(vc_s, p0, cv)
        for hk in range(2):
            cols = slice(hk * LANES, (hk + 1) * LANES)
            aq = za_ref[rows, cols]
            cq = _head_rmsnorm(zc_ref[rows, cols], cqn_ref[0, :, cols], seg_hi, seg_lo)
            if has_ctx:
                aq = _rope(aq, cos, sin)
                cq = _rope(cq, cos, sin)
            qa_s[rows, cols] = aq * scale
            qc_s[rows, cols] = cq * scale

    if has_ctx:
        zpad = jnp.zeros((WINDOW, LANES), BF16)
        for i in range(2):
            ka_s[i, 0:WINDOW, :] = zpad
            ka_s[i, WINDOW + t:2 * WINDOW + t, :] = zpad
        for i in range(4):
            va_s[i, 0:WINDOW, :] = zpad
            va_s[i, WINDOW + t:2 * WINDOW + t, :] = zpad
        for p0 in range(0, n_ctx, piece):
            rows = slice(p0, p0 + piece)
            _store_kdup(kctx_s, p0, cak_ref[0, 0, rows, :])
            _store_vsplit(vctx_s, p0, cav_ref[0, 0, rows, :])
            _store_kdup(kc_s, p0, cck_ref[0, 0, rows, :])
            _store_vsplit(vc_s, p0, ccv_ref[0, 0, rows, :])

        qb = Q_BLOCK
        span = qb + 2 * WINDOW
        qi = lax.broadcasted_iota(jnp.int32, (2 * qb, span), 0) % qb
        kj = lax.broadcasted_iota(jnp.int32, (2 * qb, span), 1)
        band = jnp.abs(kj - WINDOW - qi) <= WINDOW

        def block(b, carry):
            r0 = pl.multiple_of(b * qb, qb)
            kpos = kj + (r0 - WINDOW)
            mask = band & (kpos >= 0) & (kpos < t)
            units = []
            for hk in range(2):
                cols = slice(hk * LANES, (hk + 1) * LANES)
                segs_a = [
                    (kctx_s[hk], vctx_s[2 * hk], vctx_s[2 * hk + 1], None),
                    (ka_s[hk, pl.ds(r0, span), :], va_s[2 * hk, pl.ds(r0, span), :],
                     va_s[2 * hk + 1, pl.ds(r0, span), :], mask),
                ]
                sinks = (sink_ref[layer, 2 * hk], sink_ref[layer, 2 * hk + 1])
                units.append((qa_s[pl.ds(r0, qb), cols], segs_a, sinks))
                segs_c = [(kc_s[hk], vc_s[2 * hk], vc_s[2 * hk + 1], None)]
                units.append((qc_s[pl.ds(r0, qb), cols], segs_c, None))
            outs = _attend_many(units)
            for hk in range(2):
                cols = slice(hk * LANES, (hk + 1) * LANES)
                ao_ref[pl.ds(r0, qb), cols] = outs[2 * hk].astype(BF16)
                co_ref[pl.ds(r0, qb), cols] = outs[2 * hk + 1].astype(BF16)
            return carry

        lax.fori_loop(0, t // qb, block, 0)
    else:
        units = []
        for q in range(nseq):
            seq = slice(q * t, (q + 1) * t)
            for hk in range(2):
                cols = slice(hk * LANES, (hk + 1) * LANES)
                sinks = (sink_ref[layer, 2 * hk], sink_ref[layer, 2 * hk + 1])
                units.append((qa_s[seq, cols],
                              [(ka_s[hk, seq, :], va_s[2 * hk, seq, :], va_s[2 * hk + 1, seq, :], None)], sinks))
                units.append((qc_s[seq, cols],
                              [(kc_s[hk, seq, :], vc_s[2 * hk, seq, :], vc_s[2 * hk + 1, seq, :], None)], None))
        outs = _attend_many(units)
        for q in range(nseq):
            seq = slice(q * t, (q + 1) * t)
            for hk in range(2):
                cols = slice(hk * LANES, (hk + 1) * LANES)
                ao_ref[seq, cols] = outs[4 * q + 2 * hk].astype(BF16)
                co_ref[seq, cols] = outs[4 * q + 2 * hk + 1].astype(BF16)


def _attn_call(has_ctx, t, nseq, n_batch, row_block0, layer, za, zc, sink, cqn, ckn, seg, prev=None, rope=None,
               ctx=None, experts=None):
    n_tok = za.shape[0]
    depth = sink.shape[0]
    assert n_batch % nseq == 0 and (nseq == 1 or not has_ctx)
    tok_spec = lambda w: pl.BlockSpec((nseq * t, w), lambda b, *_: (row_block0 + b, 0))
    const = lambda shape: pl.BlockSpec(shape, lambda b, *_: (0,) * len(shape))
    layer_spec = lambda shape: pl.BlockSpec((1,) + shape, lambda b, *_: (layer,) + (0,) * len(shape))
    in_specs = [tok_spec(ZA_W), tok_spec(ZC_W), layer_spec((1, Q_W)), layer_spec((1, KV_W)), const((2, LANES, LANES))]
    args = [za, zc, cqn, ckn, seg]
    out_specs = [tok_spec(Q_W), tok_spec(Q_W)]
    out_shape = [jax.ShapeDtypeStruct((n_tok, Q_W), BF16), jax.ShapeDtypeStruct((n_tok, Q_W), BF16)]
    if has_ctx:
        n_ctx = ctx[0].shape[2]
        in_specs += [const((t, LANES)), const((t, LANES))]
        args += list(rope)
        in_specs += [pl.BlockSpec((1, 1, n_ctx, LANES), lambda b, *_: (b, layer, 0, 0))] * 4
        args += list(ctx)
        scratch = [
            pltpu.VMEM((2, t + 2 * WINDOW, LANES), BF16), pltpu.VMEM((4, t + 2 * WINDOW, LANES), BF16),
            pltpu.VMEM((2, n_ctx + t, LANES), BF16), pltpu.VMEM((4, n_ctx + t, LANES), BF16),
            pltpu.VMEM((2, n_ctx, LANES), BF16), pltpu.VMEM((4, n_ctx, LANES), BF16),
            pltpu.VMEM((t, Q_W), F32), pltpu.VMEM((t, Q_W), F32),
        ]
    else:
        cache_spec = pl.BlockSpec((nseq, 1, t, LANES), lambda b, *_: (b, layer, 0, 0))
        out_specs += [cache_spec] * 4
        out_shape += [jax.ShapeDtypeStruct((n_batch, depth, t, LANES), F32)] * 4
        n_steps = n_batch // nseq
        for w in experts:
            rows_w = w.shape[1] // n_steps
            in_specs.append(pl.BlockSpec((1, rows_w, w.shape[2]), lambda b, *_: (layer, b, 0)))
            args.append(w)
            out_specs.append(pl.BlockSpec((rows_w, w.shape[2]), lambda b, *_: (b, 0)))
            out_shape.append(jax.ShapeDtypeStruct(w.shape[1:], BF16))
        rows = nseq * t
        scratch = [
            pltpu.VMEM((2, rows, LANES), BF16), pltpu.VMEM((4, rows, LANES), BF16),
            pltpu.VMEM((2, rows, LANES), BF16), pltpu.VMEM((4, rows, LANES), BF16),
            pltpu.VMEM((rows, Q_W), F32), pltpu.VMEM((rows, Q_W), F32),
        ]
    n_real = len(args)
    aliases = {}
    if prev is not None:
        first_out = 0 if has_ctx else 2
        for k, arr in enumerate(prev):
            in_specs.append(pl.BlockSpec(memory_space=pl.ANY))
            args.append(arr)
            aliases[1 + n_real + k] = first_out + k

    def body(*refs):
        ins = refs[:1 + n_real]
        rest = refs[1 + len(args):]
        _attn_kernel(has_ctx, t, nseq, layer, *ins, *rest)

    return pl.pallas_call(
        body,
        grid_spec=pltpu.PrefetchScalarGridSpec(
            num_scalar_prefetch=1, grid=(n_batch // nseq,), in_specs=in_specs, out_specs=out_specs,
            scratch_shapes=scratch),
        out_shape=out_shape,
        input_output_aliases=aliases,
        compiler_params=_params(("arbitrary",)),
        name="attn_latent" if has_ctx else "attn_prompt",
    )(sink, *args)


def _stack_pair(x, p):
    return jnp.concatenate([x[:, (2 * p + hl) * B_DIM:(2 * p + hl + 1) * B_DIM] for hl in range(2)], axis=0)


def _delta_kernel(t, nseq, has_s0, *refs):
    if has_s0:
        (zb_ref, abc_ref, abt_ref, conv_ref, prmr_ref, bng_ref, mask_ref,
         s0f_ref, s0b_ref, o_ref, qkv_s, of_s, ob_s, sf_s, sb_s, u_s, wq_s, at_s, kd_s, eg_s,
         pre_s, suf_s, prec_s, sufc_s) = refs
    else:
        (zb_ref, abc_ref, abt_ref, conv_ref, prmr_ref, bng_ref, mask_ref,
         o_ref, sfo_ref, sbo_ref, qkv_s, of_s, ob_s, sf_s, sb_s, u_s, wq_s, at_s, kd_s, eg_s,
         pre_s, suf_s, prec_s, sufc_s) = refs
    n_chunks = t // CHUNK
    n_total = nseq * n_chunks
    s_rows = B_HEADS * B_DIM
    qk_w = B_HEADS * B_DIM

    row = lax.broadcasted_iota(jnp.int32, (t, LANES), 0)
    for q in range(nseq):
        seq = slice(q * t, (q + 1) * t)
        for j in range(3 * B_HEADS):
            cols = slice(j * LANES, (j + 1) * LANES)
            x = zb_ref[seq, cols]
            prev = jnp.where(row == 0, 0.0, pltpu.roll(x, 1, 0))
            nxt = jnp.where(row == t - 1, 0.0, pltpu.roll(x, t - 1, 0))
            y = _silu(prev * conv_ref[0, 0:1, cols] + x * conv_ref[0, 1:2, cols] + nxt * conv_ref[0, 2:3, cols])
            if j < 2 * B_HEADS:
                y = y * lax.rsqrt(jnp.sum(y * y, axis=-1, keepdims=True) + EPS)
            if j < B_HEADS:
                y = y * (B_DIM ** -0.5)
            qkv_s[seq, cols] = y

    if has_s0:
        for q in range(nseq):
            sf_s[q * s_rows:(q + 1) * s_rows, :] = s0f_ref[q, 0]
            sb_s[q * s_rows:(q + 1) * s_rows, :] = s0b_ref[q, 0]
    else:
        sf_s[...] = jnp.zeros_like(sf_s)
        sb_s[...] = jnp.zeros_like(sb_s)

    reps = nseq * t // LANES
    gr = -jnp.tile(jnp.exp(prmr_ref[0, 0]), (1, reps)) * _softplus(abt_ref[...] + jnp.tile(prmr_ref[0, 1], (1, reps)))
    seg_lane = lax.broadcasted_iota(jnp.int32, gr.shape, 1) % CHUNK
    pre, suf = gr, gr
    for s in (1, 2, 4, 8, 16, 32):
        pre = pre + jnp.where(seg_lane >= s, pltpu.roll(pre, s, 1), 0.0)
        suf = suf + jnp.where(seg_lane < CHUNK - s, pltpu.roll(suf, nseq * t - s, 1), 0.0)
    pre_s[...] = pre
    suf_s[...] = suf
    zrows = jnp.zeros((LANES - pre.shape[0], LANES), F32)
    for j in range(reps):
        tile = slice(j * LANES, (j + 1) * LANES)
        prec_s[tile, :] = jnp.concatenate([pre[:, tile], zrows], axis=0).T
        sufc_s[tile, :] = jnp.concatenate([suf[:, tile], zrows], axis=0).T
    lane_lo = lax.broadcasted_iota(jnp.int32, (1, LANES), 1) < CHUNK

    def prepare(cc, carry):
        chains = []
        for k in range(PREP_UNROLL):
            c = cc * PREP_UNROLL + k
            r0 = pl.multiple_of(c * CHUNK, CHUNK)
            b_all = _sigmoid(abc_ref[pl.ds(r0, CHUNK), :])
            run_c = (prec_s[pl.ds(r0, CHUNK), :], sufc_s[pl.ds(r0, CHUNK), :])
            tile0 = pl.multiple_of((cc * PREP_UNROLL + k - k % 2) * CHUNK, LANES)
            run = (pre_s[:, pl.ds(tile0, LANES)], suf_s[:, pl.ds(tile0, LANES)])
            run_r = tuple(pltpu.roll(x, CHUNK, 1) for x in run)
            for p in range(B_HEADS // 2):
                kst = _stack_pair(qkv_s[pl.ds(r0, CHUNK), qk_w:2 * qk_w], p)
                qst = _stack_pair(qkv_s[pl.ds(r0, CHUNK), 0:qk_w], p)
                vst = _stack_pair(qkv_s[pl.ds(r0, CHUNK), 2 * qk_w:3 * qk_w], p)
                kq = _dot_nt(jnp.concatenate([kst, qst], axis=0).astype(BF16), kst.astype(BF16))
                for d in range(2):
                    cg = 4 * d + 2 * p
                    edge = CHUNK - 1 if d == 0 else 0
                    rep_col = lambda x, col: jnp.broadcast_to(x[:, col:col + 1], (CHUNK, LANES))
                    b_rep = jnp.concatenate([rep_col(b_all, 8 + cg + hl) for hl in range(2)], axis=0)
                    gcol = jnp.concatenate([rep_col(run_c[d], cg + hl) for hl in range(2)], axis=0)
                    gtot = jnp.concatenate([rep_col(run_c[d][edge:edge + 1], cg + hl) for hl in range(2)], axis=0)
                    ra = cg
                    if k % 2 == 0:
                        grow = jnp.where(lane_lo, run[d][ra:ra + 1], run_r[d][ra + 1:ra + 2])
                    else:
                        grow = jnp.where(lane_lo, run_r[d][ra:ra + 1], run[d][ra + 1:ra + 2])
                    chains.append(dict(c=c, p=p, d=d, kst=kst, qst=qst, vst=vst, kq=kq, b_st=b_rep,
                                       gcol=gcol, gtot=gtot, grow=grow))

        for ch in chains:
            d, b_st, kq, gcol = ch["d"], ch["b_st"], ch.pop("kq"), ch["gcol"]
            decay = jnp.exp(jnp.minimum(gcol - ch.pop("grow"), 0.0))
            ch["a_mat"] = (b_st * kq[:PAIR]) * (decay * mask_ref[2 * d + 1])
            ch["attn"] = (kq[PAIR:] * (decay * mask_ref[2 * d])).astype(BF16)
            ch["t_inv"] = mask_ref[4] - ch["a_mat"] * mask_ref[5]
        for lvl in range(N_LEVELS - 1):
            for ch in chains:
                ch["t16"] = ch["t_inv"].astype(BF16)
                ch["et"] = _dot((ch["a_mat"] * mask_ref[6 + lvl]).astype(BF16), ch["t16"])
            for ch in chains:
                ch["t_inv"] = ch["t_inv"] - _dot(ch.pop("t16"), ch.pop("et").astype(BF16))
        for ch in chains:
            egc = jnp.exp(ch["gcol"])
            rk = jnp.concatenate([ch["b_st"] * ch["vst"], (ch["b_st"] * egc) * ch["kst"]], axis=1)
            ch["rk"] = _dot(ch.pop("t_inv").astype(BF16), rk.astype(BF16))
            ch["qp16"] = (ch["qst"] * egc).astype(BF16)
        for ch in chains:
            c, p, d, rk, qp16 = ch["c"], ch["p"], ch["d"], ch["rk"], ch["qp16"]
            pair_rows = slice(p * PAIR, (p + 1) * PAIR)
            w16 = rk[:, B_DIM:].astype(BF16)
            u_s[d, c, pair_rows, :] = rk[:, :B_DIM]
            at_s[d, c, p] = ch["attn"]
            kd_s[d, c, pair_rows, :] = (ch["kst"] * jnp.exp(ch["gtot"] - ch["gcol"])).astype(BF16)
            eg = jnp.exp(ch["gtot"])
            for hl in range(2):
                h = 2 * p + hl
                rows = slice(hl * CHUNK, (hl + 1) * CHUNK)
                wq_s[d, c, h * 2 * CHUNK:h * 2 * CHUNK + CHUNK, :] = w16[rows]
                wq_s[d, c, h * 2 * CHUNK + CHUNK:(h + 1) * 2 * CHUNK, :] = qp16[rows]
                eg_s[d, c, h * SUBLANES:(h + 1) * SUBLANES, :] = eg[hl * CHUNK:hl * CHUNK + SUBLANES, :]
        return carry

    lax.fori_loop(0, n_total // PREP_UNROLL, prepare, 0)

    def scan_step(i, carry):
        units = []
        for q in range(nseq):
            for d, s_ref, o_s in ((0, sf_s, of_s), (1, sb_s, ob_s)):
                c = q * n_chunks + (i if d == 0 else n_chunks - 1 - i)
                units.append(dict(q=q, d=d, c=c, s_ref=s_ref, o_s=o_s, r0=pl.multiple_of(c * CHUNK, CHUNK)))
        for un in units:
            q, d, c, s_ref = un["q"], un["d"], un["c"], un["s_ref"]
            un["x"] = []
            for h in range(B_HEADS):
                srows = slice(q * s_rows + h * B_DIM, q * s_rows + (h + 1) * B_DIM)
                un["x"].append(_dot(wq_s[d, c, h * 2 * CHUNK:(h + 1) * 2 * CHUNK, :], s_ref[srows, :].astype(BF16)))
        for un in units:
            d, c = un["d"], un["c"]
            un["vp16"], un["o"] = [], []
            for p in range(B_HEADS // 2):
                xs = un["x"][2 * p:2 * p + 2]
                v_new = jnp.concatenate(
                    [u_s[d, c, (2 * p + hl) * CHUNK:(2 * p + hl + 1) * CHUNK, :] - xs[hl][:CHUNK] for hl in range(2)],
                    axis=0)
                vp16 = v_new.astype(BF16)
                un["vp16"].append(vp16)
                un["o"].append(jnp.concatenate([xs[hl][CHUNK:] for hl in range(2)], axis=0)
                               + _dot(at_s[d, c, p], vp16))
        for un in units:
            q, d, c, s_ref, o_s, r0 = un["q"], un["d"], un["c"], un["s_ref"], un["o_s"], un["r0"]
            for h in range(B_HEADS):
                p, hl = divmod(h, 2)
                rows = slice(hl * CHUNK, (hl + 1) * CHUNK)
                srows = slice(q * s_rows + h * B_DIM, q * s_rows + (h + 1) * B_DIM)
                upd = _dot_tn(kd_s[d, c, h * CHUNK:(h + 1) * CHUNK, :], un["vp16"][p][rows])
                eg = jnp.tile(eg_s[d, c, h * SUBLANES:(h + 1) * SUBLANES, :], (B_DIM // SUBLANES, 1))
                s_ref[srows, :] = s_ref[srows, :] * eg + upd
                o_s[pl.ds(r0, CHUNK), h * B_DIM:(h + 1) * B_DIM] = un["o"][p][rows]
        return carry

    lax.fori_loop(0, n_chunks, scan_step, 0)

    if not has_s0:
        for q in range(nseq):
            sfo_ref[q, 0] = sf_s[q * s_rows:(q + 1) * s_rows, :]
            sbo_ref[q, 0] = sb_s[q * s_rows:(q + 1) * s_rows, :]

    for h in range(B_HEADS):
        cols = slice(h * B_DIM, (h + 1) * B_DIM)
        x = of_s[:, cols] + ob_s[:, cols]
        yn = x * lax.rsqrt(jnp.mean(x * x, axis=-1, keepdims=True) + EPS) * bng_ref[0]
        o_ref[:, cols] = (yn * _silu(zb_ref[:, 3 * qk_w + h * B_DIM:3 * qk_w + (h + 1) * B_DIM])).astype(BF16)


def _delta_call(has_s0, t, nseq, n_batch, row_block0, layer, zb, zab, zabt, conv, prmr, bng, masks,
                prev=None, s0=None):
    n_tok = zb.shape[0]
    depth = conv.shape[0]
    n_chunks = nseq * (t // CHUNK)
    assert n_chunks % PREP_UNROLL == 0 and PREP_UNROLL % 2 == 0 and n_batch % nseq == 0
    tok_spec = lambda w: pl.BlockSpec((nseq * t, w), lambda b: (row_block0 + b, 0))
    const = lambda shape: pl.BlockSpec(shape, lambda b: (0,) * len(shape))
    layer_spec = lambda shape: pl.BlockSpec((1,) + shape, lambda b: (layer,) + (0,) * len(shape))
    s_shape = (B_HEADS * B_DIM, B_DIM)
    s_spec = pl.BlockSpec((nseq, 1) + s_shape, lambda b: (b, layer, 0, 0))
    n_ab = zabt.shape[0]
    in_specs = [
        tok_spec(ZB_W), tok_spec(ZAB_W),
        pl.BlockSpec((n_ab, nseq * t), lambda b: (0, row_block0 + b)),
        layer_spec((3, 3 * B_HEADS * B_DIM)), layer_spec((2, n_ab, LANES)),
        layer_spec((1, B_DIM)),
        const((5 + N_LEVELS, PAIR, PAIR)),
    ]
    args = [zb, zab, zabt, conv, prmr, bng, masks]
    out_specs = [tok_spec(B_HEADS * B_DIM)]
    out_shape = [jax.ShapeDtypeStruct((n_tok, B_HEADS * B_DIM), BF16)]
    if has_s0:
        in_specs += [s_spec, s_spec]
        args += [s0[0], s0[1]]
    else:
        out_specs += [s_spec, s_spec]
        out_shape += [jax.ShapeDtypeStruct((n_batch, depth) + s_shape, F32)] * 2
    n_real = len(args)
    aliases = {}
    if prev is not None:
        first_out = 0 if has_s0 else 1
        for k, arr in enumerate(prev):
            in_specs.append(pl.BlockSpec(memory_space=pl.ANY))
            args.append(arr)
            aliases[n_real + k] = first_out + k
    rows = nseq * t
    scratch = [
        pltpu.VMEM((rows, 3 * B_HEADS * B_DIM), F32),
        pltpu.VMEM((rows, B_HEADS * B_DIM), F32), pltpu.VMEM((rows, B_HEADS * B_DIM), F32),
        pltpu.VMEM((nseq * s_shape[0], B_DIM), F32), pltpu.VMEM((nseq * s_shape[0], B_DIM), F32),
        pltpu.VMEM((2, n_chunks, BD, B_DIM), F32),
        pltpu.VMEM((2, n_chunks, 2 * BD, B_DIM), BF16),
        pltpu.VMEM((2, n_chunks, B_HEADS // 2, PAIR, PAIR), BF16),
        pltpu.VMEM((2, n_chunks, BD, B_DIM), BF16),
        pltpu.VMEM((2, n_chunks, B_HEADS * SUBLANES, LANES), F32),
        pltpu.VMEM((n_ab, rows), F32), pltpu.VMEM((n_ab, rows), F32),
        pltpu.VMEM((rows, LANES), F32), pltpu.VMEM((rows, LANES), F32),
    ]

    def body(*refs):
        _delta_kernel(t, nseq, has_s0, *refs[:n_real], *refs[len(args):])

    return pl.pallas_call(
        body,
        grid=(n_batch // nseq,),
        in_specs=in_specs,
        out_specs=out_specs,
        out_shape=out_shape,
        scratch_shapes=scratch,
        input_output_aliases=aliases,
        compiler_params=_params(("arbitrary",)),
        name="delta_latent" if has_s0 else "delta_prompt",
    )(*args)


def _outproj_router(x, ma, mb, mc, m, g, wo_ref, wr_ref, br):
    n = x.shape[0]
    b0, c0 = Q_W, Q_W + B_HEADS * B_DIM
    y = (_dot(ma.astype(BF16), wo_ref[0, 0:b0, :])
         + _dot(mb.astype(BF16), wo_ref[0, b0:c0, :])
         + _dot(mc.astype(BF16), wo_ref[0, c0:c0 + Q_W, :]))
    x1 = x + m[2:3] * y
    h2 = _modulated_norm(x1, g, m[3:4], m[4:5])
    hi, lo = _split2(h2)

    hw = _dot(hi, wr_ref[0])
    logits = (hw[:, :LANES] + hw[:, LANES:] + _dot(lo, wr_ref[0, :, :LANES]) + br).T
    gl = logits[0:N_GROUPS]
    grow = lax.broadcasted_iota(jnp.int32, gl.shape, 0)
    gmax = gl.max(axis=0, keepdims=True)
    g_sel = jnp.where(gl == gmax, grow, N_GROUPS).min(axis=0, keepdims=True)
    g_w = 1.0 / jnp.exp(gl - gmax).sum(axis=0, keepdims=True)
    el = logits[EXPERT_ROW0:EXPERT_ROW0 + N_EXPERTS]
    e_idx = lax.broadcasted_iota(jnp.int32, el.shape, 0)
    el = jnp.where((e_idx // EXPERTS_PER_GROUP) == g_sel, el, -jnp.inf)
    m1 = el.max(axis=0, keepdims=True)
    i1 = jnp.where(el == m1, e_idx, N_EXPERTS).min(axis=0, keepdims=True)
    el2 = jnp.where(e_idx == i1, -jnp.inf, el)
    m2 = el2.max(axis=0, keepdims=True)
    i2 = jnp.where(el2 == m2, e_idx, N_EXPERTS).min(axis=0, keepdims=True)
    tt = jnp.exp(m2 - m1)
    w1 = g_w / (1.0 + tt)
    w2 = w1 * tt
    gate_t = jnp.where(e_idx == i1, w1, 0.0) + jnp.where(e_idx == i2, w2, 0.0)
    gate = jnp.concatenate([gate_t, jnp.zeros((LANES - N_EXPERTS, n), F32)], axis=0).T
    return x1, hi, gate


def _ffn_kernel(x_ref, ma_ref, mb_ref, mc_ref, mod_ref, g_ref, wo_ref, wr_ref, br_ref, w1_ref, w3_ref, w2_ref,
                o_ref, h_s, gate_s):
    j = pl.program_id(1)
    tm = x_ref.shape[0]
    th = w1_ref.shape[1]
    m = mod_ref[0, 0]

    @pl.when(j == 0)
    def _():
        x1, hi, gate = _outproj_router(x_ref[...], ma_ref[...], mb_ref[...], mc_ref[...],
                                       m, g_ref[0], wo_ref, wr_ref, br_ref[0])
        o_ref[...] = x1
        h_s[...] = hi
        gate_s[...] = gate

    @pl.when(j > 0)
    def _():
        h = h_s[...]
        hid = _silu(_dot(h, w1_ref[...])) * _dot(h, w3_ref[...])
        gate = gate_s[...]
        lane = lax.broadcasted_iota(jnp.int32, gate.shape, 1)
        n_e = th // D_EXPERT
        col = lax.broadcasted_iota(jnp.int32, hid.shape, 1) // D_EXPERT
        gmat = jnp.zeros(hid.shape, F32)
        for e in range(n_e):
            ge = jnp.where(lane == (j - 1) * n_e + e, gate, 0.0).sum(axis=1, keepdims=True)
            gmat = jnp.where(col == e, ge, gmat)
        o_ref[...] += m[5:6] * _dot((hid * gmat).astype(BF16), w2_ref[...])


def _ffn_call(layer, x, ma, mb, mc, mods, g, wo, wr, br, w1, w3, w2, slot_fn, tm, th):
    n_tok = x.shape[0]
    n_h = w1.shape[1] // th
    hidden = lambda j: jnp.where(j == 0, n_h - 1, j - 1)
    tok = lambda w: pl.BlockSpec((tm, w), lambda i, j: (i, 0))
    layer_spec = lambda shape: pl.BlockSpec((1,) + shape, lambda i, j: (layer,) + (0,) * len(shape))
    return pl.pallas_call(
        _ffn_kernel,
        grid=(n_tok // tm, n_h + 1),
        in_specs=[tok(D_MODEL), tok(Q_W), tok(B_HEADS * B_DIM), tok(Q_W),
                  pl.BlockSpec((1, 1, 6, D_MODEL), lambda i, j: (layer, slot_fn(i), 0, 0)),
                  layer_spec((1, D_MODEL)), layer_spec((D_MODEL, D_MODEL)), layer_spec((D_MODEL, 2 * LANES)),
                  layer_spec((1, LANES)),
                  pl.BlockSpec((D_MODEL, th), lambda i, j: (0, hidden(j))),
                  pl.BlockSpec((D_MODEL, th), lambda i, j: (0, hidden(j))),
                  pl.BlockSpec((th, D_MODEL), lambda i, j: (hidden(j), 0))],
        out_specs=tok(D_MODEL),
        out_shape=jax.ShapeDtypeStruct((n_tok, D_MODEL), F32),
        scratch_shapes=[pltpu.VMEM((tm, D_MODEL), BF16), pltpu.VMEM((tm, LANES), F32)],
        compiler_params=_params(("arbitrary", "arbitrary")),
        name="ffn",
    )(x, ma, mb, mc, mods, g, wo, wr, br, w1, w3, w2)


def _final_norm_kernel(x_ref, g_ref, o_ref):
    x = x_ref[...]
    o_ref[...] = x * lax.rsqrt(jnp.mean(x * x, axis=-1, keepdims=True) + EPS) * g_ref[...]


def _final_norm_call(x, g, tm, row0, n_rows):
    blk0 = row0 // tm
    return pl.pallas_call(
        _final_norm_kernel,
        grid=(n_rows // tm,),
        in_specs=[pl.BlockSpec((tm, D_MODEL), lambda i: (blk0 + i, 0)), pl.BlockSpec((1, D_MODEL), lambda i: (0, 0))],
        out_specs=pl.BlockSpec((tm, D_MODEL), lambda i: (i, 0)),
        out_shape=jax.ShapeDtypeStruct((n_rows, D_MODEL), F32),
        compiler_params=_params(("arbitrary",)),
        name="final_norm",
    )(x, g)


def _rope_tables(t):
    pos = np.arange(t)
    n_freq = HEAD_DIM // 4
    inv_freq = ROPE_THETA ** (-jnp.arange(n_freq, dtype=F32) / n_freq)
    row = jnp.asarray(pos // GRID_W, F32)
    col = jnp.asarray(pos % GRID_W, F32)
    ang = jnp.concatenate([row[:, None] * inv_freq, col[:, None] * inv_freq], -1)
    cos, sin = jnp.cos(ang), jnp.sin(ang)
    cos_t = jnp.tile(jnp.concatenate([cos, cos], -1), (1, LANES // HEAD_DIM))
    sin_t = jnp.tile(jnp.concatenate([-sin, sin], -1), (1, LANES // HEAD_DIM))
    return cos_t, sin_t


def _delta_tables():
    r = np.arange(PAIR)
    same = (r[:, None] // CHUNK) == (r[None, :] // CHUNK)
    low = same & (r[:, None] >= r[None, :])
    low_s = same & (r[:, None] > r[None, :])
    up = same & (r[:, None] <= r[None, :])
    up_s = same & (r[:, None] < r[None, :])
    levels = []
    for k in range(N_LEVELS):
        s = 1 << k
        levels.append(((r[:, None] // (2 * s)) == (r[None, :] // (2 * s))) & ((r[:, None] // s) != (r[None, :] // s)))
    masks = jnp.asarray(np.stack([low, low_s, up, up_s, np.eye(PAIR, dtype=bool)] + levels).astype(np.float32))
    return masks


def _segment_mean_table():
    r = np.arange(LANES)
    seg = ((r[:, None] // HEAD_DIM) == (r[None, :] // HEAD_DIM)).astype(np.float32) / HEAD_DIM
    hi = jnp.asarray(seg, BF16)
    lo = (jnp.asarray(seg) - hi.astype(F32)).astype(BF16)
    return jnp.stack([hi, lo])


def kernel(x_prompt, x_sample, cache_a_k, cache_a_v, cache_c_k, cache_c_v, state_b_fwd, state_b_bwd, c, c_ctx, w_mod, b_mod, norm1_g, norm2_g, w_in, a_sink, b_conv, b_a_log, b_dt_bias, b_norm_g, c_q_norm, c_k_norm, w_out, w_group, b_group, w_expert, b_expert, w1, w3, w2, final_norm_g):
    n_p, t_p, d = x_prompt.shape
    n_s, t_s, _ = x_sample.shape
    depth = w_in.shape[0]
    past = cache_a_k.shape[2]
    tok_p = n_p * t_p
    n_tok = tok_p + n_s * t_s
    assert d == D_MODEL and tok_p % t_s == 0 and t_s % max(TM_PROJ, TM_FFN) == 0 and t_p % 256 == 0

    w_in_t = jnp.swapaxes(w_in, 1, 2)
    w_out16 = w_out.astype(BF16)
    pad_g = jnp.zeros((depth, d, EXPERT_ROW0 - N_GROUPS), F32)
    pad_e = jnp.zeros((depth, d, LANES - EXPERT_ROW0 - N_EXPERTS), F32)
    w_r = jnp.concatenate([w_group, pad_g, w_expert, pad_e], -1)
    w_r_hi = w_r.astype(BF16)
    w_r2 = jnp.concatenate([w_r_hi, (w_r - w_r_hi.astype(F32)).astype(BF16)], axis=-1)
    b_r = jnp.concatenate([b_group, pad_g[:, 0], b_expert, pad_e[:, 0]], -1)[:, None, :]
    cqn = jnp.tile(c_q_norm, (1, 4))[:, None, :]
    ckn = jnp.tile(c_k_norm, (1, 2))[:, None, :]
    gate_prm = jnp.stack([b_a_log.reshape(depth, 8), b_dt_bias.reshape(depth, 8)], 1)
    prmr = jnp.broadcast_to(jnp.pad(gate_prm, ((0, 0), (0, 0), (0, N_AB - 8)))[..., None],
                            (depth, 2, N_AB, LANES))
    cos_t, sin_t = _rope_tables(t_s)
    masks = _delta_tables()
    seg = _segment_mean_table()

    cond = jnp.concatenate([c_ctx[None, :], c], axis=0)
    cond_b = jnp.broadcast_to(cond[:, :, None], cond.shape + (LANES,))
    mods_all = _mods_call(cond_b, w_mod, b_mod).reshape(depth, SUBLANES, 6, d)

    def slot_fn(tm):
        per_s = t_s // tm
        first = tok_p // tm
        return lambda i: jnp.where(i < first, 0, 1 + (i - first) // per_s)

    xs = (x_prompt.reshape(tok_p, d), x_sample.reshape(n_s * t_s, d))
    blk_s = tok_p // t_s
    ctx = tuple(a.reshape(n_s, depth, past, LANES) for a in (cache_a_k, cache_a_v, cache_c_k, cache_c_v))
    s0 = tuple(a.reshape(n_s, depth, B_HEADS * B_DIM, B_DIM) for a in (state_b_fwd, state_b_bwd))
    g1, g2, bng = norm1_g[:, None, :], norm2_g[:, None, :], b_norm_g[:, None, :]
    caches = None
    states = None
    for l in range(depth):
        za, zb, zc, zab, zabt, *slab = _inproj_call(l, xs, mods_all, g1, w_in_t, slot_fn(TM_PROJ), TM_PROJ)
        x = slab[0] if slab else xs[0]

        ao, co, *rest = _attn_call(False, t_p, ATTN_NSEQ, n_p, 0, l, za, zc, a_sink, cqn, ckn, seg, prev=caches,
                                   experts=(w1, w3, w2))
        caches, (w1b, w3b, w2b) = rest[:4], rest[4:]
        ao, co = _attn_call(True, t_s, 1, n_s, blk_s, l, za, zc, a_sink, cqn, ckn, seg, prev=(ao, co),
                            rope=(cos_t, sin_t), ctx=ctx)

        bo, *states = _delta_call(False, t_p, DELTA_NSEQ, n_p, 0, l, zb, zab, zabt, b_conv, prmr, bng, masks,
                                  prev=states)
        (bo,) = _delta_call(True, t_s, 1, n_s, blk_s, l, zb, zab, zabt, b_conv, prmr, bng, masks,
                            prev=(bo,), s0=s0)

        x = _ffn_call(l, x, ao, bo, co, mods_all, g2, w_out16, w_r2, b_r, w1b, w3b, w2b, slot_fn(TM_FFN), TM_FFN, TH_FFN)
        xs = (x,)

    y_prompt = _final_norm_call(x, final_norm_g[None], TM_NORM, 0, tok_p).reshape(n_p, t_p, d)
    y_sample = _final_norm_call(x, final_norm_g[None], TM_NORM, tok_p, n_s * t_s).reshape(n_s, t_s, d)
    new_ak, new_av, new_ck, new_cv = (a.reshape(n_p, depth, t_p, 2, HEAD_DIM) for a in caches)
    new_sf, new_sb = (a.reshape(n_p, depth, B_HEADS, B_DIM, B_DIM) for a in states)
    return (y_prompt, y_sample, new_ak, new_av, new_ck, new_cv, new_sf, new_sb)
```

```python
import functools

import jax
import jax.numpy as jnp
import numpy as np
from jax import lax
from jax.experimental import pallas as pl
from jax.experimental.pallas import tpu as pltpu

F32 = jnp.float32
BF16 = jnp.bfloat16

D_MODEL = 1024
GRID_W = 64
EPS = 1e-6
NEG_INF = -1e30
ROPE_THETA = 10000.0
HEAD_DIM = 64
Q_W = 256
KV_W = 128
WINDOW = 128
Q_BLOCK = 128
B_HEADS = 4
B_DIM = 128
CHUNK = 64
BD = B_HEADS * CHUNK
PAIR = 2 * CHUNK
N_LEVELS = 6
PREP_UNROLL = 4
DELTA_NSEQ = 4
ATTN_NSEQ = 4
N_GROUPS = 4
EXPERTS_PER_GROUP = 4
N_EXPERTS = 16
D_EXPERT = 256
EXPERT_ROW0 = 8

LANES = 128
SUBLANES = 8
VMEM_LIMIT = 60000 * 1024

TM_PROJ = 512
TM_FFN, TH_FFN = 1024, 1024
TM_NORM = 512
CAST_STEPS = 16
MODS_TN = 1536

ZA_W, ZB_W, ZC_W, ZAB_W = 512, 2048, 512, 128
N_AB = 16
Z_W = ZA_W + ZB_W + ZC_W + ZAB_W


def _sigmoid(x):
    return 1.0 / (1.0 + jnp.exp(-x))


def _silu(x):
    return x * _sigmoid(x)


def _softplus(x):
    return jnp.maximum(x, 0.0) + jnp.log1p(jnp.exp(-jnp.abs(x)))


def _dot(a, b):
    return jnp.dot(a, b, preferred_element_type=F32)


def _dot_nt(a, b):
    return lax.dot_general(a, b, (((1,), (1,)), ((), ())), preferred_element_type=F32)


def _dot_tn(a, b):
    return lax.dot_general(a, b, (((0,), (0,)), ((), ())), preferred_element_type=F32)


def _split2(x):
    hi = x.astype(BF16)
    lo = (x - hi.astype(F32)).astype(BF16)
    return hi, lo


def _params(sem=None):
    return pltpu.CompilerParams(dimension_semantics=sem, vmem_limit_bytes=VMEM_LIMIT)


def _mods_kernel(cond_ref, w_ref, b_ref, o_ref, act_s):
    n_cond = cond_ref.shape[0]
    tn = w_ref.shape[2]
    reps = tn // LANES

    @pl.when((pl.program_id(0) == 0) & (pl.program_id(1) == 0))
    def _():
        act_s[...] = _silu(cond_ref[...])

    def body(kb, accs):
        r = pl.multiple_of(kb * SUBLANES, SUBLANES)
        w = w_ref[0, pl.ds(r, SUBLANES), :]
        return tuple(acc + jnp.tile(act_s[m, pl.ds(r, SUBLANES), :], (1, reps)) * w for m, acc in enumerate(accs))

    zero = jnp.zeros((SUBLANES, tn), F32)
    accs = lax.fori_loop(0, w_ref.shape[1] // SUBLANES, body, (zero,) * n_cond, unroll=4)
    rows = [jnp.sum(a, axis=0, keepdims=True) + b_ref[0] for a in accs]
    rows.append(jnp.zeros((SUBLANES - n_cond, tn), F32))
    o_ref[0] = jnp.concatenate(rows, axis=0)


def _mods_call(cond_b, w_mod, b_mod):
    depth, d, n = w_mod.shape
    tn = MODS_TN
    n_cond = cond_b.shape[0]
    return pl.pallas_call(
        _mods_kernel,
        grid=(depth, n // tn),
        in_specs=[
            pl.BlockSpec((n_cond, d, LANES), lambda l, j: (0, 0, 0)),
            pl.BlockSpec((1, d, tn), lambda l, j: (l, 0, j)),
            pl.BlockSpec((1, 1, tn), lambda l, j: (l, 0, j)),
        ],
        out_specs=pl.BlockSpec((1, SUBLANES, tn), lambda l, j: (l, 0, j)),
        out_shape=jax.ShapeDtypeStruct((depth, SUBLANES, n), F32),
        scratch_shapes=[pltpu.VMEM((n_cond, d, LANES), F32)],
        compiler_params=_params(("arbitrary", "arbitrary")),
        name="mods",
    )(cond_b, w_mod, b_mod.reshape(depth, 1, n))


def _x_specs(xs, tm):
    if len(xs) == 1:
        return [pl.BlockSpec((tm, D_MODEL), lambda i, *_: (i, 0))]
    first = xs[0].shape[0] // tm
    return [pl.BlockSpec((tm, D_MODEL), lambda i, *_: (jnp.minimum(i, first - 1), 0)),
            pl.BlockSpec((tm, D_MODEL), lambda i, *_: (jnp.maximum(i - first, 0), 0))]


def _x_tile(x_refs, first):
    if len(x_refs) == 1:
        return x_refs[0][...]
    return jnp.where(pl.program_id(0) < first, x_refs[0][...], x_refs[1][...])


def _modulated_norm(x, g, shift, scale):
    ms = jnp.mean(x * x, axis=-1, keepdims=True)
    y = x * lax.rsqrt(ms + EPS) * g
    return y * (1.0 + scale) + shift


def _inproj_kernel(n_x, first, n_w, *refs):
    x_refs = refs[:n_x]
    mod_ref, g_ref, wt_ref = refs[n_x:n_x + 3]
    w_refs = refs[n_x + 3:n_x + 3 + n_w]
    o0 = n_x + 3 + n_w
    za_ref, zb_ref, zc_ref, zab_ref, zabt_ref = refs[o0:o0 + 5]
    o1 = o0 + 5 + (1 if n_x > 1 else 0)
    wb_refs = refs[o1:o1 + n_w]
    w_s = refs[-1]
    for w_ref, wb_ref in zip(w_refs, wb_refs):
        wb_ref[...] = w_ref[0].astype(BF16)
    @pl.when(pl.program_id(0) == 0)
    def _():
        ab0 = ZA_W + ZB_W
        w_s[0:ab0, :] = wt_ref[0, 0:ab0, :].astype(BF16)
        w_s[ab0:ab0 + ZC_W, :] = wt_ref[0, ab0 + N_AB:ab0 + N_AB + ZC_W, :].astype(BF16)
        w_s[ab0 + ZC_W:ab0 + ZC_W + N_AB, :] = wt_ref[0, ab0:ab0 + N_AB, :].astype(BF16)
        w_s[ab0 + ZC_W + N_AB:Z_W, :] = jnp.zeros((ZAB_W - N_AB, D_MODEL), BF16)

    m = mod_ref[0, 0]
    x = _x_tile(x_refs, first)
    if n_x > 1:
        refs[o0 + 5][...] = x
    h = _modulated_norm(x, g_ref[0], m[0:1], m[1:2]).astype(BF16)
    za_ref[...] = _dot_nt(h, w_s[0:ZA_W, :])
    step = 512
    for j in range(ZB_W // step):
        zb_ref[:, j * step:(j + 1) * step] = _dot_nt(h, w_s[ZA_W + j * step:ZA_W + (j + 1) * step, :])
    zc_ref[...] = _dot_nt(h, w_s[ZA_W + ZB_W:ZA_W + ZB_W + ZC_W, :])
    zab = _dot_nt(h, w_s[ZA_W + ZB_W + ZC_W:Z_W, :])
    zab_ref[...] = zab
    zabt_ref[...] = zab.T[:N_AB]


def _inproj_call(layer, xs, mods, g, w, slot_fn, tm, experts=()):
    n_tok = sum(a.shape[0] for a in xs)
    n_ab = N_AB
    cast_steps = CAST_STEPS
    assert n_tok // tm >= cast_steps
    slab = lambda i: jnp.minimum(i, cast_steps - 1)
    cast_in = [pl.BlockSpec((1, e.shape[1] // cast_steps, e.shape[2]), lambda i: (layer, slab(i), 0)) for e in experts]
    cast_out = [pl.BlockSpec((e.shape[1] // cast_steps, e.shape[2]), lambda i: (slab(i), 0)) for e in experts]
    return pl.pallas_call(
        functools.partial(_inproj_kernel, len(xs), xs[0].shape[0] // tm, len(experts)),
        grid=(n_tok // tm,),
        in_specs=_x_specs(xs, tm) + [
            pl.BlockSpec((1, 1, 6, D_MODEL), lambda i: (layer, slot_fn(i), 0, 0)),
            pl.BlockSpec((1, 1, D_MODEL), lambda i: (layer, 0, 0)),
            pl.BlockSpec((1, w.shape[1], D_MODEL), lambda i: (layer, 0, 0)),
        ] + cast_in,
        out_specs=[
            pl.BlockSpec((tm, ZA_W), lambda i: (i, 0)),
            pl.BlockSpec((tm, ZB_W), lambda i: (i, 0)),
            pl.BlockSpec((tm, ZC_W), lambda i: (i, 0)),
            pl.BlockSpec((tm, ZAB_W), lambda i: (i, 0)),
            pl.BlockSpec((n_ab, tm), lambda i: (0, i)),
        ] + ([pl.BlockSpec((tm, D_MODEL), lambda i: (i, 0))] if len(xs) > 1 else []) + cast_out,
        out_shape=[
            jax.ShapeDtypeStruct((n_tok, ZA_W), F32),
            jax.ShapeDtypeStruct((n_tok, ZB_W), F32),
            jax.ShapeDtypeStruct((n_tok, ZC_W), F32),
            jax.ShapeDtypeStruct((n_tok, ZAB_W), F32),
            jax.ShapeDtypeStruct((n_ab, n_tok), F32),
        ] + ([jax.ShapeDtypeStruct((n_tok, D_MODEL), F32)] if len(xs) > 1 else [])
        + [jax.ShapeDtypeStruct(e.shape[1:], BF16) for e in experts],
        scratch_shapes=[pltpu.VMEM((Z_W, D_MODEL), BF16)],
        compiler_params=_params(("arbitrary",)),
        name="inproj",
    )(*xs, mods, g, w, *experts)


def _lane_lo(shape):
    return lax.broadcasted_iota(jnp.int32, shape, len(shape) - 1) % LANES < HEAD_DIM


def _store_kdup(dst_ref, off, k):
    n = k.shape[0]
    r = pltpu.roll(k, HEAD_DIM, 1)
    lo = _lane_lo(k.shape)
    dst_ref[0, off:off + n, :] = jnp.where(lo, k, r).astype(BF16)
    dst_ref[1, off:off + n, :] = jnp.where(lo, r, k).astype(BF16)


def _store_vsplit(dst_ref, off, v):
    n = v.shape[0]
    r = pltpu.roll(v, HEAD_DIM, 1)
    lo = _lane_lo(v.shape)
    z = jnp.zeros_like(v)
    dst_ref[0, off:off + n, :] = jnp.where(lo, v, z).astype(BF16)
    dst_ref[1, off:off + n, :] = jnp.where(lo, z, r).astype(BF16)
    dst_ref[2, off:off + n, :] = jnp.where(lo, r, z).astype(BF16)
    dst_ref[3, off:off + n, :] = jnp.where(lo, z, v).astype(BF16)


def _rope(x, cos, sin):
    first = (lax.broadcasted_iota(jnp.int32, x.shape, 1) // (HEAD_DIM // 2)) % 2 == 0
    partner = jnp.where(first, pltpu.roll(x, LANES - HEAD_DIM // 2, 1), pltpu.roll(x, HEAD_DIM // 2, 1))
    return x * cos + partner * sin


def _head_rmsnorm(x, g, seg_hi, seg_lo):
    hi, lo = _split2(x * x)
    ms = _dot(hi, seg_hi) + _dot(lo, seg_hi) + _dot(hi, seg_lo)
    return x * lax.rsqrt(ms + EPS) * g


def _attend_many(units):
    qb = units[0][0].shape[0]
    lo = _lane_lo(units[0][0].shape)
    all_scores = []
    for qt, segs, _ in units:
        z = jnp.zeros_like(qt)
        qs = jnp.concatenate([jnp.where(lo, qt, z), jnp.where(lo, z, qt)], axis=0).astype(BF16)
        scores = []
        for kdup, _, _, mask in segs:
            s = _dot_nt(qs, kdup)
            if mask is not None:
                s = jnp.where(mask, s, NEG_INF)
            scores.append(s)
        all_scores.append(scores)
    probs = []
    for (qt, segs, sink_pair), scores in zip(units, all_scores):
        m = scores[0].max(axis=1, keepdims=True)
        for s in scores[1:]:
            m = jnp.maximum(m, s.max(axis=1, keepdims=True))
        if sink_pair is not None:
            row_a = lax.broadcasted_iota(jnp.int32, (2 * qb, 1), 0) < qb
            sink = jnp.where(row_a, sink_pair[0], sink_pair[1])
            m = jnp.maximum(m, sink)
            denom = jnp.exp(sink - m)
        else:
            denom = jnp.zeros((2 * qb, 1), F32)
        ps = []
        for s in scores:
            p = jnp.exp(s - m)
            denom = denom + p.sum(axis=1, keepdims=True)
            ps.append(p.astype(BF16))
        probs.append((ps, 1.0 / denom))
    outs = []
    for (qt, segs, _), (ps, inv) in zip(units, probs):
        acc = jnp.zeros((qb, LANES), F32)
        for pb, (_, vlo, vhi, _) in zip(ps, segs):
            acc = acc + _dot(pb[:qb], vlo) + _dot(pb[qb:], vhi)
        outs.append(acc * jnp.where(lo, inv[:qb], inv[qb:]))
    return outs


def _attn_kernel(has_ctx, t, nseq, layer, *refs):
    if has_ctx:
        (sink_ref, za_ref, zc_ref, cqn_ref, ckn_ref, seg_ref, cos_ref, sin_ref,
         cak_ref, cav_ref, cck_ref, ccv_ref,
         ao_ref, co_ref,
         ka_s, va_s, kc_s, vc_s, kctx_s, vctx_s, qa_s, qc_s) = refs
    else:
        (sink_ref, za_ref, zc_ref, cqn_ref, ckn_ref, seg_ref, w1_ref,
         ao_ref, co_ref, nak_ref, nav_ref, nck_ref, ncv_ref, w1b_ref,
         ka_s, va_s, kc_s, vc_s, qa_s, qc_s) = refs
        w1b_ref[...] = w1_ref[0].astype(BF16)
    scale = HEAD_DIM ** -0.5
    seg_hi = seg_ref[0]
    seg_lo = seg_ref[1]
    piece = 256
    n_ctx = cak_ref.shape[2] if has_ctx else 0

    for p0 in range(0, nseq * t, piece):
        rows = slice(p0, p0 + piece)
        ak = za_ref[rows, Q_W:Q_W + KV_W]
        av = za_ref[rows, Q_W + KV_W:Q_W + 2 * KV_W]
        ck = _head_rmsnorm(zc_ref[rows, Q_W:Q_W + KV_W], ckn_ref[0], seg_hi, seg_lo)
        cv = zc_ref[rows, Q_W + KV_W:Q_W + 2 * KV_W]
        if has_ctx:
            cos = cos_ref[rows, :]
            sin = sin_ref[rows, :]
            ak = _rope(ak, cos, sin)
            ck = _rope(ck, cos, sin)
            _store_kdup(ka_s, WINDOW + p0, ak)
            _store_vsplit(va_s, WINDOW + p0, av)
            _store_kdup(kc_s, n_ctx + p0, ck)
            _store_vsplit(vc_s, n_ctx + p0, cv)
        else:
            crow = slice(p0 % t, p0 % t + piece)
            nak_ref[p0 // t, 0, crow, :] = ak
            nav_ref[p0 // t, 0, crow, :] = av
            nck_ref[p0 // t, 0, crow, :] = ck
            ncv_ref[p0 // t, 0, crow, :] = cv
            _store_kdup(ka_s, p0, ak)
            _store_vsplit(va_s, p0, av)
            _store_kdup(kc_s, p0, ck)
            _store_vsplit(vc_s, p0, cv)
        for hk in range(2):
            cols = slice(hk * LANES, (hk + 1) * LANES)
            aq = za_ref[rows, cols]
            cq = _head_rmsnorm(zc_ref[rows, cols], cqn_ref[0, :, cols], seg_hi, seg_lo)
            if has_ctx:
                aq = _rope(aq, cos, sin)
                cq = _rope(cq, cos, sin)
            qa_s[rows, cols] = aq * scale
            qc_s[rows, cols] = cq * scale

    if has_ctx:
        zpad = jnp.zeros((WINDOW, LANES), BF16)
        for i in range(2):
            ka_s[i, 0:WINDOW, :] = zpad
            ka_s[i, WINDOW + t:2 * WINDOW + t, :] = zpad
        for i in range(4):
            va_s[i, 0:WINDOW, :] = zpad
            va_s[i, WINDOW + t:2 * WINDOW + t, :] = zpad
        for p0 in range(0, n_ctx, piece):
            rows = slice(p0, p0 + piece)
            _store_kdup(kctx_s, p0, cak_ref[0, 0, rows, :])
            _store_vsplit(vctx_s, p0, cav_ref[0, 0, rows, :])
            _store_kdup(kc_s, p0, cck_ref[0, 0, rows, :])
            _store_vsplit(vc_s, p0, ccv_ref[0, 0, rows, :])

        qb = Q_BLOCK
        span = qb + 2 * WINDOW
        qi = lax.broadcasted_iota(jnp.int32, (2 * qb, span), 0) % qb
        kj = lax.broadcasted_iota(jnp.int32, (2 * qb, span), 1)
        band = jnp.abs(kj - WINDOW - qi) <= WINDOW

        def block(b, carry):
            r0 = pl.multiple_of(b * qb, qb)
            kpos = kj + (r0 - WINDOW)
            mask = band & (kpos >= 0) & (kpos < t)
            units = []
            for hk in range(2):
                cols = slice(hk * LANES, (hk + 1) * LANES)
                segs_a = [
                    (kctx_s[hk], vctx_s[2 * hk], vctx_s[2 * hk + 1], None),
                    (ka_s[hk, pl.ds(r0, span), :], va_s[2 * hk, pl.ds(r0, span), :],
                     va_s[2 * hk + 1, pl.ds(r0, span), :], mask),
                ]
                sinks = (sink_ref[layer, 2 * hk], sink_ref[layer, 2 * hk + 1])
                units.append((qa_s[pl.ds(r0, qb), cols], segs_a, sinks))
                segs_c = [(kc_s[hk], vc_s[2 * hk], vc_s[2 * hk + 1], None)]
                units.append((qc_s[pl.ds(r0, qb), cols], segs_c, None))
            outs = _attend_many(units)
            for hk in range(2):
                cols = slice(hk * LANES, (hk + 1) * LANES)
                ao_ref[pl.ds(r0, qb), cols] = outs[2 * hk].astype(BF16)
                co_ref[pl.ds(r0, qb), cols] = outs[2 * hk + 1].astype(BF16)
            return carry

        lax.fori_loop(0, t // qb, block, 0)
    else:
        units = []
        for q in range(nseq):
            seq = slice(q * t, (q + 1) * t)
            for hk in range(2):
                cols = slice(hk * LANES, (hk + 1) * LANES)
                sinks = (sink_ref[layer, 2 * hk], sink_ref[layer, 2 * hk + 1])
                units.append((qa_s[seq, cols],
                              [(ka_s[hk, seq, :], va_s[2 * hk, seq, :], va_s[2 * hk + 1, seq, :], None)], sinks))
                units.append((qc_s[seq, cols],
                              [(kc_s[hk, seq, :], vc_s[2 * hk, seq, :], vc_s[2 * hk + 1, seq, :], None)], None))
        outs = _attend_many(units)
        for q in range(nseq):
            seq = slice(q * t, (q + 1) * t)
            for hk in range(2):
                cols = slice(hk * LANES, (hk + 1) * LANES)
                ao_ref[seq, cols] = outs[4 * q + 2 * hk].astype(BF16)
                co_ref[seq, cols] = outs[4 * q + 2 * hk + 1].astype(BF16)


def _attn_call(has_ctx, t, nseq, n_batch, row_block0, layer, za, zc, sink, cqn, ckn, seg, prev=None, rope=None,
               ctx=None, experts=None):
    n_tok = za.shape[0]
    depth = sink.shape[0]
    assert n_batch % nseq == 0 and (nseq == 1 or not has_ctx)
    tok_spec = lambda w: pl.BlockSpec((nseq * t, w), lambda b, *_: (row_block0 + b, 0))
    const = lambda shape: pl.BlockSpec(shape, lambda b, *_: (0,) * len(shape))
    layer_spec = lambda shape: pl.BlockSpec((1,) + shape, lambda b, *_: (layer,) + (0,) * len(shape))
    in_specs = [tok_spec(ZA_W), tok_spec(ZC_W), layer_spec((1, Q_W)), layer_spec((1, KV_W)), const((2, LANES, LANES))]
    args = [za, zc, cqn, ckn, seg]
    out_specs = [tok_spec(Q_W), tok_spec(Q_W)]
    out_shape = [jax.ShapeDtypeStruct((n_tok, Q_W), BF16), jax.ShapeDtypeStruct((n_tok, Q_W), BF16)]
    if has_ctx:
        n_ctx = ctx[0].shape[2]
        in_specs += [const((t, LANES)), const((t, LANES))]
        args += list(rope)
        in_specs += [pl.BlockSpec((1, 1, n_ctx, LANES), lambda b, *_: (b, layer, 0, 0))] * 4
        args += list(ctx)
        scratch = [
            pltpu.VMEM((2, t + 2 * WINDOW, LANES), BF16), pltpu.VMEM((4, t + 2 * WINDOW, LANES), BF16),
            pltpu.VMEM((2, n_ctx + t, LANES), BF16), pltpu.VMEM((4, n_ctx + t, LANES), BF16),
            pltpu.VMEM((2, n_ctx, LANES), BF16), pltpu.VMEM((4, n_ctx, LANES), BF16),
            pltpu.VMEM((t, Q_W), F32), pltpu.VMEM((t, Q_W), F32),
        ]
    else:
        cache_spec = pl.BlockSpec((nseq, 1, t, LANES), lambda b, *_: (b, layer, 0, 0))
        out_specs += [cache_spec] * 4
        out_shape += [jax.ShapeDtypeStruct((n_batch, depth, t, LANES), F32)] * 4
        n_steps = n_batch // nseq
        for w in experts:
            rows_w = w.shape[1] // n_steps
            in_specs.append(pl.BlockSpec((1, rows_w, w.shape[2]), lambda b, *_: (layer, b, 0)))
            args.append(w)
            out_specs.append(pl.BlockSpec((rows_w, w.shape[2]), lambda b, *_: (b, 0)))
            out_shape.append(jax.ShapeDtypeStruct(w.shape[1:], BF16))
        rows = nseq * t
        scratch = [
            pltpu.VMEM((2, rows, LANES), BF16), pltpu.VMEM((4, rows, LANES), BF16),
            pltpu.VMEM((2, rows, LANES), BF16), pltpu.VMEM((4, rows, LANES), BF16),
            pltpu.VMEM((rows, Q_W), F32), pltpu.VMEM((rows, Q_W), F32),
        ]
    n_real = len(args)
    aliases = {}
    if prev is not None:
        first_out = 0 if has_ctx else 2
        for k, arr in enumerate(prev):
            in_specs.append(pl.BlockSpec(memory_space=pl.ANY))
            args.append(arr)
            aliases[1 + n_real + k] = first_out + k

    def body(*refs):
        ins = refs[:1 + n_real]
        rest = refs[1 + len(args):]
        _attn_kernel(has_ctx, t, nseq, layer, *ins, *rest)

    return pl.pallas_call(
        body,
        grid_spec=pltpu.PrefetchScalarGridSpec(
            num_scalar_prefetch=1, grid=(n_batch // nseq,), in_specs=in_specs, out_specs=out_specs,
            scratch_shapes=scratch),
        out_shape=out_shape,
        input_output_aliases=aliases,
        compiler_params=_params(("arbitrary",)),
        name="attn_latent" if has_ctx else "attn_prompt",
    )(sink, *args)


def _stack_pair(x, p):
    return jnp.concatenate([x[:, (2 * p + hl) * B_DIM:(2 * p + hl + 1) * B_DIM] for hl in range(2)], axis=0)


def _delta_kernel(t, nseq, has_s0, *refs):
    if has_s0:
        (zb_ref, abc_ref, abt_ref, conv_ref, prmr_ref, bng_ref, mask_ref,
         s0f_ref, s0b_ref, o_ref, qkv_s, of_s, ob_s, sf_s, sb_s, u_s, wq_s, at_s, kd_s, eg_s,
         pre_s, suf_s, prec_s, sufc_s) = refs
    else:
        (zb_ref, abc_ref, abt_ref, conv_ref, prmr_ref, bng_ref, mask_ref,
         o_ref, sfo_ref, sbo_ref, qkv_s, of_s, ob_s, sf_s, sb_s, u_s, wq_s, at_s, kd_s, eg_s,
         pre_s, suf_s, prec_s, sufc_s) = refs
    n_chunks = t // CHUNK
    n_total = nseq * n_chunks
    s_rows = B_HEADS * B_DIM
    qk_w = B_HEADS * B_DIM

    row = lax.broadcasted_iota(jnp.int32, (t, LANES), 0)
    for q in range(nseq):
        seq = slice(q * t, (q + 1) * t)
        for j in range(3 * B_HEADS):
            cols = slice(j * LANES, (j + 1) * LANES)
            x = zb_ref[seq, cols]
            prev = jnp.where(row == 0, 0.0, pltpu.roll(x, 1, 0))
            nxt = jnp.where(row == t - 1, 0.0, pltpu.roll(x, t - 1, 0))
            y = _silu(prev * conv_ref[0, 0:1, cols] + x * conv_ref[0, 1:2, cols] + nxt * conv_ref[0, 2:3, cols])
            if j < 2 * B_HEADS:
                y = y * lax.rsqrt(jnp.sum(y * y, axis=-1, keepdims=True) + EPS)
            if j < B_HEADS:
                y = y * (B_DIM ** -0.5)
            qkv_s[seq, cols] = y

    if has_s0:
        for q in range(nseq):
            sf_s[q * s_rows:(q + 1) * s_rows, :] = s0f_ref[q, 0]
            sb_s[q * s_rows:(q + 1) * s_rows, :] = s0b_ref[q, 0]
    else:
        sf_s[...] = jnp.zeros_like(sf_s)
        sb_s[...] = jnp.zeros_like(sb_s)

    reps = nseq * t // LANES
    gr = -jnp.tile(jnp.exp(prmr_ref[0, 0]), (1, reps)) * _softplus(abt_ref[...] + jnp.tile(prmr_ref[0, 1], (1, reps)))
    seg_lane = lax.broadcasted_iota(jnp.int32, gr.shape, 1) % CHUNK
    pre, suf = gr, gr
    for s in (1, 2, 4, 8, 16, 32):
        pre = pre + jnp.where(seg_lane >= s, pltpu.roll(pre, s, 1), 0.0)
        suf = suf + jnp.where(seg_lane < CHUNK - s, pltpu.roll(suf, nseq * t - s, 1), 0.0)
    pre_s[...] = pre
    suf_s[...] = suf
    zrows = jnp.zeros((LANES - pre.shape[0], LANES), F32)
    for j in range(reps):
        tile = slice(j * LANES, (j + 1) * LANES)
        prec_s[tile, :] = jnp.concatenate([pre[:, tile], zrows], axis=0).T
        sufc_s[tile, :] = jnp.concatenate([suf[:, tile], zrows], axis=0).T
    lane_lo = lax.broadcasted_iota(jnp.int32, (1, LANES), 1) < CHUNK

    def prepare(cc, carry):
        chains = []
        for k in range(PREP_UNROLL):
            c = cc * PREP_UNROLL + k
            r0 = pl.multiple_of(c * CHUNK, CHUNK)
            b_all = _sigmoid(abc_ref[pl.ds(r0, CHUNK), :])
            run_c = (prec_s[pl.ds(r0, CHUNK), :], sufc_s[pl.ds(r0, CHUNK), :])
            tile0 = pl.multiple_of((cc * PREP_UNROLL + k - k % 2) * CHUNK, LANES)
            run = (pre_s[:, pl.ds(tile0, LANES)], suf_s[:, pl.ds(tile0, LANES)])
            run_r = tuple(pltpu.roll(x, CHUNK, 1) for x in run)
            for p in range(B_HEADS // 2):
                kst = _stack_pair(qkv_s[pl.ds(r0, CHUNK), qk_w:2 * qk_w], p)
                qst = _stack_pair(qkv_s[pl.ds(r0, CHUNK), 0:qk_w], p)
                vst = _stack_pair(qkv_s[pl.ds(r0, CHUNK), 2 * qk_w:3 * qk_w], p)
                kq = _dot_nt(jnp.concatenate([kst, qst], axis=0).astype(BF16), kst.astype(BF16))
                for d in range(2):
                    cg = 4 * d + 2 * p
                    edge = CHUNK - 1 if d == 0 else 0
                    rep_col = lambda x, col: jnp.broadcast_to(x[:, col:col + 1], (CHUNK, LANES))
                    b_rep = jnp.concatenate([rep_col(b_all, 8 + cg + hl) for hl in range(2)], axis=0)
                    gcol = jnp.concatenate([rep_col(run_c[d], cg + hl) for hl in range(2)], axis=0)
                    gtot = jnp.concatenate([rep_col(run_c[d][edge:edge + 1], cg + hl) for hl in range(2)], axis=0)
                    ra = cg
                    if k % 2 == 0:
                        grow = jnp.where(lane_lo, run[d][ra:ra + 1], run_r[d][ra + 1:ra + 2])
                    else:
                        grow = jnp.where(lane_lo, run_r[d][ra:ra + 1], run[d][ra + 1:ra + 2])
                    chains.append(dict(c=c, p=p, d=d, kst=kst, qst=qst, vst=vst, kq=kq, b_st=b_rep,
                                       gcol=gcol, gtot=gtot, grow=grow))

        for ch in chains:
            d, b_st, kq, gcol = ch["d"], ch["b_st"], ch.pop("kq"), ch["gcol"]
            decay = jnp.exp(jnp.minimum(gcol - ch.pop("grow"), 0.0))
            ch["a_mat"] = (b_st * kq[:PAIR]) * (decay * mask_ref[2 * d + 1])
            ch["attn"] = (kq[PAIR:] * (decay * mask_ref[2 * d])).astype(BF16)
            ch["t_inv"] = mask_ref[4] - ch["a_mat"] * mask_ref[5]
        for lvl in range(N_LEVELS - 1):
            for ch in chains:
                ch["t16"] = ch["t_inv"].astype(BF16)
                ch["et"] = _dot((ch["a_mat"] * mask_ref[6 + lvl]).astype(BF16), ch["t16"])
            for ch in chains:
                ch["t_inv"] = ch["t_inv"] - _dot(ch.pop("t16"), ch.pop("et").astype(BF16))
        for ch in chains:
            egc = jnp.exp(ch["gcol"])
            rk = jnp.concatenate([ch["b_st"] * ch["vst"], (ch["b_st"] * egc) * ch["kst"]], axis=1)
            ch["rk"] = _dot(ch.pop("t_inv").astype(BF16), rk.astype(BF16))
            ch["qp16"] = (ch["qst"] * egc).astype(BF16)
        for ch in chains:
            c, p, d, rk, qp16 = ch["c"], ch["p"], ch["d"], ch["rk"], ch["qp16"]
            pair_rows = slice(p * PAIR, (p + 1) * PAIR)
            w16 = rk[:, B_DIM:].astype(BF16)
            u_s[d, c, pair_rows, :] = rk[:, :B_DIM]
            at_s[d, c, p] = ch["attn"]
            kd_s[d, c, pair_rows, :] = (ch["kst"] * jnp.exp(ch["gtot"] - ch["gcol"])).astype(BF16)
            eg = jnp.exp(ch["gtot"])
            for hl in range(2):
                h = 2 * p + hl
                rows = slice(hl * CHUNK, (hl + 1) * CHUNK)
                wq_s[d, c, h * 2 * CHUNK:h * 2 * CHUNK + CHUNK, :] = w16[rows]
                wq_s[d, c, h * 2 * CHUNK + CHUNK:(h + 1) * 2 * CHUNK, :] = qp16[rows]
                eg_s[d, c, h * SUBLANES:(h + 1) * SUBLANES, :] = eg[hl * CHUNK:hl * CHUNK + SUBLANES, :]
        return carry

    lax.fori_loop(0, n_total // PREP_UNROLL, prepare, 0)

    def scan_step(i, carry):
        units = []
        for q in range(nseq):
            for d, s_ref, o_s in ((0, sf_s, of_s), (1, sb_s, ob_s)):
                c = q * n_chunks + (i if d == 0 else n_chunks - 1 - i)
                units.append(dict(q=q, d=d, c=c, s_ref=s_ref, o_s=o_s, r0=pl.multiple_of(c * CHUNK, CHUNK)))
        for un in units:
            q, d, c, s_ref = un["q"], un["d"], un["c"], un["s_ref"]
            un["x"] = []
            for h in range(B_HEADS):
                srows = slice(q * s_rows + h * B_DIM, q * s_rows + (h + 1) * B_DIM)
                un["x"].append(_dot(wq_s[d, c, h * 2 * CHUNK:(h + 1) * 2 * CHUNK, :], s_ref[srows, :].astype(BF16)))
        for un in units:
            d, c = un["d"], un["c"]
            un["vp16"], un["o"] = [], []
            for p in range(B_HEADS // 2):
                xs = un["x"][2 * p:2 * p + 2]
                v_new = jnp.concatenate(
                    [u_s[d, c, (2 * p + hl) * CHUNK:(2 * p + hl + 1) * CHUNK, :] - xs[hl][:CHUNK] for hl in range(2)],
                    axis=0)
                vp16 = v_new.astype(BF16)
                un["vp16"].append(vp16)
                un["o"].append(jnp.concatenate([xs[hl][CHUNK:] for hl in range(2)], axis=0)
                               + _dot(at_s[d, c, p], vp16))
        for un in units:
            q, d, c, s_ref, o_s, r0 = un["q"], un["d"], un["c"], un["s_ref"], un["o_s"], un["r0"]
            for h in range(B_HEADS):
                p, hl = divmod(h, 2)
                rows = slice(hl * CHUNK, (hl + 1) * CHUNK)
                srows = slice(q * s_rows + h * B_DIM, q * s_rows + (h + 1) * B_DIM)
                upd = _dot_tn(kd_s[d, c, h * CHUNK:(h + 1) * CHUNK, :], un["vp16"][p][rows])
                eg = jnp.tile(eg_s[d, c, h * SUBLANES:(h + 1) * SUBLANES, :], (B_DIM // SUBLANES, 1))
                s_ref[srows, :] = s_ref[srows, :] * eg + upd
                o_s[pl.ds(r0, CHUNK), h * B_DIM:(h + 1) * B_DIM] = un["o"][p][rows]
        return carry

    lax.fori_loop(0, n_chunks, scan_step, 0)

    if not has_s0:
        for q in range(nseq):
            sfo_ref[q, 0] = sf_s[q * s_rows:(q + 1) * s_rows, :]
            sbo_ref[q, 0] = sb_s[q * s_rows:(q + 1) * s_rows, :]

    for h in range(B_HEADS):
        cols = slice(h * B_DIM, (h + 1) * B_DIM)
        x = of_s[:, cols] + ob_s[:, cols]
        yn = x * lax.rsqrt(jnp.mean(x * x, axis=-1, keepdims=True) + EPS) * bng_ref[0]
        o_ref[:, cols] = (yn * _silu(zb_ref[:, 3 * qk_w + h * B_DIM:3 * qk_w + (h + 1) * B_DIM])).astype(BF16)


def _delta_call(has_s0, t, nseq, n_batch, row_block0, layer, zb, zab, zabt, conv, prmr, bng, masks,
                prev=None, s0=None):
    n_tok = zb.shape[0]
    depth = conv.shape[0]
    n_chunks = nseq * (t // CHUNK)
    assert n_chunks % PREP_UNROLL == 0 and PREP_UNROLL % 2 == 0 and n_batch % nseq == 0
    tok_spec = lambda w: pl.BlockSpec((nseq * t, w), lambda b: (row_block0 + b, 0))
    const = lambda shape: pl.BlockSpec(shape, lambda b: (0,) * len(shape))
    layer_spec = lambda shape: pl.BlockSpec((1,) + shape, lambda b: (layer,) + (0,) * len(shape))
    s_shape = (B_HEADS * B_DIM, B_DIM)
    s_spec = pl.BlockSpec((nseq, 1) + s_shape, lambda b: (b, layer, 0, 0))
    n_ab = zabt.shape[0]
    in_specs = [
        tok_spec(ZB_W), tok_spec(ZAB_W),
        pl.BlockSpec((n_ab, nseq * t), lambda b: (0, row_block0 + b)),
        layer_spec((3, 3 * B_HEADS * B_DIM)), layer_spec((2, n_ab, LANES)),
        layer_spec((1, B_DIM)),
        const((5 + N_LEVELS, PAIR, PAIR)),
    ]
    args = [zb, zab, zabt, conv, prmr, bng, masks]
    out_specs = [tok_spec(B_HEADS * B_DIM)]
    out_shape = [jax.ShapeDtypeStruct((n_tok, B_HEADS * B_DIM), BF16)]
    if has_s0:
        in_specs += [s_spec, s_spec]
        args += [s0[0], s0[1]]
    else:
        out_specs += [s_spec, s_spec]
        out_shape += [jax.ShapeDtypeStruct((n_batch, depth) + s_shape, F32)] * 2
    n_real = len(args)
    aliases = {}
    if prev is not None:
        first_out = 0 if has_s0 else 1
        for k, arr in enumerate(prev):
            in_specs.append(pl.BlockSpec(memory_space=pl.ANY))
            args.append(arr)
            aliases[n_real + k] = first_out + k
    rows = nseq * t
    scratch = [
        pltpu.VMEM((rows, 3 * B_HEADS * B_DIM), F32),
        pltpu.VMEM((rows, B_HEADS * B_DIM), F32), pltpu.VMEM((rows, B_HEADS * B_DIM), F32),
        pltpu.VMEM((nseq * s_shape[0], B_DIM), F32), pltpu.VMEM((nseq * s_shape[0], B_DIM), F32),
        pltpu.VMEM((2, n_chunks, BD, B_DIM), F32),
        pltpu.VMEM((2, n_chunks, 2 * BD, B_DIM), BF16),
        pltpu.VMEM((2, n_chunks, B_HEADS // 2, PAIR, PAIR), BF16),
        pltpu.VMEM((2, n_chunks, BD, B_DIM), BF16),
        pltpu.VMEM((2, n_chunks, B_HEADS * SUBLANES, LANES), F32),
        pltpu.VMEM((n_ab, rows), F32), pltpu.VMEM((n_ab, rows), F32),
        pltpu.VMEM((rows, LANES), F32), pltpu.VMEM((rows, LANES), F32),
    ]

    def body(*refs):
        _delta_kernel(t, nseq, has_s0, *refs[:n_real], *refs[len(args):])

    return pl.pallas_call(
        body,
        grid=(n_batch // nseq,),
        in_specs=in_specs,
        out_specs=out_specs,
        out_shape=out_shape,
        scratch_shapes=scratch,
        input_output_aliases=aliases,
        compiler_params=_params(("arbitrary",)),
        name="delta_latent" if has_s0 else "delta_prompt",
    )(*args)


def _outproj_router(x, ma, mb, mc, m, g, wo_ref, wr_ref, br):
    n = x.shape[0]
    b0, c0 = Q_W, Q_W + B_HEADS * B_DIM
    y = (_dot(ma.astype(BF16), wo_ref[0, 0:b0, :])
         + _dot(mb.astype(BF16), wo_ref[0, b0:c0, :])
         + _dot(mc.astype(BF16), wo_ref[0, c0:c0 + Q_W, :]))
    x1 = x + m[2:3] * y
    h2 = _modulated_norm(x1, g, m[3:4], m[4:5])
    hi, lo = _split2(h2)

    hw = _dot(hi, wr_ref[0])
    logits = (hw[:, :LANES] + hw[:, LANES:] + _dot(lo, wr_ref[0, :, :LANES]) + br).T
    gl = logits[0:N_GROUPS]
    grow = lax.broadcasted_iota(jnp.int32, gl.shape, 0)
    gmax = gl.max(axis=0, keepdims=True)
    g_sel = jnp.where(gl == gmax, grow, N_GROUPS).min(axis=0, keepdims=True)
    g_w = 1.0 / jnp.exp(gl - gmax).sum(axis=0, keepdims=True)
    el = logits[EXPERT_ROW0:EXPERT_ROW0 + N_EXPERTS]
    e_idx = lax.broadcasted_iota(jnp.int32, el.shape, 0)
    el = jnp.where((e_idx // EXPERTS_PER_GROUP) == g_sel, el, -jnp.inf)
    m1 = el.max(axis=0, keepdims=True)
    i1 = jnp.where(el == m1, e_idx, N_EXPERTS).min(axis=0, keepdims=True)
    el2 = jnp.where(e_idx == i1, -jnp.inf, el)
    m2 = el2.max(axis=0, keepdims=True)
    i2 = jnp.where(el2 == m2, e_idx, N_EXPERTS).min(axis=0, keepdims=True)
    tt = jnp.exp(m2 - m1)
    w1 = g_w / (1.0 + tt)
    w2 = w1 * tt
    gate_t = jnp.where(e_idx == i1, w1, 0.0) + jnp.where(e_idx == i2, w2, 0.0)
    gate = jnp.concatenate([gate_t, jnp.zeros((LANES - N_EXPERTS, n), F32)], axis=0).T
    return x1, hi, gate


def _ffn_kernel(x_ref, ma_ref, mb_ref, mc_ref, mod_ref, g_ref, wo_ref, wr_ref, br_ref, w1_ref, w3_ref, w2_ref,
                o_ref, h_s, gate_s):
    j = pl.program_id(1)
    tm = x_ref.shape[0]
    th = w1_ref.shape[1]
    m = mod_ref[0, 0]

    @pl.when(j == 0)
    def _():
        x1, hi, gate = _outproj_router(x_ref[...], ma_ref[...], mb_ref[...], mc_ref[...],
                                       m, g_ref[0], wo_ref, wr_ref, br_ref[0])
        o_ref[...] = x1
        h_s[...] = hi
        gate_s[...] = gate

    @pl.when(j > 0)
    def _():
        h = h_s[...]
        hid = _silu(_dot(h, w1_ref[...])) * _dot(h, w3_ref[...])
        gate = gate_s[...]
        lane = lax.broadcasted_iota(jnp.int32, gate.shape, 1)
        n_e = th // D_EXPERT
        col = lax.broadcasted_iota(jnp.int32, hid.shape, 1) // D_EXPERT
        gmat = jnp.zeros(hid.shape, F32)
        for e in range(n_e):
            ge = jnp.where(lane == (j - 1) * n_e + e, gate, 0.0).sum(axis=1, keepdims=True)
            gmat = jnp.where(col == e, ge, gmat)
        o_ref[...] += m[5:6] * _dot((hid * gmat).astype(BF16), w2_ref[...])


def _ffn_call(layer, x, ma, mb, mc, mods, g, wo, wr, br, w1, w3, w2, slot_fn, tm, th):
    n_tok = x.shape[0]
    n_h = w1.shape[1] // th
    hidden = lambda j: jnp.where(j == 0, n_h - 1, j - 1)
    tok = lambda w: pl.BlockSpec((tm, w), lambda i, j: (i, 0))
    layer_spec = lambda shape: pl.BlockSpec((1,) + shape, lambda i, j: (layer,) + (0,) * len(shape))
    return pl.pallas_call(
        _ffn_kernel,
        grid=(n_tok // tm, n_h + 1),
        in_specs=[tok(D_MODEL), tok(Q_W), tok(B_HEADS * B_DIM), tok(Q_W),
                  pl.BlockSpec((1, 1, 6, D_MODEL), lambda i, j: (layer, slot_fn(i), 0, 0)),
                  layer_spec((1, D_MODEL)), layer_spec((D_MODEL, D_MODEL)), layer_spec((D_MODEL, 2 * LANES)),
                  layer_spec((1, LANES)),
                  pl.BlockSpec((D_MODEL, th), lambda i, j: (0, hidden(j))),
                  pl.BlockSpec((D_MODEL, th), lambda i, j: (0, hidden(j))),
                  pl.BlockSpec((th, D_MODEL), lambda i, j: (hidden(j), 0))],
        out_specs=tok(D_MODEL),
        out_shape=jax.ShapeDtypeStruct((n_tok, D_MODEL), F32),
        scratch_shapes=[pltpu.VMEM((tm, D_MODEL), BF16), pltpu.VMEM((tm, LANES), F32)],
        compiler_params=_params(("arbitrary", "arbitrary")),
        name="ffn",
    )(x, ma, mb, mc, mods, g, wo, wr, br, w1, w3, w2)


def _final_norm_kernel(x_ref, g_ref, o_ref):
    x = x_ref[...]
    o_ref[...] = x * lax.rsqrt(jnp.mean(x * x, axis=-1, keepdims=True) + EPS) * g_ref[...]


def _final_norm_call(x, g, tm, row0, n_rows):
    blk0 = row0 // tm
    return pl.pallas_call(
        _final_norm_kernel,
        grid=(n_rows // tm,),
        in_specs=[pl.BlockSpec((tm, D_MODEL), lambda i: (blk0 + i, 0)), pl.BlockSpec((1, D_MODEL), lambda i: (0, 0))],
        out_specs=pl.BlockSpec((tm, D_MODEL), lambda i: (i, 0)),
        out_shape=jax.ShapeDtypeStruct((n_rows, D_MODEL), F32),
        compiler_params=_params(("arbitrary",)),
        name="final_norm",
    )(x, g)


def _rope_tables(t):
    pos = np.arange(t)
    n_freq = HEAD_DIM // 4
    inv_freq = ROPE_THETA ** (-jnp.arange(n_freq, dtype=F32) / n_freq)
    row = jnp.asarray(pos // GRID_W, F32)
    col = jnp.asarray(pos % GRID_W, F32)
    ang = jnp.concatenate([row[:, None] * inv_freq, col[:, None] * inv_freq], -1)
    cos, sin = jnp.cos(ang), jnp.sin(ang)
    cos_t = jnp.tile(jnp.concatenate([cos, cos], -1), (1, LANES // HEAD_DIM))
    sin_t = jnp.tile(jnp.concatenate([-sin, sin], -1), (1, LANES // HEAD_DIM))
    return cos_t, sin_t


def _delta_tables():
    r = np.arange(PAIR)
    same = (r[:, None] // CHUNK) == (r[None, :] // CHUNK)
    low = same & (r[:, None] >= r[None, :])
    low_s = same & (r[:, None] > r[None, :])
    up = same & (r[:, None] <= r[None, :])
    up_s = same & (r[:, None] < r[None, :])
    levels = []
    for k in range(N_LEVELS):
        s = 1 << k
        levels.append(((r[:, None] // (2 * s)) == (r[None, :] // (2 * s))) & ((r[:, None] // s) != (r[None, :] // s)))
    masks = jnp.asarray(np.stack([low, low_s, up, up_s, np.eye(PAIR, dtype=bool)] + levels).astype(np.float32))
    return masks


def _segment_mean_table():
    r = np.arange(LANES)
    seg = ((r[:, None] // HEAD_DIM) == (r[None, :] // HEAD_DIM)).astype(np.float32) / HEAD_DIM
    hi = jnp.asarray(seg, BF16)
    lo = (jnp.asarray(seg) - hi.astype(F32)).astype(BF16)
    return jnp.stack([hi, lo])


def kernel(x_prompt, x_sample, cache_a_k, cache_a_v, cache_c_k, cache_c_v, state_b_fwd, state_b_bwd, c, c_ctx, w_mod, b_mod, norm1_g, norm2_g, w_in, a_sink, b_conv, b_a_log, b_dt_bias, b_norm_g, c_q_norm, c_k_norm, w_out, w_group, b_group, w_expert, b_expert, w1, w3, w2, final_norm_g):
    n_p, t_p, d = x_prompt.shape
    n_s, t_s, _ = x_sample.shape
    depth = w_in.shape[0]
    past = cache_a_k.shape[2]
    tok_p = n_p * t_p
    n_tok = tok_p + n_s * t_s
    assert d == D_MODEL and tok_p % t_s == 0 and t_s % max(TM_PROJ, TM_FFN) == 0 and t_p % 256 == 0

    w_in_t = jnp.swapaxes(w_in, 1, 2)
    w_out16 = w_out.astype(BF16)
    pad_g = jnp.zeros((depth, d, EXPERT_ROW0 - N_GROUPS), F32)
    pad_e = jnp.zeros((depth, d, LANES - EXPERT_ROW0 - N_EXPERTS), F32)
    w_r = jnp.concatenate([w_group, pad_g, w_expert, pad_e], -1)
    w_r_hi = w_r.astype(BF16)
    w_r2 = jnp.concatenate([w_r_hi, (w_r - w_r_hi.astype(F32)).astype(BF16)], axis=-1)
    b_r = jnp.concatenate([b_group, pad_g[:, 0], b_expert, pad_e[:, 0]], -1)[:, None, :]
    cqn = jnp.tile(c_q_norm, (1, 4))[:, None, :]
    ckn = jnp.tile(c_k_norm, (1, 2))[:, None, :]
    gate_prm = jnp.stack([b_a_log.reshape(depth, 8), b_dt_bias.reshape(depth, 8)], 1)
    prmr = jnp.broadcast_to(jnp.pad(gate_prm, ((0, 0), (0, 0), (0, N_AB - 8)))[..., None],
                            (depth, 2, N_AB, LANES))
    cos_t, sin_t = _rope_tables(t_s)
    masks = _delta_tables()
    seg = _segment_mean_table()

    cond = jnp.concatenate([c_ctx[None, :], c], axis=0)
    cond_b = jnp.broadcast_to(cond[:, :, None], cond.shape + (LANES,))
    mods_all = _mods_call(cond_b, w_mod, b_mod).reshape(depth, SUBLANES, 6, d)

    def slot_fn(tm):
        per_s = t_s // tm
        first = tok_p // tm
        return lambda i: jnp.where(i < first, 0, 1 + (i - first) // per_s)

    xs = (x_prompt.reshape(tok_p, d), x_sample.reshape(n_s * t_s, d))
    blk_s = tok_p // t_s
    ctx = tuple(a.reshape(n_s, depth, past, LANES) for a in (cache_a_k, cache_a_v, cache_c_k, cache_c_v))
    s0 = tuple(a.reshape(n_s, depth, B_HEADS * B_DIM, B_DIM) for a in (state_b_fwd, state_b_bwd))
    g1, g2, bng = norm1_g[:, None, :], norm2_g[:, None, :], b_norm_g[:, None, :]
    caches = None
    states = None
    for l in range(depth):
        za, zb, zc, zab, zabt, *rest = _inproj_call(l, xs, mods_all, g1, w_in_t, slot_fn(TM_PROJ), TM_PROJ,
                                                    experts=(w3, w2))
        x = rest[0] if len(xs) > 1 else xs[0]
        w3b, w2b = rest[-2:]

        ao, co, *rest = _attn_call(False, t_p, ATTN_NSEQ, n_p, 0, l, za, zc, a_sink, cqn, ckn, seg, prev=caches,
                                   experts=(w1,))
        caches, (w1b,) = rest[:4], rest[4:]
        ao, co = _attn_call(True, t_s, 1, n_s, blk_s, l, za, zc, a_sink, cqn, ckn, seg, prev=(ao, co),
                            rope=(cos_t, sin_t), ctx=ctx)

        bo, *states = _delta_call(False, t_p, DELTA_NSEQ, n_p, 0, l, zb, zab, zabt, b_conv, prmr, bng, masks,
                                  prev=states)
        (bo,) = _delta_call(True, t_s, 1, n_s, blk_s, l, zb, zab, zabt, b_conv, prmr, bng, masks,
                            prev=(bo,), s0=s0)

        x = _ffn_call(l, x, ao, bo, co, mods_all, g2, w_out16, w_r2, b_r, w1b, w3b, w2b, slot_fn(TM_FFN), TM_FFN, TH_FFN)
        xs = (x,)

    y_prompt = _final_norm_call(x, final_norm_g[None], TM_NORM, 0, tok_p).reshape(n_p, t_p, d)
    y_sample = _final_norm_call(x, final_norm_g[None], TM_NORM, tok_p, n_s * t_s).reshape(n_s, t_s, d)
    new_ak, new_av, new_ck, new_cv = (a.reshape(n_p, depth, t_p, 2, HEAD_DIM) for a in caches)
    new_sf, new_sb = (a.reshape(n_p, depth, B_HEADS, B_DIM, B_DIM) for a in states)
    return (y_prompt, y_sample, new_ak, new_av, new_ck, new_cv, new_sf, new_sb)
```

```python
import functools

import jax
import jax.numpy as jnp
import numpy as np
from jax import lax
from jax.experimental import pallas as pl
from jax.experimental.pallas import tpu as pltpu

F32 = jnp.float32
BF16 = jnp.bfloat16

D_MODEL = 1024
GRID_W = 64
EPS = 1e-6
NEG_INF = -1e30
ROPE_THETA = 10000.0
HEAD_DIM = 64
Q_W = 256
KV_W = 128
WINDOW = 128
Q_BLOCK = 128
B_HEADS = 4
B_DIM = 128
CHUNK = 64
BD = B_HEADS * CHUNK
PAIR = 2 * CHUNK
N_LEVELS = 6
PREP_UNROLL = 4
DELTA_NSEQ = 4
ATTN_NSEQ = 4
N_GROUPS = 4
EXPERTS_PER_GROUP = 4
N_EXPERTS = 16
D_EXPERT = 256
EXPERT_ROW0 = 8

LANES = 128
SUBLANES = 8
VMEM_LIMIT = 60000 * 1024

TM_PROJ = 512
TM_FFN, TH_FFN = 1024, 1024
TM_NORM = 512
CAST_STEPS = 16
MODS_TN = 1536

ZA_W, ZB_W, ZC_W, ZAB_W = 512, 2048, 512, 128
N_AB = 16
Z_W = ZA_W + ZB_W + ZC_W + ZAB_W


def _sigmoid(x):
    return 1.0 / (1.0 + jnp.exp(-x))


def _silu(x):
    return x * _sigmoid(x)


def _softplus(x):
    return jnp.maximum(x, 0.0) + jnp.log1p(jnp.exp(-jnp.abs(x)))


def _dot(a, b):
    return jnp.dot(a, b, preferred_element_type=F32)


def _dot_nt(a, b):
    return lax.dot_general(a, b, (((1,), (1,)), ((), ())), preferred_element_type=F32)


def _dot_tn(a, b):
    return lax.dot_general(a, b, (((0,), (0,)), ((), ())), preferred_element_type=F32)


def _split2(x):
    hi = x.astype(BF16)
    lo = (x - hi.astype(F32)).astype(BF16)
    return hi, lo


def _params(sem=None):
    return pltpu.CompilerParams(dimension_semantics=sem, vmem_limit_bytes=VMEM_LIMIT)


def _mods_kernel(cond_ref, w_ref, b_ref, o_ref, act_s):
    n_cond = cond_ref.shape[0]
    tn = w_ref.shape[2]
    reps = tn // LANES

    @pl.when((pl.program_id(0) == 0) & (pl.program_id(1) == 0))
    def _():
        act_s[...] = _silu(cond_ref[...])

    def body(kb, accs):
        r = pl.multiple_of(kb * SUBLANES, SUBLANES)
        w = w_ref[0, pl.ds(r, SUBLANES), :]
        return tuple(acc + jnp.tile(act_s[m, pl.ds(r, SUBLANES), :], (1, reps)) * w for m, acc in enumerate(accs))

    zero = jnp.zeros((SUBLANES, tn), F32)
    accs = lax.fori_loop(0, w_ref.shape[1] // SUBLANES, body, (zero,) * n_cond, unroll=4)
    rows = [jnp.sum(a, axis=0, keepdims=True) + b_ref[0] for a in accs]
    rows.append(jnp.zeros((SUBLANES - n_cond, tn), F32))
    o_ref[0] = jnp.concatenate(rows, axis=0)


def _mods_call(cond_b, w_mod, b_mod):
    depth, d, n = w_mod.shape
    tn = MODS_TN
    n_cond = cond_b.shape[0]
    return pl.pallas_call(
        _mods_kernel,
        grid=(depth, n // tn),
        in_specs=[
            pl.BlockSpec((n_cond, d, LANES), lambda l, j: (0, 0, 0)),
            pl.BlockSpec((1, d, tn), lambda l, j: (l, 0, j)),
            pl.BlockSpec((1, 1, tn), lambda l, j: (l, 0, j)),
        ],
        out_specs=pl.BlockSpec((1, SUBLANES, tn), lambda l, j: (l, 0, j)),
        out_shape=jax.ShapeDtypeStruct((depth, SUBLANES, n), F32),
        scratch_shapes=[pltpu.VMEM((n_cond, d, LANES), F32)],
        compiler_params=_params(("arbitrary", "arbitrary")),
        name="mods",
    )(cond_b, w_mod, b_mod.reshape(depth, 1, n))


def _x_specs(xs, tm):
    if len(xs) == 1:
        return [pl.BlockSpec((tm, D_MODEL), lambda i, *_: (i, 0))]
    first = xs[0].shape[0] // tm
    return [pl.BlockSpec((tm, D_MODEL), lambda i, *_: (jnp.minimum(i, first - 1), 0)),
            pl.BlockSpec((tm, D_MODEL), lambda i, *_: (jnp.maximum(i - first, 0), 0))]


def _x_tile(x_refs, first):
    if len(x_refs) == 1:
        return x_refs[0][...]
    return jnp.where(pl.program_id(0) < first, x_refs[0][...], x_refs[1][...])


def _modulated_norm(x, g, shift, scale):
    ms = jnp.mean(x * x, axis=-1, keepdims=True)
    y = x * lax.rsqrt(ms + EPS) * g
    return y * (1.0 + scale) + shift


def _inproj_kernel(n_x, first, n_w, *refs):
    x_refs = refs[:n_x]
    mod_ref, g_ref, wt_ref = refs[n_x:n_x + 3]
    w_refs = refs[n_x + 3:n_x + 3 + n_w]
    o0 = n_x + 3 + n_w
    za_ref, zb_ref, zc_ref, zab_ref, zabt_ref = refs[o0:o0 + 5]
    o1 = o0 + 5 + (1 if n_x > 1 else 0)
    wb_refs = refs[o1:o1 + n_w]
    w_s = refs[-1]
    for w_ref, wb_ref in zip(w_refs, wb_refs):
        wb_ref[...] = w_ref[0].astype(BF16)
    @pl.when(pl.program_id(0) == 0)
    def _():
        ab0 = ZA_W + ZB_W
        w_s[0:ab0, :] = wt_ref[0, 0:ab0, :].astype(BF16)
        w_s[ab0:ab0 + ZC_W, :] = wt_ref[0, ab0 + N_AB:ab0 + N_AB + ZC_W, :].astype(BF16)
        w_s[ab0 + ZC_W:ab0 + ZC_W + N_AB, :] = wt_ref[0, ab0:ab0 + N_AB, :].astype(BF16)
        w_s[ab0 + ZC_W + N_AB:Z_W, :] = jnp.zeros((ZAB_W - N_AB, D_MODEL), BF16)

    m = mod_ref[0, 0]
    x = _x_tile(x_refs, first)
    if n_x > 1:
        refs[o0 + 5][...] = x
    h = _modulated_norm(x, g_ref[0], m[0:1], m[1:2]).astype(BF16)
    za_ref[...] = _dot_nt(h, w_s[0:ZA_W, :])
    step = 512
    for j in range(ZB_W // step):
        zb_ref[:, j * step:(j + 1) * step] = _dot_nt(h, w_s[ZA_W + j * step:ZA_W + (j + 1) * step, :])
    zc_ref[...] = _dot_nt(h, w_s[ZA_W + ZB_W:ZA_W + ZB_W + ZC_W, :])
    zab = _dot_nt(h, w_s[ZA_W + ZB_W + ZC_W:Z_W, :])
    zab_ref[...] = zab
    zabt_ref[...] = zab.T[:N_AB]


def _inproj_call(layer, xs, mods, g, w, slot_fn, tm, experts=()):
    n_tok = sum(a.shape[0] for a in xs)
    n_ab = N_AB
    cast_steps = CAST_STEPS
    assert n_tok // tm >= cast_steps
    slab = lambda i: jnp.minimum(i, cast_steps - 1)
    cast_in = [pl.BlockSpec((1, e.shape[1] // cast_steps, e.shape[2]), lambda i: (layer, slab(i), 0)) for e in experts]
    cast_out = [pl.BlockSpec((e.shape[1] // cast_steps, e.shape[2]), lambda i: (slab(i), 0)) for e in experts]
    return pl.pallas_call(
        functools.partial(_inproj_kernel, len(xs), xs[0].shape[0] // tm, len(experts)),
        grid=(n_tok // tm,),
        in_specs=_x_specs(xs, tm) + [
            pl.BlockSpec((1, 1, 6, D_MODEL), lambda i: (layer, slot_fn(i), 0, 0)),
            pl.BlockSpec((1, 1, D_MODEL), lambda i: (layer, 0, 0)),
            pl.BlockSpec((1, w.shape[1], D_MODEL), lambda i: (layer, 0, 0)),
        ] + cast_in,
        out_specs=[
            pl.BlockSpec((tm, ZA_W), lambda i: (i, 0)),
            pl.BlockSpec((tm, ZB_W), lambda i: (i, 0)),
            pl.BlockSpec((tm, ZC_W), lambda i: (i, 0)),
            pl.BlockSpec((tm, ZAB_W), lambda i: (i, 0)),
            pl.BlockSpec((n_ab, tm), lambda i: (0, i)),
        ] + ([pl.BlockSpec((tm, D_MODEL), lambda i: (i, 0))] if len(xs) > 1 else []) + cast_out,
        out_shape=[
            jax.ShapeDtypeStruct((n_tok, ZA_W), F32),
            jax.ShapeDtypeStruct((n_tok, ZB_W), F32),
            jax.ShapeDtypeStruct((n_tok, ZC_W), F32),
            jax.ShapeDtypeStruct((n_tok, ZAB_W), F32),
            jax.ShapeDtypeStruct((n_ab, n_tok), F32),
        ] + ([jax.ShapeDtypeStruct((n_tok, D_MODEL), F32)] if len(xs) > 1 else [])
        + [jax.ShapeDtypeStruct(e.shape[1:], BF16) for e in experts],
        scratch_shapes=[pltpu.VMEM((Z_W, D_MODEL), BF16)],
        compiler_params=_params(("arbitrary",)),
        name="inproj",
    )(*xs, mods, g, w, *experts)


def _lane_lo(shape):
    return lax.broadcasted_iota(jnp.int32, shape, len(shape) - 1) % LANES < HEAD_DIM


def _store_kdup(dst_ref, off, k):
    n = k.shape[0]
    r = pltpu.roll(k, HEAD_DIM, 1)
    lo = _lane_lo(k.shape)
    dst_ref[0, off:off + n, :] = jnp.where(lo, k, r).astype(BF16)
    dst_ref[1, off:off + n, :] = jnp.where(lo, r, k).astype(BF16)


def _store_vsplit(dst_ref, off, v):
    n = v.shape[0]
    r = pltpu.roll(v, HEAD_DIM, 1)
    lo = _lane_lo(v.shape)
    z = jnp.zeros_like(v)
    dst_ref[0, off:off + n, :] = jnp.where(lo, v, z).astype(BF16)
    dst_ref[1, off:off + n, :] = jnp.where(lo, z, r).astype(BF16)
    dst_ref[2, off:off + n, :] = jnp.where(lo, r, z).astype(BF16)
    dst_ref[3, off:off + n, :] = jnp.where(lo, z, v).astype(BF16)


def _rope(x, cos, sin):
    first = (lax.broadcasted_iota(jnp.int32, x.shape, 1) // (HEAD_DIM // 2)) % 2 == 0
    partner = jnp.where(first, pltpu.roll(x, LANES - HEAD_DIM // 2, 1), pltpu.roll(x, HEAD_DIM // 2, 1))
    return x * cos + partner * sin


def _head_rmsnorm(x, g, seg_hi, seg_lo):
    hi, lo = _split2(x * x)
    ms = _dot(hi, seg_hi) + _dot(lo, seg_hi) + _dot(hi, seg_lo)
    return x * lax.rsqrt(ms + EPS) * g


def _attend_many(units):
    qb = units[0][0].shape[0]
    lo = _lane_lo(units[0][0].shape)
    all_scores = []
    for qt, segs, _ in units:
        z = jnp.zeros_like(qt)
        qs = jnp.concatenate([jnp.where(lo, qt, z), jnp.where(lo, z, qt)], axis=0).astype(BF16)
        scores = []
        for kdup, _, _, mask in segs:
            s = _dot_nt(qs, kdup)
            if mask is not None:
                s = jnp.where(mask, s, NEG_INF)
            scores.append(s)
        all_scores.append(scores)
    probs = []
    for (qt, segs, sink_pair), scores in zip(units, all_scores):
        m = scores[0].max(axis=1, keepdims=True)
        for s in scores[1:]:
            m = jnp.maximum(m, s.max(axis=1, keepdims=True))
        if sink_pair is not None:
            row_a = lax.broadcasted_iota(jnp.int32, (2 * qb, 1), 0) < qb
            sink = jnp.where(row_a, sink_pair[0], sink_pair[1])
            m = jnp.maximum(m, sink)
            denom = jnp.exp(sink - m)
        else:
            denom = jnp.zeros((2 * qb, 1), F32)
        ps = []
        for s in scores:
            p = jnp.exp(s - m)
            denom = denom + p.sum(axis=1, keepdims=True)
            ps.append(p.astype(BF16))
        probs.append((ps, 1.0 / denom))
    outs = []
    for (qt, segs, _), (ps, inv) in zip(units, probs):
        acc = jnp.zeros((qb, LANES), F32)
        for pb, (_, vlo, vhi, _) in zip(ps, segs):
            acc = acc + _dot(pb[:qb], vlo) + _dot(pb[qb:], vhi)
        outs.append(acc * jnp.where(lo, inv[:qb], inv[qb:]))
    return outs


def _attn_kernel(has_ctx, t, nseq, layer, *refs):
    if has_ctx:
        (sink_ref, za_ref, zc_ref, cqn_ref, ckn_ref, seg_ref, cos_ref, sin_ref,
         cak_ref, cav_ref, cck_ref, ccv_ref, w2_ref,
         ao_ref, co_ref, w2b_ref,
         ka_s, va_s, kc_s, vc_s, kctx_s, vctx_s, qa_s, qc_s) = refs
        w2b_ref[...] = w2_ref[0].astype(BF16)
    else:
        (sink_ref, za_ref, zc_ref, cqn_ref, ckn_ref, seg_ref, w1_ref,
         ao_ref, co_ref, nak_ref, nav_ref, nck_ref, ncv_ref, w1b_ref,
         ka_s, va_s, kc_s, vc_s, qa_s, qc_s) = refs
        w1b_ref[...] = w1_ref[0].astype(BF16)
    scale = HEAD_DIM ** -0.5
    seg_hi = seg_ref[0]
    seg_lo = seg_ref[1]
    piece = 256
    n_ctx = cak_ref.shape[2] if has_ctx else 0

    for p0 in range(0, nseq * t, piece):
        rows = slice(p0, p0 + piece)
        ak = za_ref[rows, Q_W:Q_W + KV_W]
        av = za_ref[rows, Q_W + KV_W:Q_W + 2 * KV_W]
        ck = _head_rmsnorm(zc_ref[rows, Q_W:Q_W + KV_W], ckn_ref[0], seg_hi, seg_lo)
        cv = zc_ref[rows, Q_W + KV_W:Q_W + 2 * KV_W]
        if has_ctx:
            cos = cos_ref[rows, :]
            sin = sin_ref[rows, :]
            ak = _rope(ak, cos, sin)
            ck = _rope(ck, cos, sin)
            _store_kdup(ka_s, WINDOW + p0, ak)
            _store_vsplit(va_s, WINDOW + p0, av)
            _store_kdup(kc_s, n_ctx + p0, ck)
            _store_vsplit(vc_s, n_ctx + p0, cv)
        else:
            crow = slice(p0 % t, p0 % t + piece)
            nak_ref[p0 // t, 0, crow, :] = ak
            nav_ref[p0 // t, 0, crow, :] = av
            nck_ref[p0 // t, 0, crow, :] = ck
            ncv_ref[p0 // t, 0, crow, :] = cv
            _store_kdup(ka_s, p0, ak)
            _store_vsplit(va_s, p0, av)
            _store_kdup(kc_s, p0, ck)
            _store_vsplit(vc_s, p0, cv)
        for hk in range(2):
            cols = slice(hk * LANES, (hk + 1) * LANES)
            aq = za_ref[rows, cols]
            cq = _head_rmsnorm(zc_ref[rows, cols], cqn_ref[0, :, cols], seg_hi, seg_lo)
            if has_ctx:
                aq = _rope(aq, cos, sin)
                cq = _rope(cq, cos, sin)
            qa_s[rows, cols] = aq * scale
            qc_s[rows, cols] = cq * scale

    if has_ctx:
        zpad = jnp.zeros((WINDOW, LANES), BF16)
        for i in range(2):
            ka_s[i, 0:WINDOW, :] = zpad
            ka_s[i, WINDOW + t:2 * WINDOW + t, :] = zpad
        for i in range(4):
            va_s[i, 0:WINDOW, :] = zpad
            va_s[i, WINDOW + t:2 * WINDOW + t, :] = zpad
        for p0 in range(0, n_ctx, piece):
            rows = slice(p0, p0 + piece)
            _store_kdup(kctx_s, p0, cak_ref[0, 0, rows, :])
            _store_vsplit(vctx_s, p0, cav_ref[0, 0, rows, :])
            _store_kdup(kc_s, p0, cck_ref[0, 0, rows, :])
            _store_vsplit(vc_s, p0, ccv_ref[0, 0, rows, :])

        qb = Q_BLOCK
        span = qb + 2 * WINDOW
        qi = lax.broadcasted_iota(jnp.int32, (2 * qb, span), 0) % qb
        kj = lax.broadcasted_iota(jnp.int32, (2 * qb, span), 1)
        band = jnp.abs(kj - WINDOW - qi) <= WINDOW

        def block(b, carry):
            r0 = pl.multiple_of(b * qb, qb)
            kpos = kj + (r0 - WINDOW)
            mask = band & (kpos >= 0) & (kpos < t)
            units = []
            for hk in range(2):
                cols = slice(hk * LANES, (hk + 1) * LANES)
                segs_a = [
                    (kctx_s[hk], vctx_s[2 * hk], vctx_s[2 * hk + 1], None),
                    (ka_s[hk, pl.ds(r0, span), :], va_s[2 * hk, pl.ds(r0, span), :],
                     va_s[2 * hk + 1, pl.ds(r0, span), :], mask),
                ]
                sinks = (sink_ref[layer, 2 * hk], sink_ref[layer, 2 * hk + 1])
                units.append((qa_s[pl.ds(r0, qb), cols], segs_a, sinks))
                segs_c = [(kc_s[hk], vc_s[2 * hk], vc_s[2 * hk + 1], None)]
                units.append((qc_s[pl.ds(r0, qb), cols], segs_c, None))
            outs = _attend_many(units)
            for hk in range(2):
                cols = slice(hk * LANES, (hk + 1) * LANES)
                ao_ref[pl.ds(r0, qb), cols] = outs[2 * hk].astype(BF16)
                co_ref[pl.ds(r0, qb), cols] = outs[2 * hk + 1].astype(BF16)
            return carry

        lax.fori_loop(0, t // qb, block, 0)
    else:
        units = []
        for q in range(nseq):
            seq = slice(q * t, (q + 1) * t)
            for hk in range(2):
                cols = slice(hk * LANES, (hk + 1) * LANES)
                sinks = (sink_ref[layer, 2 * hk], sink_ref[layer, 2 * hk + 1])
                units.append((qa_s[seq, cols],
                              [(ka_s[hk, seq, :], va_s[2 * hk, seq, :], va_s[2 * hk + 1, seq, :], None)], sinks))
                units.append((qc_s[seq, cols],
                              [(kc_s[hk, seq, :], vc_s[2 * hk, seq, :], vc_s[2 * hk + 1, seq, :], None)], None))
        outs = _attend_many(units)
        for q in range(nseq):
            seq = slice(q * t, (q + 1) * t)
            for hk in range(2):
                cols = slice(hk * LANES, (hk + 1) * LANES)
                ao_ref[seq, cols] = outs[4 * q + 2 * hk].astype(BF16)
                co_ref[seq, cols] = outs[4 * q + 2 * hk + 1].astype(BF16)


def _attn_call(has_ctx, t, nseq, n_batch, row_block0, layer, za, zc, sink, cqn, ckn, seg, prev=None, rope=None,
               ctx=None, experts=None):
    n_tok = za.shape[0]
    depth = sink.shape[0]
    assert n_batch % nseq == 0 and (nseq == 1 or not has_ctx)
    tok_spec = lambda w: pl.BlockSpec((nseq * t, w), lambda b, *_: (row_block0 + b, 0))
    const = lambda shape: pl.BlockSpec(shape, lambda b, *_: (0,) * len(shape))
    layer_spec = lambda shape: pl.BlockSpec((1,) + shape, lambda b, *_: (layer,) + (0,) * len(shape))
    in_specs = [tok_spec(ZA_W), tok_spec(ZC_W), layer_spec((1, Q_W)), layer_spec((1, KV_W)), const((2, LANES, LANES))]
    args = [za, zc, cqn, ckn, seg]
    out_specs = [tok_spec(Q_W), tok_spec(Q_W)]
    out_shape = [jax.ShapeDtypeStruct((n_tok, Q_W), BF16), jax.ShapeDtypeStruct((n_tok, Q_W), BF16)]
    if has_ctx:
        n_ctx = ctx[0].shape[2]
        in_specs += [const((t, LANES)), const((t, LANES))]
        args += list(rope)
        in_specs += [pl.BlockSpec((1, 1, n_ctx, LANES), lambda b, *_: (b, layer, 0, 0))] * 4
        args += list(ctx)
        for e in experts:
            rows_e = e.shape[1] // (n_batch // nseq)
            in_specs.append(pl.BlockSpec((1, rows_e, e.shape[2]), lambda b, *_: (layer, b, 0)))
            args.append(e)
            out_specs.append(pl.BlockSpec((rows_e, e.shape[2]), lambda b, *_: (b, 0)))
            out_shape.append(jax.ShapeDtypeStruct(e.shape[1:], BF16))
        scratch = [
            pltpu.VMEM((2, t + 2 * WINDOW, LANES), BF16), pltpu.VMEM((4, t + 2 * WINDOW, LANES), BF16),
            pltpu.VMEM((2, n_ctx + t, LANES), BF16), pltpu.VMEM((4, n_ctx + t, LANES), BF16),
            pltpu.VMEM((2, n_ctx, LANES), BF16), pltpu.VMEM((4, n_ctx, LANES), BF16),
            pltpu.VMEM((t, Q_W), F32), pltpu.VMEM((t, Q_W), F32),
        ]
    else:
        cache_spec = pl.BlockSpec((nseq, 1, t, LANES), lambda b, *_: (b, layer, 0, 0))
        out_specs += [cache_spec] * 4
        out_shape += [jax.ShapeDtypeStruct((n_batch, depth, t, LANES), F32)] * 4
        n_steps = n_batch // nseq
        for w in experts:
            rows_w = w.shape[1] // n_steps
            in_specs.append(pl.BlockSpec((1, rows_w, w.shape[2]), lambda b, *_: (layer, b, 0)))
            args.append(w)
            out_specs.append(pl.BlockSpec((rows_w, w.shape[2]), lambda b, *_: (b, 0)))
            out_shape.append(jax.ShapeDtypeStruct(w.shape[1:], BF16))
        rows = nseq * t
        scratch = [
            pltpu.VMEM((2, rows, LANES), BF16), pltpu.VMEM((4, rows, LANES), BF16),
            pltpu.VMEM((2, rows, LANES), BF16), pltpu.VMEM((4, rows, LANES), BF16),
            pltpu.VMEM((rows, Q_W), F32), pltpu.VMEM((rows, Q_W), F32),
        ]
    n_real = len(args)
    aliases = {}
    if prev is not None:
        first_out = 0 if has_ctx else 2
        for k, arr in enumerate(prev):
            in_specs.append(pl.BlockSpec(memory_space=pl.ANY))
            args.append(arr)
            aliases[1 + n_real + k] = first_out + k

    def body(*refs):
        ins = refs[:1 + n_real]
        rest = refs[1 + len(args):]
        _attn_kernel(has_ctx, t, nseq, layer, *ins, *rest)

    return pl.pallas_call(
        body,
        grid_spec=pltpu.PrefetchScalarGridSpec(
            num_scalar_prefetch=1, grid=(n_batch // nseq,), in_specs=in_specs, out_specs=out_specs,
            scratch_shapes=scratch),
        out_shape=out_shape,
        input_output_aliases=aliases,
        compiler_params=_params(("arbitrary",)),
        name="attn_latent" if has_ctx else "attn_prompt",
    )(sink, *args)


def _stack_pair(x, p):
    return jnp.concatenate([x[:, (2 * p + hl) * B_DIM:(2 * p + hl + 1) * B_DIM] for hl in range(2)], axis=0)


def _delta_kernel(t, nseq, has_s0, *refs):
    if has_s0:
        (zb_ref, abc_ref, abt_ref, conv_ref, prmr_ref, bng_ref, mask_ref,
         s0f_ref, s0b_ref, o_ref, qkv_s, of_s, ob_s, sf_s, sb_s, u_s, wq_s, at_s, kd_s, eg_s,
         pre_s, suf_s, prec_s, sufc_s) = refs
    else:
        (zb_ref, abc_ref, abt_ref, conv_ref, prmr_ref, bng_ref, mask_ref,
         o_ref, sfo_ref, sbo_ref, qkv_s, of_s, ob_s, sf_s, sb_s, u_s, wq_s, at_s, kd_s, eg_s,
         pre_s, suf_s, prec_s, sufc_s) = refs
    n_chunks = t // CHUNK
    n_total = nseq * n_chunks
    s_rows = B_HEADS * B_DIM
    qk_w = B_HEADS * B_DIM

    row = lax.broadcasted_iota(jnp.int32, (t, LANES), 0)
    for q in range(nseq):
        seq = slice(q * t, (q + 1) * t)
        for j in range(3 * B_HEADS):
            cols = slice(j * LANES, (j + 1) * LANES)
            x = zb_ref[seq, cols]
            prev = jnp.where(row == 0, 0.0, pltpu.roll(x, 1, 0))
            nxt = jnp.where(row == t - 1, 0.0, pltpu.roll(x, t - 1, 0))
            y = _silu(prev * conv_ref[0, 0:1, cols] + x * conv_ref[0, 1:2, cols] + nxt * conv_ref[0, 2:3, cols])
            if j < 2 * B_HEADS:
                y = y * lax.rsqrt(jnp.sum(y * y, axis=-1, keepdims=True) + EPS)
            if j < B_HEADS:
                y = y * (B_DIM ** -0.5)
            qkv_s[seq, cols] = y

    if has_s0:
        for q in range(nseq):
            sf_s[q * s_rows:(q + 1) * s_rows, :] = s0f_ref[q, 0]
            sb_s[q * s_rows:(q + 1) * s_rows, :] = s0b_ref[q, 0]
    else:
        sf_s[...] = jnp.zeros_like(sf_s)
        sb_s[...] = jnp.zeros_like(sb_s)

    reps = nseq * t // LANES
    gr = -jnp.tile(jnp.exp(prmr_ref[0, 0]), (1, reps)) * _softplus(abt_ref[...] + jnp.tile(prmr_ref[0, 1], (1, reps)))
    seg_lane = lax.broadcasted_iota(jnp.int32, gr.shape, 1) % CHUNK
    pre, suf = gr, gr
    for s in (1, 2, 4, 8, 16, 32):
        pre = pre + jnp.where(seg_lane >= s, pltpu.roll(pre, s, 1), 0.0)
        suf = suf + jnp.where(seg_lane < CHUNK - s, pltpu.roll(suf, nseq * t - s, 1), 0.0)
    pre_s[...] = pre
    suf_s[...] = suf
    zrows = jnp.zeros((LANES - pre.shape[0], LANES), F32)
    for j in range(reps):
        tile = slice(j * LANES, (j + 1) * LANES)
        prec_s[tile, :] = jnp.concatenate([pre[:, tile], zrows], axis=0).T
        sufc_s[tile, :] = jnp.concatenate([suf[:, tile], zrows], axis=0).T
    lane_lo = lax.broadcasted_iota(jnp.int32, (1, LANES), 1) < CHUNK

    def prepare(cc, carry):
        chains = []
        for k in range(PREP_UNROLL):
            c = cc * PREP_UNROLL + k
            r0 = pl.multiple_of(c * CHUNK, CHUNK)
            b_all = _sigmoid(abc_ref[pl.ds(r0, CHUNK), :])
            run_c = (prec_s[pl.ds(r0, CHUNK), :], sufc_s[pl.ds(r0, CHUNK), :])
            tile0 = pl.multiple_of((cc * PREP_UNROLL + k - k % 2) * CHUNK, LANES)
            run = (pre_s[:, pl.ds(tile0, LANES)], suf_s[:, pl.ds(tile0, LANES)])
            run_r = tuple(pltpu.roll(x, CHUNK, 1) for x in run)
            for p in range(B_HEADS // 2):
                kst = _stack_pair(qkv_s[pl.ds(r0, CHUNK), qk_w:2 * qk_w], p)
                qst = _stack_pair(qkv_s[pl.ds(r0, CHUNK), 0:qk_w], p)
                vst = _stack_pair(qkv_s[pl.ds(r0, CHUNK), 2 * qk_w:3 * qk_w], p)
                kq = _dot_nt(jnp.concatenate([kst, qst], axis=0).astype(BF16), kst.astype(BF16))
                for d in range(2):
                    cg = 4 * d + 2 * p
                    edge = CHUNK - 1 if d == 0 else 0
                    rep_col = lambda x, col: jnp.broadcast_to(x[:, col:col + 1], (CHUNK, LANES))
                    b_rep = jnp.concatenate([rep_col(b_all, 8 + cg + hl) for hl in range(2)], axis=0)
                    gcol = jnp.concatenate([rep_col(run_c[d], cg + hl) for hl in range(2)], axis=0)
                    gtot = jnp.concatenate([rep_col(run_c[d][edge:edge + 1], cg + hl) for hl in range(2)], axis=0)
                    ra = cg
                    if k % 2 == 0:
                        grow = jnp.where(lane_lo, run[d][ra:ra + 1], run_r[d][ra + 1:ra + 2])
                    else:
                        grow = jnp.where(lane_lo, run_r[d][ra:ra + 1], run[d][ra + 1:ra + 2])
                    chains.append(dict(c=c, p=p, d=d, kst=kst, qst=qst, vst=vst, kq=kq, b_st=b_rep,
                                       gcol=gcol, gtot=gtot, grow=grow))

        for ch in chains:
            d, b_st, kq, gcol = ch["d"], ch["b_st"], ch.pop("kq"), ch["gcol"]
            decay = jnp.exp(jnp.minimum(gcol - ch.pop("grow"), 0.0))
            ch["a_mat"] = (b_st * kq[:PAIR]) * (decay * mask_ref[2 * d + 1])
            ch["attn"] = (kq[PAIR:] * (decay * mask_ref[2 * d])).astype(BF16)
            ch["t_inv"] = mask_ref[4] - ch["a_mat"] * mask_ref[5]
        for lvl in range(N_LEVELS - 1):
            for ch in chains:
                ch["t16"] = ch["t_inv"].astype(BF16)
                ch["et"] = _dot((ch["a_mat"] * mask_ref[6 + lvl]).astype(BF16), ch["t16"])
            for ch in chains:
                ch["t_inv"] = ch["t_inv"] - _dot(ch.pop("t16"), ch.pop("et").astype(BF16))
        for ch in chains:
            egc = jnp.exp(ch["gcol"])
            rk = jnp.concatenate([ch["b_st"] * ch["vst"], (ch["b_st"] * egc) * ch["kst"]], axis=1)
            ch["rk"] = _dot(ch.pop("t_inv").astype(BF16), rk.astype(BF16))
            ch["qp16"] = (ch["qst"] * egc).astype(BF16)
        for ch in chains:
            c, p, d, rk, qp16 = ch["c"], ch["p"], ch["d"], ch["rk"], ch["qp16"]
            pair_rows = slice(p * PAIR, (p + 1) * PAIR)
            w16 = rk[:, B_DIM:].astype(BF16)
            u_s[d, c, pair_rows, :] = rk[:, :B_DIM]
            at_s[d, c, p] = ch["attn"]
            kd_s[d, c, pair_rows, :] = (ch["kst"] * jnp.exp(ch["gtot"] - ch["gcol"])).astype(BF16)
            eg = jnp.exp(ch["gtot"])
            for hl in range(2):
                h = 2 * p + hl
                rows = slice(hl * CHUNK, (hl + 1) * CHUNK)
                wq_s[d, c, h * 2 * CHUNK:h * 2 * CHUNK + CHUNK, :] = w16[rows]
                wq_s[d, c, h * 2 * CHUNK + CHUNK:(h + 1) * 2 * CHUNK, :] = qp16[rows]
                eg_s[d, c, h * SUBLANES:(h + 1) * SUBLANES, :] = eg[hl * CHUNK:hl * CHUNK + SUBLANES, :]
        return carry

    lax.fori_loop(0, n_total // PREP_UNROLL, prepare, 0)

    def scan_step(i, carry):
        units = []
        for q in range(nseq):
            for d, s_ref, o_s in ((0, sf_s, of_s), (1, sb_s, ob_s)):
                c = q * n_chunks + (i if d == 0 else n_chunks - 1 - i)
                units.append(dict(q=q, d=d, c=c, s_ref=s_ref, o_s=o_s, r0=pl.multiple_of(c * CHUNK, CHUNK)))
        for un in units:
            q, d, c, s_ref = un["q"], un["d"], un["c"], un["s_ref"]
            un["x"] = []
            for h in range(B_HEADS):
                srows = slice(q * s_rows + h * B_DIM, q * s_rows + (h + 1) * B_DIM)
                un["x"].append(_dot(wq_s[d, c, h * 2 * CHUNK:(h + 1) * 2 * CHUNK, :], s_ref[srows, :].astype(BF16)))
        for un in units:
            d, c = un["d"], un["c"]
            un["vp16"], un["o"] = [], []
            for p in range(B_HEADS // 2):
                xs = un["x"][2 * p:2 * p + 2]
                v_new = jnp.concatenate(
                    [u_s[d, c, (2 * p + hl) * CHUNK:(2 * p + hl + 1) * CHUNK, :] - xs[hl][:CHUNK] for hl in range(2)],
                    axis=0)
                vp16 = v_new.astype(BF16)
                un["vp16"].append(vp16)
                un["o"].append(jnp.concatenate([xs[hl][CHUNK:] for hl in range(2)], axis=0)
                               + _dot(at_s[d, c, p], vp16))
        for un in units:
            q, d, c, s_ref, o_s, r0 = un["q"], un["d"], un["c"], un["s_ref"], un["o_s"], un["r0"]
            for h in range(B_HEADS):
                p, hl = divmod(h, 2)
                rows = slice(hl * CHUNK, (hl + 1) * CHUNK)
                srows = slice(q * s_rows + h * B_DIM, q * s_rows + (h + 1) * B_DIM)
                upd = _dot_tn(kd_s[d, c, h * CHUNK:(h + 1) * CHUNK, :], un["vp16"][p][rows])
                eg = jnp.tile(eg_s[d, c, h * SUBLANES:(h + 1) * SUBLANES, :], (B_DIM // SUBLANES, 1))
                s_ref[srows, :] = s_ref[srows, :] * eg + upd
                o_s[pl.ds(r0, CHUNK), h * B_DIM:(h + 1) * B_DIM] = un["o"][p][rows]
        return carry

    lax.fori_loop(0, n_chunks, scan_step, 0)

    if not has_s0:
        for q in range(nseq):
            sfo_ref[q, 0] = sf_s[q * s_rows:(q + 1) * s_rows, :]
            sbo_ref[q, 0] = sb_s[q * s_rows:(q + 1) * s_rows, :]

    for h in range(B_HEADS):
        cols = slice(h * B_DIM, (h + 1) * B_DIM)
        x = of_s[:, cols] + ob_s[:, cols]
        yn = x * lax.rsqrt(jnp.mean(x * x, axis=-1, keepdims=True) + EPS) * bng_ref[0]
        o_ref[:, cols] = (yn * _silu(zb_ref[:, 3 * qk_w + h * B_DIM:3 * qk_w + (h + 1) * B_DIM])).astype(BF16)


def _delta_call(has_s0, t, nseq, n_batch, row_block0, layer, zb, zab, zabt, conv, prmr, bng, masks,
                prev=None, s0=None):
    n_tok = zb.shape[0]
    depth = conv.shape[0]
    n_chunks = nseq * (t // CHUNK)
    assert n_chunks % PREP_UNROLL == 0 and PREP_UNROLL % 2 == 0 and n_batch % nseq == 0
    tok_spec = lambda w: pl.BlockSpec((nseq * t, w), lambda b: (row_block0 + b, 0))
    const = lambda shape: pl.BlockSpec(shape, lambda b: (0,) * len(shape))
    layer_spec = lambda shape: pl.BlockSpec((1,) + shape, lambda b: (layer,) + (0,) * len(shape))
    s_shape = (B_HEADS * B_DIM, B_DIM)
    s_spec = pl.BlockSpec((nseq, 1) + s_shape, lambda b: (b, layer, 0, 0))
    n_ab = zabt.shape[0]
    in_specs = [
        tok_spec(ZB_W), tok_spec(ZAB_W),
        pl.BlockSpec((n_ab, nseq * t), lambda b: (0, row_block0 + b)),
        layer_spec((3, 3 * B_HEADS * B_DIM)), layer_spec((2, n_ab, LANES)),
        layer_spec((1, B_DIM)),
        const((5 + N_LEVELS, PAIR, PAIR)),
    ]
    args = [zb, zab, zabt, conv, prmr, bng, masks]
    out_specs = [tok_spec(B_HEADS * B_DIM)]
    out_shape = [jax.ShapeDtypeStruct((n_tok, B_HEADS * B_DIM), BF16)]
    if has_s0:
        in_specs += [s_spec, s_spec]
        args += [s0[0], s0[1]]
    else:
        out_specs += [s_spec, s_spec]
        out_shape += [jax.ShapeDtypeStruct((n_batch, depth) + s_shape, F32)] * 2
    n_real = len(args)
    aliases = {}
    if prev is not None:
        first_out = 0 if has_s0 else 1
        for k, arr in enumerate(prev):
            in_specs.append(pl.BlockSpec(memory_space=pl.ANY))
            args.append(arr)
            aliases[n_real + k] = first_out + k
    rows = nseq * t
    scratch = [
        pltpu.VMEM((rows, 3 * B_HEADS * B_DIM), F32),
        pltpu.VMEM((rows, B_HEADS * B_DIM), F32), pltpu.VMEM((rows, B_HEADS * B_DIM), F32),
        pltpu.VMEM((nseq * s_shape[0], B_DIM), F32), pltpu.VMEM((nseq * s_shape[0], B_DIM), F32),
        pltpu.VMEM((2, n_chunks, BD, B_DIM), F32),
        pltpu.VMEM((2, n_chunks, 2 * BD, B_DIM), BF16),
        pltpu.VMEM((2, n_chunks, B_HEADS // 2, PAIR, PAIR), BF16),
        pltpu.VMEM((2, n_chunks, BD, B_DIM), BF16),
        pltpu.VMEM((2, n_chunks, B_HEADS * SUBLANES, LANES), F32),
        pltpu.VMEM((n_ab, rows), F32), pltpu.VMEM((n_ab, rows), F32),
        pltpu.VMEM((rows, LANES), F32), pltpu.VMEM((rows, LANES), F32),
    ]

    def body(*refs):
        _delta_kernel(t, nseq, has_s0, *refs[:n_real], *refs[len(args):])

    return pl.pallas_call(
        body,
        grid=(n_batch // nseq,),
        in_specs=in_specs,
        out_specs=out_specs,
        out_shape=out_shape,
        scratch_shapes=scratch,
        input_output_aliases=aliases,
        compiler_params=_params(("arbitrary",)),
        name="delta_latent" if has_s0 else "delta_prompt",
    )(*args)


def _outproj_router(x, ma, mb, mc, m, g, wo_ref, wr_ref, br):
    n = x.shape[0]
    b0, c0 = Q_W, Q_W + B_HEADS * B_DIM
    y = (_dot(ma.astype(BF16), wo_ref[0, 0:b0, :])
         + _dot(mb.astype(BF16), wo_ref[0, b0:c0, :])
         + _dot(mc.astype(BF16), wo_ref[0, c0:c0 + Q_W, :]))
    x1 = x + m[2:3] * y
    h2 = _modulated_norm(x1, g, m[3:4], m[4:5])
    hi, lo = _split2(h2)

    hw = _dot(hi, wr_ref[0])
    logits = (hw[:, :LANES] + hw[:, LANES:] + _dot(lo, wr_ref[0, :, :LANES]) + br).T
    gl = logits[0:N_GROUPS]
    grow = lax.broadcasted_iota(jnp.int32, gl.shape, 0)
    gmax = gl.max(axis=0, keepdims=True)
    g_sel = jnp.where(gl == gmax, grow, N_GROUPS).min(axis=0, keepdims=True)
    g_w = 1.0 / jnp.exp(gl - gmax).sum(axis=0, keepdims=True)
    el = logits[EXPERT_ROW0:EXPERT_ROW0 + N_EXPERTS]
    e_idx = lax.broadcasted_iota(jnp.int32, el.shape, 0)
    el = jnp.where((e_idx // EXPERTS_PER_GROUP) == g_sel, el, -jnp.inf)
    m1 = el.max(axis=0, keepdims=True)
    i1 = jnp.where(el == m1, e_idx, N_EXPERTS).min(axis=0, keepdims=True)
    el2 = jnp.where(e_idx == i1, -jnp.inf, el)
    m2 = el2.max(axis=0, keepdims=True)
    i2 = jnp.where(el2 == m2, e_idx, N_EXPERTS).min(axis=0, keepdims=True)
    tt = jnp.exp(m2 - m1)
    w1 = g_w / (1.0 + tt)
    w2 = w1 * tt
    gate_t = jnp.where(e_idx == i1, w1, 0.0) + jnp.where(e_idx == i2, w2, 0.0)
    gate = jnp.concatenate([gate_t, jnp.zeros((LANES - N_EXPERTS, n), F32)], axis=0).T
    return x1, hi, gate


def _ffn_kernel(x_ref, ma_ref, mb_ref, mc_ref, mod_ref, g_ref, wo_ref, wr_ref, br_ref, w1_ref, w3_ref, w2_ref,
                o_ref, h_s, gate_s):
    j = pl.program_id(1)
    tm = x_ref.shape[0]
    th = w1_ref.shape[1]
    m = mod_ref[0, 0]

    @pl.when(j == 0)
    def _():
        x1, hi, gate = _outproj_router(x_ref[...], ma_ref[...], mb_ref[...], mc_ref[...],
                                       m, g_ref[0], wo_ref, wr_ref, br_ref[0])
        o_ref[...] = x1
        h_s[...] = hi
        gate_s[...] = gate

    @pl.when(j > 0)
    def _():
        h = h_s[...]
        hid = _silu(_dot(h, w1_ref[...])) * _dot(h, w3_ref[...])
        gate = gate_s[...]
        lane = lax.broadcasted_iota(jnp.int32, gate.shape, 1)
        n_e = th // D_EXPERT
        col = lax.broadcasted_iota(jnp.int32, hid.shape, 1) // D_EXPERT
        gmat = jnp.zeros(hid.shape, F32)
        for e in range(n_e):
            ge = jnp.where(lane == (j - 1) * n_e + e, gate, 0.0).sum(axis=1, keepdims=True)
            gmat = jnp.where(col == e, ge, gmat)
        o_ref[...] += m[5:6] * _dot((hid * gmat).astype(BF16), w2_ref[...])


def _ffn_call(layer, x, ma, mb, mc, mods, g, wo, wr, br, w1, w3, w2, slot_fn, tm, th):
    n_tok = x.shape[0]
    n_h = w1.shape[1] // th
    hidden = lambda j: jnp.where(j == 0, n_h - 1, j - 1)
    tok = lambda w: pl.BlockSpec((tm, w), lambda i, j: (i, 0))
    layer_spec = lambda shape: pl.BlockSpec((1,) + shape, lambda i, j: (layer,) + (0,) * len(shape))
    return pl.pallas_call(
        _ffn_kernel,
        grid=(n_tok // tm, n_h + 1),
        in_specs=[tok(D_MODEL), tok(Q_W), tok(B_HEADS * B_DIM), tok(Q_W),
                  pl.BlockSpec((1, 1, 6, D_MODEL), lambda i, j: (layer, slot_fn(i), 0, 0)),
                  layer_spec((1, D_MODEL)), layer_spec((D_MODEL, D_MODEL)), layer_spec((D_MODEL, 2 * LANES)),
                  layer_spec((1, LANES)),
                  pl.BlockSpec((D_MODEL, th), lambda i, j: (0, hidden(j))),
                  pl.BlockSpec((D_MODEL, th), lambda i, j: (0, hidden(j))),
                  pl.BlockSpec((th, D_MODEL), lambda i, j: (hidden(j), 0))],
        out_specs=tok(D_MODEL),
        out_shape=jax.ShapeDtypeStruct((n_tok, D_MODEL), F32),
        scratch_shapes=[pltpu.VMEM((tm, D_MODEL), BF16), pltpu.VMEM((tm, LANES), F32)],
        compiler_params=_params(("arbitrary", "arbitrary")),
        name="ffn",
    )(x, ma, mb, mc, mods, g, wo, wr, br, w1, w3, w2)


def _final_norm_kernel(x_ref, g_ref, o_ref):
    x = x_ref[...]
    o_ref[...] = x * lax.rsqrt(jnp.mean(x * x, axis=-1, keepdims=True) + EPS) * g_ref[...]


def _final_norm_call(x, g, tm, row0, n_rows):
    blk0 = row0 // tm
    return pl.pallas_call(
        _final_norm_kernel,
        grid=(n_rows // tm,),
        in_specs=[pl.BlockSpec((tm, D_MODEL), lambda i: (blk0 + i, 0)), pl.BlockSpec((1, D_MODEL), lambda i: (0, 0))],
        out_specs=pl.BlockSpec((tm, D_MODEL), lambda i: (i, 0)),
        out_shape=jax.ShapeDtypeStruct((n_rows, D_MODEL), F32),
        compiler_params=_params(("arbitrary",)),
        name="final_norm",
    )(x, g)


def _rope_tables(t):
    pos = np.arange(t)
    n_freq = HEAD_DIM // 4
    inv_freq = ROPE_THETA ** (-jnp.arange(n_freq, dtype=F32) / n_freq)
    row = jnp.asarray(pos // GRID_W, F32)
    col = jnp.asarray(pos % GRID_W, F32)
    ang = jnp.concatenate([row[:, None] * inv_freq, col[:, None] * inv_freq], -1)
    cos, sin = jnp.cos(ang), jnp.sin(ang)
    cos_t = jnp.tile(jnp.concatenate([cos, cos], -1), (1, LANES // HEAD_DIM))
    sin_t = jnp.tile(jnp.concatenate([-sin, sin], -1), (1, LANES // HEAD_DIM))
    return cos_t, sin_t


def _delta_tables():
    r = np.arange(PAIR)
    same = (r[:, None] // CHUNK) == (r[None, :] // CHUNK)
    low = same & (r[:, None] >= r[None, :])
    low_s = same & (r[:, None] > r[None, :])
    up = same & (r[:, None] <= r[None, :])
    up_s = same & (r[:, None] < r[None, :])
    levels = []
    for k in range(N_LEVELS):
        s = 1 << k
        levels.append(((r[:, None] // (2 * s)) == (r[None, :] // (2 * s))) & ((r[:, None] // s) != (r[None, :] // s)))
    masks = jnp.asarray(np.stack([low, low_s, up, up_s, np.eye(PAIR, dtype=bool)] + levels).astype(np.float32))
    return masks


def _segment_mean_table():
    r = np.arange(LANES)
    seg = ((r[:, None] // HEAD_DIM) == (r[None, :] // HEAD_DIM)).astype(np.float32) / HEAD_DIM
    hi = jnp.asarray(seg, BF16)
    lo = (jnp.asarray(seg) - hi.astype(F32)).astype(BF16)
    return jnp.stack([hi, lo])


def kernel(x_prompt, x_sample, cache_a_k, cache_a_v, cache_c_k, cache_c_v, state_b_fwd, state_b_bwd, c, c_ctx, w_mod, b_mod, norm1_g, norm2_g, w_in, a_sink, b_conv, b_a_log, b_dt_bias, b_norm_g, c_q_norm, c_k_norm, w_out, w_group, b_group, w_expert, b_expert, w1, w3, w2, final_norm_g):
    n_p, t_p, d = x_prompt.shape
    n_s, t_s, _ = x_sample.shape
    depth = w_in.shape[0]
    past = cache_a_k.shape[2]
    tok_p = n_p * t_p
    n_tok = tok_p + n_s * t_s
    assert d == D_MODEL and tok_p % t_s == 0 and t_s % max(TM_PROJ, TM_FFN) == 0 and t_p % 256 == 0

    w_in_t = jnp.swapaxes(w_in, 1, 2)
    w_out16 = w_out.astype(BF16)
    pad_g = jnp.zeros((depth, d, EXPERT_ROW0 - N_GROUPS), F32)
    pad_e = jnp.zeros((depth, d, LANES - EXPERT_ROW0 - N_EXPERTS), F32)
    w_r = jnp.concatenate([w_group, pad_g, w_expert, pad_e], -1)
    w_r_hi = w_r.astype(BF16)
    w_r2 = jnp.concatenate([w_r_hi, (w_r - w_r_hi.astype(F32)).astype(BF16)], axis=-1)
    b_r = jnp.concatenate([b_group, pad_g[:, 0], b_expert, pad_e[:, 0]], -1)[:, None, :]
    cqn = jnp.tile(c_q_norm, (1, 4))[:, None, :]
    ckn = jnp.tile(c_k_norm, (1, 2))[:, None, :]
    gate_prm = jnp.stack([b_a_log.reshape(depth, 8), b_dt_bias.reshape(depth, 8)], 1)
    prmr = jnp.broadcast_to(jnp.pad(gate_prm, ((0, 0), (0, 0), (0, N_AB - 8)))[..., None],
                            (depth, 2, N_AB, LANES))
    cos_t, sin_t = _rope_tables(t_s)
    masks = _delta_tables()
    seg = _segment_mean_table()

    cond = jnp.concatenate([c_ctx[None, :], c], axis=0)
    cond_b = jnp.broadcast_to(cond[:, :, None], cond.shape + (LANES,))
    mods_all = _mods_call(cond_b, w_mod, b_mod).reshape(depth, SUBLANES, 6, d)

    def slot_fn(tm):
        per_s = t_s // tm
        first = tok_p // tm
        return lambda i: jnp.where(i < first, 0, 1 + (i - first) // per_s)

    xs = (x_prompt.reshape(tok_p, d), x_sample.reshape(n_s * t_s, d))
    blk_s = tok_p // t_s
    ctx = tuple(a.reshape(n_s, depth, past, LANES) for a in (cache_a_k, cache_a_v, cache_c_k, cache_c_v))
    s0 = tuple(a.reshape(n_s, depth, B_HEADS * B_DIM, B_DIM) for a in (state_b_fwd, state_b_bwd))
    g1, g2, bng = norm1_g[:, None, :], norm2_g[:, None, :], b_norm_g[:, None, :]
    caches = None
    states = None
    for l in range(depth):
        za, zb, zc, zab, zabt, *rest = _inproj_call(l, xs, mods_all, g1, w_in_t, slot_fn(TM_PROJ), TM_PROJ,
                                                    experts=(w3,))
        x = rest[0] if len(xs) > 1 else xs[0]
        w3b = rest[-1]

        ao, co, *rest = _attn_call(False, t_p, ATTN_NSEQ, n_p, 0, l, za, zc, a_sink, cqn, ckn, seg, prev=caches,
                                   experts=(w1,))
        caches, (w1b,) = rest[:4], rest[4:]
        ao, co, w2b = _attn_call(True, t_s, 1, n_s, blk_s, l, za, zc, a_sink, cqn, ckn, seg, prev=(ao, co),
                                 rope=(cos_t, sin_t), ctx=ctx, experts=(w2,))

        bo, *states = _delta_call(False, t_p, DELTA_NSEQ, n_p, 0, l, zb, zab, zabt, b_conv, prmr, bng, masks,
                                  prev=states)
        (bo,) = _delta_call(True, t_s, 1, n_s, blk_s, l, zb, zab, zabt, b_conv, prmr, bng, masks,
                            prev=(bo,), s0=s0)

        x = _ffn_call(l, x, ao, bo, co, mods_all, g2, w_out16, w_r2, b_r, w1b, w3b, w2b, slot_fn(TM_FFN), TM_FFN, TH_FFN)
        xs = (x,)

    y_prompt = _final_norm_call(x, final_norm_g[None], TM_NORM, 0, tok_p).reshape(n_p, t_p, d)
    y_sample = _final_norm_call(x, final_norm_g[None], TM_NORM, tok_p, n_s * t_s).reshape(n_s, t_s, d)
    new_ak, new_av, new_ck, new_cv = (a.reshape(n_p, depth, t_p, 2, HEAD_DIM) for a in caches)
    new_sf, new_sb = (a.reshape(n_p, depth, B_HEADS, B_DIM, B_DIM) for a in states)
    return (y_prompt, y_sample, new_ak, new_av, new_ck, new_cv, new_sf, new_sb)
```

```python
import functools

import jax
import jax.numpy as jnp
import numpy as np
from jax import lax
from jax.experimental import pallas as pl
from jax.experimental.pallas import tpu as pltpu

F32 = jnp.float32
BF16 = jnp.bfloat16

D_MODEL = 1024
GRID_W = 64
EPS = 1e-6
NEG_INF = -1e30
ROPE_THETA = 10000.0
HEAD_DIM = 64
Q_W = 256
KV_W = 128
WINDOW = 128
Q_BLOCK = 128
B_HEADS = 4
B_DIM = 128
CHUNK = 64
BD = B_HEADS * CHUNK
PAIR = 2 * CHUNK
N_LEVELS = 6
PREP_UNROLL = 4
DELTA_NSEQ = 4
ATTN_NSEQ = 4
N_GROUPS = 4
EXPERTS_PER_GROUP = 4
N_EXPERTS = 16
D_EXPERT = 256
EXPERT_ROW0 = 8

LANES = 128
SUBLANES = 8
VMEM_LIMIT = 60000 * 1024

TM_PROJ = 512
TM_FFN, TH_FFN = 1024, 1024
TM_NORM = 512
CAST_STEPS = 16
MODS_TN = 1536

ZA_W, ZB_W, ZC_W, ZAB_W = 512, 2048, 512, 128
N_AB = 16
Z_W = ZA_W + ZB_W + ZC_W + ZAB_W


def _sigmoid(x):
    return 1.0 / (1.0 + jnp.exp(-x))


def _silu(x):
    return x * _sigmoid(x)


def _softplus(x):
    return jnp.maximum(x, 0.0) + jnp.log1p(jnp.exp(-jnp.abs(x)))


def _dot(a, b):
    return jnp.dot(a, b, preferred_element_type=F32)


def _dot_nt(a, b):
    return lax.dot_general(a, b, (((1,), (1,)), ((), ())), preferred_element_type=F32)


def _dot_tn(a, b):
    return lax.dot_general(a, b, (((0,), (0,)), ((), ())), preferred_element_type=F32)


def _split2(x):
    hi = x.astype(BF16)
    lo = (x - hi.astype(F32)).astype(BF16)
    return hi, lo


def _params(sem=None):
    return pltpu.CompilerParams(dimension_semantics=sem, vmem_limit_bytes=VMEM_LIMIT)


def _mods_kernel(cond_ref, w_ref, b_ref, o_ref, act_s):
    n_cond = cond_ref.shape[0]
    tn = w_ref.shape[2]
    reps = tn // LANES

    @pl.when((pl.program_id(0) == 0) & (pl.program_id(1) == 0))
    def _():
        act_s[...] = _silu(cond_ref[...])

    def body(kb, accs):
        r = pl.multiple_of(kb * SUBLANES, SUBLANES)
        w = w_ref[0, pl.ds(r, SUBLANES), :]
        return tuple(acc + jnp.tile(act_s[m, pl.ds(r, SUBLANES), :], (1, reps)) * w for m, acc in enumerate(accs))

    zero = jnp.zeros((SUBLANES, tn), F32)
    accs = lax.fori_loop(0, w_ref.shape[1] // SUBLANES, body, (zero,) * n_cond, unroll=4)
    rows = [jnp.sum(a, axis=0, keepdims=True) + b_ref[0] for a in accs]
    rows.append(jnp.zeros((SUBLANES - n_cond, tn), F32))
    o_ref[0] = jnp.concatenate(rows, axis=0)


def _mods_call(cond_b, w_mod, b_mod):
    depth, d, n = w_mod.shape
    tn = MODS_TN
    n_cond = cond_b.shape[0]
    return pl.pallas_call(
        _mods_kernel,
        grid=(depth, n // tn),
        in_specs=[
            pl.BlockSpec((n_cond, d, LANES), lambda l, j: (0, 0, 0)),
            pl.BlockSpec((1, d, tn), lambda l, j: (l, 0, j)),
            pl.BlockSpec((1, 1, tn), lambda l, j: (l, 0, j)),
        ],
        out_specs=pl.BlockSpec((1, SUBLANES, tn), lambda l, j: (l, 0, j)),
        out_shape=jax.ShapeDtypeStruct((depth, SUBLANES, n), F32),
        scratch_shapes=[pltpu.VMEM((n_cond, d, LANES), F32)],
        compiler_params=_params(("arbitrary", "arbitrary")),
        name="mods",
    )(cond_b, w_mod, b_mod.reshape(depth, 1, n))


def _x_specs(xs, tm):
    if len(xs) == 1:
        return [pl.BlockSpec((tm, D_MODEL), lambda i, *_: (i, 0))]
    first = xs[0].shape[0] // tm
    return [pl.BlockSpec((tm, D_MODEL), lambda i, *_: (jnp.minimum(i, first - 1), 0)),
            pl.BlockSpec((tm, D_MODEL), lambda i, *_: (jnp.maximum(i - first, 0), 0))]


def _x_tile(x_refs, first):
    if len(x_refs) == 1:
        return x_refs[0][...]
    return jnp.where(pl.program_id(0) < first, x_refs[0][...], x_refs[1][...])


def _modulated_norm(x, g, shift, scale):
    ms = jnp.mean(x * x, axis=-1, keepdims=True)
    y = x * lax.rsqrt(ms + EPS) * g
    return y * (1.0 + scale) + shift


def _inproj_kernel(n_x, first, n_w, *refs):
    x_refs = refs[:n_x]
    mod_ref, g_ref, wt_ref = refs[n_x:n_x + 3]
    w_refs = refs[n_x + 3:n_x + 3 + n_w]
    o0 = n_x + 3 + n_w
    za_ref, zb_ref, zc_ref, zab_ref, zabt_ref = refs[o0:o0 + 5]
    o1 = o0 + 5 + (1 if n_x > 1 else 0)
    wb_refs = refs[o1:o1 + n_w]
    w_s = refs[-1]
    for w_ref, wb_ref in zip(w_refs, wb_refs):
        wb_ref[...] = w_ref[0].astype(BF16)
    @pl.when(pl.program_id(0) == 0)
    def _():
        ab0 = ZA_W + ZB_W
        w_s[0:ab0, :] = wt_ref[0, 0:ab0, :].astype(BF16)
        w_s[ab0:ab0 + ZC_W, :] = wt_ref[0, ab0 + N_AB:ab0 + N_AB + ZC_W, :].astype(BF16)
        w_s[ab0 + ZC_W:ab0 + ZC_W + N_AB, :] = wt_ref[0, ab0:ab0 + N_AB, :].astype(BF16)
        w_s[ab0 + ZC_W + N_AB:Z_W, :] = jnp.zeros((ZAB_W - N_AB, D_MODEL), BF16)

    m = mod_ref[0, 0]
    x = _x_tile(x_refs, first)
    if n_x > 1:
        refs[o0 + 5][...] = x
    h = _modulated_norm(x, g_ref[0], m[0:1], m[1:2]).astype(BF16)
    za_ref[...] = _dot_nt(h, w_s[0:ZA_W, :])
    step = 512
    for j in range(ZB_W // step):
        zb_ref[:, j * step:(j + 1) * step] = _dot_nt(h, w_s[ZA_W + j * step:ZA_W + (j + 1) * step, :])
    zc_ref[...] = _dot_nt(h, w_s[ZA_W + ZB_W:ZA_W + ZB_W + ZC_W, :])
    zab = _dot_nt(h, w_s[ZA_W + ZB_W + ZC_W:Z_W, :])
    zab_ref[...] = zab
    zabt_ref[...] = zab.T[:N_AB]


def _inproj_call(layer, xs, mods, g, w, slot_fn, tm, experts=()):
    n_tok = sum(a.shape[0] for a in xs)
    n_ab = N_AB
    cast_steps = CAST_STEPS
    assert n_tok // tm >= cast_steps
    slab = lambda i: jnp.minimum(i, cast_steps - 1)
    cast_in = [pl.BlockSpec((1, e.shape[1] // cast_steps, e.shape[2]), lambda i: (layer, slab(i), 0)) for e in experts]
    cast_out = [pl.BlockSpec((e.shape[1] // cast_steps, e.shape[2]), lambda i: (slab(i), 0)) for e in experts]
    return pl.pallas_call(
        functools.partial(_inproj_kernel, len(xs), xs[0].shape[0] // tm, len(experts)),
        grid=(n_tok // tm,),
        in_specs=_x_specs(xs, tm) + [
            pl.BlockSpec((1, 1, 6, D_MODEL), lambda i: (layer, slot_fn(i), 0, 0)),
            pl.BlockSpec((1, 1, D_MODEL), lambda i: (layer, 0, 0)),
            pl.BlockSpec((1, w.shape[1], D_MODEL), lambda i: (layer, 0, 0)),
        ] + cast_in,
        out_specs=[
            pl.BlockSpec((tm, ZA_W), lambda i: (i, 0)),
            pl.BlockSpec((tm, ZB_W), lambda i: (i, 0)),
            pl.BlockSpec((tm, ZC_W), lambda i: (i, 0)),
            pl.BlockSpec((tm, ZAB_W), lambda i: (i, 0)),
            pl.BlockSpec((n_ab, tm), lambda i: (0, i)),
        ] + ([pl.BlockSpec((tm, D_MODEL), lambda i: (i, 0))] if len(xs) > 1 else []) + cast_out,
        out_shape=[
            jax.ShapeDtypeStruct((n_tok, ZA_W), F32),
            jax.ShapeDtypeStruct((n_tok, ZB_W), F32),
            jax.ShapeDtypeStruct((n_tok, ZC_W), F32),
            jax.ShapeDtypeStruct((n_tok, ZAB_W), F32),
            jax.ShapeDtypeStruct((n_ab, n_tok), F32),
        ] + ([jax.ShapeDtypeStruct((n_tok, D_MODEL), F32)] if len(xs) > 1 else [])
        + [jax.ShapeDtypeStruct(e.shape[1:], BF16) for e in experts],
        scratch_shapes=[pltpu.VMEM((Z_W, D_MODEL), BF16)],
        compiler_params=_params(("arbitrary",)),
        name="inproj",
    )(*xs, mods, g, w, *experts)


def _lane_lo(shape):
    return lax.broadcasted_iota(jnp.int32, shape, len(shape) - 1) % LANES < HEAD_DIM


def _store_kdup(dst_ref, off, k):
    n = k.shape[0]
    r = pltpu.roll(k, HEAD_DIM, 1)
    lo = _lane_lo(k.shape)
    dst_ref[0, off:off + n, :] = jnp.where(lo, k, r).astype(BF16)
    dst_ref[1, off:off + n, :] = jnp.where(lo, r, k).astype(BF16)


def _store_vsplit(dst_ref, off, v):
    n = v.shape[0]
    r = pltpu.roll(v, HEAD_DIM, 1)
    lo = _lane_lo(v.shape)
    z = jnp.zeros_like(v)
    dst_ref[0, off:off + n, :] = jnp.where(lo, v, z).astype(BF16)
    dst_ref[1, off:off + n, :] = jnp.where(lo, z, r).astype(BF16)
    dst_ref[2, off:off + n, :] = jnp.where(lo, r, z).astype(BF16)
    dst_ref[3, off:off + n, :] = jnp.where(lo, z, v).astype(BF16)


def _rope(x, cos, sin):
    first = (lax.broadcasted_iota(jnp.int32, x.shape, 1) // (HEAD_DIM // 2)) % 2 == 0
    partner = jnp.where(first, pltpu.roll(x, LANES - HEAD_DIM // 2, 1), pltpu.roll(x, HEAD_DIM // 2, 1))
    return x * cos + partner * sin


def _head_rmsnorm(x, g, seg_hi, seg_lo):
    hi, lo = _split2(x * x)
    ms = _dot(hi, seg_hi) + _dot(lo, seg_hi) + _dot(hi, seg_lo)
    return x * lax.rsqrt(ms + EPS) * g


def _attend_many(units):
    qb = units[0][0].shape[0]
    lo = _lane_lo(units[0][0].shape)
    all_scores = []
    for qt, segs, _ in units:
        z = jnp.zeros_like(qt)
        qs = jnp.concatenate([jnp.where(lo, qt, z), jnp.where(lo, z, qt)], axis=0).astype(BF16)
        scores = []
        for kdup, _, _, mask in segs:
            s = _dot_nt(qs, kdup)
            if mask is not None:
                s = jnp.where(mask, s, NEG_INF)
            scores.append(s)
        all_scores.append(scores)
    probs = []
    for (qt, segs, sink_pair), scores in zip(units, all_scores):
        m = scores[0].max(axis=1, keepdims=True)
        for s in scores[1:]:
            m = jnp.maximum(m, s.max(axis=1, keepdims=True))
        if sink_pair is not None:
            row_a = lax.broadcasted_iota(jnp.int32, (2 * qb, 1), 0) < qb
            sink = jnp.where(row_a, sink_pair[0], sink_pair[1])
            m = jnp.maximum(m, sink)
            denom = jnp.exp(sink - m)
        else:
            denom = jnp.zeros((2 * qb, 1), F32)
        ps = []
        for s in scores:
            p = jnp.exp(s - m)
            denom = denom + p.sum(axis=1, keepdims=True)
            ps.append(p.astype(BF16))
        probs.append((ps, 1.0 / denom))
    outs = []
    for (qt, segs, _), (ps, inv) in zip(units, probs):
        acc = jnp.zeros((qb, LANES), F32)
        for pb, (_, vlo, vhi, _) in zip(ps, segs):
            acc = acc + _dot(pb[:qb], vlo) + _dot(pb[qb:], vhi)
        outs.append(acc * jnp.where(lo, inv[:qb], inv[qb:]))
    return outs


def _attn_kernel(has_ctx, t, nseq, layer, *refs):
    if has_ctx:
        (sink_ref, za_ref, zc_ref, cqn_ref, ckn_ref, seg_ref, cos_ref, sin_ref,
         cak_ref, cav_ref, cck_ref, ccv_ref,
         ao_ref, co_ref,
         ka_s, va_s, kc_s, vc_s, kctx_s, vctx_s, qa_s, qc_s) = refs
    else:
        (sink_ref, za_ref, zc_ref, cqn_ref, ckn_ref, seg_ref, w1_ref, wo_ref,
         ao_ref, co_ref, nak_ref, nav_ref, nck_ref, ncv_ref, w1b_ref, wob_ref,
         ka_s, va_s, kc_s, vc_s, qa_s, qc_s) = refs
        w1b_ref[...] = w1_ref[0].astype(BF16)
        wob_ref[...] = wo_ref[0].astype(BF16)
    scale = HEAD_DIM ** -0.5
    seg_hi = seg_ref[0]
    seg_lo = seg_ref[1]
    piece = 256
    n_ctx = cak_ref.shape[2] if has_ctx else 0

    for p0 in range(0, nseq * t, piece):
        rows = slice(p0, p0 + piece)
        ak = za_ref[rows, Q_W:Q_W + KV_W]
        av = za_ref[rows, Q_W + KV_W:Q_W + 2 * KV_W]
        ck = _head_rmsnorm(zc_ref[rows, Q_W:Q_W + KV_W], ckn_ref[0], seg_hi, seg_lo)
        cv = zc_ref[rows, Q_W + KV_W:Q_W + 2 * KV_W]
        if has_ctx:
            cos = cos_ref[rows, :]
            sin = sin_ref[rows, :]
            ak = _rope(ak, cos, sin)
            ck = _rope(ck, cos, sin)
            _store_kdup(ka_s, WINDOW + p0, ak)
            _store_vsplit(va_s, WINDOW + p0, av)
            _store_kdup(kc_s, n_ctx + p0, ck)
            _store_vsplit(vc_s, n_ctx + p0, cv)
        else:
            crow = slice(p0 % t, p0 % t + piece)
            nak_ref[p0 // t, 0, crow, :] = ak
            nav_ref[p0 // t, 0, crow, :] = av
            nck_ref[p0 // t, 0, crow, :] = ck
            ncv_ref[p0 // t, 0, crow, :] = cv
            _store_kdup(ka_s, p0, ak)
            _store_vsplit(va_s, p0, av)
            _store_kdup(kc_s, p0, ck)
            _store_vsplit(vc_s, p0, cv)
        for hk in range(2):
            cols = slice(hk * LANES, (hk + 1) * LANES)
            aq = za_ref[rows, cols]
            cq = _head_rmsnorm(zc_ref[rows, cols], cqn_ref[0, :, cols], seg_hi, seg_lo)
            if has_ctx:
                aq = _rope(aq, cos, sin)
                cq = _rope(cq, cos, sin)
            qa_s[rows, cols] = aq * scale
            qc_s[rows, cols] = cq * scale

    if has_ctx:
        zpad = jnp.zeros((WINDOW, LANES), BF16)
        for i in range(2):
            ka_s[i, 0:WINDOW, :] = zpad
            ka_s[i, WINDOW + t:2 * WINDOW + t, :] = zpad
        for i in range(4):
            va_s[i, 0:WINDOW, :] = zpad
            va_s[i, WINDOW + t:2 * WINDOW + t, :] = zpad
        for p0 in range(0, n_ctx, piece):
            rows = slice(p0, p0 + piece)
            _store_kdup(kctx_s, p0, cak_ref[0, 0, rows, :])
            _store_vsplit(vctx_s, p0, cav_ref[0, 0, rows, :])
            _store_kdup(kc_s, p0, cck_ref[0, 0, rows, :])
            _store_vsplit(vc_s, p0, ccv_ref[0, 0, rows, :])

        qb = Q_BLOCK
        span = qb + 2 * WINDOW
        qi = lax.broadcasted_iota(jnp.int32, (2 * qb, span), 0) % qb
        kj = lax.broadcasted_iota(jnp.int32, (2 * qb, span), 1)
        band = jnp.abs(kj - WINDOW - qi) <= WINDOW

        def block(b, carry):
            r0 = pl.multiple_of(b * qb, qb)
            kpos = kj + (r0 - WINDOW)
            mask = band & (kpos >= 0) & (kpos < t)
            units = []
            for hk in range(2):
                cols = slice(hk * LANES, (hk + 1) * LANES)
                segs_a = [
                    (kctx_s[hk], vctx_s[2 * hk], vctx_s[2 * hk + 1], None),
                    (ka_s[hk, pl.ds(r0, span), :], va_s[2 * hk, pl.ds(r0, span), :],
                     va_s[2 * hk + 1, pl.ds(r0, span), :], mask),
                ]
                sinks = (sink_ref[layer, 2 * hk], sink_ref[layer, 2 * hk + 1])
                units.append((qa_s[pl.ds(r0, qb), cols], segs_a, sinks))
                segs_c = [(kc_s[hk], vc_s[2 * hk], vc_s[2 * hk + 1], None)]
                units.append((qc_s[pl.ds(r0, qb), cols], segs_c, None))
            outs = _attend_many(units)
            for hk in range(2):
                cols = slice(hk * LANES, (hk + 1) * LANES)
                ao_ref[pl.ds(r0, qb), cols] = outs[2 * hk].astype(BF16)
                co_ref[pl.ds(r0, qb), cols] = outs[2 * hk + 1].astype(BF16)
            return carry

        lax.fori_loop(0, t // qb, block, 0)
    else:
        units = []
        for q in range(nseq):
            seq = slice(q * t, (q + 1) * t)
            for hk in range(2):
                cols = slice(hk * LANES, (hk + 1) * LANES)
                sinks = (sink_ref[layer, 2 * hk], sink_ref[layer, 2 * hk + 1])
                units.append((qa_s[seq, cols],
                              [(ka_s[hk, seq, :], va_s[2 * hk, seq, :], va_s[2 * hk + 1, seq, :], None)], sinks))
                units.append((qc_s[seq, cols],
                              [(kc_s[hk, seq, :], vc_s[2 * hk, seq, :], vc_s[2 * hk + 1, seq, :], None)], None))
        outs = _attend_many(units)
        for q in range(nseq):
            seq = slice(q * t, (q + 1) * t)
            for hk in range(2):
                cols = slice(hk * LANES, (hk + 1) * LANES)
                ao_ref[seq, cols] = outs[4 * q + 2 * hk].astype(BF16)
                co_ref[seq, cols] = outs[4 * q + 2 * hk + 1].astype(BF16)


def _attn_call(has_ctx, t, nseq, n_batch, row_block0, layer, za, zc, sink, cqn, ckn, seg, prev=None, rope=None,
               ctx=None, experts=None):
    n_tok = za.shape[0]
    depth = sink.shape[0]
    assert n_batch % nseq == 0 and (nseq == 1 or not has_ctx)
    tok_spec = lambda w: pl.BlockSpec((nseq * t, w), lambda b, *_: (row_block0 + b, 0))
    const = lambda shape: pl.BlockSpec(shape, lambda b, *_: (0,) * len(shape))
    layer_spec = lambda shape: pl.BlockSpec((1,) + shape, lambda b, *_: (layer,) + (0,) * len(shape))
    in_specs = [tok_spec(ZA_W), tok_spec(ZC_W), layer_spec((1, Q_W)), layer_spec((1, KV_W)), const((2, LANES, LANES))]
    args = [za, zc, cqn, ckn, seg]
    out_specs = [tok_spec(Q_W), tok_spec(Q_W)]
    out_shape = [jax.ShapeDtypeStruct((n_tok, Q_W), BF16), jax.ShapeDtypeStruct((n_tok, Q_W), BF16)]
    if has_ctx:
        n_ctx = ctx[0].shape[2]
        in_specs += [const((t, LANES)), const((t, LANES))]
        args += list(rope)
        in_specs += [pl.BlockSpec((1, 1, n_ctx, LANES), lambda b, *_: (b, layer, 0, 0))] * 4
        args += list(ctx)
        scratch = [
            pltpu.VMEM((2, t + 2 * WINDOW, LANES), BF16), pltpu.VMEM((4, t + 2 * WINDOW, LANES), BF16),
            pltpu.VMEM((2, n_ctx + t, LANES), BF16), pltpu.VMEM((4, n_ctx + t, LANES), BF16),
            pltpu.VMEM((2, n_ctx, LANES), BF16), pltpu.VMEM((4, n_ctx, LANES), BF16),
            pltpu.VMEM((t, Q_W), F32), pltpu.VMEM((t, Q_W), F32),
        ]
    else:
        cache_spec = pl.BlockSpec((nseq, 1, t, LANES), lambda b, *_: (b, layer, 0, 0))
        out_specs += [cache_spec] * 4
        out_shape += [jax.ShapeDtypeStruct((n_batch, depth, t, LANES), F32)] * 4
        n_steps = n_batch // nseq
        for w in experts:
            rows_w = w.shape[1] // n_steps
            in_specs.append(pl.BlockSpec((1, rows_w, w.shape[2]), lambda b, *_: (layer, b, 0)))
            args.append(w)
            out_specs.append(pl.BlockSpec((rows_w, w.shape[2]), lambda b, *_: (b, 0)))
            out_shape.append(jax.ShapeDtypeStruct(w.shape[1:], BF16))
        rows = nseq * t
        scratch = [
            pltpu.VMEM((2, rows, LANES), BF16), pltpu.VMEM((4, rows, LANES), BF16),
            pltpu.VMEM((2, rows, LANES), BF16), pltpu.VMEM((4, rows, LANES), BF16),
            pltpu.VMEM((rows, Q_W), F32), pltpu.VMEM((rows, Q_W), F32),
        ]
    n_real = len(args)
    aliases = {}
    if prev is not None:
        first_out = 0 if has_ctx else 2
        for k, arr in enumerate(prev):
            in_specs.append(pl.BlockSpec(memory_space=pl.ANY))
            args.append(arr)
            aliases[1 + n_real + k] = first_out + k

    def body(*refs):
        ins = refs[:1 + n_real]
        rest = refs[1 + len(args):]
        _attn_kernel(has_ctx, t, nseq, layer, *ins, *rest)

    return pl.pallas_call(
        body,
        grid_spec=pltpu.PrefetchScalarGridSpec(
            num_scalar_prefetch=1, grid=(n_batch // nseq,), in_specs=in_specs, out_specs=out_specs,
            scratch_shapes=scratch),
        out_shape=out_shape,
        input_output_aliases=aliases,
        compiler_params=_params(("arbitrary",)),
        name="attn_latent" if has_ctx else "attn_prompt",
    )(sink, *args)


def _stack_pair(x, p):
    return jnp.concatenate([x[:, (2 * p + hl) * B_DIM:(2 * p + hl + 1) * B_DIM] for hl in range(2)], axis=0)


def _delta_kernel(t, nseq, has_s0, *refs):
    if has_s0:
        (zb_ref, abc_ref, abt_ref, conv_ref, prmr_ref, bng_ref, mask_ref,
         s0f_ref, s0b_ref, o_ref, qkv_s, of_s, ob_s, sf_s, sb_s, u_s, wq_s, at_s, kd_s, eg_s,
         pre_s, suf_s, prec_s, sufc_s) = refs
    else:
        (zb_ref, abc_ref, abt_ref, conv_ref, prmr_ref, bng_ref, mask_ref,
         o_ref, sfo_ref, sbo_ref, qkv_s, of_s, ob_s, sf_s, sb_s, u_s, wq_s, at_s, kd_s, eg_s,
         pre_s, suf_s, prec_s, sufc_s) = refs
    n_chunks = t // CHUNK
    n_total = nseq * n_chunks
    s_rows = B_HEADS * B_DIM
    qk_w = B_HEADS * B_DIM

    row = lax.broadcasted_iota(jnp.int32, (t, LANES), 0)
    for q in range(nseq):
        seq = slice(q * t, (q + 1) * t)
        for j in range(3 * B_HEADS):
            cols = slice(j * LANES, (j + 1) * LANES)
            x = zb_ref[seq, cols]
            prev = jnp.where(row == 0, 0.0, pltpu.roll(x, 1, 0))
            nxt = jnp.where(row == t - 1, 0.0, pltpu.roll(x, t - 1, 0))
            y = _silu(prev * conv_ref[0, 0:1, cols] + x * conv_ref[0, 1:2, cols] + nxt * conv_ref[0, 2:3, cols])
            if j < 2 * B_HEADS:
                y = y * lax.rsqrt(jnp.sum(y * y, axis=-1, keepdims=True) + EPS)
            if j < B_HEADS:
                y = y * (B_DIM ** -0.5)
            qkv_s[seq, cols] = y

    if has_s0:
        for q in range(nseq):
            sf_s[q * s_rows:(q + 1) * s_rows, :] = s0f_ref[q, 0]
            sb_s[q * s_rows:(q + 1) * s_rows, :] = s0b_ref[q, 0]
    else:
        sf_s[...] = jnp.zeros_like(sf_s)
        sb_s[...] = jnp.zeros_like(sb_s)

    reps = nseq * t // LANES
    gr = -jnp.tile(jnp.exp(prmr_ref[0, 0]), (1, reps)) * _softplus(abt_ref[...] + jnp.tile(prmr_ref[0, 1], (1, reps)))
    seg_lane = lax.broadcasted_iota(jnp.int32, gr.shape, 1) % CHUNK
    pre, suf = gr, gr
    for s in (1, 2, 4, 8, 16, 32):
        pre = pre + jnp.where(seg_lane >= s, pltpu.roll(pre, s, 1), 0.0)
        suf = suf + jnp.where(seg_lane < CHUNK - s, pltpu.roll(suf, nseq * t - s, 1), 0.0)
    pre_s[...] = pre
    suf_s[...] = suf
    zrows = jnp.zeros((LANES - pre.shape[0], LANES), F32)
    for j in range(reps):
        tile = slice(j * LANES, (j + 1) * LANES)
        prec_s[tile, :] = jnp.concatenate([pre[:, tile], zrows], axis=0).T
        sufc_s[tile, :] = jnp.concatenate([suf[:, tile], zrows], axis=0).T
    lane_lo = lax.broadcasted_iota(jnp.int32, (1, LANES), 1) < CHUNK

    def prepare(cc, carry):
        chains = []
        for k in range(PREP_UNROLL):
            c = cc * PREP_UNROLL + k
            r0 = pl.multiple_of(c * CHUNK, CHUNK)
            b_all = _sigmoid(abc_ref[pl.ds(r0, CHUNK), :])
            run_c = (prec_s[pl.ds(r0, CHUNK), :], sufc_s[pl.ds(r0, CHUNK), :])
            tile0 = pl.multiple_of((cc * PREP_UNROLL + k - k % 2) * CHUNK, LANES)
            run = (pre_s[:, pl.ds(tile0, LANES)], suf_s[:, pl.ds(tile0, LANES)])
            run_r = tuple(pltpu.roll(x, CHUNK, 1) for x in run)
            for p in range(B_HEADS // 2):
                kst = _stack_pair(qkv_s[pl.ds(r0, CHUNK), qk_w:2 * qk_w], p)
                qst = _stack_pair(qkv_s[pl.ds(r0, CHUNK), 0:qk_w], p)
                vst = _stack_pair(qkv_s[pl.ds(r0, CHUNK), 2 * qk_w:3 * qk_w], p)
                kq = _dot_nt(jnp.concatenate([kst, qst], axis=0).astype(BF16), kst.astype(BF16))
                for d in range(2):
                    cg = 4 * d + 2 * p
                    edge = CHUNK - 1 if d == 0 else 0
                    rep_col = lambda x, col: jnp.broadcast_to(x[:, col:col + 1], (CHUNK, LANES))
                    b_rep = jnp.concatenate([rep_col(b_all, 8 + cg + hl) for hl in range(2)], axis=0)
                    gcol = jnp.concatenate([rep_col(run_c[d], cg + hl) for hl in range(2)], axis=0)
                    gtot = jnp.concatenate([rep_col(run_c[d][edge:edge + 1], cg + hl) for hl in range(2)], axis=0)
                    ra = cg
                    if k % 2 == 0:
                        grow = jnp.where(lane_lo, run[d][ra:ra + 1], run_r[d][ra + 1:ra + 2])
                    else:
                        grow = jnp.where(lane_lo, run_r[d][ra:ra + 1], run[d][ra + 1:ra + 2])
                    chains.append(dict(c=c, p=p, d=d, kst=kst, qst=qst, vst=vst, kq=kq, b_st=b_rep,
                                       gcol=gcol, gtot=gtot, grow=grow))

        for ch in chains:
            d, b_st, kq, gcol = ch["d"], ch["b_st"], ch.pop("kq"), ch["gcol"]
            decay = jnp.exp(jnp.minimum(gcol - ch.pop("grow"), 0.0))
            ch["a_mat"] = (b_st * kq[:PAIR]) * (decay * mask_ref[2 * d + 1])
            ch["attn"] = (kq[PAIR:] * (decay * mask_ref[2 * d])).astype(BF16)
            ch["t_inv"] = mask_ref[4] - ch["a_mat"] * mask_ref[5]
        for lvl in range(N_LEVELS - 1):
            for ch in chains:
                ch["t16"] = ch["t_inv"].astype(BF16)
                ch["et"] = _dot((ch["a_mat"] * mask_ref[6 + lvl]).astype(BF16), ch["t16"])
            for ch in chains:
                ch["t_inv"] = ch["t_inv"] - _dot(ch.pop("t16"), ch.pop("et").astype(BF16))
        for ch in chains:
            egc = jnp.exp(ch["gcol"])
            rk = jnp.concatenate([ch["b_st"] * ch["vst"], (ch["b_st"] * egc) * ch["kst"]], axis=1)
            ch["rk"] = _dot(ch.pop("t_inv").astype(BF16), rk.astype(BF16))
            ch["qp16"] = (ch["qst"] * egc).astype(BF16)
        for ch in chains:
            c, p, d, rk, qp16 = ch["c"], ch["p"], ch["d"], ch["rk"], ch["qp16"]
            pair_rows = slice(p * PAIR, (p + 1) * PAIR)
            w16 = rk[:, B_DIM:].astype(BF16)
            u_s[d, c, pair_rows, :] = rk[:, :B_DIM]
            at_s[d, c, p] = ch["attn"]
            kd_s[d, c, pair_rows, :] = (ch["kst"] * jnp.exp(ch["gtot"] - ch["gcol"])).astype(BF16)
            eg = jnp.exp(ch["gtot"])
            for hl in range(2):
                h = 2 * p + hl
                rows = slice(hl * CHUNK, (hl + 1) * CHUNK)
                wq_s[d, c, h * 2 * CHUNK:h * 2 * CHUNK + CHUNK, :] = w16[rows]
                wq_s[d, c, h * 2 * CHUNK + CHUNK:(h + 1) * 2 * CHUNK, :] = qp16[rows]
                eg_s[d, c, h * SUBLANES:(h + 1) * SUBLANES, :] = eg[hl * CHUNK:hl * CHUNK + SUBLANES, :]
        return carry

    lax.fori_loop(0, n_total // PREP_UNROLL, prepare, 0)

    def scan_step(i, carry):
        units = []
        for q in range(nseq):
            for d, s_ref, o_s in ((0, sf_s, of_s), (1, sb_s, ob_s)):
                c = q * n_chunks + (i if d == 0 else n_chunks - 1 - i)
                units.append(dict(q=q, d=d, c=c, s_ref=s_ref, o_s=o_s, r0=pl.multiple_of(c * CHUNK, CHUNK)))
        for un in units:
            q, d, c, s_ref = un["q"], un["d"], un["c"], un["s_ref"]
            un["x"] = []
            for h in range(B_HEADS):
                srows = slice(q * s_rows + h * B_DIM, q * s_rows + (h + 1) * B_DIM)
                un["x"].append(_dot(wq_s[d, c, h * 2 * CHUNK:(h + 1) * 2 * CHUNK, :], s_ref[srows, :].astype(BF16)))
        for un in units:
            d, c = un["d"], un["c"]
            un["vp16"], un["o"] = [], []
            for p in range(B_HEADS // 2):
                xs = un["x"][2 * p:2 * p + 2]
                v_new = jnp.concatenate(
                    [u_s[d, c, (2 * p + hl) * CHUNK:(2 * p + hl + 1) * CHUNK, :] - xs[hl][:CHUNK] for hl in range(2)],
                    axis=0)
                vp16 = v_new.astype(BF16)
                un["vp16"].append(vp16)
                un["o"].append(jnp.concatenate([xs[hl][CHUNK:] for hl in range(2)], axis=0)
                               + _dot(at_s[d, c, p], vp16))
        for un in units:
            q, d, c, s_ref, o_s, r0 = un["q"], un["d"], un["c"], un["s_ref"], un["o_s"], un["r0"]
            for h in range(B_HEADS):
                p, hl = divmod(h, 2)
                rows = slice(hl * CHUNK, (hl + 1) * CHUNK)
                srows = slice(q * s_rows + h * B_DIM, q * s_rows + (h + 1) * B_DIM)
                upd = _dot_tn(kd_s[d, c, h * CHUNK:(h + 1) * CHUNK, :], un["vp16"][p][rows])
                eg = jnp.tile(eg_s[d, c, h * SUBLANES:(h + 1) * SUBLANES, :], (B_DIM // SUBLANES, 1))
                s_ref[srows, :] = s_ref[srows, :] * eg + upd
                o_s[pl.ds(r0, CHUNK), h * B_DIM:(h + 1) * B_DIM] = un["o"][p][rows]
        return carry

    lax.fori_loop(0, n_chunks, scan_step, 0)

    if not has_s0:
        for q in range(nseq):
            sfo_ref[q, 0] = sf_s[q * s_rows:(q + 1) * s_rows, :]
            sbo_ref[q, 0] = sb_s[q * s_rows:(q + 1) * s_rows, :]

    for h in range(B_HEADS):
        cols = slice(h * B_DIM, (h + 1) * B_DIM)
        x = of_s[:, cols] + ob_s[:, cols]
        yn = x * lax.rsqrt(jnp.mean(x * x, axis=-1, keepdims=True) + EPS) * bng_ref[0]
        o_ref[:, cols] = (yn * _silu(zb_ref[:, 3 * qk_w + h * B_DIM:3 * qk_w + (h + 1) * B_DIM])).astype(BF16)


def _delta_call(has_s0, t, nseq, n_batch, row_block0, layer, zb, zab, zabt, conv, prmr, bng, masks,
                prev=None, s0=None):
    n_tok = zb.shape[0]
    depth = conv.shape[0]
    n_chunks = nseq * (t // CHUNK)
    assert n_chunks % PREP_UNROLL == 0 and PREP_UNROLL % 2 == 0 and n_batch % nseq == 0
    tok_spec = lambda w: pl.BlockSpec((nseq * t, w), lambda b: (row_block0 + b, 0))
    const = lambda shape: pl.BlockSpec(shape, lambda b: (0,) * len(shape))
    layer_spec = lambda shape: pl.BlockSpec((1,) + shape, lambda b: (layer,) + (0,) * len(shape))
    s_shape = (B_HEADS * B_DIM, B_DIM)
    s_spec = pl.BlockSpec((nseq, 1) + s_shape, lambda b: (b, layer, 0, 0))
    n_ab = zabt.shape[0]
    in_specs = [
        tok_spec(ZB_W), tok_spec(ZAB_W),
        pl.BlockSpec((n_ab, nseq * t), lambda b: (0, row_block0 + b)),
        layer_spec((3, 3 * B_HEADS * B_DIM)), layer_spec((2, n_ab, LANES)),
        layer_spec((1, B_DIM)),
        const((5 + N_LEVELS, PAIR, PAIR)),
    ]
    args = [zb, zab, zabt, conv, prmr, bng, masks]
    out_specs = [tok_spec(B_HEADS * B_DIM)]
    out_shape = [jax.ShapeDtypeStruct((n_tok, B_HEADS * B_DIM), BF16)]
    if has_s0:
        in_specs += [s_spec, s_spec]
        args += [s0[0], s0[1]]
    else:
        out_specs += [s_spec, s_spec]
        out_shape += [jax.ShapeDtypeStruct((n_batch, depth) + s_shape, F32)] * 2
    n_real = len(args)
    aliases = {}
    if prev is not None:
        first_out = 0 if has_s0 else 1
        for k, arr in enumerate(prev):
            in_specs.append(pl.BlockSpec(memory_space=pl.ANY))
            args.append(arr)
            aliases[n_real + k] = first_out + k
    rows = nseq * t
    scratch = [
        pltpu.VMEM((rows, 3 * B_HEADS * B_DIM), F32),
        pltpu.VMEM((rows, B_HEADS * B_DIM), F32), pltpu.VMEM((rows, B_HEADS * B_DIM), F32),
        pltpu.VMEM((nseq * s_shape[0], B_DIM), F32), pltpu.VMEM((nseq * s_shape[0], B_DIM), F32),
        pltpu.VMEM((2, n_chunks, BD, B_DIM), F32),
        pltpu.VMEM((2, n_chunks, 2 * BD, B_DIM), BF16),
        pltpu.VMEM((2, n_chunks, B_HEADS // 2, PAIR, PAIR), BF16),
        pltpu.VMEM((2, n_chunks, BD, B_DIM), BF16),
        pltpu.VMEM((2, n_chunks, B_HEADS * SUBLANES, LANES), F32),
        pltpu.VMEM((n_ab, rows), F32), pltpu.VMEM((n_ab, rows), F32),
        pltpu.VMEM((rows, LANES), F32), pltpu.VMEM((rows, LANES), F32),
    ]

    def body(*refs):
        _delta_kernel(t, nseq, has_s0, *refs[:n_real], *refs[len(args):])

    return pl.pallas_call(
        body,
        grid=(n_batch // nseq,),
        in_specs=in_specs,
        out_specs=out_specs,
        out_shape=out_shape,
        scratch_shapes=scratch,
        input_output_aliases=aliases,
        compiler_params=_params(("arbitrary",)),
        name="delta_latent" if has_s0 else "delta_prompt",
    )(*args)


def _outproj_router(x, ma, mb, mc, m, g, wo_ref, wr_ref, br):
    n = x.shape[0]
    b0, c0 = Q_W, Q_W + B_HEADS * B_DIM
    y = (_dot(ma.astype(BF16), wo_ref[0, 0:b0, :])
         + _dot(mb.astype(BF16), wo_ref[0, b0:c0, :])
         + _dot(mc.astype(BF16), wo_ref[0, c0:c0 + Q_W, :]))
    x1 = x + m[2:3] * y
    h2 = _modulated_norm(x1, g, m[3:4], m[4:5])
    hi, lo = _split2(h2)

    hw = _dot(hi, wr_ref[0])
    logits = (hw[:, :LANES] + hw[:, LANES:] + _dot(lo, wr_ref[0, :, :LANES]) + br).T
    gl = logits[0:N_GROUPS]
    grow = lax.broadcasted_iota(jnp.int32, gl.shape, 0)
    gmax = gl.max(axis=0, keepdims=True)
    g_sel = jnp.where(gl == gmax, grow, N_GROUPS).min(axis=0, keepdims=True)
    g_w = 1.0 / jnp.exp(gl - gmax).sum(axis=0, keepdims=True)
    el = logits[EXPERT_ROW0:EXPERT_ROW0 + N_EXPERTS]
    e_idx = lax.broadcasted_iota(jnp.int32, el.shape, 0)
    el = jnp.where((e_idx // EXPERTS_PER_GROUP) == g_sel, el, -jnp.inf)
    m1 = el.max(axis=0, keepdims=True)
    i1 = jnp.where(el == m1, e_idx, N_EXPERTS).min(axis=0, keepdims=True)
    el2 = jnp.where(e_idx == i1, -jnp.inf, el)
    m2 = el2.max(axis=0, keepdims=True)
    i2 = jnp.where(el2 == m2, e_idx, N_EXPERTS).min(axis=0, keepdims=True)
    tt = jnp.exp(m2 - m1)
    w1 = g_w / (1.0 + tt)
    w2 = w1 * tt
    gate_t = jnp.where(e_idx == i1, w1, 0.0) + jnp.where(e_idx == i2, w2, 0.0)
    gate = jnp.concatenate([gate_t, jnp.zeros((LANES - N_EXPERTS, n), F32)], axis=0).T
    return x1, hi, gate


def _ffn_kernel(x_ref, ma_ref, mb_ref, mc_ref, mod_ref, g_ref, wo_ref, wr_ref, br_ref, w1_ref, w3_ref, w2_ref,
                o_ref, h_s, gate_s):
    j = pl.program_id(1)
    tm = x_ref.shape[0]
    th = w1_ref.shape[1]
    m = mod_ref[0, 0]

    @pl.when(j == 0)
    def _():
        x1, hi, gate = _outproj_router(x_ref[...], ma_ref[...], mb_ref[...], mc_ref[...],
                                       m, g_ref[0], wo_ref, wr_ref, br_ref[0])
        o_ref[...] = x1
        h_s[...] = hi
        gate_s[...] = gate

    @pl.when(j > 0)
    def _():
        h = h_s[...]
        hid = _silu(_dot(h, w1_ref[...])) * _dot(h, w3_ref[...])
        gate = gate_s[...]
        lane = lax.broadcasted_iota(jnp.int32, gate.shape, 1)
        n_e = th // D_EXPERT
        col = lax.broadcasted_iota(jnp.int32, hid.shape, 1) // D_EXPERT
        gmat = jnp.zeros(hid.shape, F32)
        for e in range(n_e):
            ge = jnp.where(lane == (j - 1) * n_e + e, gate, 0.0).sum(axis=1, keepdims=True)
            gmat = jnp.where(col == e, ge, gmat)
        o_ref[...] += m[5:6] * _dot((hid * gmat).astype(BF16), w2_ref[...])


def _ffn_call(layer, x, ma, mb, mc, mods, g, wo, wr, br, w1, w3, w2, slot_fn, tm, th):
    n_tok = x.shape[0]
    n_h = w1.shape[1] // th
    hidden = lambda j: jnp.where(j == 0, n_h - 1, j - 1)
    tok = lambda w: pl.BlockSpec((tm, w), lambda i, j: (i, 0))
    layer_spec = lambda shape: pl.BlockSpec((1,) + shape, lambda i, j: (layer,) + (0,) * len(shape))
    return pl.pallas_call(
        _ffn_kernel,
        grid=(n_tok // tm, n_h + 1),
        in_specs=[tok(D_MODEL), tok(Q_W), tok(B_HEADS * B_DIM), tok(Q_W),
                  pl.BlockSpec((1, 1, 6, D_MODEL), lambda i, j: (layer, slot_fn(i), 0, 0)),
                  layer_spec((1, D_MODEL)), pl.BlockSpec((1, D_MODEL, D_MODEL), lambda i, j: (0, 0, 0)),
                  layer_spec((D_MODEL, 2 * LANES)),
                  layer_spec((1, LANES)),
                  pl.BlockSpec((D_MODEL, th), lambda i, j: (0, hidden(j))),
                  pl.BlockSpec((D_MODEL, th), lambda i, j: (0, hidden(j))),
                  pl.BlockSpec((th, D_MODEL), lambda i, j: (hidden(j), 0))],
        out_specs=tok(D_MODEL),
        out_shape=jax.ShapeDtypeStruct((n_tok, D_MODEL), F32),
        scratch_shapes=[pltpu.VMEM((tm, D_MODEL), BF16), pltpu.VMEM((tm, LANES), F32)],
        compiler_params=_params(("arbitrary", "arbitrary")),
        name="ffn",
    )(x, ma, mb, mc, mods, g, wo, wr, br, w1, w3, w2)


def _final_norm_kernel(x_ref, g_ref, o_ref):
    x = x_ref[...]
    o_ref[...] = x * lax.rsqrt(jnp.mean(x * x, axis=-1, keepdims=True) + EPS) * g_ref[...]


def _final_norm_call(x, g, tm, row0, n_rows):
    blk0 = row0 // tm
    return pl.pallas_call(
        _final_norm_kernel,
        grid=(n_rows // tm,),
        in_specs=[pl.BlockSpec((tm, D_MODEL), lambda i: (blk0 + i, 0)), pl.BlockSpec((1, D_MODEL), lambda i: (0, 0))],
        out_specs=pl.BlockSpec((tm, D_MODEL), lambda i: (i, 0)),
        out_shape=jax.ShapeDtypeStruct((n_rows, D_MODEL), F32),
        compiler_params=_params(("arbitrary",)),
        name="final_norm",
    )(x, g)


def _rope_tables(t):
    pos = np.arange(t)
    n_freq = HEAD_DIM // 4
    inv_freq = ROPE_THETA ** (-jnp.arange(n_freq, dtype=F32) / n_freq)
    row = jnp.asarray(pos // GRID_W, F32)
    col = jnp.asarray(pos % GRID_W, F32)
    ang = jnp.concatenate([row[:, None] * inv_freq, col[:, None] * inv_freq], -1)
    cos, sin = jnp.cos(ang), jnp.sin(ang)
    cos_t = jnp.tile(jnp.concatenate([cos, cos], -1), (1, LANES // HEAD_DIM))
    sin_t = jnp.tile(jnp.concatenate([-sin, sin], -1), (1, LANES // HEAD_DIM))
    return cos_t, sin_t


def _delta_tables():
    r = np.arange(PAIR)
    same = (r[:, None] // CHUNK) == (r[None, :] // CHUNK)
    low = same & (r[:, None] >= r[None, :])
    low_s = same & (r[:, None] > r[None, :])
    up = same & (r[:, None] <= r[None, :])
    up_s = same & (r[:, None] < r[None, :])
    levels = []
    for k in range(N_LEVELS):
        s = 1 << k
        levels.append(((r[:, None] // (2 * s)) == (r[None, :] // (2 * s))) & ((r[:, None] // s) != (r[None, :] // s)))
    masks = jnp.asarray(np.stack([low, low_s, up, up_s, np.eye(PAIR, dtype=bool)] + levels).astype(np.float32))
    return masks


def _segment_mean_table():
    r = np.arange(LANES)
    seg = ((r[:, None] // HEAD_DIM) == (r[None, :] // HEAD_DIM)).astype(np.float32) / HEAD_DIM
    hi = jnp.asarray(seg, BF16)
    lo = (jnp.asarray(seg) - hi.astype(F32)).astype(BF16)
    return jnp.stack([hi, lo])


def kernel(x_prompt, x_sample, cache_a_k, cache_a_v, cache_c_k, cache_c_v, state_b_fwd, state_b_bwd, c, c_ctx, w_mod, b_mod, norm1_g, norm2_g, w_in, a_sink, b_conv, b_a_log, b_dt_bias, b_norm_g, c_q_norm, c_k_norm, w_out, w_group, b_group, w_expert, b_expert, w1, w3, w2, final_norm_g):
    n_p, t_p, d = x_prompt.shape
    n_s, t_s, _ = x_sample.shape
    depth = w_in.shape[0]
    past = cache_a_k.shape[2]
    tok_p = n_p * t_p
    n_tok = tok_p + n_s * t_s
    assert d == D_MODEL and tok_p % t_s == 0 and t_s % max(TM_PROJ, TM_FFN) == 0 and t_p % 256 == 0

    w_in_t = jnp.swapaxes(w_in, 1, 2)
    pad_g = jnp.zeros((depth, d, EXPERT_ROW0 - N_GROUPS), F32)
    pad_e = jnp.zeros((depth, d, LANES - EXPERT_ROW0 - N_EXPERTS), F32)
    w_r = jnp.concatenate([w_group, pad_g, w_expert, pad_e], -1)
    w_r_hi = w_r.astype(BF16)
    w_r2 = jnp.concatenate([w_r_hi, (w_r - w_r_hi.astype(F32)).astype(BF16)], axis=-1)
    b_r = jnp.concatenate([b_group, pad_g[:, 0], b_expert, pad_e[:, 0]], -1)[:, None, :]
    cqn = jnp.tile(c_q_norm, (1, 4))[:, None, :]
    ckn = jnp.tile(c_k_norm, (1, 2))[:, None, :]
    gate_prm = jnp.stack([b_a_log.reshape(depth, 8), b_dt_bias.reshape(depth, 8)], 1)
    prmr = jnp.broadcast_to(jnp.pad(gate_prm, ((0, 0), (0, 0), (0, N_AB - 8)))[..., None],
                            (depth, 2, N_AB, LANES))
    cos_t, sin_t = _rope_tables(t_s)
    masks = _delta_tables()
    seg = _segment_mean_table()

    cond = jnp.concatenate([c_ctx[None, :], c], axis=0)
    cond_b = jnp.broadcast_to(cond[:, :, None], cond.shape + (LANES,))
    mods_all = _mods_call(cond_b, w_mod, b_mod).reshape(depth, SUBLANES, 6, d)

    def slot_fn(tm):
        per_s = t_s // tm
        first = tok_p // tm
        return lambda i: jnp.where(i < first, 0, 1 + (i - first) // per_s)

    xs = (x_prompt.reshape(tok_p, d), x_sample.reshape(n_s * t_s, d))
    blk_s = tok_p // t_s
    ctx = tuple(a.reshape(n_s, depth, past, LANES) for a in (cache_a_k, cache_a_v, cache_c_k, cache_c_v))
    s0 = tuple(a.reshape(n_s, depth, B_HEADS * B_DIM, B_DIM) for a in (state_b_fwd, state_b_bwd))
    g1, g2, bng = norm1_g[:, None, :], norm2_g[:, None, :], b_norm_g[:, None, :]
    caches = None
    states = None
    for l in range(depth):
        za, zb, zc, zab, zabt, *rest = _inproj_call(l, xs, mods_all, g1, w_in_t, slot_fn(TM_PROJ), TM_PROJ,
                                                    experts=(w3, w2))
        x = rest[0] if len(xs) > 1 else xs[0]
        w3b, w2b = rest[-2:]

        ao, co, *rest = _attn_call(False, t_p, ATTN_NSEQ, n_p, 0, l, za, zc, a_sink, cqn, ckn, seg, prev=caches,
                                   experts=(w1, w_out))
        caches, (w1b, wob) = rest[:4], rest[4:]
        ao, co = _attn_call(True, t_s, 1, n_s, blk_s, l, za, zc, a_sink, cqn, ckn, seg, prev=(ao, co),
                            rope=(cos_t, sin_t), ctx=ctx)

        bo, *states = _delta_call(False, t_p, DELTA_NSEQ, n_p, 0, l, zb, zab, zabt, b_conv, prmr, bng, masks,
                                  prev=states)
        (bo,) = _delta_call(True, t_s, 1, n_s, blk_s, l, zb, zab, zabt, b_conv, prmr, bng, masks,
                            prev=(bo,), s0=s0)

        x = _ffn_call(l, x, ao, bo, co, mods_all, g2, wob[None], w_r2, b_r, w1b, w3b, w2b, slot_fn(TM_FFN), TM_FFN, TH_FFN)
        xs = (x,)

    y_prompt = _final_norm_call(x, final_norm_g[None], TM_NORM, 0, tok_p).reshape(n_p, t_p, d)
    y_sample = _final_norm_call(x, final_norm_g[None], TM_NORM, tok_p, n_s * t_s).reshape(n_s, t_s, d)
    new_ak, new_av, new_ck, new_cv = (a.reshape(n_p, depth, t_p, 2, HEAD_DIM) for a in caches)
    new_sf, new_sb = (a.reshape(n_p, depth, B_HEADS, B_DIM, B_DIM) for a in states)
    return (y_prompt, y_sample, new_ak, new_av, new_ck, new_cv, new_sf, new_sb)
```

```python
import functools

import jax
import jax.numpy as jnp
import numpy as np
from jax import lax
from jax.experimental import pallas as pl
from jax.experimental.pallas import tpu as pltpu

F32 = jnp.float32
BF16 = jnp.bfloat16

D_MODEL = 1024
GRID_W = 64
EPS = 1e-6
NEG_INF = -1e30
ROPE_THETA = 10000.0
HEAD_DIM = 64
Q_W = 256
KV_W = 128
WINDOW = 128
Q_BLOCK = 128
B_HEADS = 4
B_DIM = 128
CHUNK = 64
BD = B_HEADS * CHUNK
PAIR = 2 * CHUNK
N_LEVELS = 6
PREP_UNROLL = 4
DELTA_NSEQ = 4
ATTN_NSEQ = 4
N_GROUPS = 4
EXPERTS_PER_GROUP = 4
N_EXPERTS = 16
D_EXPERT = 256
EXPERT_ROW0 = 8

LANES = 128
SUBLANES = 8
VMEM_LIMIT = 60000 * 1024

TM_PROJ = 512
TM_FFN, TH_FFN = 1024, 1024
TM_NORM = 512
CAST_STEPS = 16
MODS_TN = 1536

ZA_W, ZB_W, ZC_W, ZAB_W = 512, 2048, 512, 128
N_AB = 16
Z_W = ZA_W + ZB_W + ZC_W + ZAB_W


def _sigmoid(x):
    return 0.5 * jnp.tanh(0.5 * x) + 0.5


def _silu(x):
    return x * _sigmoid(x)


def _softplus(x):
    return jnp.maximum(x, 0.0) + jnp.log1p(jnp.exp(-jnp.abs(x)))


def _dot(a, b):
    return jnp.dot(a, b, preferred_element_type=F32)


def _dot_nt(a, b):
    return lax.dot_general(a, b, (((1,), (1,)), ((), ())), preferred_element_type=F32)


def _dot_tn(a, b):
    return lax.dot_general(a, b, (((0,), (0,)), ((), ())), preferred_element_type=F32)


def _split2(x):
    hi = x.astype(BF16)
    lo = (x - hi.astype(F32)).astype(BF16)
    return hi, lo


def _params(sem=None):
    return pltpu.CompilerParams(dimension_semantics=sem, vmem_limit_bytes=VMEM_LIMIT)


def _mods_kernel(cond_ref, w_ref, b_ref, o_ref, act_s):
    n_cond = cond_ref.shape[0]
    tn = w_ref.shape[2]
    reps = tn // LANES

    @pl.when((pl.program_id(0) == 0) & (pl.program_id(1) == 0))
    def _():
        act_s[...] = _silu(cond_ref[...])

    def body(kb, accs):
        r = pl.multiple_of(kb * SUBLANES, SUBLANES)
        w = w_ref[0, pl.ds(r, SUBLANES), :]
        return tuple(acc + jnp.tile(act_s[m, pl.ds(r, SUBLANES), :], (1, reps)) * w for m, acc in enumerate(accs))

    zero = jnp.zeros((SUBLANES, tn), F32)
    accs = lax.fori_loop(0, w_ref.shape[1] // SUBLANES, body, (zero,) * n_cond, unroll=4)
    rows = [jnp.sum(a, axis=0, keepdims=True) + b_ref[0] for a in accs]
    rows.append(jnp.zeros((SUBLANES - n_cond, tn), F32))
    o_ref[0] = jnp.concatenate(rows, axis=0)


def _mods_call(cond_b, w_mod, b_mod):
    depth, d, n = w_mod.shape
    tn = MODS_TN
    n_cond = cond_b.shape[0]
    return pl.pallas_call(
        _mods_kernel,
        grid=(depth, n // tn),
        in_specs=[
            pl.BlockSpec((n_cond, d, LANES), lambda l, j: (0, 0, 0)),
            pl.BlockSpec((1, d, tn), lambda l, j: (l, 0, j)),
            pl.BlockSpec((1, 1, tn), lambda l, j: (l, 0, j)),
        ],
        out_specs=pl.BlockSpec((1, SUBLANES, tn), lambda l, j: (l, 0, j)),
        out_shape=jax.ShapeDtypeStruct((depth, SUBLANES, n), F32),
        scratch_shapes=[pltpu.VMEM((n_cond, d, LANES), F32)],
        compiler_params=_params(("arbitrary", "arbitrary")),
        name="mods",
    )(cond_b, w_mod, b_mod.reshape(depth, 1, n))


def _x_specs(xs, tm):
    if len(xs) == 1:
        return [pl.BlockSpec((tm, D_MODEL), lambda i, *_: (i, 0))]
    first = xs[0].shape[0] // tm
    return [pl.BlockSpec((tm, D_MODEL), lambda i, *_: (jnp.minimum(i, first - 1), 0)),
            pl.BlockSpec((tm, D_MODEL), lambda i, *_: (jnp.maximum(i - first, 0), 0))]


def _x_tile(x_refs, first):
    if len(x_refs) == 1:
        return x_refs[0][...]
    return jnp.where(pl.program_id(0) < first, x_refs[0][...], x_refs[1][...])


def _modulated_norm(x, g, shift, scale):
    ms = jnp.mean(x * x, axis=-1, keepdims=True)
    y = x * lax.rsqrt(ms + EPS) * g
    return y * (1.0 + scale) + shift


def _inproj_kernel(n_x, first, n_w, *refs):
    x_refs = refs[:n_x]
    mod_ref, g_ref, wt_ref = refs[n_x:n_x + 3]
    w_refs = refs[n_x + 3:n_x + 3 + n_w]
    o0 = n_x + 3 + n_w
    za_ref, zb_ref, zc_ref, zab_ref, zabt_ref = refs[o0:o0 + 5]
    o1 = o0 + 5 + (1 if n_x > 1 else 0)
    wb_refs = refs[o1:o1 + n_w]
    w_s = refs[-1]
    for w_ref, wb_ref in zip(w_refs, wb_refs):
        wb_ref[...] = w_ref[0].astype(BF16)
    @pl.when(pl.program_id(0) == 0)
    def _():
        ab0 = ZA_W + ZB_W
        w_s[0:ab0, :] = wt_ref[0, 0:ab0, :].astype(BF16)
        w_s[ab0:ab0 + ZC_W, :] = wt_ref[0, ab0 + N_AB:ab0 + N_AB + ZC_W, :].astype(BF16)
        w_s[ab0 + ZC_W:ab0 + ZC_W + N_AB, :] = wt_ref[0, ab0:ab0 + N_AB, :].astype(BF16)
        w_s[ab0 + ZC_W + N_AB:Z_W, :] = jnp.zeros((ZAB_W - N_AB, D_MODEL), BF16)

    m = mod_ref[0, 0]
    x = _x_tile(x_refs, first)
    if n_x > 1:
        refs[o0 + 5][...] = x
    h = _modulated_norm(x, g_ref[0], m[0:1], m[1:2]).astype(BF16)
    za_ref[...] = _dot_nt(h, w_s[0:ZA_W, :])
    step = 512
    for j in range(ZB_W // step):
        zb_ref[:, j * step:(j + 1) * step] = _dot_nt(h, w_s[ZA_W + j * step:ZA_W + (j + 1) * step, :])
    zc_ref[...] = _dot_nt(h, w_s[ZA_W + ZB_W:ZA_W + ZB_W + ZC_W, :])
    zab = _dot_nt(h, w_s[ZA_W + ZB_W + ZC_W:Z_W, :])
    zab_ref[...] = zab
    zabt_ref[...] = zab.T[:N_AB]


def _inproj_call(layer, xs, mods, g, w, slot_fn, tm, experts=()):
    n_tok = sum(a.shape[0] for a in xs)
    n_ab = N_AB
    cast_steps = CAST_STEPS
    assert n_tok // tm >= cast_steps
    slab = lambda i: jnp.minimum(i, cast_steps - 1)
    cast_in = [pl.BlockSpec((1, e.shape[1] // cast_steps, e.shape[2]), lambda i: (layer, slab(i), 0)) for e in experts]
    cast_out = [pl.BlockSpec((e.shape[1] // cast_steps, e.shape[2]), lambda i: (slab(i), 0)) for e in experts]
    return pl.pallas_call(
        functools.partial(_inproj_kernel, len(xs), xs[0].shape[0] // tm, len(experts)),
        grid=(n_tok // tm,),
        in_specs=_x_specs(xs, tm) + [
            pl.BlockSpec((1, 1, 6, D_MODEL), lambda i: (layer, slot_fn(i), 0, 0)),
            pl.BlockSpec((1, 1, D_MODEL), lambda i: (layer, 0, 0)),
            pl.BlockSpec((1, w.shape[1], D_MODEL), lambda i: (layer, 0, 0)),
        ] + cast_in,
        out_specs=[
            pl.BlockSpec((tm, ZA_W), lambda i: (i, 0)),
            pl.BlockSpec((tm, ZB_W), lambda i: (i, 0)),
            pl.BlockSpec((tm, ZC_W), lambda i: (i, 0)),
            pl.BlockSpec((tm, ZAB_W), lambda i: (i, 0)),
            pl.BlockSpec((n_ab, tm), lambda i: (0, i)),
        ] + ([pl.BlockSpec((tm, D_MODEL), lambda i: (i, 0))] if len(xs) > 1 else []) + cast_out,
        out_shape=[
            jax.ShapeDtypeStruct((n_tok, ZA_W), F32),
            jax.ShapeDtypeStruct((n_tok, ZB_W), F32),
            jax.ShapeDtypeStruct((n_tok, ZC_W), F32),
            jax.ShapeDtypeStruct((n_tok, ZAB_W), F32),
            jax.ShapeDtypeStruct((n_ab, n_tok), F32),
        ] + ([jax.ShapeDtypeStruct((n_tok, D_MODEL), F32)] if len(xs) > 1 else [])
        + [jax.ShapeDtypeStruct(e.shape[1:], BF16) for e in experts],
        scratch_shapes=[pltpu.VMEM((Z_W, D_MODEL), BF16)],
        compiler_params=_params(("arbitrary",)),
        name="inproj",
    )(*xs, mods, g, w, *experts)


def _lane_lo(shape):
    return lax.broadcasted_iota(jnp.int32, shape, len(shape) - 1) % LANES < HEAD_DIM


def _store_kdup(dst_ref, off, k):
    n = k.shape[0]
    r = pltpu.roll(k, HEAD_DIM, 1)
    lo = _lane_lo(k.shape)
    dst_ref[0, off:off + n, :] = jnp.where(lo, k, r).astype(BF16)
    dst_ref[1, off:off + n, :] = jnp.where(lo, r, k).astype(BF16)


def _store_vsplit(dst_ref, off, v):
    n = v.shape[0]
    r = pltpu.roll(v, HEAD_DIM, 1)
    lo = _lane_lo(v.shape)
    z = jnp.zeros_like(v)
    dst_ref[0, off:off + n, :] = jnp.where(lo, v, z).astype(BF16)
    dst_ref[1, off:off + n, :] = jnp.where(lo, z, r).astype(BF16)
    dst_ref[2, off:off + n, :] = jnp.where(lo, r, z).astype(BF16)
    dst_ref[3, off:off + n, :] = jnp.where(lo, z, v).astype(BF16)


def _rope(x, cos, sin):
    first = (lax.broadcasted_iota(jnp.int32, x.shape, 1) // (HEAD_DIM // 2)) % 2 == 0
    partner = jnp.where(first, pltpu.roll(x, LANES - HEAD_DIM // 2, 1), pltpu.roll(x, HEAD_DIM // 2, 1))
    return x * cos + partner * sin


def _head_rmsnorm(x, g, seg_hi, seg_lo):
    hi, lo = _split2(x * x)
    ms = _dot(hi, seg_hi) + _dot(lo, seg_hi) + _dot(hi, seg_lo)
    return x * lax.rsqrt(ms + EPS) * g


def _attend_many(units):
    qb = units[0][0].shape[0]
    lo = _lane_lo(units[0][0].shape)
    all_scores = []
    for qt, segs, _ in units:
        z = jnp.zeros_like(qt)
        qs = jnp.concatenate([jnp.where(lo, qt, z), jnp.where(lo, z, qt)], axis=0).astype(BF16)
        scores = []
        for kdup, _, _, mask in segs:
            s = _dot_nt(qs, kdup)
            if mask is not None:
                s = jnp.where(mask, s, NEG_INF)
            scores.append(s)
        all_scores.append(scores)
    probs = []
    for (qt, segs, sink_pair), scores in zip(units, all_scores):
        m = scores[0].max(axis=1, keepdims=True)
        for s in scores[1:]:
            m = jnp.maximum(m, s.max(axis=1, keepdims=True))
        if sink_pair is not None:
            row_a = lax.broadcasted_iota(jnp.int32, (2 * qb, 1), 0) < qb
            sink = jnp.where(row_a, sink_pair[0], sink_pair[1])
            m = jnp.maximum(m, sink)
            denom = jnp.exp(sink - m)
        else:
            denom = jnp.zeros((2 * qb, 1), F32)
        ps = []
        for s in scores:
            p = jnp.exp(s - m)
            denom = denom + p.sum(axis=1, keepdims=True)
            ps.append(p.astype(BF16))
        probs.append((ps, 1.0 / denom))
    outs = []
    for (qt, segs, _), (ps, inv) in zip(units, probs):
        acc = jnp.zeros((qb, LANES), F32)
        for pb, (_, vlo, vhi, _) in zip(ps, segs):
            acc = acc + _dot(pb[:qb], vlo) + _dot(pb[qb:], vhi)
        outs.append(acc * jnp.where(lo, inv[:qb], inv[qb:]))
    return outs


def _attn_kernel(has_ctx, t, nseq, layer, *refs):
    if has_ctx:
        (sink_ref, za_ref, zc_ref, cqn_ref, ckn_ref, seg_ref, cos_ref, sin_ref,
         cak_ref, cav_ref, cck_ref, ccv_ref,
         ao_ref, co_ref,
         ka_s, va_s, kc_s, vc_s, kctx_s, vctx_s, qa_s, qc_s) = refs
    else:
        (sink_ref, za_ref, zc_ref, cqn_ref, ckn_ref, seg_ref, w1_ref,
         ao_ref, co_ref, nak_ref, nav_ref, nck_ref, ncv_ref, w1b_ref,
         ka_s, va_s, kc_s, vc_s, qa_s, qc_s) = refs
        w1b_ref[...] = w1_ref[0].astype(BF16)
    scale = HEAD_DIM ** -0.5
    seg_hi = seg_ref[0]
    seg_lo = seg_ref[1]
    piece = 256
    n_ctx = cak_ref.shape[2] if has_ctx else 0

    for p0 in range(0, nseq * t, piece):
        rows = slice(p0, p0 + piece)
        ak = za_ref[rows, Q_W:Q_W + KV_W]
        av = za_ref[rows, Q_W + KV_W:Q_W + 2 * KV_W]
        ck = _head_rmsnorm(zc_ref[rows, Q_W:Q_W + KV_W], ckn_ref[0], seg_hi, seg_lo)
        cv = zc_ref[rows, Q_W + KV_W:Q_W + 2 * KV_W]
        if has_ctx:
            cos = cos_ref[rows, :]
            sin = sin_ref[rows, :]
            ak = _rope(ak, cos, sin)
            ck = _rope(ck, cos, sin)
            _store_kdup(ka_s, WINDOW + p0, ak)
            _store_vsplit(va_s, WINDOW + p0, av)
            _store_kdup(kc_s, n_ctx + p0, ck)
            _store_vsplit(vc_s, n_ctx + p0, cv)
        else:
            crow = slice(p0 % t, p0 % t + piece)
            nak_ref[p0 // t, 0, crow, :] = ak
            nav_ref[p0 // t, 0, crow, :] = av
            nck_ref[p0 // t, 0, crow, :] = ck
            ncv_ref[p0 // t, 0, crow, :] = cv
            _store_kdup(ka_s, p0, ak)
            _store_vsplit(va_s, p0, av)
            _store_kdup(kc_s, p0, ck)
            _store_vsplit(vc_s, p0, cv)
        for hk in range(2):
            cols = slice(hk * LANES, (hk + 1) * LANES)
            aq = za_ref[rows, cols]
            cq = _head_rmsnorm(zc_ref[rows, cols], cqn_ref[0, :, cols], seg_hi, seg_lo)
            if has_ctx:
                aq = _rope(aq, cos, sin)
                cq = _rope(cq, cos, sin)
            qa_s[rows, cols] = aq * scale
            qc_s[rows, cols] = cq * scale

    if has_ctx:
        zpad = jnp.zeros((WINDOW, LANES), BF16)
        for i in range(2):
            ka_s[i, 0:WINDOW, :] = zpad
            ka_s[i, WINDOW + t:2 * WINDOW + t, :] = zpad
        for i in range(4):
            va_s[i, 0:WINDOW, :] = zpad
            va_s[i, WINDOW + t:2 * WINDOW + t, :] = zpad
        for p0 in range(0, n_ctx, piece):
            rows = slice(p0, p0 + piece)
            _store_kdup(kctx_s, p0, cak_ref[0, 0, rows, :])
            _store_vsplit(vctx_s, p0, cav_ref[0, 0, rows, :])
            _store_kdup(kc_s, p0, cck_ref[0, 0, rows, :])
            _store_vsplit(vc_s, p0, ccv_ref[0, 0, rows, :])

        qb = Q_BLOCK
        span = qb + 2 * WINDOW
        qi = lax.broadcasted_iota(jnp.int32, (2 * qb, span), 0) % qb
        kj = lax.broadcasted_iota(jnp.int32, (2 * qb, span), 1)
        band = jnp.abs(kj - WINDOW - qi) <= WINDOW

        def block(b, carry):
            r0 = pl.multiple_of(b * qb, qb)
            kpos = kj + (r0 - WINDOW)
            mask = band & (kpos >= 0) & (kpos < t)
            units = []
            for hk in range(2):
                cols = slice(hk * LANES, (hk + 1) * LANES)
                segs_a = [
                    (kctx_s[hk], vctx_s[2 * hk], vctx_s[2 * hk + 1], None),
                    (ka_s[hk, pl.ds(r0, span), :], va_s[2 * hk, pl.ds(r0, span), :],
                     va_s[2 * hk + 1, pl.ds(r0, span), :], mask),
                ]
                sinks = (sink_ref[layer, 2 * hk], sink_ref[layer, 2 * hk + 1])
                units.append((qa_s[pl.ds(r0, qb), cols], segs_a, sinks))
                segs_c = [(kc_s[hk], vc_s[2 * hk], vc_s[2 * hk + 1], None)]
                units.append((qc_s[pl.ds(r0, qb), cols], segs_c, None))
            outs = _attend_many(units)
            for hk in range(2):
                cols = slice(hk * LANES, (hk + 1) * LANES)
                ao_ref[pl.ds(r0, qb), cols] = outs[2 * hk].astype(BF16)
                co_ref[pl.ds(r0, qb), cols] = outs[2 * hk + 1].astype(BF16)
            return carry

        lax.fori_loop(0, t // qb, block, 0)
    else:
        units = []
        for q in range(nseq):
            seq = slice(q * t, (q + 1) * t)
            for hk in range(2):
                cols = slice(hk * LANES, (hk + 1) * LANES)
                sinks = (sink_ref[layer, 2 * hk], sink_ref[layer, 2 * hk + 1])
                units.append((qa_s[seq, cols],
                              [(ka_s[hk, seq, :], va_s[2 * hk, seq, :], va_s[2 * hk + 1, seq, :], None)], sinks))
                units.append((qc_s[seq, cols],
                              [(kc_s[hk, seq, :], vc_s[2 * hk, seq, :], vc_s[2 * hk + 1, seq, :], None)], None))
        outs = _attend_many(units)
        for q in range(nseq):
            seq = slice(q * t, (q + 1) * t)
            for hk in range(2):
                cols = slice(hk * LANES, (hk + 1) * LANES)
                ao_ref[seq, cols] = outs[4 * q + 2 * hk].astype(BF16)
                co_ref[seq, cols] = outs[4 * q + 2 * hk + 1].astype(BF16)


def _attn_call(has_ctx, t, nseq, n_batch, row_block0, layer, za, zc, sink, cqn, ckn, seg, prev=None, rope=None,
               ctx=None, experts=None):
    n_tok = za.shape[0]
    depth = sink.shape[0]
    assert n_batch % nseq == 0 and (nseq == 1 or not has_ctx)
    tok_spec = lambda w: pl.BlockSpec((nseq * t, w), lambda b, *_: (row_block0 + b, 0))
    const = lambda shape: pl.BlockSpec(shape, lambda b, *_: (0,) * len(shape))
    layer_spec = lambda shape: pl.BlockSpec((1,) + shape, lambda b, *_: (layer,) + (0,) * len(shape))
    in_specs = [tok_spec(ZA_W), tok_spec(ZC_W), layer_spec((1, Q_W)), layer_spec((1, KV_W)), const((2, LANES, LANES))]
    args = [za, zc, cqn, ckn, seg]
    out_specs = [tok_spec(Q_W), tok_spec(Q_W)]
    out_shape = [jax.ShapeDtypeStruct((n_tok, Q_W), BF16), jax.ShapeDtypeStruct((n_tok, Q_W), BF16)]
    if has_ctx:
        n_ctx = ctx[0].shape[2]
        in_specs += [const((t, LANES)), const((t, LANES))]
        args += list(rope)
        in_specs += [pl.BlockSpec((1, 1, n_ctx, LANES), lambda b, *_: (b, layer, 0, 0))] * 4
        args += list(ctx)
        scratch = [
            pltpu.VMEM((2, t + 2 * WINDOW, LANES), BF16), pltpu.VMEM((4, t + 2 * WINDOW, LANES), BF16),
            pltpu.VMEM((2, n_ctx + t, LANES), BF16), pltpu.VMEM((4, n_ctx + t, LANES), BF16),
            pltpu.VMEM((2, n_ctx, LANES), BF16), pltpu.VMEM((4, n_ctx, LANES), BF16),
            pltpu.VMEM((t, Q_W), F32), pltpu.VMEM((t, Q_W), F32),
        ]
    else:
        cache_spec = pl.BlockSpec((nseq, 1, t, LANES), lambda b, *_: (b, layer, 0, 0))
        out_specs += [cache_spec] * 4
        out_shape += [jax.ShapeDtypeStruct((n_batch, depth, t, LANES), F32)] * 4
        n_steps = n_batch // nseq
        for w in experts:
            rows_w = w.shape[1] // n_steps
            in_specs.append(pl.BlockSpec((1, rows_w, w.shape[2]), lambda b, *_: (layer, b, 0)))
            args.append(w)
            out_specs.append(pl.BlockSpec((rows_w, w.shape[2]), lambda b, *_: (b, 0)))
            out_shape.append(jax.ShapeDtypeStruct(w.shape[1:], BF16))
        rows = nseq * t
        scratch = [
            pltpu.VMEM((2, rows, LANES), BF16), pltpu.VMEM((4, rows, LANES), BF16),
            pltpu.VMEM((2, rows, LANES), BF16), pltpu.VMEM((4, rows, LANES), BF16),
            pltpu.VMEM((rows, Q_W), F32), pltpu.VMEM((rows, Q_W), F32),
        ]
    n_real = len(args)
    aliases = {}
    if prev is not None:
        first_out = 0 if has_ctx else 2
        for k, arr in enumerate(prev):
            in_specs.append(pl.BlockSpec(memory_space=pl.ANY))
            args.append(arr)
            aliases[1 + n_real + k] = first_out + k

    def body(*refs):
        ins = refs[:1 + n_real]
        rest = refs[1 + len(args):]
        _attn_kernel(has_ctx, t, nseq, layer, *ins, *rest)

    return pl.pallas_call(
        body,
        grid_spec=pltpu.PrefetchScalarGridSpec(
            num_scalar_prefetch=1, grid=(n_batch // nseq,), in_specs=in_specs, out_specs=out_specs,
            scratch_shapes=scratch),
        out_shape=out_shape,
        input_output_aliases=aliases,
        compiler_params=_params(("arbitrary",)),
        name="attn_latent" if has_ctx else "attn_prompt",
    )(sink, *args)


def _stack_pair(x, p):
    return jnp.concatenate([x[:, (2 * p + hl) * B_DIM:(2 * p + hl + 1) * B_DIM] for hl in range(2)], axis=0)


def _delta_kernel(t, nseq, has_s0, *refs):
    if has_s0:
        (zb_ref, abc_ref, abt_ref, conv_ref, prmr_ref, bng_ref, mask_ref,
         s0f_ref, s0b_ref, o_ref, qkv_s, of_s, ob_s, sf_s, sb_s, u_s, wq_s, at_s, kd_s, eg_s,
         pre_s, suf_s, prec_s, sufc_s) = refs
    else:
        (zb_ref, abc_ref, abt_ref, conv_ref, prmr_ref, bng_ref, mask_ref,
         o_ref, sfo_ref, sbo_ref, qkv_s, of_s, ob_s, sf_s, sb_s, u_s, wq_s, at_s, kd_s, eg_s,
         pre_s, suf_s, prec_s, sufc_s) = refs
    n_chunks = t // CHUNK
    n_total = nseq * n_chunks
    s_rows = B_HEADS * B_DIM
    qk_w = B_HEADS * B_DIM

    row = lax.broadcasted_iota(jnp.int32, (t, LANES), 0)
    for q in range(nseq):
        seq = slice(q * t, (q + 1) * t)
        for j in range(3 * B_HEADS):
            cols = slice(j * LANES, (j + 1) * LANES)
            x = zb_ref[seq, cols]
            prev = jnp.where(row == 0, 0.0, pltpu.roll(x, 1, 0))
            nxt = jnp.where(row == t - 1, 0.0, pltpu.roll(x, t - 1, 0))
            y = _silu(prev * conv_ref[0, 0:1, cols] + x * conv_ref[0, 1:2, cols] + nxt * conv_ref[0, 2:3, cols])
            if j < 2 * B_HEADS:
                y = y * lax.rsqrt(jnp.sum(y * y, axis=-1, keepdims=True) + EPS)
            if j < B_HEADS:
                y = y * (B_DIM ** -0.5)
            qkv_s[seq, cols] = y

    if has_s0:
        for q in range(nseq):
            sf_s[q * s_rows:(q + 1) * s_rows, :] = s0f_ref[q, 0]
            sb_s[q * s_rows:(q + 1) * s_rows, :] = s0b_ref[q, 0]
    else:
        sf_s[...] = jnp.zeros_like(sf_s)
        sb_s[...] = jnp.zeros_like(sb_s)

    reps = nseq * t // LANES
    gr = -jnp.tile(jnp.exp(prmr_ref[0, 0]), (1, reps)) * _softplus(abt_ref[...] + jnp.tile(prmr_ref[0, 1], (1, reps)))
    seg_lane = lax.broadcasted_iota(jnp.int32, gr.shape, 1) % CHUNK
    pre, suf = gr, gr
    for s in (1, 2, 4, 8, 16, 32):
        pre = pre + jnp.where(seg_lane >= s, pltpu.roll(pre, s, 1), 0.0)
        suf = suf + jnp.where(seg_lane < CHUNK - s, pltpu.roll(suf, nseq * t - s, 1), 0.0)
    pre_s[...] = pre
    suf_s[...] = suf
    zrows = jnp.zeros((LANES - pre.shape[0], LANES), F32)
    for j in range(reps):
        tile = slice(j * LANES, (j + 1) * LANES)
        prec_s[tile, :] = jnp.concatenate([pre[:, tile], zrows], axis=0).T
        sufc_s[tile, :] = jnp.concatenate([suf[:, tile], zrows], axis=0).T
    lane_lo = lax.broadcasted_iota(jnp.int32, (1, LANES), 1) < CHUNK

    def prepare(cc, carry):
        chains = []
        for k in range(PREP_UNROLL):
            c = cc * PREP_UNROLL + k
            r0 = pl.multiple_of(c * CHUNK, CHUNK)
            b_all = _sigmoid(abc_ref[pl.ds(r0, CHUNK), :])
            run_c = (prec_s[pl.ds(r0, CHUNK), :], sufc_s[pl.ds(r0, CHUNK), :])
            tile0 = pl.multiple_of((cc * PREP_UNROLL + k - k % 2) * CHUNK, LANES)
            run = (pre_s[:, pl.ds(tile0, LANES)], suf_s[:, pl.ds(tile0, LANES)])
            run_r = tuple(pltpu.roll(x, CHUNK, 1) for x in run)
            for p in range(B_HEADS // 2):
                kst = _stack_pair(qkv_s[pl.ds(r0, CHUNK), qk_w:2 * qk_w], p)
                qst = _stack_pair(qkv_s[pl.ds(r0, CHUNK), 0:qk_w], p)
                vst = _stack_pair(qkv_s[pl.ds(r0, CHUNK), 2 * qk_w:3 * qk_w], p)
                kq = _dot_nt(jnp.concatenate([kst, qst], axis=0).astype(BF16), kst.astype(BF16))
                for d in range(2):
                    cg = 4 * d + 2 * p
                    edge = CHUNK - 1 if d == 0 else 0
                    rep_col = lambda x, col: jnp.broadcast_to(x[:, col:col + 1], (CHUNK, LANES))
                    b_rep = jnp.concatenate([rep_col(b_all, 8 + cg + hl) for hl in range(2)], axis=0)
                    gcol = jnp.concatenate([rep_col(run_c[d], cg + hl) for hl in range(2)], axis=0)
                    gtot = jnp.concatenate([rep_col(run_c[d][edge:edge + 1], cg + hl) for hl in range(2)], axis=0)
                    ra = cg
                    if k % 2 == 0:
                        grow = jnp.where(lane_lo, run[d][ra:ra + 1], run_r[d][ra + 1:ra + 2])
                    else:
                        grow = jnp.where(lane_lo, run_r[d][ra:ra + 1], run[d][ra + 1:ra + 2])
                    chains.append(dict(c=c, p=p, d=d, kst=kst, qst=qst, vst=vst, kq=kq, b_st=b_rep,
                                       gcol=gcol, gtot=gtot, grow=grow))

        for ch in chains:
            d, b_st, kq, gcol = ch["d"], ch["b_st"], ch.pop("kq"), ch["gcol"]
            decay = jnp.exp(jnp.minimum(gcol - ch.pop("grow"), 0.0))
            ch["a_mat"] = (b_st * kq[:PAIR]) * (decay * mask_ref[2 * d + 1])
            ch["attn"] = (kq[PAIR:] * (decay * mask_ref[2 * d])).astype(BF16)
            ch["t_inv"] = mask_ref[4] - ch["a_mat"] * mask_ref[5]
        for lvl in range(N_LEVELS - 1):
            for ch in chains:
                ch["t16"] = ch["t_inv"].astype(BF16)
                ch["et"] = _dot((ch["a_mat"] * mask_ref[6 + lvl]).astype(BF16), ch["t16"])
            for ch in chains:
                ch["t_inv"] = ch["t_inv"] - _dot(ch.pop("t16"), ch.pop("et").astype(BF16))
        for ch in chains:
            egc = jnp.exp(ch["gcol"])
            rk = jnp.concatenate([ch["b_st"] * ch["vst"], (ch["b_st"] * egc) * ch["kst"]], axis=1)
            ch["rk"] = _dot(ch.pop("t_inv").astype(BF16), rk.astype(BF16))
            ch["qp16"] = (ch["qst"] * egc).astype(BF16)
        for ch in chains:
            c, p, d, rk, qp16 = ch["c"], ch["p"], ch["d"], ch["rk"], ch["qp16"]
            pair_rows = slice(p * PAIR, (p + 1) * PAIR)
            w16 = rk[:, B_DIM:].astype(BF16)
            u_s[d, c, pair_rows, :] = rk[:, :B_DIM]
            at_s[d, c, p] = ch["attn"]
            kd_s[d, c, pair_rows, :] = (ch["kst"] * jnp.exp(ch["gtot"] - ch["gcol"])).astype(BF16)
            eg = jnp.exp(ch["gtot"])
            for hl in range(2):
                h = 2 * p + hl
                rows = slice(hl * CHUNK, (hl + 1) * CHUNK)
                wq_s[d, c, h * 2 * CHUNK:h * 2 * CHUNK + CHUNK, :] = w16[rows]
                wq_s[d, c, h * 2 * CHUNK + CHUNK:(h + 1) * 2 * CHUNK, :] = qp16[rows]
                eg_s[d, c, h * SUBLANES:(h + 1) * SUBLANES, :] = eg[hl * CHUNK:hl * CHUNK + SUBLANES, :]
        return carry

    lax.fori_loop(0, n_total // PREP_UNROLL, prepare, 0)

    def scan_step(i, carry):
        units = []
        for q in range(nseq):
            for d, s_ref, o_s in ((0, sf_s, of_s), (1, sb_s, ob_s)):
                c = q * n_chunks + (i if d == 0 else n_chunks - 1 - i)
                units.append(dict(q=q, d=d, c=c, s_ref=s_ref, o_s=o_s, r0=pl.multiple_of(c * CHUNK, CHUNK)))
        for un in units:
            q, d, c, s_ref = un["q"], un["d"], un["c"], un["s_ref"]
            un["x"] = []
            for h in range(B_HEADS):
                srows = slice(q * s_rows + h * B_DIM, q * s_rows + (h + 1) * B_DIM)
                un["x"].append(_dot(wq_s[d, c, h * 2 * CHUNK:(h + 1) * 2 * CHUNK, :], s_ref[srows, :].astype(BF16)))
        for un in units:
            d, c = un["d"], un["c"]
            un["vp16"], un["o"] = [], []
            for p in range(B_HEADS // 2):
                xs = un["x"][2 * p:2 * p + 2]
                v_new = jnp.concatenate(
                    [u_s[d, c, (2 * p + hl) * CHUNK:(2 * p + hl + 1) * CHUNK, :] - xs[hl][:CHUNK] for hl in range(2)],
                    axis=0)
                vp16 = v_new.astype(BF16)
                un["vp16"].append(vp16)
                un["o"].append(jnp.concatenate([xs[hl][CHUNK:] for hl in range(2)], axis=0)
                               + _dot(at_s[d, c, p], vp16))
        for un in units:
            q, d, c, s_ref, o_s, r0 = un["q"], un["d"], un["c"], un["s_ref"], un["o_s"], un["r0"]
            for h in range(B_HEADS):
                p, hl = divmod(h, 2)
                rows = slice(hl * CHUNK, (hl + 1) * CHUNK)
                srows = slice(q * s_rows + h * B_DIM, q * s_rows + (h + 1) * B_DIM)
                upd = _dot_tn(kd_s[d, c, h * CHUNK:(h + 1) * CHUNK, :], un["vp16"][p][rows])
                eg = jnp.tile(eg_s[d, c, h * SUBLANES:(h + 1) * SUBLANES, :], (B_DIM // SUBLANES, 1))
                s_ref[srows, :] = s_ref[srows, :] * eg + upd
                o_s[pl.ds(r0, CHUNK), h * B_DIM:(h + 1) * B_DIM] = un["o"][p][rows]
        return carry

    lax.fori_loop(0, n_chunks, scan_step, 0)

    if not has_s0:
        for q in range(nseq):
            sfo_ref[q, 0] = sf_s[q * s_rows:(q + 1) * s_rows, :]
            sbo_ref[q, 0] = sb_s[q * s_rows:(q + 1) * s_rows, :]

    for h in range(B_HEADS):
        cols = slice(h * B_DIM, (h + 1) * B_DIM)
        x = of_s[:, cols] + ob_s[:, cols]
        yn = x * lax.rsqrt(jnp.mean(x * x, axis=-1, keepdims=True) + EPS) * bng_ref[0]
        o_ref[:, cols] = (yn * _silu(zb_ref[:, 3 * qk_w + h * B_DIM:3 * qk_w + (h + 1) * B_DIM])).astype(BF16)


def _delta_call(has_s0, t, nseq, n_batch, row_block0, layer, zb, zab, zabt, conv, prmr, bng, masks,
                prev=None, s0=None):
    n_tok = zb.shape[0]
    depth = conv.shape[0]
    n_chunks = nseq * (t // CHUNK)
    assert n_chunks % PREP_UNROLL == 0 and PREP_UNROLL % 2 == 0 and n_batch % nseq == 0
    tok_spec = lambda w: pl.BlockSpec((nseq * t, w), lambda b: (row_block0 + b, 0))
    const = lambda shape: pl.BlockSpec(shape, lambda b: (0,) * len(shape))
    layer_spec = lambda shape: pl.BlockSpec((1,) + shape, lambda b: (layer,) + (0,) * len(shape))
    s_shape = (B_HEADS * B_DIM, B_DIM)
    s_spec = pl.BlockSpec((nseq, 1) + s_shape, lambda b: (b, layer, 0, 0))
    n_ab = zabt.shape[0]
    in_specs = [
        tok_spec(ZB_W), tok_spec(ZAB_W),
        pl.BlockSpec((n_ab, nseq * t), lambda b: (0, row_block0 + b)),
        layer_spec((3, 3 * B_HEADS * B_DIM)), layer_spec((2, n_ab, LANES)),
        layer_spec((1, B_DIM)),
        const((5 + N_LEVELS, PAIR, PAIR)),
    ]
    args = [zb, zab, zabt, conv, prmr, bng, masks]
    out_specs = [tok_spec(B_HEADS * B_DIM)]
    out_shape = [jax.ShapeDtypeStruct((n_tok, B_HEADS * B_DIM), BF16)]
    if has_s0:
        in_specs += [s_spec, s_spec]
        args += [s0[0], s0[1]]
    else:
        out_specs += [s_spec, s_spec]
        out_shape += [jax.ShapeDtypeStruct((n_batch, depth) + s_shape, F32)] * 2
    n_real = len(args)
    aliases = {}
    if prev is not None:
        first_out = 0 if has_s0 else 1
        for k, arr in enumerate(prev):
            in_specs.append(pl.BlockSpec(memory_space=pl.ANY))
            args.append(arr)
            aliases[n_real + k] = first_out + k
    rows = nseq * t
    scratch = [
        pltpu.VMEM((rows, 3 * B_HEADS * B_DIM), F32),
        pltpu.VMEM((rows, B_HEADS * B_DIM), F32), pltpu.VMEM((rows, B_HEADS * B_DIM), F32),
        pltpu.VMEM((nseq * s_shape[0], B_DIM), F32), pltpu.VMEM((nseq * s_shape[0], B_DIM), F32),
        pltpu.VMEM((2, n_chunks, BD, B_DIM), F32),
        pltpu.VMEM((2, n_chunks, 2 * BD, B_DIM), BF16),
        pltpu.VMEM((2, n_chunks, B_HEADS // 2, PAIR, PAIR), BF16),
        pltpu.VMEM((2, n_chunks, BD, B_DIM), BF16),
        pltpu.VMEM((2, n_chunks, B_HEADS * SUBLANES, LANES), F32),
        pltpu.VMEM((n_ab, rows), F32), pltpu.VMEM((n_ab, rows), F32),
        pltpu.VMEM((rows, LANES), F32), pltpu.VMEM((rows, LANES), F32),
    ]

    def body(*refs):
        _delta_kernel(t, nseq, has_s0, *refs[:n_real], *refs[len(args):])

    return pl.pallas_call(
        body,
        grid=(n_batch // nseq,),
        in_specs=in_specs,
        out_specs=out_specs,
        out_shape=out_shape,
        scratch_shapes=scratch,
        input_output_aliases=aliases,
        compiler_params=_params(("arbitrary",)),
        name="delta_latent" if has_s0 else "delta_prompt",
    )(*args)


def _outproj_router(x, ma, mb, mc, m, g, wo_ref, wr_ref, br):
    n = x.shape[0]
    b0, c0 = Q_W, Q_W + B_HEADS * B_DIM
    y = (_dot(ma.astype(BF16), wo_ref[0, 0:b0, :])
         + _dot(mb.astype(BF16), wo_ref[0, b0:c0, :])
         + _dot(mc.astype(BF16), wo_ref[0, c0:c0 + Q_W, :]))
    x1 = x + m[2:3] * y
    h2 = _modulated_norm(x1, g, m[3:4], m[4:5])
    hi, lo = _split2(h2)

    hw = _dot(hi, wr_ref[0])
    logits = (hw[:, :LANES] + hw[:, LANES:] + _dot(lo, wr_ref[0, :, :LANES]) + br).T
    gl = logits[0:N_GROUPS]
    grow = lax.broadcasted_iota(jnp.int32, gl.shape, 0)
    gmax = gl.max(axis=0, keepdims=True)
    g_sel = jnp.where(gl == gmax, grow, N_GROUPS).min(axis=0, keepdims=True)
    g_w = 1.0 / jnp.exp(gl - gmax).sum(axis=0, keepdims=True)
    el = logits[EXPERT_ROW0:EXPERT_ROW0 + N_EXPERTS]
    e_idx = lax.broadcasted_iota(jnp.int32, el.shape, 0)
    el = jnp.where((e_idx // EXPERTS_PER_GROUP) == g_sel, el, -jnp.inf)
    m1 = el.max(axis=0, keepdims=True)
    i1 = jnp.where(el == m1, e_idx, N_EXPERTS).min(axis=0, keepdims=True)
    el2 = jnp.where(e_idx == i1, -jnp.inf, el)
    m2 = el2.max(axis=0, keepdims=True)
    i2 = jnp.where(el2 == m2, e_idx, N_EXPERTS).min(axis=0, keepdims=True)
    tt = jnp.exp(m2 - m1)
    w1 = g_w / (1.0 + tt)
    w2 = w1 * tt
    gate_t = jnp.where(e_idx == i1, w1, 0.0) + jnp.where(e_idx == i2, w2, 0.0)
    gate = jnp.concatenate([gate_t, jnp.zeros((LANES - N_EXPERTS, n), F32)], axis=0).T
    return x1, hi, gate


def _ffn_kernel(x_ref, ma_ref, mb_ref, mc_ref, mod_ref, g_ref, wo_ref, wr_ref, br_ref, w1_ref, w3_ref, w2_ref,
                o_ref, h_s, gate_s):
    j = pl.program_id(1)
    tm = x_ref.shape[0]
    th = w1_ref.shape[1]
    m = mod_ref[0, 0]

    @pl.when(j == 0)
    def _():
        x1, hi, gate = _outproj_router(x_ref[...], ma_ref[...], mb_ref[...], mc_ref[...],
                                       m, g_ref[0], wo_ref, wr_ref, br_ref[0])
        o_ref[...] = x1
        h_s[...] = hi
        gate_s[...] = gate

    @pl.when(j > 0)
    def _():
        h = h_s[...]
        hid = _silu(_dot(h, w1_ref[...])) * _dot(h, w3_ref[...])
        gate = gate_s[...]
        lane = lax.broadcasted_iota(jnp.int32, gate.shape, 1)
        n_e = th // D_EXPERT
        col = lax.broadcasted_iota(jnp.int32, hid.shape, 1) // D_EXPERT
        gmat = jnp.zeros(hid.shape, F32)
        for e in range(n_e):
            ge = jnp.where(lane == (j - 1) * n_e + e, gate, 0.0).sum(axis=1, keepdims=True)
            gmat = jnp.where(col == e, ge, gmat)
        o_ref[...] += m[5:6] * _dot((hid * gmat).astype(BF16), w2_ref[...])


def _ffn_call(layer, x, ma, mb, mc, mods, g, wo, wr, br, w1, w3, w2, slot_fn, tm, th):
    n_tok = x.shape[0]
    n_h = w1.shape[1] // th
    hidden = lambda j: jnp.where(j == 0, n_h - 1, j - 1)
    tok = lambda w: pl.BlockSpec((tm, w), lambda i, j: (i, 0))
    layer_spec = lambda shape: pl.BlockSpec((1,) + shape, lambda i, j: (layer,) + (0,) * len(shape))
    return pl.pallas_call(
        _ffn_kernel,
        grid=(n_tok // tm, n_h + 1),
        in_specs=[tok(D_MODEL), tok(Q_W), tok(B_HEADS * B_DIM), tok(Q_W),
                  pl.BlockSpec((1, 1, 6, D_MODEL), lambda i, j: (layer, slot_fn(i), 0, 0)),
                  layer_spec((1, D_MODEL)), layer_spec((D_MODEL, D_MODEL)), layer_spec((D_MODEL, 2 * LANES)),
                  layer_spec((1, LANES)),
                  pl.BlockSpec((D_MODEL, th), lambda i, j: (0, hidden(j))),
                  pl.BlockSpec((D_MODEL, th), lambda i, j: (0, hidden(j))),
                  pl.BlockSpec((th, D_MODEL), lambda i, j: (hidden(j), 0))],
        out_specs=tok(D_MODEL),
        out_shape=jax.ShapeDtypeStruct((n_tok, D_MODEL), F32),
        scratch_shapes=[pltpu.VMEM((tm, D_MODEL), BF16), pltpu.VMEM((tm, LANES), F32)],
        compiler_params=_params(("arbitrary", "arbitrary")),
        name="ffn",
    )(x, ma, mb, mc, mods, g, wo, wr, br, w1, w3, w2)


def _final_norm_kernel(x_ref, g_ref, o_ref):
    x = x_ref[...]
    o_ref[...] = x * lax.rsqrt(jnp.mean(x * x, axis=-1, keepdims=True) + EPS) * g_ref[...]


def _final_norm_call(x, g, tm, row0, n_rows):
    blk0 = row0 // tm
    return pl.pallas_call(
        _final_norm_kernel,
        grid=(n_rows // tm,),
        in_specs=[pl.BlockSpec((tm, D_MODEL), lambda i: (blk0 + i, 0)), pl.BlockSpec((1, D_MODEL), lambda i: (0, 0))],
        out_specs=pl.BlockSpec((tm, D_MODEL), lambda i: (i, 0)),
        out_shape=jax.ShapeDtypeStruct((n_rows, D_MODEL), F32),
        compiler_params=_params(("arbitrary",)),
        name="final_norm",
    )(x, g)


def _rope_tables(t):
    pos = np.arange(t)
    n_freq = HEAD_DIM // 4
    inv_freq = ROPE_THETA ** (-jnp.arange(n_freq, dtype=F32) / n_freq)
    row = jnp.asarray(pos // GRID_W, F32)
    col = jnp.asarray(pos % GRID_W, F32)
    ang = jnp.concatenate([row[:, None] * inv_freq, col[:, None] * inv_freq], -1)
    cos, sin = jnp.cos(ang), jnp.sin(ang)
    cos_t = jnp.tile(jnp.concatenate([cos, cos], -1), (1, LANES // HEAD_DIM))
    sin_t = jnp.tile(jnp.concatenate([-sin, sin], -1), (1, LANES // HEAD_DIM))
    return cos_t, sin_t


def _delta_tables():
    r = np.arange(PAIR)
    same = (r[:, None] // CHUNK) == (r[None, :] // CHUNK)
    low = same & (r[:, None] >= r[None, :])
    low_s = same & (r[:, None] > r[None, :])
    up = same & (r[:, None] <= r[None, :])
    up_s = same & (r[:, None] < r[None, :])
    levels = []
    for k in range(N_LEVELS):
        s = 1 << k
        levels.append(((r[:, None] // (2 * s)) == (r[None, :] // (2 * s))) & ((r[:, None] // s) != (r[None, :] // s)))
    masks = jnp.asarray(np.stack([low, low_s, up, up_s, np.eye(PAIR, dtype=bool)] + levels).astype(np.float32))
    return masks


def _segment_mean_table():
    r = np.arange(LANES)
    seg = ((r[:, None] // HEAD_DIM) == (r[None, :] // HEAD_DIM)).astype(np.float32) / HEAD_DIM
    hi = jnp.asarray(seg, BF16)
    lo = (jnp.asarray(seg) - hi.astype(F32)).astype(BF16)
    return jnp.stack([hi, lo])


def kernel(x_prompt, x_sample, cache_a_k, cache_a_v, cache_c_k, cache_c_v, state_b_fwd, state_b_bwd, c, c_ctx, w_mod, b_mod, norm1_g, norm2_g, w_in, a_sink, b_conv, b_a_log, b_dt_bias, b_norm_g, c_q_norm, c_k_norm, w_out, w_group, b_group, w_expert, b_expert, w1, w3, w2, final_norm_g):
    n_p, t_p, d = x_prompt.shape
    n_s, t_s, _ = x_sample.shape
    depth = w_in.shape[0]
    past = cache_a_k.shape[2]
    tok_p = n_p * t_p
    n_tok = tok_p + n_s * t_s
    assert d == D_MODEL and tok_p % t_s == 0 and t_s % max(TM_PROJ, TM_FFN) == 0 and t_p % 256 == 0

    w_in_t = jnp.swapaxes(w_in, 1, 2)
    w_out16 = w_out.astype(BF16)
    pad_g = jnp.zeros((depth, d, EXPERT_ROW0 - N_GROUPS), F32)
    pad_e = jnp.zeros((depth, d, LANES - EXPERT_ROW0 - N_EXPERTS), F32)
    w_r = jnp.concatenate([w_group, pad_g, w_expert, pad_e], -1)
    w_r_hi = w_r.astype(BF16)
    w_r2 = jnp.concatenate([w_r_hi, (w_r - w_r_hi.astype(F32)).astype(BF16)], axis=-1)
    b_r = jnp.concatenate([b_group, pad_g[:, 0], b_expert, pad_e[:, 0]], -1)[:, None, :]
    cqn = jnp.tile(c_q_norm, (1, 4))[:, None, :]
    ckn = jnp.tile(c_k_norm, (1, 2))[:, None, :]
    gate_prm = jnp.stack([b_a_log.reshape(depth, 8), b_dt_bias.reshape(depth, 8)], 1)
    prmr = jnp.broadcast_to(jnp.pad(gate_prm, ((0, 0), (0, 0), (0, N_AB - 8)))[..., None],
                            (depth, 2, N_AB, LANES))
    cos_t, sin_t = _rope_tables(t_s)
    masks = _delta_tables()
    seg = _segment_mean_table()

    cond = jnp.concatenate([c_ctx[None, :], c], axis=0)
    cond_b = jnp.broadcast_to(cond[:, :, None], cond.shape + (LANES,))
    mods_all = _mods_call(cond_b, w_mod, b_mod).reshape(depth, SUBLANES, 6, d)

    def slot_fn(tm):
        per_s = t_s // tm
        first = tok_p // tm
        return lambda i: jnp.where(i < first, 0, 1 + (i - first) // per_s)

    xs = (x_prompt.reshape(tok_p, d), x_sample.reshape(n_s * t_s, d))
    blk_s = tok_p // t_s
    ctx = tuple(a.reshape(n_s, depth, past, LANES) for a in (cache_a_k, cache_a_v, cache_c_k, cache_c_v))
    s0 = tuple(a.reshape(n_s, depth, B_HEADS * B_DIM, B_DIM) for a in (state_b_fwd, state_b_bwd))
    g1, g2, bng = norm1_g[:, None, :], norm2_g[:, None, :], b_norm_g[:, None, :]
    caches = None
    states = None
    for l in range(depth):
        za, zb, zc, zab, zabt, *rest = _inproj_call(l, xs, mods_all, g1, w_in_t, slot_fn(TM_PROJ), TM_PROJ,
                                                    experts=(w3, w2))
        x = rest[0] if len(xs) > 1 else xs[0]
        w3b, w2b = rest[-2:]

        ao, co, *rest = _attn_call(False, t_p, ATTN_NSEQ, n_p, 0, l, za, zc, a_sink, cqn, ckn, seg, prev=caches,
                                   experts=(w1,))
        caches, (w1b,) = rest[:4], rest[4:]
        ao, co = _attn_call(True, t_s, 1, n_s, blk_s, l, za, zc, a_sink, cqn, ckn, seg, prev=(ao, co),
                            rope=(cos_t, sin_t), ctx=ctx)

        bo, *states = _delta_call(False, t_p, DELTA_NSEQ, n_p, 0, l, zb, zab, zabt, b_conv, prmr, bng, masks,
                                  prev=states)
        (bo,) = _delta_call(True, t_s, 1, n_s, blk_s, l, zb, zab, zabt, b_conv, prmr, bng, masks,
                            prev=(bo,), s0=s0)

        x = _ffn_call(l, x, ao, bo, co, mods_all, g2, w_out16, w_r2, b_r, w1b, w3b, w2b, slot_fn(TM_FFN), TM_FFN, TH_FFN)
        xs = (x,)

    y_prompt = _final_norm_call(x, final_norm_g[None], TM_NORM, 0, tok_p).reshape(n_p, t_p, d)
    y_sample = _final_norm_call(x, final_norm_g[None], TM_NORM, tok_p, n_s * t_s).reshape(n_s, t_s, d)
    new_ak, new_av, new_ck, new_cv = (a.reshape(n_p, depth, t_p, 2, HEAD_DIM) for a in caches)
    new_sf, new_sb = (a.reshape(n_p, depth, B_HEADS, B_DIM, B_DIM) for a in states)
    return (y_prompt, y_sample, new_ak, new_av, new_ck, new_cv, new_sf, new_sb)
```

```python
import functools

import jax
import jax.numpy as jnp
import numpy as np
from jax import lax
from jax.experimental import pallas as pl
from jax.experimental.pallas import tpu as pltpu

F32 = jnp.float32
BF16 = jnp.bfloat16

D_MODEL = 1024
GRID_W = 64
EPS = 1e-6
NEG_INF = -1e30
ROPE_THETA = 10000.0
HEAD_DIM = 64
Q_W = 256
KV_W = 128
WINDOW = 128
Q_BLOCK = 128
B_HEADS = 4
B_DIM = 128
CHUNK = 64
BD = B_HEADS * CHUNK
PAIR = 2 * CHUNK
N_LEVELS = 6
PREP_UNROLL = 4
DELTA_NSEQ = 4
ATTN_NSEQ = 4
N_GROUPS = 4
EXPERTS_PER_GROUP = 4
N_EXPERTS = 16
D_EXPERT = 256
EXPERT_ROW0 = 8

LANES = 128
SUBLANES = 8
VMEM_LIMIT = 60000 * 1024

TM_PROJ = 512
TM_FFN, TH_FFN = 1024, 1024
TM_NORM = 512
CAST_STEPS = 16
MODS_TN = 1536

ZA_W, ZB_W, ZC_W, ZAB_W = 512, 2048, 512, 128
N_AB = 16
Z_W = ZA_W + ZB_W + ZC_W + ZAB_W


def _sigmoid(x):
    return 0.5 * jnp.tanh(0.5 * x) + 0.5


def _silu(x):
    h = 0.5 * x
    return h + h * jnp.tanh(h)


def _softplus(x):
    return jnp.maximum(x, 0.0) + jnp.log1p(jnp.exp(-jnp.abs(x)))


def _dot(a, b):
    return jnp.dot(a, b, preferred_element_type=F32)


def _dot_nt(a, b):
    return lax.dot_general(a, b, (((1,), (1,)), ((), ())), preferred_element_type=F32)


def _dot_tn(a, b):
    return lax.dot_general(a, b, (((0,), (0,)), ((), ())), preferred_element_type=F32)


def _split2(x):
    hi = x.astype(BF16)
    lo = (x - hi.astype(F32)).astype(BF16)
    return hi, lo


def _params(sem=None):
    return pltpu.CompilerParams(dimension_semantics=sem, vmem_limit_bytes=VMEM_LIMIT)


def _mods_kernel(cond_ref, w_ref, b_ref, o_ref, act_s):
    n_cond = cond_ref.shape[0]
    tn = w_ref.shape[2]
    reps = tn // LANES

    @pl.when((pl.program_id(0) == 0) & (pl.program_id(1) == 0))
    def _():
        act_s[...] = _silu(cond_ref[...])

    def body(kb, accs):
        r = pl.multiple_of(kb * SUBLANES, SUBLANES)
        w = w_ref[0, pl.ds(r, SUBLANES), :]
        return tuple(acc + jnp.tile(act_s[m, pl.ds(r, SUBLANES), :], (1, reps)) * w for m, acc in enumerate(accs))

    zero = jnp.zeros((SUBLANES, tn), F32)
    accs = lax.fori_loop(0, w_ref.shape[1] // SUBLANES, body, (zero,) * n_cond, unroll=4)
    rows = [jnp.sum(a, axis=0, keepdims=True) + b_ref[0] for a in accs]
    rows.append(jnp.zeros((SUBLANES - n_cond, tn), F32))
    o_ref[0] = jnp.concatenate(rows, axis=0)


def _mods_call(cond_b, w_mod, b_mod):
    depth, d, n = w_mod.shape
    tn = MODS_TN
    n_cond = cond_b.shape[0]
    return pl.pallas_call(
        _mods_kernel,
        grid=(depth, n // tn),
        in_specs=[
            pl.BlockSpec((n_cond, d, LANES), lambda l, j: (0, 0, 0)),
            pl.BlockSpec((1, d, tn), lambda l, j: (l, 0, j)),
            pl.BlockSpec((1, 1, tn), lambda l, j: (l, 0, j)),
        ],
        out_specs=pl.BlockSpec((1, SUBLANES, tn), lambda l, j: (l, 0, j)),
        out_shape=jax.ShapeDtypeStruct((depth, SUBLANES, n), F32),
        scratch_shapes=[pltpu.VMEM((n_cond, d, LANES), F32)],
        compiler_params=_params(("arbitrary", "arbitrary")),
        name="mods",
    )(cond_b, w_mod, b_mod.reshape(depth, 1, n))


def _x_specs(xs, tm):
    if len(xs) == 1:
        return [pl.BlockSpec((tm, D_MODEL), lambda i, *_: (i, 0))]
    first = xs[0].shape[0] // tm
    return [pl.BlockSpec((tm, D_MODEL), lambda i, *_: (jnp.minimum(i, first - 1), 0)),
            pl.BlockSpec((tm, D_MODEL), lambda i, *_: (jnp.maximum(i - first, 0), 0))]


def _x_tile(x_refs, first):
    if len(x_refs) == 1:
        return x_refs[0][...]
    return jnp.where(pl.program_id(0) < first, x_refs[0][...], x_refs[1][...])


def _modulated_norm(x, g, shift, scale):
    ms = jnp.mean(x * x, axis=-1, keepdims=True)
    y = x * lax.rsqrt(ms + EPS) * g
    return y * (1.0 + scale) + shift


def _inproj_kernel(n_x, first, n_w, *refs):
    x_refs = refs[:n_x]
    mod_ref, g_ref, wt_ref = refs[n_x:n_x + 3]
    w_refs = refs[n_x + 3:n_x + 3 + n_w]
    o0 = n_x + 3 + n_w
    za_ref, zb_ref, zc_ref, zab_ref, zabt_ref = refs[o0:o0 + 5]
    o1 = o0 + 5 + (1 if n_x > 1 else 0)
    wb_refs = refs[o1:o1 + n_w]
    w_s = refs[-1]
    for w_ref, wb_ref in zip(w_refs, wb_refs):
        wb_ref[...] = w_ref[0].astype(BF16)
    @pl.when(pl.program_id(0) == 0)
    def _():
        ab0 = ZA_W + ZB_W
        w_s[0:ab0, :] = wt_ref[0, 0:ab0, :].astype(BF16)
        w_s[ab0:ab0 + ZC_W, :] = wt_ref[0, ab0 + N_AB:ab0 + N_AB + ZC_W, :].astype(BF16)
        w_s[ab0 + ZC_W:ab0 + ZC_W + N_AB, :] = wt_ref[0, ab0:ab0 + N_AB, :].astype(BF16)
        w_s[ab0 + ZC_W + N_AB:Z_W, :] = jnp.zeros((ZAB_W - N_AB, D_MODEL), BF16)

    m = mod_ref[0, 0]
    x = _x_tile(x_refs, first)
    if n_x > 1:
        refs[o0 + 5][...] = x
    h = _modulated_norm(x, g_ref[0], m[0:1], m[1:2]).astype(BF16)
    za_ref[...] = _dot_nt(h, w_s[0:ZA_W, :])
    step = 512
    for j in range(ZB_W // step):
        zb_ref[:, j * step:(j + 1) * step] = _dot_nt(h, w_s[ZA_W + j * step:ZA_W + (j + 1) * step, :])
    zc_ref[...] = _dot_nt(h, w_s[ZA_W + ZB_W:ZA_W + ZB_W + ZC_W, :])
    zab = _dot_nt(h, w_s[ZA_W + ZB_W + ZC_W:Z_W, :])
    zab_ref[...] = zab
    zabt_ref[...] = zab.T[:N_AB]


def _inproj_call(layer, xs, mods, g, w, slot_fn, tm, experts=()):
    n_tok = sum(a.shape[0] for a in xs)
    n_ab = N_AB
    cast_steps = CAST_STEPS
    assert n_tok // tm >= cast_steps
    slab = lambda i: jnp.minimum(i, cast_steps - 1)
    cast_in = [pl.BlockSpec((1, e.shape[1] // cast_steps, e.shape[2]), lambda i: (layer, slab(i), 0)) for e in experts]
    cast_out = [pl.BlockSpec((e.shape[1] // cast_steps, e.shape[2]), lambda i: (slab(i), 0)) for e in experts]
    return pl.pallas_call(
        functools.partial(_inproj_kernel, len(xs), xs[0].shape[0] // tm, len(experts)),
        grid=(n_tok // tm,),
        in_specs=_x_specs(xs, tm) + [
            pl.BlockSpec((1, 1, 6, D_MODEL), lambda i: (layer, slot_fn(i), 0, 0)),
            pl.BlockSpec((1, 1, D_MODEL), lambda i: (layer, 0, 0)),
            pl.BlockSpec((1, w.shape[1], D_MODEL), lambda i: (layer, 0, 0)),
        ] + cast_in,
        out_specs=[
            pl.BlockSpec((tm, ZA_W), lambda i: (i, 0)),
            pl.BlockSpec((tm, ZB_W), lambda i: (i, 0)),
            pl.BlockSpec((tm, ZC_W), lambda i: (i, 0)),
            pl.BlockSpec((tm, ZAB_W), lambda i: (i, 0)),
            pl.BlockSpec((n_ab, tm), lambda i: (0, i)),
        ] + ([pl.BlockSpec((tm, D_MODEL), lambda i: (i, 0))] if len(xs) > 1 else []) + cast_out,
        out_shape=[
            jax.ShapeDtypeStruct((n_tok, ZA_W), F32),
            jax.ShapeDtypeStruct((n_tok, ZB_W), F32),
            jax.ShapeDtypeStruct((n_tok, ZC_W), F32),
            jax.ShapeDtypeStruct((n_tok, ZAB_W), F32),
            jax.ShapeDtypeStruct((n_ab, n_tok), F32),
        ] + ([jax.ShapeDtypeStruct((n_tok, D_MODEL), F32)] if len(xs) > 1 else [])
        + [jax.ShapeDtypeStruct(e.shape[1:], BF16) for e in experts],
        scratch_shapes=[pltpu.VMEM((Z_W, D_MODEL), BF16)],
        compiler_params=_params(("arbitrary",)),
        name="inproj",
    )(*xs, mods, g, w, *experts)


def _lane_lo(shape):
    return lax.broadcasted_iota(jnp.int32, shape, len(shape) - 1) % LANES < HEAD_DIM


def _store_kdup(dst_ref, off, k):
    n = k.shape[0]
    r = pltpu.roll(k, HEAD_DIM, 1)
    lo = _lane_lo(k.shape)
    dst_ref[0, off:off + n, :] = jnp.where(lo, k, r).astype(BF16)
    dst_ref[1, off:off + n, :] = jnp.where(lo, r, k).astype(BF16)


def _store_vsplit(dst_ref, off, v):
    n = v.shape[0]
    r = pltpu.roll(v, HEAD_DIM, 1)
    lo = _lane_lo(v.shape)
    z = jnp.zeros_like(v)
    dst_ref[0, off:off + n, :] = jnp.where(lo, v, z).astype(BF16)
    dst_ref[1, off:off + n, :] = jnp.where(lo, z, r).astype(BF16)
    dst_ref[2, off:off + n, :] = jnp.where(lo, r, z).astype(BF16)
    dst_ref[3, off:off + n, :] = jnp.where(lo, z, v).astype(BF16)


def _rope(x, cos, sin):
    first = (lax.broadcasted_iota(jnp.int32, x.shape, 1) // (HEAD_DIM // 2)) % 2 == 0
    partner = jnp.where(first, pltpu.roll(x, LANES - HEAD_DIM // 2, 1), pltpu.roll(x, HEAD_DIM // 2, 1))
    return x * cos + partner * sin


def _head_rmsnorm(x, g, seg_hi, seg_lo):
    hi, lo = _split2(x * x)
    ms = _dot(hi, seg_hi) + _dot(lo, seg_hi) + _dot(hi, seg_lo)
    return x * lax.rsqrt(ms + EPS) * g


def _attend_many(units):
    qb = units[0][0].shape[0]
    lo = _lane_lo(units[0][0].shape)
    all_scores = []
    for qt, segs, _ in units:
        z = jnp.zeros_like(qt)
        qs = jnp.concatenate([jnp.where(lo, qt, z), jnp.where(lo, z, qt)], axis=0).astype(BF16)
        scores = []
        for kdup, _, _, mask in segs:
            s = _dot_nt(qs, kdup)
            if mask is not None:
                s = jnp.where(mask, s, NEG_INF)
            scores.append(s)
        all_scores.append(scores)
    probs = []
    for (qt, segs, sink_pair), scores in zip(units, all_scores):
        m = scores[0].max(axis=1, keepdims=True)
        for s in scores[1:]:
            m = jnp.maximum(m, s.max(axis=1, keepdims=True))
        if sink_pair is not None:
            row_a = lax.broadcasted_iota(jnp.int32, (2 * qb, 1), 0) < qb
            sink = jnp.where(row_a, sink_pair[0], sink_pair[1])
            m = jnp.maximum(m, sink)
            denom = jnp.exp(sink - m)
        else:
            denom = jnp.zeros((2 * qb, 1), F32)
        ps = []
        for s in scores:
            p = jnp.exp(s - m)
            denom = denom + p.sum(axis=1, keepdims=True)
            ps.append(p.astype(BF16))
        probs.append((ps, 1.0 / denom))
    outs = []
    for (qt, segs, _), (ps, inv) in zip(units, probs):
        acc = jnp.zeros((qb, LANES), F32)
        for pb, (_, vlo, vhi, _) in zip(ps, segs):
            acc = acc + _dot(pb[:qb], vlo) + _dot(pb[qb:], vhi)
        outs.append(acc * jnp.where(lo, inv[:qb], inv[qb:]))
    return outs


def _attn_kernel(has_ctx, t, nseq, layer, *refs):
    if has_ctx:
        (sink_ref, za_ref, zc_ref, cqn_ref, ckn_ref, seg_ref, cos_ref, sin_ref,
         cak_ref, cav_ref, cck_ref, ccv_ref,
         ao_ref, co_ref,
         ka_s, va_s, kc_s, vc_s, kctx_s, vctx_s, qa_s, qc_s) = refs
    else:
        (sink_ref, za_ref, zc_ref, cqn_ref, ckn_ref, seg_ref, w1_ref,
         ao_ref, co_ref, nak_ref, nav_ref, nck_ref, ncv_ref, w1b_ref,
         ka_s, va_s, kc_s, vc_s, qa_s, qc_s) = refs
        w1b_ref[...] = w1_ref[0].astype(BF16)
    scale = HEAD_DIM ** -0.5
    seg_hi = seg_ref[0]
    seg_lo = seg_ref[1]
    piece = 256
    n_ctx = cak_ref.shape[2] if has_ctx else 0

    for p0 in range(0, nseq * t, piece):
        rows = slice(p0, p0 + piece)
        ak = za_ref[rows, Q_W:Q_W + KV_W]
        av = za_ref[rows, Q_W + KV_W:Q_W + 2 * KV_W]
        ck = _head_rmsnorm(zc_ref[rows, Q_W:Q_W + KV_W], ckn_ref[0], seg_hi, seg_lo)
        cv = zc_ref[rows, Q_W + KV_W:Q_W + 2 * KV_W]
        if has_ctx:
            cos = cos_ref[rows, :]
            sin = sin_ref[rows, :]
            ak = _rope(ak, cos, sin)
            ck = _rope(ck, cos, sin)
            _store_kdup(ka_s, WINDOW + p0, ak)
            _store_vsplit(va_s, WINDOW + p0, av)
            _store_kdup(kc_s, n_ctx + p0, ck)
            _store_vsplit(vc_s, n_ctx + p0, cv)
        else:
            crow = slice(p0 % t, p0 % t + piece)
            nak_ref[p0 // t, 0, crow, :] = ak
            nav_ref[p0 // t, 0, crow, :] = av
            nck_ref[p0 // t, 0, crow, :] = ck
            ncv_ref[p0 // t, 0, crow, :] = cv
            _store_kdup(ka_s, p0, ak)
            _store_vsplit(va_s, p0, av)
            _store_kdup(kc_s, p0, ck)
            _store_vsplit(vc_s, p0, cv)
        for hk in range(2):
            cols = slice(hk * LANES, (hk + 1) * LANES)
            aq = za_ref[rows, cols]
            cq = _head_rmsnorm(zc_ref[rows, cols], cqn_ref[0, :, cols], seg_hi, seg_lo)
            if has_ctx:
                aq = _rope(aq, cos, sin)
                cq = _rope(cq, cos, sin)
            qa_s[rows, cols] = aq * scale
            qc_s[rows, cols] = cq * scale

    if has_ctx:
        zpad = jnp.zeros((WINDOW, LANES), BF16)
        for i in range(2):
            ka_s[i, 0:WINDOW, :] = zpad
            ka_s[i, WINDOW + t:2 * WINDOW + t, :] = zpad
        for i in range(4):
            va_s[i, 0:WINDOW, :] = zpad
            va_s[i, WINDOW + t:2 * WINDOW + t, :] = zpad
        for p0 in range(0, n_ctx, piece):
            rows = slice(p0, p0 + piece)
            _store_kdup(kctx_s, p0, cak_ref[0, 0, rows, :])
            _store_vsplit(vctx_s, p0, cav_ref[0, 0, rows, :])
            _store_kdup(kc_s, p0, cck_ref[0, 0, rows, :])
            _store_vsplit(vc_s, p0, ccv_ref[0, 0, rows, :])

        qb = Q_BLOCK
        span = qb + 2 * WINDOW
        qi = lax.broadcasted_iota(jnp.int32, (2 * qb, span), 0) % qb
        kj = lax.broadcasted_iota(jnp.int32, (2 * qb, span), 1)
        band = jnp.abs(kj - WINDOW - qi) <= WINDOW

        def block(b, carry):
            r0 = pl.multiple_of(b * qb, qb)
            kpos = kj + (r0 - WINDOW)
            mask = band & (kpos >= 0) & (kpos < t)
            units = []
            for hk in range(2):
                cols = slice(hk * LANES, (hk + 1) * LANES)
                segs_a = [
                    (kctx_s[hk], vctx_s[2 * hk], vctx_s[2 * hk + 1], None),
                    (ka_s[hk, pl.ds(r0, span), :], va_s[2 * hk, pl.ds(r0, span), :],
                     va_s[2 * hk + 1, pl.ds(r0, span), :], mask),
                ]
                sinks = (sink_ref[layer, 2 * hk], sink_ref[layer, 2 * hk + 1])
                units.append((qa_s[pl.ds(r0, qb), cols], segs_a, sinks))
                segs_c = [(kc_s[hk], vc_s[2 * hk], vc_s[2 * hk + 1], None)]
                units.append((qc_s[pl.ds(r0, qb), cols], segs_c, None))
            outs = _attend_many(units)
            for hk in range(2):
                cols = slice(hk * LANES, (hk + 1) * LANES)
                ao_ref[pl.ds(r0, qb), cols] = outs[2 * hk].astype(BF16)
                co_ref[pl.ds(r0, qb), cols] = outs[2 * hk + 1].astype(BF16)
            return carry

        lax.fori_loop(0, t // qb, block, 0)
    else:
        units = []
        for q in range(nseq):
            seq = slice(q * t, (q + 1) * t)
            for hk in range(2):
                cols = slice(hk * LANES, (hk + 1) * LANES)
                sinks = (sink_ref[layer, 2 * hk], sink_ref[layer, 2 * hk + 1])
                units.append((qa_s[seq, cols],
                              [(ka_s[hk, seq, :], va_s[2 * hk, seq, :], va_s[2 * hk + 1, seq, :], None)], sinks))
                units.append((qc_s[seq, cols],
                              [(kc_s[hk, seq, :], vc_s[2 * hk, seq, :], vc_s[2 * hk + 1, seq, :], None)], None))
        outs = _attend_many(units)
        for q in range(nseq):
            seq = slice(q * t, (q + 1) * t)
            for hk in range(2):
                cols = slice(hk * LANES, (hk + 1) * LANES)
                ao_ref[seq, cols] = outs[4 * q + 2 * hk].astype(BF16)
                co_ref[seq, cols] = outs[4 * q + 2 * hk + 1].astype(BF16)


def _attn_call(has_ctx, t, nseq, n_batch, row_block0, layer, za, zc, sink, cqn, ckn, seg, prev=None, rope=None,
               ctx=None, experts=None):
    n_tok = za.shape[0]
    depth = sink.shape[0]
    assert n_batch % nseq == 0 and (nseq == 1 or not has_ctx)
    tok_spec = lambda w: pl.BlockSpec((nseq * t, w), lambda b, *_: (row_block0 + b, 0))
    const = lambda shape: pl.BlockSpec(shape, lambda b, *_: (0,) * len(shape))
    layer_spec = lambda shape: pl.BlockSpec((1,) + shape, lambda b, *_: (layer,) + (0,) * len(shape))
    in_specs = [tok_spec(ZA_W), tok_spec(ZC_W), layer_spec((1, Q_W)), layer_spec((1, KV_W)), const((2, LANES, LANES))]
    args = [za, zc, cqn, ckn, seg]
    out_specs = [tok_spec(Q_W), tok_spec(Q_W)]
    out_shape = [jax.ShapeDtypeStruct((n_tok, Q_W), BF16), jax.ShapeDtypeStruct((n_tok, Q_W), BF16)]
    if has_ctx:
        n_ctx = ctx[0].shape[2]
        in_specs += [const((t, LANES)), const((t, LANES))]
        args += list(rope)
        in_specs += [pl.BlockSpec((1, 1, n_ctx, LANES), lambda b, *_: (b, layer, 0, 0))] * 4
        args += list(ctx)
        scratch = [
            pltpu.VMEM((2, t + 2 * WINDOW, LANES), BF16), pltpu.VMEM((4, t + 2 * WINDOW, LANES), BF16),
            pltpu.VMEM((2, n_ctx + t, LANES), BF16), pltpu.VMEM((4, n_ctx + t, LANES), BF16),
            pltpu.VMEM((2, n_ctx, LANES), BF16), pltpu.VMEM((4, n_ctx, LANES), BF16),
            pltpu.VMEM((t, Q_W), F32), pltpu.VMEM((t, Q_W), F32),
        ]
    else:
        cache_spec = pl.BlockSpec((nseq, 1, t, LANES), lambda b, *_: (b, layer, 0, 0))
        out_specs += [cache_spec] * 4
        out_shape += [jax.ShapeDtypeStruct((n_batch, depth, t, LANES), F32)] * 4
        n_steps = n_batch // nseq
        for w in experts:
            rows_w = w.shape[1] // n_steps
            in_specs.append(pl.BlockSpec((1, rows_w, w.shape[2]), lambda b, *_: (layer, b, 0)))
            args.append(w)
            out_specs.append(pl.BlockSpec((rows_w, w.shape[2]), lambda b, *_: (b, 0)))
            out_shape.append(jax.ShapeDtypeStruct(w.shape[1:], BF16))
        rows = nseq * t
        scratch = [
            pltpu.VMEM((2, rows, LANES), BF16), pltpu.VMEM((4, rows, LANES), BF16),
            pltpu.VMEM((2, rows, LANES), BF16), pltpu.VMEM((4, rows, LANES), BF16),
            pltpu.VMEM((rows, Q_W), F32), pltpu.VMEM((rows, Q_W), F32),
        ]
    n_real = len(args)
    aliases = {}
    if prev is not None:
        first_out = 0 if has_ctx else 2
        for k, arr in enumerate(prev):
            in_specs.append(pl.BlockSpec(memory_space=pl.ANY))
            args.append(arr)
            aliases[1 + n_real + k] = first_out + k

    def body(*refs):
        ins = refs[:1 + n_real]
        rest = refs[1 + len(args):]
        _attn_kernel(has_ctx, t, nseq, layer, *ins, *rest)

    return pl.pallas_call(
        body,
        grid_spec=pltpu.PrefetchScalarGridSpec(
            num_scalar_prefetch=1, grid=(n_batch // nseq,), in_specs=in_specs, out_specs=out_specs,
            scratch_shapes=scratch),
        out_shape=out_shape,
        input_output_aliases=aliases,
        compiler_params=_params(("arbitrary",)),
        name="attn_latent" if has_ctx else "attn_prompt",
    )(sink, *args)


def _stack_pair(x, p):
    return jnp.concatenate([x[:, (2 * p + hl) * B_DIM:(2 * p + hl + 1) * B_DIM] for hl in range(2)], axis=0)


def _delta_kernel(t, nseq, has_s0, *refs):
    if has_s0:
        (zb_ref, abc_ref, abt_ref, conv_ref, prmr_ref, bng_ref, mask_ref,
         s0f_ref, s0b_ref, o_ref, qkv_s, of_s, ob_s, sf_s, sb_s, u_s, wq_s, at_s, kd_s, eg_s,
         pre_s, suf_s, prec_s, sufc_s) = refs
    else:
        (zb_ref, abc_ref, abt_ref, conv_ref, prmr_ref, bng_ref, mask_ref,
         o_ref, sfo_ref, sbo_ref, qkv_s, of_s, ob_s, sf_s, sb_s, u_s, wq_s, at_s, kd_s, eg_s,
         pre_s, suf_s, prec_s, sufc_s) = refs
    n_chunks = t // CHUNK
    n_total = nseq * n_chunks
    s_rows = B_HEADS * B_DIM
    qk_w = B_HEADS * B_DIM

    row = lax.broadcasted_iota(jnp.int32, (t, LANES), 0)
    for q in range(nseq):
        seq = slice(q * t, (q + 1) * t)
        for j in range(3 * B_HEADS):
            cols = slice(j * LANES, (j + 1) * LANES)
            x = zb_ref[seq, cols]
            prev = jnp.where(row == 0, 0.0, pltpu.roll(x, 1, 0))
            nxt = jnp.where(row == t - 1, 0.0, pltpu.roll(x, t - 1, 0))
            y = _silu(prev * conv_ref[0, 0:1, cols] + x * conv_ref[0, 1:2, cols] + nxt * conv_ref[0, 2:3, cols])
            if j < 2 * B_HEADS:
                y = y * lax.rsqrt(jnp.sum(y * y, axis=-1, keepdims=True) + EPS)
            if j < B_HEADS:
                y = y * (B_DIM ** -0.5)
            qkv_s[seq, cols] = y

    if has_s0:
        for q in range(nseq):
            sf_s[q * s_rows:(q + 1) * s_rows, :] = s0f_ref[q, 0]
            sb_s[q * s_rows:(q + 1) * s_rows, :] = s0b_ref[q, 0]
    else:
        sf_s[...] = jnp.zeros_like(sf_s)
        sb_s[...] = jnp.zeros_like(sb_s)

    reps = nseq * t // LANES
    gr = -jnp.tile(jnp.exp(prmr_ref[0, 0]), (1, reps)) * _softplus(abt_ref[...] + jnp.tile(prmr_ref[0, 1], (1, reps)))
    seg_lane = lax.broadcasted_iota(jnp.int32, gr.shape, 1) % CHUNK
    pre, suf = gr, gr
    for s in (1, 2, 4, 8, 16, 32):
        pre = pre + jnp.where(seg_lane >= s, pltpu.roll(pre, s, 1), 0.0)
        suf = suf + jnp.where(seg_lane < CHUNK - s, pltpu.roll(suf, nseq * t - s, 1), 0.0)
    pre_s[...] = pre
    suf_s[...] = suf
    zrows = jnp.zeros((LANES - pre.shape[0], LANES), F32)
    for j in range(reps):
        tile = slice(j * LANES, (j + 1) * LANES)
        prec_s[tile, :] = jnp.concatenate([pre[:, tile], zrows], axis=0).T
        sufc_s[tile, :] = jnp.concatenate([suf[:, tile], zrows], axis=0).T
    lane_lo = lax.broadcasted_iota(jnp.int32, (1, LANES), 1) < CHUNK

    def prepare(cc, carry):
        chains = []
        for k in range(PREP_UNROLL):
            c = cc * PREP_UNROLL + k
            r0 = pl.multiple_of(c * CHUNK, CHUNK)
            b_all = _sigmoid(abc_ref[pl.ds(r0, CHUNK), :])
            run_c = (prec_s[pl.ds(r0, CHUNK), :], sufc_s[pl.ds(r0, CHUNK), :])
            tile0 = pl.multiple_of((cc * PREP_UNROLL + k - k % 2) * CHUNK, LANES)
            run = (pre_s[:, pl.ds(tile0, LANES)], suf_s[:, pl.ds(tile0, LANES)])
            run_r = tuple(pltpu.roll(x, CHUNK, 1) for x in run)
            for p in range(B_HEADS // 2):
                kst = _stack_pair(qkv_s[pl.ds(r0, CHUNK), qk_w:2 * qk_w], p)
                qst = _stack_pair(qkv_s[pl.ds(r0, CHUNK), 0:qk_w], p)
                vst = _stack_pair(qkv_s[pl.ds(r0, CHUNK), 2 * qk_w:3 * qk_w], p)
                kq = _dot_nt(jnp.concatenate([kst, qst], axis=0).astype(BF16), kst.astype(BF16))
                for d in range(2):
                    cg = 4 * d + 2 * p
                    edge = CHUNK - 1 if d == 0 else 0
                    rep_col = lambda x, col: jnp.broadcast_to(x[:, col:col + 1], (CHUNK, LANES))
                    b_rep = jnp.concatenate([rep_col(b_all, 8 + cg + hl) for hl in range(2)], axis=0)
                    gcol = jnp.concatenate([rep_col(run_c[d], cg + hl) for hl in range(2)], axis=0)
                    gtot = jnp.concatenate([rep_col(run_c[d][edge:edge + 1], cg + hl) for hl in range(2)], axis=0)
                    ra = cg
                    if k % 2 == 0:
                        grow = jnp.where(lane_lo, run[d][ra:ra + 1], run_r[d][ra + 1:ra + 2])
                    else:
                        grow = jnp.where(lane_lo, run_r[d][ra:ra + 1], run[d][ra + 1:ra + 2])
                    chains.append(dict(c=c, p=p, d=d, kst=kst, qst=qst, vst=vst, kq=kq, b_st=b_rep,
                                       gcol=gcol, gtot=gtot, grow=grow))

        for ch in chains:
            d, b_st, kq, gcol = ch["d"], ch["b_st"], ch.pop("kq"), ch["gcol"]
            decay = jnp.exp(jnp.minimum(gcol - ch.pop("grow"), 0.0))
            ch["a_mat"] = (b_st * kq[:PAIR]) * (decay * mask_ref[2 * d + 1])
            ch["attn"] = (kq[PAIR:] * (decay * mask_ref[2 * d])).astype(BF16)
            ch["t_inv"] = mask_ref[4] - ch["a_mat"] * mask_ref[5]
        for lvl in range(N_LEVELS - 1):
            for ch in chains:
                ch["t16"] = ch["t_inv"].astype(BF16)
                ch["et"] = _dot((ch["a_mat"] * mask_ref[6 + lvl]).astype(BF16), ch["t16"])
            for ch in chains:
                ch["t_inv"] = ch["t_inv"] - _dot(ch.pop("t16"), ch.pop("et").astype(BF16))
        for ch in chains:
            egc = jnp.exp(ch["gcol"])
            rk = jnp.concatenate([ch["b_st"] * ch["vst"], (ch["b_st"] * egc) * ch["kst"]], axis=1)
            ch["rk"] = _dot(ch.pop("t_inv").astype(BF16), rk.astype(BF16))
            ch["qp16"] = (ch["qst"] * egc).astype(BF16)
        for ch in chains:
            c, p, d, rk, qp16 = ch["c"], ch["p"], ch["d"], ch["rk"], ch["qp16"]
            pair_rows = slice(p * PAIR, (p + 1) * PAIR)
            w16 = rk[:, B_DIM:].astype(BF16)
            u_s[d, c, pair_rows, :] = rk[:, :B_DIM]
            at_s[d, c, p] = ch["attn"]
            kd_s[d, c, pair_rows, :] = (ch["kst"] * jnp.exp(ch["gtot"] - ch["gcol"])).astype(BF16)
            eg = jnp.exp(ch["gtot"])
            for hl in range(2):
                h = 2 * p + hl
                rows = slice(hl * CHUNK, (hl + 1) * CHUNK)
                wq_s[d, c, h * 2 * CHUNK:h * 2 * CHUNK + CHUNK, :] = w16[rows]
                wq_s[d, c, h * 2 * CHUNK + CHUNK:(h + 1) * 2 * CHUNK, :] = qp16[rows]
                eg_s[d, c, h * SUBLANES:(h + 1) * SUBLANES, :] = eg[hl * CHUNK:hl * CHUNK + SUBLANES, :]
        return carry

    lax.fori_loop(0, n_total // PREP_UNROLL, prepare, 0)

    def scan_step(i, carry):
        units = []
        for q in range(nseq):
            for d, s_ref, o_s in ((0, sf_s, of_s), (1, sb_s, ob_s)):
                c = q * n_chunks + (i if d == 0 else n_chunks - 1 - i)
                units.append(dict(q=q, d=d, c=c, s_ref=s_ref, o_s=o_s, r0=pl.multiple_of(c * CHUNK, CHUNK)))
        for un in units:
            q, d, c, s_ref = un["q"], un["d"], un["c"], un["s_ref"]
            un["x"] = []
            for h in range(B_HEADS):
                srows = slice(q * s_rows + h * B_DIM, q * s_rows + (h + 1) * B_DIM)
                un["x"].append(_dot(wq_s[d, c, h * 2 * CHUNK:(h + 1) * 2 * CHUNK, :], s_ref[srows, :].astype(BF16)))
        for un in units:
            d, c = un["d"], un["c"]
            un["vp16"], un["o"] = [], []
            for p in range(B_HEADS // 2):
                xs = un["x"][2 * p:2 * p + 2]
                v_new = jnp.concatenate(
                    [u_s[d, c, (2 * p + hl) * CHUNK:(2 * p + hl + 1) * CHUNK, :] - xs[hl][:CHUNK] for hl in range(2)],
                    axis=0)
                vp16 = v_new.astype(BF16)
                un["vp16"].append(vp16)
                un["o"].append(jnp.concatenate([xs[hl][CHUNK:] for hl in range(2)], axis=0)
                               + _dot(at_s[d, c, p], vp16))
        for un in units:
            q, d, c, s_ref, o_s, r0 = un["q"], un["d"], un["c"], un["s_ref"], un["o_s"], un["r0"]
            for h in range(B_HEADS):
                p, hl = divmod(h, 2)
                rows = slice(hl * CHUNK, (hl + 1) * CHUNK)
                srows = slice(q * s_rows + h * B_DIM, q * s_rows + (h + 1) * B_DIM)
                upd = _dot_tn(kd_s[d, c, h * CHUNK:(h + 1) * CHUNK, :], un["vp16"][p][rows])
                eg = jnp.tile(eg_s[d, c, h * SUBLANES:(h + 1) * SUBLANES, :], (B_DIM // SUBLANES, 1))
                s_ref[srows, :] = s_ref[srows, :] * eg + upd
                o_s[pl.ds(r0, CHUNK), h * B_DIM:(h + 1) * B_DIM] = un["o"][p][rows]
        return carry

    lax.fori_loop(0, n_chunks, scan_step, 0)

    if not has_s0:
        for q in range(nseq):
            sfo_ref[q, 0] = sf_s[q * s_rows:(q + 1) * s_rows, :]
            sbo_ref[q, 0] = sb_s[q * s_rows:(q + 1) * s_rows, :]

    for h in range(B_HEADS):
        cols = slice(h * B_DIM, (h + 1) * B_DIM)
        x = of_s[:, cols] + ob_s[:, cols]
        yn = x * lax.rsqrt(jnp.mean(x * x, axis=-1, keepdims=True) + EPS) * bng_ref[0]
        o_ref[:, cols] = (yn * _silu(zb_ref[:, 3 * qk_w + h * B_DIM:3 * qk_w + (h + 1) * B_DIM])).astype(BF16)


def _delta_call(has_s0, t, nseq, n_batch, row_block0, layer, zb, zab, zabt, conv, prmr, bng, masks,
                prev=None, s0=None):
    n_tok = zb.shape[0]
    depth = conv.shape[0]
    n_chunks = nseq * (t // CHUNK)
    assert n_chunks % PREP_UNROLL == 0 and PREP_UNROLL % 2 == 0 and n_batch % nseq == 0
    tok_spec = lambda w: pl.BlockSpec((nseq * t, w), lambda b: (row_block0 + b, 0))
    const = lambda shape: pl.BlockSpec(shape, lambda b: (0,) * len(shape))
    layer_spec = lambda shape: pl.BlockSpec((1,) + shape, lambda b: (layer,) + (0,) * len(shape))
    s_shape = (B_HEADS * B_DIM, B_DIM)
    s_spec = pl.BlockSpec((nseq, 1) + s_shape, lambda b: (b, layer, 0, 0))
    n_ab = zabt.shape[0]
    in_specs = [
        tok_spec(ZB_W), tok_spec(ZAB_W),
        pl.BlockSpec((n_ab, nseq * t), lambda b: (0, row_block0 + b)),
        layer_spec((3, 3 * B_HEADS * B_DIM)), layer_spec((2, n_ab, LANES)),
        layer_spec((1, B_DIM)),
        const((5 + N_LEVELS, PAIR, PAIR)),
    ]
    args = [zb, zab, zabt, conv, prmr, bng, masks]
    out_specs = [tok_spec(B_HEADS * B_DIM)]
    out_shape = [jax.ShapeDtypeStruct((n_tok, B_HEADS * B_DIM), BF16)]
    if has_s0:
        in_specs += [s_spec, s_spec]
        args += [s0[0], s0[1]]
    else:
        out_specs += [s_spec, s_spec]
        out_shape += [jax.ShapeDtypeStruct((n_batch, depth) + s_shape, F32)] * 2
    n_real = len(args)
    aliases = {}
    if prev is not None:
        first_out = 0 if has_s0 else 1
        for k, arr in enumerate(prev):
            in_specs.append(pl.BlockSpec(memory_space=pl.ANY))
            args.append(arr)
            aliases[n_real + k] = first_out + k
    rows = nseq * t
    scratch = [
        pltpu.VMEM((rows, 3 * B_HEADS * B_DIM), F32),
        pltpu.VMEM((rows, B_HEADS * B_DIM), F32), pltpu.VMEM((rows, B_HEADS * B_DIM), F32),
        pltpu.VMEM((nseq * s_shape[0], B_DIM), F32), pltpu.VMEM((nseq * s_shape[0], B_DIM), F32),
        pltpu.VMEM((2, n_chunks, BD, B_DIM), F32),
        pltpu.VMEM((2, n_chunks, 2 * BD, B_DIM), BF16),
        pltpu.VMEM((2, n_chunks, B_HEADS // 2, PAIR, PAIR), BF16),
        pltpu.VMEM((2, n_chunks, BD, B_DIM), BF16),
        pltpu.VMEM((2, n_chunks, B_HEADS * SUBLANES, LANES), F32),
        pltpu.VMEM((n_ab, rows), F32), pltpu.VMEM((n_ab, rows), F32),
        pltpu.VMEM((rows, LANES), F32), pltpu.VMEM((rows, LANES), F32),
    ]

    def body(*refs):
        _delta_kernel(t, nseq, has_s0, *refs[:n_real], *refs[len(args):])

    return pl.pallas_call(
        body,
        grid=(n_batch // nseq,),
        in_specs=in_specs,
        out_specs=out_specs,
        out_shape=out_shape,
        scratch_shapes=scratch,
        input_output_aliases=aliases,
        compiler_params=_params(("arbitrary",)),
        name="delta_latent" if has_s0 else "delta_prompt",
    )(*args)


def _outproj_router(x, ma, mb, mc, m, g, wo_ref, wr_ref, br):
    n = x.shape[0]
    b0, c0 = Q_W, Q_W + B_HEADS * B_DIM
    y = (_dot(ma.astype(BF16), wo_ref[0, 0:b0, :])
         + _dot(mb.astype(BF16), wo_ref[0, b0:c0, :])
         + _dot(mc.astype(BF16), wo_ref[0, c0:c0 + Q_W, :]))
    x1 = x + m[2:3] * y
    h2 = _modulated_norm(x1, g, m[3:4], m[4:5])
    hi, lo = _split2(h2)

    hw = _dot(hi, wr_ref[0])
    logits = (hw[:, :LANES] + hw[:, LANES:] + _dot(lo, wr_ref[0, :, :LANES]) + br).T
    gl = logits[0:N_GROUPS]
    grow = lax.broadcasted_iota(jnp.int32, gl.shape, 0)
    gmax = gl.max(axis=0, keepdims=True)
    g_sel = jnp.where(gl == gmax, grow, N_GROUPS).min(axis=0, keepdims=True)
    g_w = 1.0 / jnp.exp(gl - gmax).sum(axis=0, keepdims=True)
    el = logits[EXPERT_ROW0:EXPERT_ROW0 + N_EXPERTS]
    e_idx = lax.broadcasted_iota(jnp.int32, el.shape, 0)
    el = jnp.where((e_idx // EXPERTS_PER_GROUP) == g_sel, el, -jnp.inf)
    m1 = el.max(axis=0, keepdims=True)
    i1 = jnp.where(el == m1, e_idx, N_EXPERTS).min(axis=0, keepdims=True)
    el2 = jnp.where(e_idx == i1, -jnp.inf, el)
    m2 = el2.max(axis=0, keepdims=True)
    i2 = jnp.where(el2 == m2, e_idx, N_EXPERTS).min(axis=0, keepdims=True)
    tt = jnp.exp(m2 - m1)
    w1 = g_w / (1.0 + tt)
    w2 = w1 * tt
    gate_t = jnp.where(e_idx == i1, w1, 0.0) + jnp.where(e_idx == i2, w2, 0.0)
    gate = jnp.concatenate([gate_t, jnp.zeros((LANES - N_EXPERTS, n), F32)], axis=0).T
    return x1, hi, gate


def _ffn_kernel(x_ref, ma_ref, mb_ref, mc_ref, mod_ref, g_ref, wo_ref, wr_ref, br_ref, w1_ref, w3_ref, w2_ref,
                o_ref, h_s, gate_s):
    j = pl.program_id(1)
    tm = x_ref.shape[0]
    th = w1_ref.shape[1]
    m = mod_ref[0, 0]

    @pl.when(j == 0)
    def _():
        x1, hi, gate = _outproj_router(x_ref[...], ma_ref[...], mb_ref[...], mc_ref[...],
                                       m, g_ref[0], wo_ref, wr_ref, br_ref[0])
        o_ref[...] = x1
        h_s[...] = hi
        gate_s[...] = gate

    @pl.when(j > 0)
    def _():
        h = h_s[...]
        hid = _silu(_dot(h, w1_ref[...])) * _dot(h, w3_ref[...])
        gate = gate_s[...]
        lane = lax.broadcasted_iota(jnp.int32, gate.shape, 1)
        n_e = th // D_EXPERT
        col = lax.broadcasted_iota(jnp.int32, hid.shape, 1) // D_EXPERT
        gmat = jnp.zeros(hid.shape, F32)
        for e in range(n_e):
            ge = jnp.where(lane == (j - 1) * n_e + e, gate, 0.0).sum(axis=1, keepdims=True)
            gmat = jnp.where(col == e, ge, gmat)
        o_ref[...] += m[5:6] * _dot((hid * gmat).astype(BF16), w2_ref[...])


def _ffn_call(layer, x, ma, mb, mc, mods, g, wo, wr, br, w1, w3, w2, slot_fn, tm, th):
    n_tok = x.shape[0]
    n_h = w1.shape[1] // th
    hidden = lambda j: jnp.where(j == 0, n_h - 1, j - 1)
    tok = lambda w: pl.BlockSpec((tm, w), lambda i, j: (i, 0))
    layer_spec = lambda shape: pl.BlockSpec((1,) + shape, lambda i, j: (layer,) + (0,) * len(shape))
    return pl.pallas_call(
        _ffn_kernel,
        grid=(n_tok // tm, n_h + 1),
        in_specs=[tok(D_MODEL), tok(Q_W), tok(B_HEADS * B_DIM), tok(Q_W),
                  pl.BlockSpec((1, 1, 6, D_MODEL), lambda i, j: (layer, slot_fn(i), 0, 0)),
                  layer_spec((1, D_MODEL)), layer_spec((D_MODEL, D_MODEL)), layer_spec((D_MODEL, 2 * LANES)),
                  layer_spec((1, LANES)),
                  pl.BlockSpec((D_MODEL, th), lambda i, j: (0, hidden(j))),
                  pl.BlockSpec((D_MODEL, th), lambda i, j: (0, hidden(j))),
                  pl.BlockSpec((th, D_MODEL), lambda i, j: (hidden(j), 0))],
        out_specs=tok(D_MODEL),
        out_shape=jax.ShapeDtypeStruct((n_tok, D_MODEL), F32),
        scratch_shapes=[pltpu.VMEM((tm, D_MODEL), BF16), pltpu.VMEM((tm, LANES), F32)],
        compiler_params=_params(("arbitrary", "arbitrary")),
        name="ffn",
    )(x, ma, mb, mc, mods, g, wo, wr, br, w1, w3, w2)


def _final_norm_kernel(x_ref, g_ref, o_ref):
    x = x_ref[...]
    o_ref[...] = x * lax.rsqrt(jnp.mean(x * x, axis=-1, keepdims=True) + EPS) * g_ref[...]


def _final_norm_call(x, g, tm, row0, n_rows):
    blk0 = row0 // tm
    return pl.pallas_call(
        _final_norm_kernel,
        grid=(n_rows // tm,),
        in_specs=[pl.BlockSpec((tm, D_MODEL), lambda i: (blk0 + i, 0)), pl.BlockSpec((1, D_MODEL), lambda i: (0, 0))],
        out_specs=pl.BlockSpec((tm, D_MODEL), lambda i: (i, 0)),
        out_shape=jax.ShapeDtypeStruct((n_rows, D_MODEL), F32),
        compiler_params=_params(("arbitrary",)),
        name="final_norm",
    )(x, g)


def _rope_tables(t):
    pos = np.arange(t)
    n_freq = HEAD_DIM // 4
    inv_freq = ROPE_THETA ** (-jnp.arange(n_freq, dtype=F32) / n_freq)
    row = jnp.asarray(pos // GRID_W, F32)
    col = jnp.asarray(pos % GRID_W, F32)
    ang = jnp.concatenate([row[:, None] * inv_freq, col[:, None] * inv_freq], -1)
    cos, sin = jnp.cos(ang), jnp.sin(ang)
    cos_t = jnp.tile(jnp.concatenate([cos, cos], -1), (1, LANES // HEAD_DIM))
    sin_t = jnp.tile(jnp.concatenate([-sin, sin], -1), (1, LANES // HEAD_DIM))
    return cos_t, sin_t


def _delta_tables():
    r = np.arange(PAIR)
    same = (r[:, None] // CHUNK) == (r[None, :] // CHUNK)
    low = same & (r[:, None] >= r[None, :])
    low_s = same & (r[:, None] > r[None, :])
    up = same & (r[:, None] <= r[None, :])
    up_s = same & (r[:, None] < r[None, :])
    levels = []
    for k in range(N_LEVELS):
        s = 1 << k
        levels.append(((r[:, None] // (2 * s)) == (r[None, :] // (2 * s))) & ((r[:, None] // s) != (r[None, :] // s)))
    masks = jnp.asarray(np.stack([low, low_s, up, up_s, np.eye(PAIR, dtype=bool)] + levels).astype(np.float32))
    return masks


def _segment_mean_table():
    r = np.arange(LANES)
    seg = ((r[:, None] // HEAD_DIM) == (r[None, :] // HEAD_DIM)).astype(np.float32) / HEAD_DIM
    hi = jnp.asarray(seg, BF16)
    lo = (jnp.asarray(seg) - hi.astype(F32)).astype(BF16)
    return jnp.stack([hi, lo])


def kernel(x_prompt, x_sample, cache_a_k, cache_a_v, cache_c_k, cache_c_v, state_b_fwd, state_b_bwd, c, c_ctx, w_mod, b_mod, norm1_g, norm2_g, w_in, a_sink, b_conv, b_a_log, b_dt_bias, b_norm_g, c_q_norm, c_k_norm, w_out, w_group, b_group, w_expert, b_expert, w1, w3, w2, final_norm_g):
    n_p, t_p, d = x_prompt.shape
    n_s, t_s, _ = x_sample.shape
    depth = w_in.shape[0]
    past = cache_a_k.shape[2]
    tok_p = n_p * t_p
    n_tok = tok_p + n_s * t_s
    assert d == D_MODEL and tok_p % t_s == 0 and t_s % max(TM_PROJ, TM_FFN) == 0 and t_p % 256 == 0

    w_in_t = jnp.swapaxes(w_in, 1, 2)
    w_out16 = w_out.astype(BF16)
    pad_g = jnp.zeros((depth, d, EXPERT_ROW0 - N_GROUPS), F32)
    pad_e = jnp.zeros((depth, d, LANES - EXPERT_ROW0 - N_EXPERTS), F32)
    w_r = jnp.concatenate([w_group, pad_g, w_expert, pad_e], -1)
    w_r_hi = w_r.astype(BF16)
    w_r2 = jnp.concatenate([w_r_hi, (w_r - w_r_hi.astype(F32)).astype(BF16)], axis=-1)
    b_r = jnp.concatenate([b_group, pad_g[:, 0], b_expert, pad_e[:, 0]], -1)[:, None, :]
    cqn = jnp.tile(c_q_norm, (1, 4))[:, None, :]
    ckn = jnp.tile(c_k_norm, (1, 2))[:, None, :]
    gate_prm = jnp.stack([b_a_log.reshape(depth, 8), b_dt_bias.reshape(depth, 8)], 1)
    prmr = jnp.broadcast_to(jnp.pad(gate_prm, ((0, 0), (0, 0), (0, N_AB - 8)))[..., None],
                            (depth, 2, N_AB, LANES))
    cos_t, sin_t = _rope_tables(t_s)
    masks = _delta_tables()
    seg = _segment_mean_table()

    cond = jnp.concatenate([c_ctx[None, :], c], axis=0)
    cond_b = jnp.broadcast_to(cond[:, :, None], cond.shape + (LANES,))
    mods_all = _mods_call(cond_b, w_mod, b_mod).reshape(depth, SUBLANES, 6, d)

    def slot_fn(tm):
        per_s = t_s // tm
        first = tok_p // tm
        return lambda i: jnp.where(i < first, 0, 1 + (i - first) // per_s)

    xs = (x_prompt.reshape(tok_p, d), x_sample.reshape(n_s * t_s, d))
    blk_s = tok_p // t_s
    ctx = tuple(a.reshape(n_s, depth, past, LANES) for a in (cache_a_k, cache_a_v, cache_c_k, cache_c_v))
    s0 = tuple(a.reshape(n_s, depth, B_HEADS * B_DIM, B_DIM) for a in (state_b_fwd, state_b_bwd))
    g1, g2, bng = norm1_g[:, None, :], norm2_g[:, None, :], b_norm_g[:, None, :]
    caches = None
    states = None
    for l in range(depth):
        za, zb, zc, zab, zabt, *rest = _inproj_call(l, xs, mods_all, g1, w_in_t, slot_fn(TM_PROJ), TM_PROJ,
                                                    experts=(w3, w2))
        x = rest[0] if len(xs) > 1 else xs[0]
        w3b, w2b = rest[-2:]

        ao, co, *rest = _attn_call(False, t_p, ATTN_NSEQ, n_p, 0, l, za, zc, a_sink, cqn, ckn, seg, prev=caches,
                                   experts=(w1,))
        caches, (w1b,) = rest[:4], rest[4:]
        ao, co = _attn_call(True, t_s, 1, n_s, blk_s, l, za, zc, a_sink, cqn, ckn, seg, prev=(ao, co),
                            rope=(cos_t, sin_t), ctx=ctx)

        bo, *states = _delta_call(False, t_p, DELTA_NSEQ, n_p, 0, l, zb, zab, zabt, b_conv, prmr, bng, masks,
                                  prev=states)
        (bo,) = _delta_call(True, t_s, 1, n_s, blk_s, l, zb, zab, zabt, b_conv, prmr, bng, masks,
                            prev=(bo,), s0=s0)

        x = _ffn_call(l, x, ao, bo, co, mods_all, g2, w_out16, w_r2, b_r, w1b, w3b, w2b, slot_fn(TM_FFN), TM_FFN, TH_FFN)
        xs = (x,)

    y_prompt = _final_norm_call(x, final_norm_g[None], TM_NORM, 0, tok_p).reshape(n_p, t_p, d)
    y_sample = _final_norm_call(x, final_norm_g[None], TM_NORM, tok_p, n_s * t_s).reshape(n_s, t_s, d)
    new_ak, new_av, new_ck, new_cv = (a.reshape(n_p, depth, t_p, 2, HEAD_DIM) for a in caches)
    new_sf, new_sb = (a.reshape(n_p, depth, B_HEADS, B_DIM, B_DIM) for a in states)
    return (y_prompt, y_sample, new_ak, new_av, new_ck, new_cv, new_sf, new_sb)
```

```python
import functools

import jax
import jax.numpy as jnp
import numpy as np
from jax import lax
from jax.experimental import pallas as pl
from jax.experimental.pallas import tpu as pltpu

F32 = jnp.float32
BF16 = jnp.bfloat16

D_MODEL = 1024
GRID_W = 64
EPS = 1e-6
NEG_INF = -1e30
ROPE_THETA = 10000.0
HEAD_DIM = 64
Q_W = 256
KV_W = 128
WINDOW = 128
Q_BLOCK = 128
B_HEADS = 4
B_DIM = 128
CHUNK = 64
BD = B_HEADS * CHUNK
PAIR = 2 * CHUNK
N_LEVELS = 6
PREP_UNROLL = 4
DELTA_NSEQ = 4
ATTN_NSEQ = 4
N_GROUPS = 4
EXPERTS_PER_GROUP = 4
N_EXPERTS = 16
D_EXPERT = 256
EXPERT_ROW0 = 8

LANES = 128
SUBLANES = 8
VMEM_LIMIT = 60000 * 1024

TM_PROJ = 512
TM_FFN, TH_FFN = 1024, 1024
TM_NORM = 512
CAST_STEPS = 16
MODS_TN = 1536

ZA_W, ZB_W, ZC_W, ZAB_W = 512, 2048, 512, 128
N_AB = 16
Z_W = ZA_W + ZB_W + ZC_W + ZAB_W


def _sigmoid(x):
    return 0.5 * jnp.tanh(0.5 * x) + 0.5


def _silu_half(h):
    return h + h * jnp.tanh(h)


def _silu(x):
    return _silu_half(0.5 * x)


def _softplus(x):
    return jnp.maximum(x, 0.0) + jnp.log1p(jnp.exp(-jnp.abs(x)))


def _dot(a, b):
    return jnp.dot(a, b, preferred_element_type=F32)


def _dot_nt(a, b):
    return lax.dot_general(a, b, (((1,), (1,)), ((), ())), preferred_element_type=F32)


def _dot_tn(a, b):
    return lax.dot_general(a, b, (((0,), (0,)), ((), ())), preferred_element_type=F32)


def _split2(x):
    hi = x.astype(BF16)
    lo = (x - hi.astype(F32)).astype(BF16)
    return hi, lo


def _params(sem=None):
    return pltpu.CompilerParams(dimension_semantics=sem, vmem_limit_bytes=VMEM_LIMIT)


def _mods_kernel(cond_ref, w_ref, b_ref, o_ref, act_s):
    n_cond = cond_ref.shape[0]
    tn = w_ref.shape[2]
    reps = tn // LANES

    @pl.when((pl.program_id(0) == 0) & (pl.program_id(1) == 0))
    def _():
        act_s[...] = _silu(cond_ref[...])

    def body(kb, accs):
        r = pl.multiple_of(kb * SUBLANES, SUBLANES)
        w = w_ref[0, pl.ds(r, SUBLANES), :]
        return tuple(acc + jnp.tile(act_s[m, pl.ds(r, SUBLANES), :], (1, reps)) * w for m, acc in enumerate(accs))

    zero = jnp.zeros((SUBLANES, tn), F32)
    accs = lax.fori_loop(0, w_ref.shape[1] // SUBLANES, body, (zero,) * n_cond, unroll=4)
    rows = [jnp.sum(a, axis=0, keepdims=True) + b_ref[0] for a in accs]
    rows.append(jnp.zeros((SUBLANES - n_cond, tn), F32))
    o_ref[0] = jnp.concatenate(rows, axis=0)


def _mods_call(cond_b, w_mod, b_mod):
    depth, d, n = w_mod.shape
    tn = MODS_TN
    n_cond = cond_b.shape[0]
    return pl.pallas_call(
        _mods_kernel,
        grid=(depth, n // tn),
        in_specs=[
            pl.BlockSpec((n_cond, d, LANES), lambda l, j: (0, 0, 0)),
            pl.BlockSpec((1, d, tn), lambda l, j: (l, 0, j)),
            pl.BlockSpec((1, 1, tn), lambda l, j: (l, 0, j)),
        ],
        out_specs=pl.BlockSpec((1, SUBLANES, tn), lambda l, j: (l, 0, j)),
        out_shape=jax.ShapeDtypeStruct((depth, SUBLANES, n), F32),
        scratch_shapes=[pltpu.VMEM((n_cond, d, LANES), F32)],
        compiler_params=_params(("arbitrary", "arbitrary")),
        name="mods",
    )(cond_b, w_mod, b_mod.reshape(depth, 1, n))


def _x_specs(xs, tm):
    if len(xs) == 1:
        return [pl.BlockSpec((tm, D_MODEL), lambda i, *_: (i, 0))]
    first = xs[0].shape[0] // tm
    return [pl.BlockSpec((tm, D_MODEL), lambda i, *_: (jnp.minimum(i, first - 1), 0)),
            pl.BlockSpec((tm, D_MODEL), lambda i, *_: (jnp.maximum(i - first, 0), 0))]


def _x_tile(x_refs, first):
    if len(x_refs) == 1:
        return x_refs[0][...]
    return jnp.where(pl.program_id(0) < first, x_refs[0][...], x_refs[1][...])


def _modulated_norm(x, g, shift, scale):
    ms = jnp.mean(x * x, axis=-1, keepdims=True)
    y = x * lax.rsqrt(ms + EPS) * g
    return y * (1.0 + scale) + shift


def _inproj_kernel(n_x, first, n_w, *refs):
    x_refs = refs[:n_x]
    mod_ref, g_ref, wt_ref = refs[n_x:n_x + 3]
    w_refs = refs[n_x + 3:n_x + 3 + n_w]
    o0 = n_x + 3 + n_w
    za_ref, zb_ref, zc_ref, zab_ref, zabt_ref = refs[o0:o0 + 5]
    o1 = o0 + 5 + (1 if n_x > 1 else 0)
    wb_refs = refs[o1:o1 + n_w]
    w_s = refs[-1]
    for w_ref, wb_ref in zip(w_refs, wb_refs):
        wb_ref[...] = w_ref[0].astype(BF16)
    @pl.when(pl.program_id(0) == 0)
    def _():
        ab0 = ZA_W + ZB_W
        w_s[0:ab0, :] = wt_ref[0, 0:ab0, :].astype(BF16)
        w_s[ab0:ab0 + ZC_W, :] = wt_ref[0, ab0 + N_AB:ab0 + N_AB + ZC_W, :].astype(BF16)
        w_s[ab0 + ZC_W:ab0 + ZC_W + N_AB, :] = wt_ref[0, ab0:ab0 + N_AB, :].astype(BF16)
        w_s[ab0 + ZC_W + N_AB:Z_W, :] = jnp.zeros((ZAB_W - N_AB, D_MODEL), BF16)

    m = mod_ref[0, 0]
    x = _x_tile(x_refs, first)
    if n_x > 1:
        refs[o0 + 5][...] = x
    h = _modulated_norm(x, g_ref[0], m[0:1], m[1:2]).astype(BF16)
    za_ref[...] = _dot_nt(h, w_s[0:ZA_W, :])
    step = 512
    for j in range(ZB_W // step):
        zb_ref[:, j * step:(j + 1) * step] = _dot_nt(h, w_s[ZA_W + j * step:ZA_W + (j + 1) * step, :])
    zc_ref[...] = _dot_nt(h, w_s[ZA_W + ZB_W:ZA_W + ZB_W + ZC_W, :])
    zab = _dot_nt(h, w_s[ZA_W + ZB_W + ZC_W:Z_W, :])
    zab_ref[...] = zab
    zabt_ref[...] = zab.T[:N_AB]


def _inproj_call(layer, xs, mods, g, w, slot_fn, tm, experts=()):
    n_tok = sum(a.shape[0] for a in xs)
    n_ab = N_AB
    cast_steps = CAST_STEPS
    assert n_tok // tm >= cast_steps
    slab = lambda i: jnp.minimum(i, cast_steps - 1)
    cast_in = [pl.BlockSpec((1, e.shape[1] // cast_steps, e.shape[2]), lambda i: (layer, slab(i), 0)) for e in experts]
    cast_out = [pl.BlockSpec((e.shape[1] // cast_steps, e.shape[2]), lambda i: (slab(i), 0)) for e in experts]
    return pl.pallas_call(
        functools.partial(_inproj_kernel, len(xs), xs[0].shape[0] // tm, len(experts)),
        grid=(n_tok // tm,),
        in_specs=_x_specs(xs, tm) + [
            pl.BlockSpec((1, 1, 6, D_MODEL), lambda i: (layer, slot_fn(i), 0, 0)),
            pl.BlockSpec((1, 1, D_MODEL), lambda i: (layer, 0, 0)),
            pl.BlockSpec((1, w.shape[1], D_MODEL), lambda i: (layer, 0, 0)),
        ] + cast_in,
        out_specs=[
            pl.BlockSpec((tm, ZA_W), lambda i: (i, 0)),
            pl.BlockSpec((tm, ZB_W), lambda i: (i, 0)),
            pl.BlockSpec((tm, ZC_W), lambda i: (i, 0)),
            pl.BlockSpec((tm, ZAB_W), lambda i: (i, 0)),
            pl.BlockSpec((n_ab, tm), lambda i: (0, i)),
        ] + ([pl.BlockSpec((tm, D_MODEL), lambda i: (i, 0))] if len(xs) > 1 else []) + cast_out,
        out_shape=[
            jax.ShapeDtypeStruct((n_tok, ZA_W), F32),
            jax.ShapeDtypeStruct((n_tok, ZB_W), F32),
            jax.ShapeDtypeStruct((n_tok, ZC_W), F32),
            jax.ShapeDtypeStruct((n_tok, ZAB_W), F32),
            jax.ShapeDtypeStruct((n_ab, n_tok), F32),
        ] + ([jax.ShapeDtypeStruct((n_tok, D_MODEL), F32)] if len(xs) > 1 else [])
        + [jax.ShapeDtypeStruct(e.shape[1:], BF16) for e in experts],
        scratch_shapes=[pltpu.VMEM((Z_W, D_MODEL), BF16)],
        compiler_params=_params(("arbitrary",)),
        name="inproj",
    )(*xs, mods, g, w, *experts)


def _lane_lo(shape):
    return lax.broadcasted_iota(jnp.int32, shape, len(shape) - 1) % LANES < HEAD_DIM


def _store_kdup(dst_ref, off, k):
    n = k.shape[0]
    r = pltpu.roll(k, HEAD_DIM, 1)
    lo = _lane_lo(k.shape)
    dst_ref[0, off:off + n, :] = jnp.where(lo, k, r).astype(BF16)
    dst_ref[1, off:off + n, :] = jnp.where(lo, r, k).astype(BF16)


def _store_vsplit(dst_ref, off, v):
    n = v.shape[0]
    r = pltpu.roll(v, HEAD_DIM, 1)
    lo = _lane_lo(v.shape)
    z = jnp.zeros_like(v)
    dst_ref[0, off:off + n, :] = jnp.where(lo, v, z).astype(BF16)
    dst_ref[1, off:off + n, :] = jnp.where(lo, z, r).astype(BF16)
    dst_ref[2, off:off + n, :] = jnp.where(lo, r, z).astype(BF16)
    dst_ref[3, off:off + n, :] = jnp.where(lo, z, v).astype(BF16)


def _rope(x, cos, sin):
    first = (lax.broadcasted_iota(jnp.int32, x.shape, 1) // (HEAD_DIM // 2)) % 2 == 0
    partner = jnp.where(first, pltpu.roll(x, LANES - HEAD_DIM // 2, 1), pltpu.roll(x, HEAD_DIM // 2, 1))
    return x * cos + partner * sin


def _head_rmsnorm(x, g, seg_hi, seg_lo):
    hi, lo = _split2(x * x)
    ms = _dot(hi, seg_hi) + _dot(lo, seg_hi) + _dot(hi, seg_lo)
    return x * lax.rsqrt(ms + EPS) * g


def _attend_many(units):
    qb = units[0][0].shape[0]
    lo = _lane_lo(units[0][0].shape)
    all_scores = []
    for qt, segs, _ in units:
        z = jnp.zeros_like(qt)
        qs = jnp.concatenate([jnp.where(lo, qt, z), jnp.where(lo, z, qt)], axis=0).astype(BF16)
        scores = []
        for kdup, _, _, mask in segs:
            s = _dot_nt(qs, kdup)
            if mask is not None:
                s = jnp.where(mask, s, NEG_INF)
            scores.append(s)
        all_scores.append(scores)
    probs = []
    for (qt, segs, sink_pair), scores in zip(units, all_scores):
        m = scores[0].max(axis=1, keepdims=True)
        for s in scores[1:]:
            m = jnp.maximum(m, s.max(axis=1, keepdims=True))
        if sink_pair is not None:
            row_a = lax.broadcasted_iota(jnp.int32, (2 * qb, 1), 0) < qb
            sink = jnp.where(row_a, sink_pair[0], sink_pair[1])
            m = jnp.maximum(m, sink)
            denom = jnp.exp(sink - m)
        else:
            denom = jnp.zeros((2 * qb, 1), F32)
        ps = []
        for s in scores:
            p = jnp.exp(s - m)
            denom = denom + p.sum(axis=1, keepdims=True)
            ps.append(p.astype(BF16))
        probs.append((ps, 1.0 / denom))
    outs = []
    for (qt, segs, _), (ps, inv) in zip(units, probs):
        acc = jnp.zeros((qb, LANES), F32)
        for pb, (_, vlo, vhi, _) in zip(ps, segs):
            acc = acc + _dot(pb[:qb], vlo) + _dot(pb[qb:], vhi)
        outs.append(acc * jnp.where(lo, inv[:qb], inv[qb:]))
    return outs


def _attn_kernel(has_ctx, t, nseq, layer, *refs):
    if has_ctx:
        (sink_ref, za_ref, zc_ref, cqn_ref, ckn_ref, seg_ref, cos_ref, sin_ref,
         cak_ref, cav_ref, cck_ref, ccv_ref,
         ao_ref, co_ref,
         ka_s, va_s, kc_s, vc_s, kctx_s, vctx_s, qa_s, qc_s) = refs
    else:
        (sink_ref, za_ref, zc_ref, cqn_ref, ckn_ref, seg_ref, w1_ref,
         ao_ref, co_ref, nak_ref, nav_ref, nck_ref, ncv_ref, w1b_ref,
         ka_s, va_s, kc_s, vc_s, qa_s, qc_s) = refs
        w1b_ref[...] = w1_ref[0].astype(BF16)
    scale = HEAD_DIM ** -0.5
    seg_hi = seg_ref[0]
    seg_lo = seg_ref[1]
    piece = 256
    n_ctx = cak_ref.shape[2] if has_ctx else 0

    for p0 in range(0, nseq * t, piece):
        rows = slice(p0, p0 + piece)
        ak = za_ref[rows, Q_W:Q_W + KV_W]
        av = za_ref[rows, Q_W + KV_W:Q_W + 2 * KV_W]
        ck = _head_rmsnorm(zc_ref[rows, Q_W:Q_W + KV_W], ckn_ref[0], seg_hi, seg_lo)
        cv = zc_ref[rows, Q_W + KV_W:Q_W + 2 * KV_W]
        if has_ctx:
            cos = cos_ref[rows, :]
            sin = sin_ref[rows, :]
            ak = _rope(ak, cos, sin)
            ck = _rope(ck, cos, sin)
            _store_kdup(ka_s, WINDOW + p0, ak)
            _store_vsplit(va_s, WINDOW + p0, av)
            _store_kdup(kc_s, n_ctx + p0, ck)
            _store_vsplit(vc_s, n_ctx + p0, cv)
        else:
            crow = slice(p0 % t, p0 % t + piece)
            nak_ref[p0 // t, 0, crow, :] = ak
            nav_ref[p0 // t, 0, crow, :] = av
            nck_ref[p0 // t, 0, crow, :] = ck
            ncv_ref[p0 // t, 0, crow, :] = cv
            _store_kdup(ka_s, p0, ak)
            _store_vsplit(va_s, p0, av)
            _store_kdup(kc_s, p0, ck)
            _store_vsplit(vc_s, p0, cv)
        for hk in range(2):
            cols = slice(hk * LANES, (hk + 1) * LANES)
            aq = za_ref[rows, cols]
            cq = _head_rmsnorm(zc_ref[rows, cols], cqn_ref[0, :, cols], seg_hi, seg_lo)
            if has_ctx:
                aq = _rope(aq, cos, sin)
                cq = _rope(cq, cos, sin)
            qa_s[rows, cols] = aq * scale
            qc_s[rows, cols] = cq * scale

    if has_ctx:
        zpad = jnp.zeros((WINDOW, LANES), BF16)
        for i in range(2):
            ka_s[i, 0:WINDOW, :] = zpad
            ka_s[i, WINDOW + t:2 * WINDOW + t, :] = zpad
        for i in range(4):
            va_s[i, 0:WINDOW, :] = zpad
            va_s[i, WINDOW + t:2 * WINDOW + t, :] = zpad
        for p0 in range(0, n_ctx, piece):
            rows = slice(p0, p0 + piece)
            _store_kdup(kctx_s, p0, cak_ref[0, 0, rows, :])
            _store_vsplit(vctx_s, p0, cav_ref[0, 0, rows, :])
            _store_kdup(kc_s, p0, cck_ref[0, 0, rows, :])
            _store_vsplit(vc_s, p0, ccv_ref[0, 0, rows, :])

        qb = Q_BLOCK
        span = qb + 2 * WINDOW
        qi = lax.broadcasted_iota(jnp.int32, (2 * qb, span), 0) % qb
        kj = lax.broadcasted_iota(jnp.int32, (2 * qb, span), 1)
        band = jnp.abs(kj - WINDOW - qi) <= WINDOW

        def block(b, carry):
            r0 = pl.multiple_of(b * qb, qb)
            kpos = kj + (r0 - WINDOW)
            mask = band & (kpos >= 0) & (kpos < t)
            units = []
            for hk in range(2):
                cols = slice(hk * LANES, (hk + 1) * LANES)
                segs_a = [
                    (kctx_s[hk], vctx_s[2 * hk], vctx_s[2 * hk + 1], None),
                    (ka_s[hk, pl.ds(r0, span), :], va_s[2 * hk, pl.ds(r0, span), :],
                     va_s[2 * hk + 1, pl.ds(r0, span), :], mask),
                ]
                sinks = (sink_ref[layer, 2 * hk], sink_ref[layer, 2 * hk + 1])
                units.append((qa_s[pl.ds(r0, qb), cols], segs_a, sinks))
                segs_c = [(kc_s[hk], vc_s[2 * hk], vc_s[2 * hk + 1], None)]
                units.append((qc_s[pl.ds(r0, qb), cols], segs_c, None))
            outs = _attend_many(units)
            for hk in range(2):
                cols = slice(hk * LANES, (hk + 1) * LANES)
                ao_ref[pl.ds(r0, qb), cols] = outs[2 * hk].astype(BF16)
                co_ref[pl.ds(r0, qb), cols] = outs[2 * hk + 1].astype(BF16)
            return carry

        lax.fori_loop(0, t // qb, block, 0)
    else:
        units = []
        for q in range(nseq):
            seq = slice(q * t, (q + 1) * t)
            for hk in range(2):
                cols = slice(hk * LANES, (hk + 1) * LANES)
                sinks = (sink_ref[layer, 2 * hk], sink_ref[layer, 2 * hk + 1])
                units.append((qa_s[seq, cols],
                              [(ka_s[hk, seq, :], va_s[2 * hk, seq, :], va_s[2 * hk + 1, seq, :], None)], sinks))
                units.append((qc_s[seq, cols],
                              [(kc_s[hk, seq, :], vc_s[2 * hk, seq, :], vc_s[2 * hk + 1, seq, :], None)], None))
        outs = _attend_many(units)
        for q in range(nseq):
            seq = slice(q * t, (q + 1) * t)
            for hk in range(2):
                cols = slice(hk * LANES, (hk + 1) * LANES)
                ao_ref[seq, cols] = outs[4 * q + 2 * hk].astype(BF16)
                co_ref[seq, cols] = outs[4 * q + 2 * hk + 1].astype(BF16)


def _attn_call(has_ctx, t, nseq, n_batch, row_block0, layer, za, zc, sink, cqn, ckn, seg, prev=None, rope=None,
               ctx=None, experts=None):
    n_tok = za.shape[0]
    depth = sink.shape[0]
    assert n_batch % nseq == 0 and (nseq == 1 or not has_ctx)
    tok_spec = lambda w: pl.BlockSpec((nseq * t, w), lambda b, *_: (row_block0 + b, 0))
    const = lambda shape: pl.BlockSpec(shape, lambda b, *_: (0,) * len(shape))
    layer_spec = lambda shape: pl.BlockSpec((1,) + shape, lambda b, *_: (layer,) + (0,) * len(shape))
    in_specs = [tok_spec(ZA_W), tok_spec(ZC_W), layer_spec((1, Q_W)), layer_spec((1, KV_W)), const((2, LANES, LANES))]
    args = [za, zc, cqn, ckn, seg]
    out_specs = [tok_spec(Q_W), tok_spec(Q_W)]
    out_shape = [jax.ShapeDtypeStruct((n_tok, Q_W), BF16), jax.ShapeDtypeStruct((n_tok, Q_W), BF16)]
    if has_ctx:
        n_ctx = ctx[0].shape[2]
        in_specs += [const((t, LANES)), const((t, LANES))]
        args += list(rope)
        in_specs += [pl.BlockSpec((1, 1, n_ctx, LANES), lambda b, *_: (b, layer, 0, 0))] * 4
        args += list(ctx)
        scratch = [
            pltpu.VMEM((2, t + 2 * WINDOW, LANES), BF16), pltpu.VMEM((4, t + 2 * WINDOW, LANES), BF16),
            pltpu.VMEM((2, n_ctx + t, LANES), BF16), pltpu.VMEM((4, n_ctx + t, LANES), BF16),
            pltpu.VMEM((2, n_ctx, LANES), BF16), pltpu.VMEM((4, n_ctx, LANES), BF16),
            pltpu.VMEM((t, Q_W), F32), pltpu.VMEM((t, Q_W), F32),
        ]
    else:
        cache_spec = pl.BlockSpec((nseq, 1, t, LANES), lambda b, *_: (b, layer, 0, 0))
        out_specs += [cache_spec] * 4
        out_shape += [jax.ShapeDtypeStruct((n_batch, depth, t, LANES), F32)] * 4
        n_steps = n_batch // nseq
        for w in experts:
            rows_w = w.shape[1] // n_steps
            in_specs.append(pl.BlockSpec((1, rows_w, w.shape[2]), lambda b, *_: (layer, b, 0)))
            args.append(w)
            out_specs.append(pl.BlockSpec((rows_w, w.shape[2]), lambda b, *_: (b, 0)))
            out_shape.append(jax.ShapeDtypeStruct(w.shape[1:], BF16))
        rows = nseq * t
        scratch = [
            pltpu.VMEM((2, rows, LANES), BF16), pltpu.VMEM((4, rows, LANES), BF16),
            pltpu.VMEM((2, rows, LANES), BF16), pltpu.VMEM((4, rows, LANES), BF16),
            pltpu.VMEM((rows, Q_W), F32), pltpu.VMEM((rows, Q_W), F32),
        ]
    n_real = len(args)
    aliases = {}
    if prev is not None:
        first_out = 0 if has_ctx else 2
        for k, arr in enumerate(prev):
            in_specs.append(pl.BlockSpec(memory_space=pl.ANY))
            args.append(arr)
            aliases[1 + n_real + k] = first_out + k

    def body(*refs):
        ins = refs[:1 + n_real]
        rest = refs[1 + len(args):]
        _attn_kernel(has_ctx, t, nseq, layer, *ins, *rest)

    return pl.pallas_call(
        body,
        grid_spec=pltpu.PrefetchScalarGridSpec(
            num_scalar_prefetch=1, grid=(n_batch // nseq,), in_specs=in_specs, out_specs=out_specs,
            scratch_shapes=scratch),
        out_shape=out_shape,
        input_output_aliases=aliases,
        compiler_params=_params(("arbitrary",)),
        name="attn_latent" if has_ctx else "attn_prompt",
    )(sink, *args)


def _stack_pair(x, p):
    return jnp.concatenate([x[:, (2 * p + hl) * B_DIM:(2 * p + hl + 1) * B_DIM] for hl in range(2)], axis=0)


def _delta_kernel(t, nseq, has_s0, *refs):
    if has_s0:
        (zb_ref, abc_ref, abt_ref, conv_ref, prmr_ref, bng_ref, mask_ref,
         s0f_ref, s0b_ref, o_ref, qkv_s, of_s, ob_s, sf_s, sb_s, u_s, wq_s, at_s, kd_s, eg_s,
         pre_s, suf_s, prec_s, sufc_s) = refs
    else:
        (zb_ref, abc_ref, abt_ref, conv_ref, prmr_ref, bng_ref, mask_ref,
         o_ref, sfo_ref, sbo_ref, qkv_s, of_s, ob_s, sf_s, sb_s, u_s, wq_s, at_s, kd_s, eg_s,
         pre_s, suf_s, prec_s, sufc_s) = refs
    n_chunks = t // CHUNK
    n_total = nseq * n_chunks
    s_rows = B_HEADS * B_DIM
    qk_w = B_HEADS * B_DIM

    row = lax.broadcasted_iota(jnp.int32, (t, LANES), 0)
    for q in range(nseq):
        seq = slice(q * t, (q + 1) * t)
        for j in range(3 * B_HEADS):
            cols = slice(j * LANES, (j + 1) * LANES)
            x = zb_ref[seq, cols]
            prev = jnp.where(row == 0, 0.0, pltpu.roll(x, 1, 0))
            nxt = jnp.where(row == t - 1, 0.0, pltpu.roll(x, t - 1, 0))
            cw = 0.5 * conv_ref[0, :, cols]
            y = _silu_half(prev * cw[0:1] + x * cw[1:2] + nxt * cw[2:3])
            if j < 2 * B_HEADS:
                y = y * lax.rsqrt(jnp.sum(y * y, axis=-1, keepdims=True) + EPS)
            if j < B_HEADS:
                y = y * (B_DIM ** -0.5)
            qkv_s[seq, cols] = y

    if has_s0:
        for q in range(nseq):
            sf_s[q * s_rows:(q + 1) * s_rows, :] = s0f_ref[q, 0]
            sb_s[q * s_rows:(q + 1) * s_rows, :] = s0b_ref[q, 0]
    else:
        sf_s[...] = jnp.zeros_like(sf_s)
        sb_s[...] = jnp.zeros_like(sb_s)

    reps = nseq * t // LANES
    gr = -jnp.tile(jnp.exp(prmr_ref[0, 0]), (1, reps)) * _softplus(abt_ref[...] + jnp.tile(prmr_ref[0, 1], (1, reps)))
    seg_lane = lax.broadcasted_iota(jnp.int32, gr.shape, 1) % CHUNK
    pre, suf = gr, gr
    for s in (1, 2, 4, 8, 16, 32):
        pre = pre + jnp.where(seg_lane >= s, pltpu.roll(pre, s, 1), 0.0)
        suf = suf + jnp.where(seg_lane < CHUNK - s, pltpu.roll(suf, nseq * t - s, 1), 0.0)
    pre_s[...] = pre
    suf_s[...] = suf
    zrows = jnp.zeros((LANES - pre.shape[0], LANES), F32)
    for j in range(reps):
        tile = slice(j * LANES, (j + 1) * LANES)
        prec_s[tile, :] = jnp.concatenate([pre[:, tile], zrows], axis=0).T
        sufc_s[tile, :] = jnp.concatenate([suf[:, tile], zrows], axis=0).T
    lane_lo = lax.broadcasted_iota(jnp.int32, (1, LANES), 1) < CHUNK

    def prepare(cc, carry):
        chains = []
        for k in range(PREP_UNROLL):
            c = cc * PREP_UNROLL + k
            r0 = pl.multiple_of(c * CHUNK, CHUNK)
            b_all = _sigmoid(abc_ref[pl.ds(r0, CHUNK), :])
            run_c = (prec_s[pl.ds(r0, CHUNK), :], sufc_s[pl.ds(r0, CHUNK), :])
            tile0 = pl.multiple_of((cc * PREP_UNROLL + k - k % 2) * CHUNK, LANES)
            run = (pre_s[:, pl.ds(tile0, LANES)], suf_s[:, pl.ds(tile0, LANES)])
            run_r = tuple(pltpu.roll(x, CHUNK, 1) for x in run)
            for p in range(B_HEADS // 2):
                kst = _stack_pair(qkv_s[pl.ds(r0, CHUNK), qk_w:2 * qk_w], p)
                qst = _stack_pair(qkv_s[pl.ds(r0, CHUNK), 0:qk_w], p)
                vst = _stack_pair(qkv_s[pl.ds(r0, CHUNK), 2 * qk_w:3 * qk_w], p)
                kq = _dot_nt(jnp.concatenate([kst, qst], axis=0).astype(BF16), kst.astype(BF16))
                for d in range(2):
                    cg = 4 * d + 2 * p
                    edge = CHUNK - 1 if d == 0 else 0
                    rep_col = lambda x, col: jnp.broadcast_to(x[:, col:col + 1], (CHUNK, LANES))
                    b_rep = jnp.concatenate([rep_col(b_all, 8 + cg + hl) for hl in range(2)], axis=0)
                    gcol = jnp.concatenate([rep_col(run_c[d], cg + hl) for hl in range(2)], axis=0)
                    gtot = jnp.concatenate([rep_col(run_c[d][edge:edge + 1], cg + hl) for hl in range(2)], axis=0)
                    ra = cg
                    if k % 2 == 0:
                        grow = jnp.where(lane_lo, run[d][ra:ra + 1], run_r[d][ra + 1:ra + 2])
                    else:
                        grow = jnp.where(lane_lo, run_r[d][ra:ra + 1], run[d][ra + 1:ra + 2])
                    chains.append(dict(c=c, p=p, d=d, kst=kst, qst=qst, vst=vst, kq=kq, b_st=b_rep,
                                       gcol=gcol, gtot=gtot, grow=grow))

        for ch in chains:
            d, b_st, kq, gcol = ch["d"], ch["b_st"], ch.pop("kq"), ch["gcol"]
            decay = jnp.exp(jnp.minimum(gcol - ch.pop("grow"), 0.0))
            ch["a_mat"] = (b_st * kq[:PAIR]) * (decay * mask_ref[2 * d + 1])
            ch["attn"] = (kq[PAIR:] * (decay * mask_ref[2 * d])).astype(BF16)
            ch["t_inv"] = mask_ref[4] - ch["a_mat"] * mask_ref[5]
        for lvl in range(N_LEVELS - 1):
            for ch in chains:
                ch["t16"] = ch["t_inv"].astype(BF16)
                ch["et"] = _dot((ch["a_mat"] * mask_ref[6 + lvl]).astype(BF16), ch["t16"])
            for ch in chains:
                ch["t_inv"] = ch["t_inv"] - _dot(ch.pop("t16"), ch.pop("et").astype(BF16))
        for ch in chains:
            egc = jnp.exp(ch["gcol"])
            rk = jnp.concatenate([ch["b_st"] * ch["vst"], (ch["b_st"] * egc) * ch["kst"]], axis=1)
            ch["rk"] = _dot(ch.pop("t_inv").astype(BF16), rk.astype(BF16))
            ch["qp16"] = (ch["qst"] * egc).astype(BF16)
        for ch in chains:
            c, p, d, rk, qp16 = ch["c"], ch["p"], ch["d"], ch["rk"], ch["qp16"]
            pair_rows = slice(p * PAIR, (p + 1) * PAIR)
            w16 = rk[:, B_DIM:].astype(BF16)
            u_s[d, c, pair_rows, :] = rk[:, :B_DIM]
            at_s[d, c, p] = ch["attn"]
            kd_s[d, c, pair_rows, :] = (ch["kst"] * jnp.exp(ch["gtot"] - ch["gcol"])).astype(BF16)
            eg = jnp.exp(ch["gtot"])
            for hl in range(2):
                h = 2 * p + hl
                rows = slice(hl * CHUNK, (hl + 1) * CHUNK)
                wq_s[d, c, h * 2 * CHUNK:h * 2 * CHUNK + CHUNK, :] = w16[rows]
                wq_s[d, c, h * 2 * CHUNK + CHUNK:(h + 1) * 2 * CHUNK, :] = qp16[rows]
                eg_s[d, c, h * SUBLANES:(h + 1) * SUBLANES, :] = eg[hl * CHUNK:hl * CHUNK + SUBLANES, :]
        return carry

    lax.fori_loop(0, n_total // PREP_UNROLL, prepare, 0)

    def scan_step(i, carry):
        units = []
        for q in range(nseq):
            for d, s_ref, o_s in ((0, sf_s, of_s), (1, sb_s, ob_s)):
                c = q * n_chunks + (i if d == 0 else n_chunks - 1 - i)
                units.append(dict(q=q, d=d, c=c, s_ref=s_ref, o_s=o_s, r0=pl.multiple_of(c * CHUNK, CHUNK)))
        for un in units:
            q, d, c, s_ref = un["q"], un["d"], un["c"], un["s_ref"]
            un["x"] = []
            for h in range(B_HEADS):
                srows = slice(q * s_rows + h * B_DIM, q * s_rows + (h + 1) * B_DIM)
                un["x"].append(_dot(wq_s[d, c, h * 2 * CHUNK:(h + 1) * 2 * CHUNK, :], s_ref[srows, :].astype(BF16)))
        for un in units:
            d, c = un["d"], un["c"]
            un["vp16"], un["o"] = [], []
            for p in range(B_HEADS // 2):
                xs = un["x"][2 * p:2 * p + 2]
                v_new = jnp.concatenate(
                    [u_s[d, c, (2 * p + hl) * CHUNK:(2 * p + hl + 1) * CHUNK, :] - xs[hl][:CHUNK] for hl in range(2)],
                    axis=0)
                vp16 = v_new.astype(BF16)
                un["vp16"].append(vp16)
                un["o"].append(jnp.concatenate([xs[hl][CHUNK:] for hl in range(2)], axis=0)
                               + _dot(at_s[d, c, p], vp16))
        for un in units:
            q, d, c, s_ref, o_s, r0 = un["q"], un["d"], un["c"], un["s_ref"], un["o_s"], un["r0"]
            for h in range(B_HEADS):
                p, hl = divmod(h, 2)
                rows = slice(hl * CHUNK, (hl + 1) * CHUNK)
                srows = slice(q * s_rows + h * B_DIM, q * s_rows + (h + 1) * B_DIM)
                upd = _dot_tn(kd_s[d, c, h * CHUNK:(h + 1) * CHUNK, :], un["vp16"][p][rows])
                eg = jnp.tile(eg_s[d, c, h * SUBLANES:(h + 1) * SUBLANES, :], (B_DIM // SUBLANES, 1))
                s_ref[srows, :] = s_ref[srows, :] * eg + upd
                o_s[pl.ds(r0, CHUNK), h * B_DIM:(h + 1) * B_DIM] = un["o"][p][rows]
        return carry

    lax.fori_loop(0, n_chunks, scan_step, 0)

    if not has_s0:
        for q in range(nseq):
            sfo_ref[q, 0] = sf_s[q * s_rows:(q + 1) * s_rows, :]
            sbo_ref[q, 0] = sb_s[q * s_rows:(q + 1) * s_rows, :]

    for h in range(B_HEADS):
        cols = slice(h * B_DIM, (h + 1) * B_DIM)
        x = of_s[:, cols] + ob_s[:, cols]
        yn = x * lax.rsqrt(jnp.mean(x * x, axis=-1, keepdims=True) + EPS) * bng_ref[0]
        o_ref[:, cols] = (yn * _silu(zb_ref[:, 3 * qk_w + h * B_DIM:3 * qk_w + (h + 1) * B_DIM])).astype(BF16)


def _delta_call(has_s0, t, nseq, n_batch, row_block0, layer, zb, zab, zabt, conv, prmr, bng, masks,
                prev=None, s0=None):
    n_tok = zb.shape[0]
    depth = conv.shape[0]
    n_chunks = nseq * (t // CHUNK)
    assert n_chunks % PREP_UNROLL == 0 and PREP_UNROLL % 2 == 0 and n_batch % nseq == 0
    tok_spec = lambda w: pl.BlockSpec((nseq * t, w), lambda b: (row_block0 + b, 0))
    const = lambda shape: pl.BlockSpec(shape, lambda b: (0,) * len(shape))
    layer_spec = lambda shape: pl.BlockSpec((1,) + shape, lambda b: (layer,) + (0,) * len(shape))
    s_shape = (B_HEADS * B_DIM, B_DIM)
    s_spec = pl.BlockSpec((nseq, 1) + s_shape, lambda b: (b, layer, 0, 0))
    n_ab = zabt.shape[0]
    in_specs = [
        tok_spec(ZB_W), tok_spec(ZAB_W),
        pl.BlockSpec((n_ab, nseq * t), lambda b: (0, row_block0 + b)),
        layer_spec((3, 3 * B_HEADS * B_DIM)), layer_spec((2, n_ab, LANES)),
        layer_spec((1, B_DIM)),
        const((5 + N_LEVELS, PAIR, PAIR)),
    ]
    args = [zb, zab, zabt, conv, prmr, bng, masks]
    out_specs = [tok_spec(B_HEADS * B_DIM)]
    out_shape = [jax.ShapeDtypeStruct((n_tok, B_HEADS * B_DIM), BF16)]
    if has_s0:
        in_specs += [s_spec, s_spec]
        args += [s0[0], s0[1]]
    else:
        out_specs += [s_spec, s_spec]
        out_shape += [jax.ShapeDtypeStruct((n_batch, depth) + s_shape, F32)] * 2
    n_real = len(args)
    aliases = {}
    if prev is not None:
        first_out = 0 if has_s0 else 1
        for k, arr in enumerate(prev):
            in_specs.append(pl.BlockSpec(memory_space=pl.ANY))
            args.append(arr)
            aliases[n_real + k] = first_out + k
    rows = nseq * t
    scratch = [
        pltpu.VMEM((rows, 3 * B_HEADS * B_DIM), F32),
        pltpu.VMEM((rows, B_HEADS * B_DIM), F32), pltpu.VMEM((rows, B_HEADS * B_DIM), F32),
        pltpu.VMEM((nseq * s_shape[0], B_DIM), F32), pltpu.VMEM((nseq * s_shape[0], B_DIM), F32),
        pltpu.VMEM((2, n_chunks, BD, B_DIM), F32),
        pltpu.VMEM((2, n_chunks, 2 * BD, B_DIM), BF16),
        pltpu.VMEM((2, n_chunks, B_HEADS // 2, PAIR, PAIR), BF16),
        pltpu.VMEM((2, n_chunks, BD, B_DIM), BF16),
        pltpu.VMEM((2, n_chunks, B_HEADS * SUBLANES, LANES), F32),
        pltpu.VMEM((n_ab, rows), F32), pltpu.VMEM((n_ab, rows), F32),
        pltpu.VMEM((rows, LANES), F32), pltpu.VMEM((rows, LANES), F32),
    ]

    def body(*refs):
        _delta_kernel(t, nseq, has_s0, *refs[:n_real], *refs[len(args):])

    return pl.pallas_call(
        body,
        grid=(n_batch // nseq,),
        in_specs=in_specs,
        out_specs=out_specs,
        out_shape=out_shape,
        scratch_shapes=scratch,
        input_output_aliases=aliases,
        compiler_params=_params(("arbitrary",)),
        name="delta_latent" if has_s0 else "delta_prompt",
    )(*args)


def _outproj_router(x, ma, mb, mc, m, g, wo_ref, wr_ref, br):
    n = x.shape[0]
    b0, c0 = Q_W, Q_W + B_HEADS * B_DIM
    y = (_dot(ma.astype(BF16), wo_ref[0, 0:b0, :])
         + _dot(mb.astype(BF16), wo_ref[0, b0:c0, :])
         + _dot(mc.astype(BF16), wo_ref[0, c0:c0 + Q_W, :]))
    x1 = x + m[2:3] * y
    h2 = _modulated_norm(x1, g, m[3:4], m[4:5])
    hi, lo = _split2(h2)

    hw = _dot(hi, wr_ref[0])
    logits = (hw[:, :LANES] + hw[:, LANES:] + _dot(lo, wr_ref[0, :, :LANES]) + br).T
    gl = logits[0:N_GROUPS]
    grow = lax.broadcasted_iota(jnp.int32, gl.shape, 0)
    gmax = gl.max(axis=0, keepdims=True)
    g_sel = jnp.where(gl == gmax, grow, N_GROUPS).min(axis=0, keepdims=True)
    g_w = 1.0 / jnp.exp(gl - gmax).sum(axis=0, keepdims=True)
    el = logits[EXPERT_ROW0:EXPERT_ROW0 + N_EXPERTS]
    e_idx = lax.broadcasted_iota(jnp.int32, el.shape, 0)
    el = jnp.where((e_idx // EXPERTS_PER_GROUP) == g_sel, el, -jnp.inf)
    m1 = el.max(axis=0, keepdims=True)
    i1 = jnp.where(el == m1, e_idx, N_EXPERTS).min(axis=0, keepdims=True)
    el2 = jnp.where(e_idx == i1, -jnp.inf, el)
    m2 = el2.max(axis=0, keepdims=True)
    i2 = jnp.where(el2 == m2, e_idx, N_EXPERTS).min(axis=0, keepdims=True)
    tt = jnp.exp(m2 - m1)
    w1 = g_w / (1.0 + tt)
    w2 = w1 * tt
    gate_t = jnp.where(e_idx == i1, w1, 0.0) + jnp.where(e_idx == i2, w2, 0.0)
    gate = jnp.concatenate([gate_t, jnp.zeros((LANES - N_EXPERTS, n), F32)], axis=0).T
    return x1, hi, gate


def _ffn_kernel(x_ref, ma_ref, mb_ref, mc_ref, mod_ref, g_ref, wo_ref, wr_ref, br_ref, w1_ref, w3_ref, w2_ref,
                o_ref, h_s, gate_s):
    j = pl.program_id(1)
    tm = x_ref.shape[0]
    th = w1_ref.shape[1]
    m = mod_ref[0, 0]

    @pl.when(j == 0)
    def _():
        x1, hi, gate = _outproj_router(x_ref[...], ma_ref[...], mb_ref[...], mc_ref[...],
                                       m, g_ref[0], wo_ref, wr_ref, br_ref[0])
        o_ref[...] = x1
        h_s[...] = hi
        gate_s[...] = gate

    @pl.when(j > 0)
    def _():
        h = h_s[...]
        hid = _silu(_dot(h, w1_ref[...])) * _dot(h, w3_ref[...])
        gate = gate_s[...]
        lane = lax.broadcasted_iota(jnp.int32, gate.shape, 1)
        n_e = th // D_EXPERT
        col = lax.broadcasted_iota(jnp.int32, hid.shape, 1) // D_EXPERT
        gmat = jnp.zeros(hid.shape, F32)
        for e in range(n_e):
            ge = jnp.where(lane == (j - 1) * n_e + e, gate, 0.0).sum(axis=1, keepdims=True)
            gmat = jnp.where(col == e, ge, gmat)
        o_ref[...] += m[5:6] * _dot((hid * gmat).astype(BF16), w2_ref[...])


def _ffn_call(layer, x, ma, mb, mc, mods, g, wo, wr, br, w1, w3, w2, slot_fn, tm, th):
    n_tok = x.shape[0]
    n_h = w1.shape[1] // th
    hidden = lambda j: jnp.where(j == 0, n_h - 1, j - 1)
    tok = lambda w: pl.BlockSpec((tm, w), lambda i, j: (i, 0))
    layer_spec = lambda shape: pl.BlockSpec((1,) + shape, lambda i, j: (layer,) + (0,) * len(shape))
    return pl.pallas_call(
        _ffn_kernel,
        grid=(n_tok // tm, n_h + 1),
        in_specs=[tok(D_MODEL), tok(Q_W), tok(B_HEADS * B_DIM), tok(Q_W),
                  pl.BlockSpec((1, 1, 6, D_MODEL), lambda i, j: (layer, slot_fn(i), 0, 0)),
                  layer_spec((1, D_MODEL)), layer_spec((D_MODEL, D_MODEL)), layer_spec((D_MODEL, 2 * LANES)),
                  layer_spec((1, LANES)),
                  pl.BlockSpec((D_MODEL, th), lambda i, j: (0, hidden(j))),
                  pl.BlockSpec((D_MODEL, th), lambda i, j: (0, hidden(j))),
                  pl.BlockSpec((th, D_MODEL), lambda i, j: (hidden(j), 0))],
        out_specs=tok(D_MODEL),
        out_shape=jax.ShapeDtypeStruct((n_tok, D_MODEL), F32),
        scratch_shapes=[pltpu.VMEM((tm, D_MODEL), BF16), pltpu.VMEM((tm, LANES), F32)],
        compiler_params=_params(("arbitrary", "arbitrary")),
        name="ffn",
    )(x, ma, mb, mc, mods, g, wo, wr, br, w1, w3, w2)


def _final_norm_kernel(x_ref, g_ref, o_ref):
    x = x_ref[...]
    o_ref[...] = x * lax.rsqrt(jnp.mean(x * x, axis=-1, keepdims=True) + EPS) * g_ref[...]


def _final_norm_call(x, g, tm, row0, n_rows):
    blk0 = row0 // tm
    return pl.pallas_call(
        _final_norm_kernel,
        grid=(n_rows // tm,),
        in_specs=[pl.BlockSpec((tm, D_MODEL), lambda i: (blk0 + i, 0)), pl.BlockSpec((1, D_MODEL), lambda i: (0, 0))],
        out_specs=pl.BlockSpec((tm, D_MODEL), lambda i: (i, 0)),
        out_shape=jax.ShapeDtypeStruct((n_rows, D_MODEL), F32),
        compiler_params=_params(("arbitrary",)),
        name="final_norm",
    )(x, g)


def _rope_tables(t):
    pos = np.arange(t)
    n_freq = HEAD_DIM // 4
    inv_freq = ROPE_THETA ** (-jnp.arange(n_freq, dtype=F32) / n_freq)
    row = jnp.asarray(pos // GRID_W, F32)
    col = jnp.asarray(pos % GRID_W, F32)
    ang = jnp.concatenate([row[:, None] * inv_freq, col[:, None] * inv_freq], -1)
    cos, sin = jnp.cos(ang), jnp.sin(ang)
    cos_t = jnp.tile(jnp.concatenate([cos, cos], -1), (1, LANES // HEAD_DIM))
    sin_t = jnp.tile(jnp.concatenate([-sin, sin], -1), (1, LANES // HEAD_DIM))
    return cos_t, sin_t


def _delta_tables():
    r = np.arange(PAIR)
    same = (r[:, None] // CHUNK) == (r[None, :] // CHUNK)
    low = same & (r[:, None] >= r[None, :])
    low_s = same & (r[:, None] > r[None, :])
    up = same & (r[:, None] <= r[None, :])
    up_s = same & (r[:, None] < r[None, :])
    levels = []
    for k in range(N_LEVELS):
        s = 1 << k
        levels.append(((r[:, None] // (2 * s)) == (r[None, :] // (2 * s))) & ((r[:, None] // s) != (r[None, :] // s)))
    masks = jnp.asarray(np.stack([low, low_s, up, up_s, np.eye(PAIR, dtype=bool)] + levels).astype(np.float32))
    return masks


def _segment_mean_table():
    r = np.arange(LANES)
    seg = ((r[:, None] // HEAD_DIM) == (r[None, :] // HEAD_DIM)).astype(np.float32) / HEAD_DIM
    hi = jnp.asarray(seg, BF16)
    lo = (jnp.asarray(seg) - hi.astype(F32)).astype(BF16)
    return jnp.stack([hi, lo])


def kernel(x_prompt, x_sample, cache_a_k, cache_a_v, cache_c_k, cache_c_v, state_b_fwd, state_b_bwd, c, c_ctx, w_mod, b_mod, norm1_g, norm2_g, w_in, a_sink, b_conv, b_a_log, b_dt_bias, b_norm_g, c_q_norm, c_k_norm, w_out, w_group, b_group, w_expert, b_expert, w1, w3, w2, final_norm_g):
    n_p, t_p, d = x_prompt.shape
    n_s, t_s, _ = x_sample.shape
    depth = w_in.shape[0]
    past = cache_a_k.shape[2]
    tok_p = n_p * t_p
    n_tok = tok_p + n_s * t_s
    assert d == D_MODEL and tok_p % t_s == 0 and t_s % max(TM_PROJ, TM_FFN) == 0 and t_p % 256 == 0

    w_in_t = jnp.swapaxes(w_in, 1, 2)
    w_out16 = w_out.astype(BF16)
    pad_g = jnp.zeros((depth, d, EXPERT_ROW0 - N_GROUPS), F32)
    pad_e = jnp.zeros((depth, d, LANES - EXPERT_ROW0 - N_EXPERTS), F32)
    w_r = jnp.concatenate([w_group, pad_g, w_expert, pad_e], -1)
    w_r_hi = w_r.astype(BF16)
    w_r2 = jnp.concatenate([w_r_hi, (w_r - w_r_hi.astype(F32)).astype(BF16)], axis=-1)
    b_r = jnp.concatenate([b_group, pad_g[:, 0], b_expert, pad_e[:, 0]], -1)[:, None, :]
    cqn = jnp.tile(c_q_norm, (1, 4))[:, None, :]
    ckn = jnp.tile(c_k_norm, (1, 2))[:, None, :]
    gate_prm = jnp.stack([b_a_log.reshape(depth, 8), b_dt_bias.reshape(depth, 8)], 1)
    prmr = jnp.broadcast_to(jnp.pad(gate_prm, ((0, 0), (0, 0), (0, N_AB - 8)))[..., None],
                            (depth, 2, N_AB, LANES))
    cos_t, sin_t = _rope_tables(t_s)
    masks = _delta_tables()
    seg = _segment_mean_table()

    cond = jnp.concatenate([c_ctx[None, :], c], axis=0)
    cond_b = jnp.broadcast_to(cond[:, :, None], cond.shape + (LANES,))
    mods_all = _mods_call(cond_b, w_mod, b_mod).reshape(depth, SUBLANES, 6, d)

    def slot_fn(tm):
        per_s = t_s // tm
        first = tok_p // tm
        return lambda i: jnp.where(i < first, 0, 1 + (i - first) // per_s)

    xs = (x_prompt.reshape(tok_p, d), x_sample.reshape(n_s * t_s, d))
    blk_s = tok_p // t_s
    ctx = tuple(a.reshape(n_s, depth, past, LANES) for a in (cache_a_k, cache_a_v, cache_c_k, cache_c_v))
    s0 = tuple(a.reshape(n_s, depth, B_HEADS * B_DIM, B_DIM) for a in (state_b_fwd, state_b_bwd))
    g1, g2, bng = norm1_g[:, None, :], norm2_g[:, None, :], b_norm_g[:, None, :]
    caches = None
    states = None
    for l in range(depth):
        za, zb, zc, zab, zabt, *rest = _inproj_call(l, xs, mods_all, g1, w_in_t, slot_fn(TM_PROJ), TM_PROJ,
                                                    experts=(w3, w2))
        x = rest[0] if len(xs) > 1 else xs[0]
        w3b, w2b = rest[-2:]

        ao, co, *rest = _attn_call(False, t_p, ATTN_NSEQ, n_p, 0, l, za, zc, a_sink, cqn, ckn, seg, prev=caches,
                                   experts=(w1,))
        caches, (w1b,) = rest[:4], rest[4:]
        ao, co = _attn_call(True, t_s, 1, n_s, blk_s, l, za, zc, a_sink, cqn, ckn, seg, prev=(ao, co),
                            rope=(cos_t, sin_t), ctx=ctx)

        bo, *states = _delta_call(False, t_p, DELTA_NSEQ, n_p, 0, l, zb, zab, zabt, b_conv, prmr, bng, masks,
                                  prev=states)
        (bo,) = _delta_call(True, t_s, 1, n_s, blk_s, l, zb, zab, zabt, b_conv, prmr, bng, masks,
                            prev=(bo,), s0=s0)

        x = _ffn_call(l, x, ao, bo, co, mods_all, g2, w_out16, w_r2, b_r, w1b, w3b, w2b, slot_fn(TM_FFN), TM_FFN, TH_FFN)
        xs = (x,)

    y_prompt = _final_norm_call(x, final_norm_g[None], TM_NORM, 0, tok_p).reshape(n_p, t_p, d)
    y_sample = _final_norm_call(x, final_norm_g[None], TM_NORM, tok_p, n_s * t_s).reshape(n_s, t_s, d)
    new_ak, new_av, new_ck, new_cv = (a.reshape(n_p, depth, t_p, 2, HEAD_DIM) for a in caches)
    new_sf, new_sb = (a.reshape(n_p, depth, B_HEADS, B_DIM, B_DIM) for a in states)
    return (y_prompt, y_sample, new_ak, new_av, new_ck, new_cv, new_sf, new_sb)
```
